```python
import jax, jax.numpy as jnp
from jax import lax
import numpy as np

D_MODEL = 1024
BATCH = 2
SEQ = 8192
DEPTH = 2

N_META = 16
BLOCK = 128
WINDOW = 128
HEAD_DIM = 64
N_Q_HEADS = D_MODEL // HEAD_DIM
N_KV_HEADS = 4
Q_PER_KV = N_Q_HEADS // N_KV_HEADS
QKV_DIM = (N_Q_HEADS + 2 * N_KV_HEADS) * HEAD_DIM
RWKV_HEAD = 64
RWKV_HEADS = D_MODEL // RWKV_HEAD
DECAY_LORA = max(32, int(round(1.8 * D_MODEL ** 0.5 / 32)) * 32)
AAA_LORA = max(32, int(round(1.8 * D_MODEL ** 0.5 / 32)) * 32)
GATE_LORA = max(32, int(round(0.6 * D_MODEL ** 0.8 / 32)) * 32)
N_MIX = 6
D_FF = 7 * D_MODEL // 2
N_EXPERTS = 8
TOP_K = 2
N_ATTN = (DEPTH + 1) // 2
N_RWKV = DEPTH // 2
NORM_EPS = 1e-5
GN_EPS = 64e-5

kernel_name = 'hybrid_swa_sink_rwkv7_moe_meta'


def rmsnorm(x, g, eps=NORM_EPS):
    xf = x.astype(jnp.float32)
    y = xf * lax.rsqrt(jnp.mean(xf * xf, axis=-1, keepdims=True) + eps)
    return (y * g.astype(jnp.float32)).astype(x.dtype)


def swiglu(h, w_gate, w_up, w_down):
    return (jax.nn.silu(h @ w_gate) * (h @ w_up)) @ w_down


def swa_sink_attention(h, w_qkv, q_gain, k_gain, sinks, w_o):
    B, T, _ = h.shape
    pad = BLOCK - N_META
    Tp = T + pad
    nb = Tp // BLOCK
    qkv = jnp.pad(h @ w_qkv, ((0, 0), (pad, 0), (0, 0)))
    q, k, v = jnp.split(qkv, [N_Q_HEADS * HEAD_DIM, (N_Q_HEADS + N_KV_HEADS) * HEAD_DIM], axis=-1)
    q = rmsnorm(q.reshape(B, Tp, N_Q_HEADS, HEAD_DIM), q_gain)
    k = rmsnorm(k.reshape(B, Tp, N_KV_HEADS, HEAD_DIM), k_gain)
    v = v.reshape(B, Tp, N_KV_HEADS, HEAD_DIM)
    qb = q.reshape(B, nb, BLOCK, N_KV_HEADS, Q_PER_KV, HEAD_DIM)
    kb = k.reshape(B, nb, BLOCK, N_KV_HEADS, HEAD_DIM)
    vb = v.reshape(B, nb, BLOCK, N_KV_HEADS, HEAD_DIM)

    def with_prev(t):
        prev = jnp.pad(t, ((0, 0), (1, 0), (0, 0), (0, 0), (0, 0)))[:, :-1]
        return jnp.concatenate([prev, t], axis=2)

    k_band, v_band = with_prev(kb), with_prev(vb)
    k_meta, v_meta = k[:, pad:BLOCK], v[:, pad:BLOCK]
    scale = HEAD_DIM ** -0.5
    s_band = jnp.einsum('bnqkgd,bnjkd->bnkgqj', qb, k_band, preferred_element_type=jnp.float32) * scale
    s_meta = jnp.einsum('bnqkgd,bmkd->bnkgqm', qb, k_meta, preferred_element_type=jnp.float32) * scale

    q_pos = jnp.arange(nb)[:, None] * BLOCK + jnp.arange(BLOCK)[None, :]
    k_pos = (jnp.arange(nb)[:, None] - 1) * BLOCK + jnp.arange(2 * BLOCK)[None, :]
    diff = q_pos[:, :, None] - k_pos[:, None, :]
    band_ok = (k_pos[:, None, :] >= BLOCK) & (diff >= 0) & (diff < WINDOW)
    meta_pos = pad + jnp.arange(N_META)
    meta_ok = meta_pos[None, None, :] <= q_pos[:, :, None]
    s_band = jnp.where(band_ok[None, :, None, None], s_band, -jnp.inf)
    s_meta = jnp.where(meta_ok[None, :, None, None], s_meta, -jnp.inf)
    sink = jnp.broadcast_to(sinks.astype(jnp.float32).reshape(1, 1, N_KV_HEADS, Q_PER_KV, 1, 1),
                            s_meta.shape[:-1] + (1,))
    p = jax.nn.softmax(jnp.concatenate([s_meta, s_band, sink], axis=-1), axis=-1)
    p_meta = p[..., :N_META].astype(v.dtype)
    p_band = p[..., N_META:N_META + 2 * BLOCK].astype(v.dtype)
    o = (jnp.einsum('bnkgqm,bmkd->bnqkgd', p_meta, v_meta)
         + jnp.einsum('bnkgqj,bnjkd->bnqkgd', p_band, v_band))
    o = o.reshape(B, Tp, N_Q_HEADS * HEAD_DIM)[:, pad:]
    return o @ w_o


def rwkv7_time_mix(h, mix, w0, w1, w2, a0, a1, a2, g1, g2, k_k, k_a, r_k,
                   w_r, w_k, w_v, w_o, gn_w, gn_b):
    B, T, D = h.shape
    H, N = RWKV_HEADS, RWKV_HEAD
    xx = jnp.pad(h, ((0, 0), (1, 0), (0, 0)))[:, :-1] - h
    xr, xw, xk, xv, xa, xg = [h + xx * mix[i] for i in range(N_MIX)]
    r = xr @ w_r
    k = xk @ w_k
    v = xv @ w_v
    w = -jax.nn.softplus(-(w0 + jnp.tanh(xw @ w1) @ w2)) - 0.5
    a = jax.nn.sigmoid(a0 + (xa @ a1) @ a2)
    g = jax.nn.sigmoid(xg @ g1) @ g2
    kk = (k * k_k).astype(jnp.float32).reshape(B, T, H, N)
    kk = kk / jnp.maximum(jnp.linalg.norm(kk, axis=-1, keepdims=True), 1e-12)
    k = k * (1 + (a - 1) * k_a)

    to_h = lambda t: t.astype(jnp.float32).reshape(B, T, H, N)
    r_h, k_h, v_h, a_h = to_h(r), to_h(k), to_h(v), to_h(a)
    decay = jnp.exp(-jnp.exp(to_h(w)))
    aa = -kk
    bb = kk * a_h

    def step(S, inp):
        r_t, w_t, k_t, v_t, a_t, b_t = inp
        sa = jnp.einsum('bhvk,bhk->bhv', S, a_t)
        S = S * w_t[:, :, None, :] + sa[..., None] * b_t[:, :, None, :] + v_t[..., None] * k_t[:, :, None, :]
        return S, jnp.einsum('bhvk,bhk->bhv', S, r_t)

    xs = tuple(jnp.swapaxes(t, 0, 1) for t in (r_h, decay, k_h, v_h, aa, bb))
    S0 = jnp.zeros((B, H, N, N), jnp.float32)
    _, out = lax.scan(step, S0, xs)
    out = jnp.swapaxes(out, 0, 1)
    mu = jnp.mean(out, axis=-1, keepdims=True)
    var = jnp.mean(jnp.square(out - mu), axis=-1, keepdims=True)
    out = (out - mu) * lax.rsqrt(var + GN_EPS) * gn_w.astype(jnp.float32).reshape(H, N) \
        + gn_b.astype(jnp.float32).reshape(H, N)
    bonus = jnp.sum(r_h * k_h * r_k.astype(jnp.float32), axis=-1, keepdims=True) * v_h
    y = (out + bonus).reshape(B, T, D).astype(h.dtype)
    return (y * g) @ w_o


def moe_swiglu(h, w_router, w_gate, w_up, w_down):
    probs = jax.nn.softmax((h @ w_router).astype(jnp.float32), axis=-1)
    top_p, top_i = lax.top_k(probs, TOP_K)
    top_p = top_p / jnp.sum(top_p, axis=-1, keepdims=True)
    gates = jnp.sum(jax.nn.one_hot(top_i, N_EXPERTS, dtype=jnp.float32) * top_p[..., None], axis=-2)
    y = jnp.zeros(h.shape, jnp.float32)
    for e in range(N_EXPERTS):
        y = y + gates[..., e:e + 1] * swiglu(h, w_gate[e], w_up[e], w_down[e]).astype(jnp.float32)
    return y.astype(h.dtype)


def setup_inputs(seed: int = 0) -> dict:
    key = jax.random.key(seed)
    ks = iter(jax.random.split(key, 48))
    nrm = lambda shape, scale: jax.random.normal(next(ks), shape, jnp.float32) * scale
    gain = lambda shape: 1.0 + nrm(shape, 0.02)
    D, F, E = D_MODEL, D_FF, N_EXPERTS
    NA, NR = N_ATTN, N_RWKV
    ND, NM = N_ATTN, N_RWKV
    return {
        'x': nrm((BATCH, SEQ, D), 1.0),
        'meta_tokens': nrm((N_META, D), 1.0),
        'mixer_norm': gain((DEPTH, D)),
        'ffn_norm': gain((DEPTH, D)),
        'attn_w_qkv': nrm((NA, D, QKV_DIM), D ** -0.5),
        'attn_q_norm': gain((NA, HEAD_DIM)),
        'attn_k_norm': gain((NA, HEAD_DIM)),
        'attn_sinks': nrm((NA, N_Q_HEADS), 0.5),
        'attn_w_o': nrm((NA, N_Q_HEADS * HEAD_DIM, D), (N_Q_HEADS * HEAD_DIM) ** -0.5),
        'rwkv_mix': jax.random.uniform(next(ks), (NR, N_MIX, D), jnp.float32),
        'rwkv_w0': jax.random.uniform(next(ks), (NR, D), jnp.float32, -2.0, 2.0),
        'rwkv_w1': nrm((NR, D, DECAY_LORA), D ** -0.5),
        'rwkv_w2': nrm((NR, DECAY_LORA, D), DECAY_LORA ** -0.5),
        'rwkv_a0': nrm((NR, D), 0.1),
        'rwkv_a1': nrm((NR, D, AAA_LORA), D ** -0.5),
        'rwkv_a2': nrm((NR, AAA_LORA, D), AAA_LORA ** -0.5),
        'rwkv_g1': nrm((NR, D, GATE_LORA), D ** -0.5),
        'rwkv_g2': nrm((NR, GATE_LORA, D), GATE_LORA ** -0.5),
        'rwkv_k_k': 0.85 + nrm((NR, D), 0.02),
        'rwkv_k_a': gain((NR, D)),
        'rwkv_r_k': nrm((NR, RWKV_HEADS, RWKV_HEAD), 0.1),
        'rwkv_w_r': nrm((NR, D, D), D ** -0.5),
        'rwkv_w_k': nrm((NR, D, D), D ** -0.5),
        'rwkv_w_v': nrm((NR, D, D), D ** -0.5),
        'rwkv_w_o': nrm((NR, D, D), D ** -0.5),
        'rwkv_gn_w': gain((NR, D)),
        'rwkv_gn_b': nrm((NR, D), 0.02),
        'ffn_w_gate': nrm((ND, D, F), D ** -0.5),
        'ffn_w_up': nrm((ND, D, F), D ** -0.5),
        'ffn_w_down': nrm((ND, F, D), F ** -0.5),
        'moe_router': nrm((NM, D, E), D ** -0.5),
        'moe_w_gate': nrm((NM, E, D, F), D ** -0.5),
        'moe_w_up': nrm((NM, E, D, F), D ** -0.5),
        'moe_w_down': nrm((NM, E, F, D), F ** -0.5),
    }


def reference(x, meta_tokens, mixer_norm, ffn_norm,
              attn_w_qkv, attn_q_norm, attn_k_norm, attn_sinks, attn_w_o,
              rwkv_mix, rwkv_w0, rwkv_w1, rwkv_w2, rwkv_a0, rwkv_a1, rwkv_a2,
              rwkv_g1, rwkv_g2, rwkv_k_k, rwkv_k_a, rwkv_r_k,
              rwkv_w_r, rwkv_w_k, rwkv_w_v, rwkv_w_o, rwkv_gn_w, rwkv_gn_b,
              ffn_w_gate, ffn_w_up, ffn_w_down,
              moe_router, moe_w_gate, moe_w_up, moe_w_down):
    B = x.shape[0]
    meta = jnp.broadcast_to(meta_tokens.astype(x.dtype)[None], (B, N_META, x.shape[-1]))
    h = jnp.concatenate([meta, x], axis=1)
    for i in range(DEPTH):
        j = i // 2
        u = rmsnorm(h, mixer_norm[i])
        if i % 2 == 0:
            h = h + swa_sink_attention(u, attn_w_qkv[j], attn_q_norm[j], attn_k_norm[j],
                                       attn_sinks[j], attn_w_o[j])
        else:
            h = h + rwkv7_time_mix(u, rwkv_mix[j], rwkv_w0[j], rwkv_w1[j], rwkv_w2[j],
                                   rwkv_a0[j], rwkv_a1[j], rwkv_a2[j], rwkv_g1[j], rwkv_g2[j],
                                   rwkv_k_k[j], rwkv_k_a[j], rwkv_r_k[j],
                                   rwkv_w_r[j], rwkv_w_k[j], rwkv_w_v[j], rwkv_w_o[j],
                                   rwkv_gn_w[j], rwkv_gn_b[j])
        u = rmsnorm(h, ffn_norm[i])
        if i % 2 == 0:
            h = h + swiglu(u, ffn_w_gate[j], ffn_w_up[j], ffn_w_down[j])
        else:
            h = h + moe_swiglu(u, moe_router[j], moe_w_gate[j], moe_w_up[j], moe_w_down[j])
    return h[:, N_META:]
```

```python
import functools

import jax
import jax.numpy as jnp
from jax import lax
from jax.experimental import pallas as pl
from jax.experimental.pallas import tpu as pltpu

F32 = jnp.float32
BF16 = jnp.bfloat16

D_MODEL = 1024
BATCH = 2
SEQ = 8192
N_META = 16
BLOCK = 128
PAD = BLOCK - N_META
TP = SEQ + BLOCK
N_ROWS = BATCH * TP
N_BLOCKS = TP // BLOCK
META_BLOCK = N_BLOCKS - 1
HEAD_DIM = 64
N_Q_HEADS = 16
N_KV_HEADS = 4
Q_PER_KV = 4
QKV_DIM = (N_Q_HEADS + 2 * N_KV_HEADS) * HEAD_DIM
RWKV_HEADS = 16
RWKV_HEAD = 64
D_FF = 3584
N_EXPERTS = 8
NORM_EPS = 1e-5
GN_EPS = 64e-5
CHUNK = 64
N_CHUNKS = TP // CHUNK
META_CHUNK0 = SEQ // CHUNK

ROW_TILE = 640
PROJ_TILE = 320
FF_TILE = 512
VMEM_LIMIT = 56 * 1024 * 1024
NEG = -1e30

NT_DIMS = (((1,), (1,)), ((), ()))
TN_DIMS = (((0,), (0,)), ((), ()))


def _params(*sem):
    return pltpu.CompilerParams(dimension_semantics=sem, vmem_limit_bytes=VMEM_LIMIT)


def _rms(x, gain):
    return x * lax.rsqrt(jnp.mean(x * x, axis=-1, keepdims=True) + NORM_EPS) * gain


def _sigmoid(x):
    return 1.0 / (1.0 + jnp.exp(-x))


def _dot(a, b):
    return jnp.dot(a, b, preferred_element_type=F32)


def _qkv_kernel(h_ref, g_ref, w_ref, o_ref):
    u = _rms(h_ref[...], g_ref[...])
    o_ref[...] = _dot(u.astype(BF16), w_ref[...]).astype(BF16)


def _qkv(h, gain, w):
    return pl.pallas_call(
        _qkv_kernel,
        grid=(N_ROWS // ROW_TILE,),
        in_specs=[pl.BlockSpec((ROW_TILE, D_MODEL), lambda i: (i, 0)),
                  pl.BlockSpec((1, D_MODEL), lambda i: (0, 0)),
                  pl.BlockSpec((D_MODEL, QKV_DIM), lambda i: (0, 0))],
        out_specs=pl.BlockSpec((ROW_TILE, QKV_DIM), lambda i: (i, 0)),
        out_shape=jax.ShapeDtypeStruct((N_ROWS, QKV_DIM), BF16),
        compiler_params=_params("parallel"),
    )(h, gain, w)


def _attn_kernel(sink_ref, h_ref, q_ref, kc_ref, vc_ref, kp_ref, vp_ref, km_ref, vm_ref,
                 qg_ref, kg_ref, wo_ref, o_ref, o_scr):
    j = pl.program_id(1)
    n_keys = N_META + 2 * BLOCK
    rowi = lax.broadcasted_iota(jnp.int32, (BLOCK, n_keys), 0)
    col = lax.broadcasted_iota(jnp.int32, (BLOCK, n_keys), 1)
    far = 4 * BLOCK
    is_real = j < META_BLOCK
    meta_off = jnp.where(is_real, -far, PAD)
    prev_off = jnp.where(jnp.logical_and(j >= 1, is_real), 0, far)
    cur_off = jnp.where(is_real, 0, far)
    meta_ok = jnp.logical_and(col < N_META, col + meta_off <= rowi)
    prev_ok = jnp.logical_and(jnp.logical_and(col >= N_META, col < N_META + BLOCK),
                              col - N_META > rowi + prev_off)
    cur_ok = jnp.logical_and(col >= N_META + BLOCK, col - (N_META + BLOCK) + cur_off <= rowi)
    ok = jnp.logical_or(jnp.logical_or(meta_ok, prev_ok), cur_ok)
    ok4 = jnp.concatenate([ok] * Q_PER_KV, axis=0)

    q_all = q_ref[0].astype(F32)
    kc, vc = kc_ref[0].astype(F32), vc_ref[0].astype(F32)
    kp, vp = kp_ref[0].astype(F32), vp_ref[0].astype(F32)
    km, vm = km_ref[0, PAD:, :].astype(F32), vm_ref[0, PAD:, :].astype(F32)
    qg, kg = qg_ref[...], kg_ref[...]
    scale = HEAD_DIM ** -0.5
    for kv in range(N_KV_HEADS):
        ks = slice(kv * HEAD_DIM, (kv + 1) * HEAD_DIM)
        kcat = jnp.concatenate([km[:, ks], kp[:, ks], kc[:, ks]], axis=0)
        vcat = jnp.concatenate([vm[:, ks], vp[:, ks], vc[:, ks]], axis=0).astype(BF16)
        kcat = _rms(kcat, kg).astype(BF16)
        heads = range(kv * Q_PER_KV, (kv + 1) * Q_PER_KV)
        q4 = jnp.concatenate([q_all[:, hd * HEAD_DIM:(hd + 1) * HEAD_DIM] for hd in heads], axis=0)
        q4 = (_rms(q4, qg) * scale).astype(BF16)
        s = lax.dot_general(q4, kcat, NT_DIMS, preferred_element_type=F32)
        s = jnp.where(ok4, s, NEG)
        sink = jnp.concatenate([jnp.full((BLOCK, 1), sink_ref[hd], F32) for hd in heads], axis=0)
        m = jnp.maximum(jnp.max(s, axis=-1, keepdims=True), sink)
        e = jnp.exp(s - m)
        den = jnp.sum(e, axis=-1, keepdims=True) + jnp.exp(sink - m)
        o4 = _dot(e.astype(BF16), vcat) / den
        for g, hd in enumerate(heads):
            o_scr[:, hd * HEAD_DIM:(hd + 1) * HEAD_DIM] = o4[g * BLOCK:(g + 1) * BLOCK, :]
    o_ref[0] = h_ref[0] + _dot(o_scr[...].astype(BF16), wo_ref[...])


def _attention(h, qkv, sinks, q_gain, k_gain, w_o):
    kcol, vcol = N_Q_HEADS * HEAD_DIM // 256, N_Q_HEADS * HEAD_DIM // 256 + 1
    kvw = N_KV_HEADS * HEAD_DIM
    prev = lambda j: jnp.maximum(j - 1, 0)
    return pl.pallas_call(
        _attn_kernel,
        grid=(BATCH, N_BLOCKS),
        in_specs=[pl.BlockSpec(memory_space=pltpu.SMEM),
                  pl.BlockSpec((1, BLOCK, D_MODEL), lambda b, j: (b, j, 0)),
                  pl.BlockSpec((1, BLOCK, D_MODEL), lambda b, j: (b, j, 0)),
                  pl.BlockSpec((1, BLOCK, kvw), lambda b, j: (b, j, kcol)),
                  pl.BlockSpec((1, BLOCK, kvw), lambda b, j: (b, j, vcol)),
                  pl.BlockSpec((1, BLOCK, kvw), lambda b, j: (b, prev(j), kcol)),
                  pl.BlockSpec((1, BLOCK, kvw), lambda b, j: (b, prev(j), vcol)),
                  pl.BlockSpec((1, BLOCK, kvw), lambda b, j: (b, META_BLOCK, kcol)),
                  pl.BlockSpec((1, BLOCK, kvw), lambda b, j: (b, META_BLOCK, vcol)),
                  pl.BlockSpec((1, HEAD_DIM), lambda b, j: (0, 0)),
                  pl.BlockSpec((1, HEAD_DIM), lambda b, j: (0, 0)),
                  pl.BlockSpec((D_MODEL, D_MODEL), lambda b, j: (0, 0))],
        out_specs=pl.BlockSpec((1, BLOCK, D_MODEL), lambda b, j: (b, j, 0)),
        out_shape=jax.ShapeDtypeStruct((BATCH, TP, D_MODEL), F32),
        scratch_shapes=[pltpu.VMEM((BLOCK, D_MODEL), F32)],
        compiler_params=_params("parallel", "parallel"),
    )(sinks, h, qkv, qkv, qkv, qkv, qkv, qkv, qkv, q_gain, k_gain, w_o)


def _ffn_kernel(h_ref, g_ref, wg_ref, wu_ref, wd_ref, o_ref, u_scr, acc):
    f = pl.program_id(1)

    @pl.when(f == 0)
    def _():
        u_scr[...] = _rms(h_ref[...], g_ref[...]).astype(BF16)
        acc[...] = jnp.zeros_like(acc)

    u = u_scr[...]
    a = _dot(u, wg_ref[...])
    b = _dot(u, wu_ref[...])
    acc[...] += _dot((a * _sigmoid(a) * b).astype(BF16), wd_ref[...])

    @pl.when(f == pl.num_programs(1) - 1)
    def _():
        o_ref[...] = h_ref[...] + acc[...]


def _ffn(h, gain, wg, wu, wd):
    return pl.pallas_call(
        _ffn_kernel,
        grid=(N_ROWS // ROW_TILE, D_FF // FF_TILE),
        in_specs=[pl.BlockSpec((ROW_TILE, D_MODEL), lambda i, f: (i, 0)),
                  pl.BlockSpec((1, D_MODEL), lambda i, f: (0, 0)),
                  pl.BlockSpec((D_MODEL, FF_TILE), lambda i, f: (0, f)),
                  pl.BlockSpec((D_MODEL, FF_TILE), lambda i, f: (0, f)),
                  pl.BlockSpec((FF_TILE, D_MODEL), lambda i, f: (f, 0))],
        out_specs=pl.BlockSpec((ROW_TILE, D_MODEL), lambda i, f: (i, 0)),
        out_shape=jax.ShapeDtypeStruct((N_ROWS, D_MODEL), F32),
        scratch_shapes=[pltpu.VMEM((ROW_TILE, D_MODEL), BF16), pltpu.VMEM((ROW_TILE, D_MODEL), F32)],
        compiler_params=_params("parallel", "arbitrary"),
    )(h, gain, wg, wu, wd)


def _rwkv_proj_kernel(h_ref, hp_ref, g_ref, mix_ref, w0_ref, a0_ref, kk_ref, ka_ref,
                      wr_ref, wk_ref, wv_ref, w1_ref, w2_ref, a1_ref, a2_ref, g1_ref, g2_ref,
                      r_out, ld_out, k_out, v_out, kk_out, a_out, g_out):
    i = pl.program_id(0)
    tiles_per_batch = TP // PROJ_TILE
    r0 = (i % tiles_per_batch) * PROJ_TILE
    local = lax.broadcasted_iota(jnp.int32, (PROJ_TILE, 1), 0)
    lrow = local + r0
    gain = g_ref[...]
    is_pad = jnp.logical_and(lrow >= SEQ, lrow < SEQ + PAD)
    u = jnp.where(is_pad, 0.0, _rms(h_ref[...], gain))
    u_prev_tile = _rms(hp_ref[7:8, :], gain)
    xprev = pltpu.roll(u, 1, 0)
    xprev = jnp.where(local == 0, u_prev_tile, xprev)
    xprev = jnp.where(lrow == SEQ, 0.0, xprev)
    xx = xprev - u
    mix = mix_ref[...]
    lerp = lambda n: (u + xx * mix[n:n + 1, :]).astype(BF16)
    xr, xw, xk, xv, xa, xg = [lerp(n) for n in range(6)]
    r = _dot(xr, wr_ref[...])
    k = _dot(xk, wk_ref[...])
    v = _dot(xv, wv_ref[...])
    lw = _dot(jnp.tanh(_dot(xw, w1_ref[...])).astype(BF16), w2_ref[...])
    z = -(w0_ref[...] + lw)
    softplus = jnp.maximum(z, 0.0) + jnp.log(1.0 + jnp.exp(-jnp.abs(z)))
    w = -softplus - 0.5
    a = _sigmoid(a0_ref[...] + _dot(_dot(xa, a1_ref[...]).astype(BF16), a2_ref[...]))
    g = _dot(_sigmoid(_dot(xg, g1_ref[...])).astype(BF16), g2_ref[...])
    r_out[...] = r
    ld_out[...] = -jnp.exp(w)
    kk_out[...] = k * kk_ref[...]
    k_out[...] = k * (1.0 + (a - 1.0) * ka_ref[...])
    v_out[...] = v
    a_out[...] = a
    g_out[...] = g


def _rwkv_proj(h, gain, mix, w0, a0, k_k, k_a, w_r, w_k, w_v, w1, w2, a1, a2, g1, g2):
    tiles_per_batch = TP // PROJ_TILE
    rows8 = PROJ_TILE // 8

    def prev_map(i):
        b = i // tiles_per_batch
        first = (i % tiles_per_batch) == 0
        return (jnp.where(first, (b * TP + TP - 8) // 8, i * rows8 - 1), 0)

    row = pl.BlockSpec((PROJ_TILE, D_MODEL), lambda i: (i, 0))
    full = lambda a: pl.BlockSpec(a.shape, lambda i: (0,) * a.ndim)
    smalls = (gain, mix, w0, a0, k_k, k_a, w_r, w_k, w_v, w1, w2, a1, a2, g1, g2)
    return pl.pallas_call(
        _rwkv_proj_kernel,
        grid=(N_ROWS // PROJ_TILE,),
        in_specs=[row, pl.BlockSpec((8, D_MODEL), prev_map)] + [full(a) for a in smalls],
        out_specs=[row] * 7,
        out_shape=[jax.ShapeDtypeStruct((N_ROWS, D_MODEL), F32)] * 7,
        compiler_params=_params("parallel"),
    )(h, h, *smalls)


def _scan_prep_kernel(r_ref, ld_ref, k_ref, v_ref, kk_ref, a_ref, rk_ref,
                      rp_out, op_out, g_out, ht_out, bonus_out, pl_out):
    L, N = CHUNK, RWKV_HEAD
    ri = lax.broadcasted_iota(jnp.int32, (L, L), 0)
    ci = lax.broadcasted_iota(jnp.int32, (L, L), 1)
    incl = ci <= ri
    strict = ci < ri
    eye = jnp.where(ci == ri, 1.0, 0.0).astype(F32)
    ld = ld_ref[0]
    cs = jnp.dot(jnp.where(incl, 1.0, 0.0).astype(F32), ld, precision=lax.Precision.HIGHEST,
                 preferred_element_type=F32)
    p_all = jnp.exp(cs)
    pprev_all = jnp.exp(cs - ld)
    pinv_all = jnp.exp(-cs)
    pl_all = p_all[L - 1:L, :]
    pl_out[0] = pl_all
    r_all, k_all, v_all, kk_all, a_all = r_ref[0], k_ref[0], v_ref[0], kk_ref[0], a_ref[0]
    rk_all = rk_ref[...]
    for hd in range(RWKV_HEADS):
        sl = slice(hd * N, (hd + 1) * N)
        kk = kk_all[:, sl]
        nrm = jnp.sqrt(jnp.sum(kk * kk, axis=-1, keepdims=True))
        kk = kk / jnp.maximum(nrm, 1e-12)
        r, k, v = r_all[:, sl], k_all[:, sl], v_all[:, sl]
        at = -kk * pprev_all[:, sl]
        rt = r * p_all[:, sl]
        bt = kk * a_all[:, sl] * pinv_all[:, sl]
        kt = k * pinv_all[:, sl]
        vb = v.astype(BF16)
        ar = jnp.concatenate([at, rt], axis=0).astype(BF16)
        bk = jnp.concatenate([bt, kt], axis=0).astype(BF16)
        big = lax.dot_general(ar, bk, NT_DIMS, preferred_element_type=F32)
        a_ab = jnp.where(strict, big[:L, :L], 0.0)
        a_ak = jnp.where(strict, big[:L, L:], 0.0)
        a_rb = jnp.where(incl, big[L:, :L], 0.0)
        a_rk = jnp.where(incl, big[L:, L:], 0.0)
        x = a_ab
        tm = eye + x
        for _ in range(5):
            xb = x.astype(BF16)
            x = _dot(xb, xb)
            tm = tm + _dot(tm.astype(BF16), x.astype(BF16))
        av = _dot(jnp.concatenate([a_ak, a_rk], axis=0).astype(BF16), vb)
        y = jnp.concatenate([at, av[:L]], axis=1).astype(BF16)
        wu = _dot(tm.astype(BF16), y)
        aw = _dot(a_rb.astype(BF16), wu.astype(BF16))
        rp_out[0, :, sl] = rt + aw[:, :N]
        op_out[0, :, sl] = av[L:] + aw[:, N:]
        plh = pl_all[:, sl]
        bh = (bt * plh).astype(BF16)
        kh = (kt * plh).astype(BF16)
        g_out[0, :, sl] = lax.dot_general(bh, wu[:, :N].astype(BF16), TN_DIMS, preferred_element_type=F32)
        ht_out[0, :, sl] = (lax.dot_general(wu[:, N:].astype(BF16), bh, TN_DIMS, preferred_element_type=F32)
                            + lax.dot_general(vb, kh, TN_DIMS, preferred_element_type=F32))
        bonus_out[0, :, sl] = jnp.sum(r * k * rk_all[:, sl], axis=-1, keepdims=True) * v


def _scan_prep(r, ld, k, v, kk, a, r_k):
    blk = pl.BlockSpec((1, CHUNK, D_MODEL), lambda b, c: (b, c, 0))
    big = jax.ShapeDtypeStruct((BATCH, TP, D_MODEL), F32)
    return pl.pallas_call(
        _scan_prep_kernel,
        grid=(BATCH, N_CHUNKS),
        in_specs=[blk] * 6 + [pl.BlockSpec((1, D_MODEL), lambda b, c: (0, 0))],
        out_specs=[blk] * 5 + [pl.BlockSpec((1, 1, D_MODEL), lambda b, c: (b * N_CHUNKS + c, 0, 0))],
        out_shape=[big] * 5 + [jax.ShapeDtypeStruct((BATCH * N_CHUNKS, 1, D_MODEL), F32)],
        compiler_params=_params("parallel", "parallel"),
    )(r, ld, k, v, kk, a, r_k)


def _scan_kernel(rp_ref, op_ref, g_ref, ht_ref, bonus_ref, pl_ref, gw_ref, gb_ref, y_ref, s_scr):
    c = pl.program_id(0)
    N = RWKV_HEAD

    @pl.when(c == 0)
    def _():
        s_scr[...] = jnp.zeros_like(s_scr)

    for b in range(BATCH):
        for hd in range(RWKV_HEADS):
            sl = slice(hd * N, (hd + 1) * N)
            s0 = s_scr[b, :, sl]
            s0b = s0.astype(BF16)
            o = lax.dot_general(rp_ref[b, :, sl].astype(BF16), s0b, NT_DIMS,
                                preferred_element_type=F32) + op_ref[b, :, sl]
            sg = lax.dot_general(s0b, g_ref[b, :, sl].astype(BF16), NT_DIMS, preferred_element_type=F32)
            s_scr[b, :, sl] = s0 * pl_ref[b, 0, :, sl] + sg + ht_ref[b, :, sl]
            mu = jnp.mean(o, axis=-1, keepdims=True)
            d = o - mu
            var = jnp.mean(d * d, axis=-1, keepdims=True)
            y_ref[b, :, sl] = (d * lax.rsqrt(var + GN_EPS) * gw_ref[:, sl] + gb_ref[:, sl]
                               + bonus_ref[b, :, sl])


def _scan(rp, op, g, ht, bonus, p_last, gn_w, gn_b):
    phys = lambda c: (c + META_CHUNK0) % N_CHUNKS
    blk = pl.BlockSpec((BATCH, CHUNK, D_MODEL), lambda c: (0, phys(c), 0))
    vec = pl.BlockSpec((1, D_MODEL), lambda c: (0, 0))
    return pl.pallas_call(
        _scan_kernel,
        grid=(N_CHUNKS,),
        in_specs=[blk] * 5 + [pl.BlockSpec((BATCH, 1, 1, D_MODEL), lambda c: (0, phys(c), 0, 0)), vec, vec],
        out_specs=blk,
        out_shape=jax.ShapeDtypeStruct((BATCH, TP, D_MODEL), F32),
        scratch_shapes=[pltpu.VMEM((BATCH, RWKV_HEAD, D_MODEL), F32)],
        compiler_params=_params("arbitrary"),
    )(rp, op, g, ht, bonus, p_last, gn_w, gn_b)


def _rwkv_out_kernel(y_ref, g_ref, h_ref, wo_ref, gain_ref, wr_ref, h_out, u_out, gates_out):
    h = h_ref[...] + _dot((y_ref[...] * g_ref[...]).astype(BF16), wo_ref[...])
    h_out[...] = h
    u = _rms(h, gain_ref[...])
    u_out[...] = u.astype(BF16)
    logits = jnp.dot(u, wr_ref[...], precision=lax.Precision.HIGHEST, preferred_element_type=F32)
    e = jnp.exp(logits - jnp.max(logits, axis=-1, keepdims=True))
    probs = e / jnp.sum(e, axis=-1, keepdims=True)
    idx = lax.broadcasted_iota(jnp.int32, probs.shape, 1).astype(F32)
    m1 = jnp.max(probs, axis=-1, keepdims=True)
    i1 = jnp.min(jnp.where(probs == m1, idx, float(N_EXPERTS)), axis=-1, keepdims=True)
    sel1 = idx == i1
    rest = jnp.where(sel1, -1.0, probs)
    m2 = jnp.max(rest, axis=-1, keepdims=True)
    i2 = jnp.min(jnp.where(rest == m2, idx, float(N_EXPERTS)), axis=-1, keepdims=True)
    sel2 = idx == i2
    gates_out[...] = (jnp.where(sel1, m1, 0.0) + jnp.where(sel2, m2, 0.0)) / (m1 + m2)


def _rwkv_out(y, g, h, w_o, gain, w_router):
    row = pl.BlockSpec((ROW_TILE, D_MODEL), lambda i: (i, 0))
    return pl.pallas_call(
        _rwkv_out_kernel,
        grid=(N_ROWS // ROW_TILE,),
        in_specs=[row, row, row,
                  pl.BlockSpec((D_MODEL, D_MODEL), lambda i: (0, 0)),
                  pl.BlockSpec((1, D_MODEL), lambda i: (0, 0)),
                  pl.BlockSpec((D_MODEL, N_EXPERTS), lambda i: (0, 0))],
        out_specs=[row, row, pl.BlockSpec((ROW_TILE, N_EXPERTS), lambda i: (i, 0))],
        out_shape=[jax.ShapeDtypeStruct((N_ROWS, D_MODEL), F32),
                   jax.ShapeDtypeStruct((N_ROWS, D_MODEL), BF16),
                   jax.ShapeDtypeStruct((N_ROWS, N_EXPERTS), F32)],
        compiler_params=_params("parallel"),
    )(y, g, h, w_o, gain, w_router)


def _moe_kernel(u_ref, gates_ref, h_ref, wg_ref, wu_ref, wd_ref, o_ref, acc):
    e = pl.program_id(1)
    f = pl.program_id(2)

    @pl.when(jnp.logical_and(e == 0, f == 0))
    def _():
        acc[...] = jnp.zeros_like(acc)

    gates = gates_ref[...]
    idx = lax.broadcasted_iota(jnp.int32, gates.shape, 1)
    gate = jnp.sum(jnp.where(idx == e, gates, 0.0), axis=-1, keepdims=True)
    u = u_ref[...]
    a = _dot(u, wg_ref[0])
    b = _dot(u, wu_ref[0])
    acc[...] += _dot((a * _sigmoid(a) * b * gate).astype(BF16), wd_ref[0])

    @pl.when(jnp.logical_and(e == pl.num_programs(1) - 1, f == pl.num_programs(2) - 1))
    def _():
        o_ref[...] = h_ref[...] + acc[...]


def _moe(u, gates, h, wg, wu, wd):
    row = lambda w: pl.BlockSpec((ROW_TILE, w), lambda i, e, f: (i, 0))
    return pl.pallas_call(
        _moe_kernel,
        grid=(N_ROWS // ROW_TILE, N_EXPERTS, D_FF // FF_TILE),
        in_specs=[row(D_MODEL), row(N_EXPERTS), row(D_MODEL),
                  pl.BlockSpec((1, D_MODEL, FF_TILE), lambda i, e, f: (e, 0, f)),
                  pl.BlockSpec((1, D_MODEL, FF_TILE), lambda i, e, f: (e, 0, f)),
                  pl.BlockSpec((1, FF_TILE, D_MODEL), lambda i, e, f: (e, f, 0))],
        out_specs=row(D_MODEL),
        out_shape=jax.ShapeDtypeStruct((N_ROWS, D_MODEL), F32),
        scratch_shapes=[pltpu.VMEM((ROW_TILE, D_MODEL), F32)],
        compiler_params=_params("parallel", "arbitrary", "arbitrary"),
    )(u, gates, h, wg, wu, wd)


def kernel(x, meta_tokens, mixer_norm, ffn_norm, attn_w_qkv, attn_q_norm, attn_k_norm, attn_sinks, attn_w_o, rwkv_mix, rwkv_w0, rwkv_w1, rwkv_w2, rwkv_a0, rwkv_a1, rwkv_a2, rwkv_g1, rwkv_g2, rwkv_k_k, rwkv_k_a, rwkv_r_k, rwkv_w_r, rwkv_w_k, rwkv_w_v, rwkv_w_o, rwkv_gn_w, rwkv_gn_b, ffn_w_gate, ffn_w_up, ffn_w_down, moe_router, moe_w_gate, moe_w_up, moe_w_down):
    bf = lambda a: a.astype(BF16)
    vec = lambda a: a.reshape(1, -1).astype(F32)
    meta = jnp.broadcast_to(meta_tokens.astype(F32)[None], (BATCH, N_META, D_MODEL))
    h = jnp.concatenate([x, jnp.zeros((BATCH, PAD, D_MODEL), F32), meta], axis=1)

    qkv = _qkv(h.reshape(N_ROWS, D_MODEL), vec(mixer_norm[0]), bf(attn_w_qkv[0]))
    h = _attention(h, qkv.reshape(BATCH, TP, QKV_DIM), attn_sinks[0].astype(F32),
                   vec(attn_q_norm[0]), vec(attn_k_norm[0]), bf(attn_w_o[0]))
    h = _ffn(h.reshape(N_ROWS, D_MODEL), vec(ffn_norm[0]), bf(ffn_w_gate[0]), bf(ffn_w_up[0]),
             bf(ffn_w_down[0]))

    r, ld, k, v, kk, a, g = _rwkv_proj(
        h, vec(mixer_norm[1]), rwkv_mix[0], vec(rwkv_w0[0]), vec(rwkv_a0[0]), vec(rwkv_k_k[0]),
        vec(rwkv_k_a[0]), bf(rwkv_w_r[0]), bf(rwkv_w_k[0]), bf(rwkv_w_v[0]), bf(rwkv_w1[0]),
        bf(rwkv_w2[0]), bf(rwkv_a1[0]), bf(rwkv_a2[0]), bf(rwkv_g1[0]), bf(rwkv_g2[0]))
    b3 = lambda t: t.reshape(BATCH, TP, D_MODEL)
    rp, op, gm, ht, bonus, p_last = _scan_prep(b3(r), b3(ld), b3(k), b3(v), b3(kk), b3(a),
                                               vec(rwkv_r_k[0]))
    y = _scan(rp, op, gm, ht, bonus, p_last.reshape(BATCH, N_CHUNKS, 1, D_MODEL),
              vec(rwkv_gn_w[0]), vec(rwkv_gn_b[0]))
    h, u, gates = _rwkv_out(y.reshape(N_ROWS, D_MODEL), g, h, bf(rwkv_w_o[0]), vec(ffn_norm[1]),
                            moe_router[0].astype(F32))
    h = _moe(u, gates, h, bf(moe_w_gate[0]), bf(moe_w_up[0]), bf(moe_w_down[0]))
    return h.reshape(BATCH, TP, D_MODEL)[:, :SEQ]
```

```python
import functools

import jax
import jax.numpy as jnp
from jax import lax
from jax.experimental import pallas as pl
from jax.experimental.pallas import tpu as pltpu

F32 = jnp.float32
BF16 = jnp.bfloat16

D_MODEL = 1024
BATCH = 2
SEQ = 8192
N_META = 16
BLOCK = 128
PAD = BLOCK - N_META
TP = SEQ + BLOCK
N_ROWS = BATCH * TP
N_BLOCKS = TP // BLOCK
META_BLOCK = N_BLOCKS - 1
HEAD_DIM = 64
N_Q_HEADS = 16
N_KV_HEADS = 4
Q_PER_KV = 4
QKV_DIM = (N_Q_HEADS + 2 * N_KV_HEADS) * HEAD_DIM
RWKV_HEADS = 16
RWKV_HEAD = 64
D_FF = 3584
N_EXPERTS = 8
NORM_EPS = 1e-5
GN_EPS = 64e-5
CHUNK = 64
N_CHUNKS = TP // CHUNK
META_CHUNK0 = SEQ // CHUNK
HEADS_PER_GROUP = 4
GW = HEADS_PER_GROUP * RWKV_HEAD
N_GROUPS = RWKV_HEADS // HEADS_PER_GROUP

ROW_TILE = 640
PROJ_TILE = 320
FF_TILE = 512
VMEM_LIMIT = 56 * 1024 * 1024
NEG = -1e30

NT_DIMS = (((1,), (1,)), ((), ()))
TN_DIMS = (((0,), (0,)), ((), ()))


def _params(*sem):
    return pltpu.CompilerParams(dimension_semantics=sem, vmem_limit_bytes=VMEM_LIMIT)


def _rms(x, gain):
    return x * lax.rsqrt(jnp.mean(x * x, axis=-1, keepdims=True) + NORM_EPS) * gain


def _sigmoid(x):
    return 1.0 / (1.0 + jnp.exp(-x))


def _dot(a, b):
    return jnp.dot(a, b, preferred_element_type=F32)


def _qkv_kernel(h_ref, g_ref, w_ref, o_ref):
    u = _rms(h_ref[...], g_ref[...])
    o_ref[...] = _dot(u.astype(BF16), w_ref[...]).astype(BF16)


def _qkv(h, gain, w):
    return pl.pallas_call(
        _qkv_kernel,
        grid=(N_ROWS // ROW_TILE,),
        in_specs=[pl.BlockSpec((ROW_TILE, D_MODEL), lambda i: (i, 0)),
                  pl.BlockSpec((1, D_MODEL), lambda i: (0, 0)),
                  pl.BlockSpec((D_MODEL, QKV_DIM), lambda i: (0, 0))],
        out_specs=pl.BlockSpec((ROW_TILE, QKV_DIM), lambda i: (i, 0)),
        out_shape=jax.ShapeDtypeStruct((N_ROWS, QKV_DIM), BF16),
        compiler_params=_params("parallel"),
    )(h, gain, w)


def _attn_kernel(sink_ref, h_ref, q_ref, kc_ref, vc_ref, kp_ref, vp_ref, km_ref, vm_ref,
                 qg_ref, kg_ref, wo_ref, o_ref, o_scr):
    j = pl.program_id(1)
    n_keys = N_META + 2 * BLOCK
    rowi = lax.broadcasted_iota(jnp.int32, (BLOCK, n_keys), 0)
    col = lax.broadcasted_iota(jnp.int32, (BLOCK, n_keys), 1)
    far = 4 * BLOCK
    is_real = j < META_BLOCK
    meta_off = jnp.where(is_real, -far, PAD)
    prev_off = jnp.where(jnp.logical_and(j >= 1, is_real), 0, far)
    cur_off = jnp.where(is_real, 0, far)
    meta_ok = jnp.logical_and(col < N_META, col + meta_off <= rowi)
    prev_ok = jnp.logical_and(jnp.logical_and(col >= N_META, col < N_META + BLOCK),
                              col - N_META > rowi + prev_off)
    cur_ok = jnp.logical_and(col >= N_META + BLOCK, col - (N_META + BLOCK) + cur_off <= rowi)
    ok = jnp.logical_or(jnp.logical_or(meta_ok, prev_ok), cur_ok)
    ok4 = jnp.concatenate([ok] * Q_PER_KV, axis=0)

    q_all = q_ref[0].astype(F32)
    kc, vc = kc_ref[0].astype(F32), vc_ref[0].astype(F32)
    kp, vp = kp_ref[0].astype(F32), vp_ref[0].astype(F32)
    km, vm = km_ref[0, PAD:, :].astype(F32), vm_ref[0, PAD:, :].astype(F32)
    qg, kg = qg_ref[...], kg_ref[...]
    scale = HEAD_DIM ** -0.5
    for kv in range(N_KV_HEADS):
        ks = slice(kv * HEAD_DIM, (kv + 1) * HEAD_DIM)
        kcat = jnp.concatenate([km[:, ks], kp[:, ks], kc[:, ks]], axis=0)
        vcat = jnp.concatenate([vm[:, ks], vp[:, ks], vc[:, ks]], axis=0).astype(BF16)
        kcat = _rms(kcat, kg).astype(BF16)
        heads = range(kv * Q_PER_KV, (kv + 1) * Q_PER_KV)
        q4 = jnp.concatenate([q_all[:, hd * HEAD_DIM:(hd + 1) * HEAD_DIM] for hd in heads], axis=0)
        q4 = (_rms(q4, qg) * scale).astype(BF16)
        s = lax.dot_general(q4, kcat, NT_DIMS, preferred_element_type=F32)
        s = jnp.where(ok4, s, NEG)
        sink = jnp.concatenate([jnp.full((BLOCK, 1), sink_ref[hd], F32) for hd in heads], axis=0)
        m = jnp.maximum(jnp.max(s, axis=-1, keepdims=True), sink)
        e = jnp.exp(s - m)
        den = jnp.sum(e, axis=-1, keepdims=True) + jnp.exp(sink - m)
        o4 = _dot(e.astype(BF16), vcat) / den
        for g, hd in enumerate(heads):
            o_scr[:, hd * HEAD_DIM:(hd + 1) * HEAD_DIM] = o4[g * BLOCK:(g + 1) * BLOCK, :]
    o_ref[0] = h_ref[0] + _dot(o_scr[...].astype(BF16), wo_ref[...])


def _attention(h, qkv, sinks, q_gain, k_gain, w_o):
    kcol, vcol = N_Q_HEADS * HEAD_DIM // 256, N_Q_HEADS * HEAD_DIM // 256 + 1
    kvw = N_KV_HEADS * HEAD_DIM
    prev = lambda j: jnp.maximum(j - 1, 0)
    return pl.pallas_call(
        _attn_kernel,
        grid=(BATCH, N_BLOCKS),
        in_specs=[pl.BlockSpec(memory_space=pltpu.SMEM),
                  pl.BlockSpec((1, BLOCK, D_MODEL), lambda b, j: (b, j, 0)),
                  pl.BlockSpec((1, BLOCK, D_MODEL), lambda b, j: (b, j, 0)),
                  pl.BlockSpec((1, BLOCK, kvw), lambda b, j: (b, j, kcol)),
                  pl.BlockSpec((1, BLOCK, kvw), lambda b, j: (b, j, vcol)),
                  pl.BlockSpec((1, BLOCK, kvw), lambda b, j: (b, prev(j), kcol)),
                  pl.BlockSpec((1, BLOCK, kvw), lambda b, j: (b, prev(j), vcol)),
                  pl.BlockSpec((1, BLOCK, kvw), lambda b, j: (b, META_BLOCK, kcol)),
                  pl.BlockSpec((1, BLOCK, kvw), lambda b, j: (b, META_BLOCK, vcol)),
                  pl.BlockSpec((1, HEAD_DIM), lambda b, j: (0, 0)),
                  pl.BlockSpec((1, HEAD_DIM), lambda b, j: (0, 0)),
                  pl.BlockSpec((D_MODEL, D_MODEL), lambda b, j: (0, 0))],
        out_specs=pl.BlockSpec((1, BLOCK, D_MODEL), lambda b, j: (b, j, 0)),
        out_shape=jax.ShapeDtypeStruct((BATCH, TP, D_MODEL), F32),
        scratch_shapes=[pltpu.VMEM((BLOCK, D_MODEL), F32)],
        compiler_params=_params("parallel", "parallel"),
    )(sinks, h, qkv, qkv, qkv, qkv, qkv, qkv, qkv, q_gain, k_gain, w_o)


def _ffn_kernel(h_ref, g_ref, wg_ref, wu_ref, wd_ref, o_ref, u_scr, acc):
    f = pl.program_id(1)

    @pl.when(f == 0)
    def _():
        u_scr[...] = _rms(h_ref[...], g_ref[...]).astype(BF16)
        acc[...] = jnp.zeros_like(acc)

    u = u_scr[...]
    a = _dot(u, wg_ref[...])
    b = _dot(u, wu_ref[...])
    acc[...] += _dot((a * _sigmoid(a) * b).astype(BF16), wd_ref[...])

    @pl.when(f == pl.num_programs(1) - 1)
    def _():
        o_ref[...] = h_ref[...] + acc[...]


def _ffn(h, gain, wg, wu, wd):
    return pl.pallas_call(
        _ffn_kernel,
        grid=(N_ROWS // ROW_TILE, D_FF // FF_TILE),
        in_specs=[pl.BlockSpec((ROW_TILE, D_MODEL), lambda i, f: (i, 0)),
                  pl.BlockSpec((1, D_MODEL), lambda i, f: (0, 0)),
                  pl.BlockSpec((D_MODEL, FF_TILE), lambda i, f: (0, f)),
                  pl.BlockSpec((D_MODEL, FF_TILE), lambda i, f: (0, f)),
                  pl.BlockSpec((FF_TILE, D_MODEL), lambda i, f: (f, 0))],
        out_specs=pl.BlockSpec((ROW_TILE, D_MODEL), lambda i, f: (i, 0)),
        out_shape=jax.ShapeDtypeStruct((N_ROWS, D_MODEL), F32),
        scratch_shapes=[pltpu.VMEM((ROW_TILE, D_MODEL), BF16), pltpu.VMEM((ROW_TILE, D_MODEL), F32)],
        compiler_params=_params("parallel", "arbitrary"),
    )(h, gain, wg, wu, wd)


def _rwkv_proj_kernel(h_ref, hp_ref, g_ref, mix_ref, w0_ref, a0_ref, kk_ref, ka_ref,
                      wr_ref, wk_ref, wv_ref, w1_ref, w2_ref, a1_ref, a2_ref, g1_ref, g2_ref,
                      r_out, ld_out, k_out, v_out, kk_out, a_out, g_out):
    i = pl.program_id(0)
    tiles_per_batch = TP // PROJ_TILE
    r0 = (i % tiles_per_batch) * PROJ_TILE
    local = lax.broadcasted_iota(jnp.int32, (PROJ_TILE, 1), 0)
    lrow = local + r0
    gain = g_ref[...]
    is_pad = jnp.logical_and(lrow >= SEQ, lrow < SEQ + PAD)
    u = jnp.where(is_pad, 0.0, _rms(h_ref[...], gain))
    u_prev_tile = _rms(hp_ref[7:8, :], gain)
    xprev = pltpu.roll(u, 1, 0)
    xprev = jnp.where(local == 0, u_prev_tile, xprev)
    xprev = jnp.where(lrow == SEQ, 0.0, xprev)
    xx = xprev - u
    mix = mix_ref[...]
    lerp = lambda n: (u + xx * mix[n:n + 1, :]).astype(BF16)
    xr, xw, xk, xv, xa, xg = [lerp(n) for n in range(6)]
    r = _dot(xr, wr_ref[...])
    k = _dot(xk, wk_ref[...])
    v = _dot(xv, wv_ref[...])
    lw = _dot(jnp.tanh(_dot(xw, w1_ref[...])).astype(BF16), w2_ref[...])
    z = -(w0_ref[...] + lw)
    softplus = jnp.maximum(z, 0.0) + jnp.log(1.0 + jnp.exp(-jnp.abs(z)))
    w = -softplus - 0.5
    a = _sigmoid(a0_ref[...] + _dot(_dot(xa, a1_ref[...]).astype(BF16), a2_ref[...]))
    g = _dot(_sigmoid(_dot(xg, g1_ref[...])).astype(BF16), g2_ref[...])
    r_out[...] = r
    ld_out[...] = -jnp.exp(w)
    kk_out[...] = k * kk_ref[...]
    k_out[...] = k * (1.0 + (a - 1.0) * ka_ref[...])
    v_out[...] = v
    a_out[...] = a
    g_out[...] = g


def _rwkv_proj(h, gain, mix, w0, a0, k_k, k_a, w_r, w_k, w_v, w1, w2, a1, a2, g1, g2):
    tiles_per_batch = TP // PROJ_TILE
    rows8 = PROJ_TILE // 8

    def prev_map(i):
        b = i // tiles_per_batch
        first = (i % tiles_per_batch) == 0
        return (jnp.where(first, (b * TP + TP - 8) // 8, i * rows8 - 1), 0)

    row = pl.BlockSpec((PROJ_TILE, D_MODEL), lambda i: (i, 0))
    full = lambda a: pl.BlockSpec(a.shape, lambda i: (0,) * a.ndim)
    smalls = (gain, mix, w0, a0, k_k, k_a, w_r, w_k, w_v, w1, w2, a1, a2, g1, g2)
    return pl.pallas_call(
        _rwkv_proj_kernel,
        grid=(N_ROWS // PROJ_TILE,),
        in_specs=[row, pl.BlockSpec((8, D_MODEL), prev_map)] + [full(a) for a in smalls],
        out_specs=[row] * 7,
        out_shape=[jax.ShapeDtypeStruct((N_ROWS, D_MODEL), F32)] * 7,
        compiler_params=_params("parallel"),
    )(h, h, *smalls)


def _group_masks():
    ri = lax.broadcasted_iota(jnp.int32, (GW, GW), 0) // RWKV_HEAD
    ci = lax.broadcasted_iota(jnp.int32, (GW, GW), 1) // RWKV_HEAD
    return jnp.where(ri == ci, 1.0, 0.0).astype(F32)


def _bd(x, mask):
    return (jnp.concatenate([x] * HEADS_PER_GROUP, axis=0) * mask).astype(BF16)


def _diag_blocks(full, mask):
    m = full * mask
    n = RWKV_HEAD
    return (m[0:n] + m[n:2 * n]) + (m[2 * n:3 * n] + m[3 * n:4 * n])


def _head_sum(x, mask_bf):
    hi = x.astype(BF16)
    lo = (x - hi.astype(F32)).astype(BF16)
    return _dot(hi, mask_bf) + _dot(lo, mask_bf)


def _scan_prep_kernel(r_ref, ld_ref, k_ref, v_ref, kk_ref, a_ref, rk_ref,
                      rp_out, op_out, g_out, ht_out, bonus_out, pl_out):
    L = CHUNK
    groups = range(N_GROUPS)
    gsl = [slice(g * GW, (g + 1) * GW) for g in groups]
    mask = _group_masks()
    mask_bf = mask.astype(BF16)
    ri = lax.broadcasted_iota(jnp.int32, (L, GW), 0)
    ci = lax.broadcasted_iota(jnp.int32, (L, GW), 1) % RWKV_HEAD
    incl = ci <= ri
    strict = ci < ri
    eye = jnp.where(ci == ri, 1.0, 0.0).astype(F32)
    t_r = lax.broadcasted_iota(jnp.int32, (L, L), 0)
    t_c = lax.broadcasted_iota(jnp.int32, (L, L), 1)
    ld = ld_ref[0]
    cs = jnp.dot(jnp.where(t_c <= t_r, 1.0, 0.0).astype(F32), ld, precision=lax.Precision.HIGHEST,
                 preferred_element_type=F32)
    p_all = jnp.exp(cs)
    pprev_all = jnp.exp(cs - ld)
    pinv_all = jnp.exp(-cs)
    pl_all = p_all[L - 1:L, :]
    pl_out[0] = pl_all
    r_all, k_all, v_all, kk_all, a_all = r_ref[0], k_ref[0], v_ref[0], kk_ref[0], a_ref[0]
    rk_all = rk_ref[...]
    kk2 = kk_all * kk_all
    rkk = r_all * k_all * rk_all
    rt_all = r_all * p_all
    kt_all = k_all * pinv_all

    at, rt, bt, kt, v = [], [], [], [], []
    for g in groups:
        sl = gsl[g]
        nrm = jnp.sqrt(_head_sum(kk2[:, sl], mask_bf))
        kk = kk_all[:, sl] / jnp.maximum(nrm, 1e-12)
        at.append(-kk * pprev_all[:, sl])
        bt.append(kk * a_all[:, sl] * pinv_all[:, sl])
        rt.append(rt_all[:, sl])
        kt.append(kt_all[:, sl])
        v.append(v_all[:, sl])
        bonus_out[0, :, sl] = _head_sum(rkk[:, sl], mask_bf) * v_all[:, sl]

    a_ab, a_ak, a_rb, a_rk = [], [], [], []
    for g in groups:
        lhs = jnp.concatenate([at[g], rt[g]], axis=0).astype(BF16)
        rhs = jnp.concatenate([_bd(bt[g], mask), _bd(kt[g], mask)], axis=0)
        big = lax.dot_general(lhs, rhs, NT_DIMS, preferred_element_type=F32)
        a_ab.append(jnp.where(strict, big[:L, :GW], 0.0))
        a_ak.append(jnp.where(strict, big[:L, GW:], 0.0))
        a_rb.append(jnp.where(incl, big[L:, :GW], 0.0))
        a_rk.append(jnp.where(incl, big[L:, GW:], 0.0))

    x = [_dot(a_ab[g].astype(BF16), _bd(a_ab[g], mask)) for g in groups]
    inv = [eye + a_ab[g] for g in groups]
    for step in range(5):
        for g in groups:
            rhs = _bd(x[g], mask)
            if step < 4:
                res = _dot(jnp.concatenate([x[g], inv[g]], axis=0).astype(BF16), rhs)
                x[g] = res[:L]
                inv[g] = inv[g] + res[L:]
            else:
                inv[g] = inv[g] + _dot(inv[g].astype(BF16), rhs)

    av = [_dot(jnp.concatenate([a_ak[g], a_rk[g]], axis=0).astype(BF16), _bd(v[g], mask)) for g in groups]
    wu = [_dot(inv[g].astype(BF16), jnp.concatenate([_bd(at[g], mask), _bd(av[g][:L], mask)], axis=1))
          for g in groups]
    aw = [_dot(a_rb[g].astype(BF16),
               jnp.concatenate([_bd(wu[g][:, :GW], mask), _bd(wu[g][:, GW:], mask)], axis=1))
          for g in groups]
    for g in groups:
        sl = gsl[g]
        rp_out[0, :, sl] = rt[g] + aw[g][:, :GW]
        op_out[0, :, sl] = av[g][L:] + aw[g][:, GW:]
        plg = pl_all[:, sl]
        bh = (bt[g] * plg).astype(BF16)
        kh = (kt[g] * plg).astype(BF16)
        w_b, u0_b = wu[g][:, :GW].astype(BF16), wu[g][:, GW:].astype(BF16)
        gfull = lax.dot_general(bh, w_b, TN_DIMS, preferred_element_type=F32)
        g_out[0, :, sl] = _diag_blocks(gfull, mask)
        hfull = lax.dot_general(jnp.concatenate([u0_b, v[g].astype(BF16)], axis=0),
                                jnp.concatenate([bh, kh], axis=0), TN_DIMS, preferred_element_type=F32)
        ht_out[0, :, sl] = _diag_blocks(hfull, mask)


def _scan_prep(r, ld, k, v, kk, a, r_k):
    blk = pl.BlockSpec((1, CHUNK, D_MODEL), lambda b, c: (b, c, 0))
    big = jax.ShapeDtypeStruct((BATCH, TP, D_MODEL), F32)
    return pl.pallas_call(
        _scan_prep_kernel,
        grid=(BATCH, N_CHUNKS),
        in_specs=[blk] * 6 + [pl.BlockSpec((1, D_MODEL), lambda b, c: (0, 0))],
        out_specs=[blk] * 5 + [pl.BlockSpec((1, 1, D_MODEL), lambda b, c: (b * N_CHUNKS + c, 0, 0))],
        out_shape=[big] * 5 + [jax.ShapeDtypeStruct((BATCH * N_CHUNKS, 1, D_MODEL), F32)],
        compiler_params=_params("parallel", "parallel"),
    )(r, ld, k, v, kk, a, r_k)


def _scan_kernel(rp_ref, op_ref, g_ref, ht_ref, bonus_ref, pl_ref, gw_ref, gb_ref, y_ref, s_scr):
    c = pl.program_id(0)

    @pl.when(c == 0)
    def _():
        s_scr[...] = jnp.zeros_like(s_scr)

    mask = _group_masks()
    mask_bf = mask.astype(BF16)
    units = [(b, slice(g * GW, (g + 1) * GW)) for b in range(BATCH) for g in range(N_GROUPS)]
    s0 = [s_scr[b, :, sl] for b, sl in units]
    o = [lax.dot_general(rp_ref[b, :, sl].astype(BF16), _bd(s0[n], mask), NT_DIMS,
                         preferred_element_type=F32) + op_ref[b, :, sl]
         for n, (b, sl) in enumerate(units)]
    sg = [lax.dot_general(s0[n].astype(BF16), _bd(g_ref[b, :, sl], mask), NT_DIMS,
                          preferred_element_type=F32)
          for n, (b, sl) in enumerate(units)]
    for n, (b, sl) in enumerate(units):
        s_scr[b, :, sl] = s0[n] * pl_ref[b, 0, :, sl] + sg[n] + ht_ref[b, :, sl]
    inv_n = 1.0 / RWKV_HEAD
    mu = [_head_sum(o[n], mask_bf) * inv_n for n in range(len(units))]
    d = [o[n] - mu[n] for n in range(len(units))]
    var = [_head_sum(d[n] * d[n], mask_bf) * inv_n for n in range(len(units))]
    for n, (b, sl) in enumerate(units):
        y_ref[b, :, sl] = (d[n] * lax.rsqrt(var[n] + GN_EPS) * gw_ref[:, sl] + gb_ref[:, sl]
                           + bonus_ref[b, :, sl])


def _scan(rp, op, g, ht, bonus, p_last, gn_w, gn_b):
    phys = lambda c: (c + META_CHUNK0) % N_CHUNKS
    blk = pl.BlockSpec((BATCH, CHUNK, D_MODEL), lambda c: (0, phys(c), 0))
    vec = pl.BlockSpec((1, D_MODEL), lambda c: (0, 0))
    return pl.pallas_call(
        _scan_kernel,
        grid=(N_CHUNKS,),
        in_specs=[blk] * 5 + [pl.BlockSpec((BATCH, 1, 1, D_MODEL), lambda c: (0, phys(c), 0, 0)), vec, vec],
        out_specs=blk,
        out_shape=jax.ShapeDtypeStruct((BATCH, TP, D_MODEL), F32),
        scratch_shapes=[pltpu.VMEM((BATCH, RWKV_HEAD, D_MODEL), F32)],
        compiler_params=_params("arbitrary"),
    )(rp, op, g, ht, bonus, p_last, gn_w, gn_b)


def _rwkv_out_kernel(y_ref, g_ref, h_ref, wo_ref, gain_ref, wr_ref, h_out, u_out, gates_out):
    h = h_ref[...] + _dot((y_ref[...] * g_ref[...]).astype(BF16), wo_ref[...])
    h_out[...] = h
    u = _rms(h, gain_ref[...])
    u_out[...] = u.astype(BF16)
    logits = jnp.dot(u, wr_ref[...], precision=lax.Precision.HIGHEST, preferred_element_type=F32)
    e = jnp.exp(logits - jnp.max(logits, axis=-1, keepdims=True))
    probs = e / jnp.sum(e, axis=-1, keepdims=True)
    idx = lax.broadcasted_iota(jnp.int32, probs.shape, 1).astype(F32)
    m1 = jnp.max(probs, axis=-1, keepdims=True)
    i1 = jnp.min(jnp.where(probs == m1, idx, float(N_EXPERTS)), axis=-1, keepdims=True)
    sel1 = idx == i1
    rest = jnp.where(sel1, -1.0, probs)
    m2 = jnp.max(rest, axis=-1, keepdims=True)
    i2 = jnp.min(jnp.where(rest == m2, idx, float(N_EXPERTS)), axis=-1, keepdims=True)
    sel2 = idx == i2
    gates_out[...] = (jnp.where(sel1, m1, 0.0) + jnp.where(sel2, m2, 0.0)) / (m1 + m2)


def _rwkv_out(y, g, h, w_o, gain, w_router):
    row = pl.BlockSpec((ROW_TILE, D_MODEL), lambda i: (i, 0))
    return pl.pallas_call(
        _rwkv_out_kernel,
        grid=(N_ROWS // ROW_TILE,),
        in_specs=[row, row, row,
                  pl.BlockSpec((D_MODEL, D_MODEL), lambda i: (0, 0)),
                  pl.BlockSpec((1, D_MODEL), lambda i: (0, 0)),
                  pl.BlockSpec((D_MODEL, N_EXPERTS), lambda i: (0, 0))],
        out_specs=[row, row, pl.BlockSpec((ROW_TILE, N_EXPERTS), lambda i: (i, 0))],
        out_shape=[jax.ShapeDtypeStruct((N_ROWS, D_MODEL), F32),
                   jax.ShapeDtypeStruct((N_ROWS, D_MODEL), BF16),
                   jax.ShapeDtypeStruct((N_ROWS, N_EXPERTS), F32)],
        compiler_params=_params("parallel"),
    )(y, g, h, w_o, gain, w_router)


def _moe_kernel(u_ref, gates_ref, h_ref, wg_ref, wu_ref, wd_ref, o_ref, acc):
    e = pl.program_id(1)
    f = pl.program_id(2)

    @pl.when(jnp.logical_and(e == 0, f == 0))
    def _():
        acc[...] = jnp.zeros_like(acc)

    gates = gates_ref[...]
    idx = lax.broadcasted_iota(jnp.int32, gates.shape, 1)
    gate = jnp.sum(jnp.where(idx == e, gates, 0.0), axis=-1, keepdims=True)
    u = u_ref[...]
    a = _dot(u, wg_ref[0])
    b = _dot(u, wu_ref[0])
    acc[...] += _dot((a * _sigmoid(a) * b * gate).astype(BF16), wd_ref[0])

    @pl.when(jnp.logical_and(e == pl.num_programs(1) - 1, f == pl.num_programs(2) - 1))
    def _():
        o_ref[...] = h_ref[...] + acc[...]


def _moe(u, gates, h, wg, wu, wd):
    row = lambda w: pl.BlockSpec((ROW_TILE, w), lambda i, e, f: (i, 0))
    return pl.pallas_call(
        _moe_kernel,
        grid=(N_ROWS // ROW_TILE, N_EXPERTS, D_FF // FF_TILE),
        in_specs=[row(D_MODEL), row(N_EXPERTS), row(D_MODEL),
                  pl.BlockSpec((1, D_MODEL, FF_TILE), lambda i, e, f: (e, 0, f)),
                  pl.BlockSpec((1, D_MODEL, FF_TILE), lambda i, e, f: (e, 0, f)),
                  pl.BlockSpec((1, FF_TILE, D_MODEL), lambda i, e, f: (e, f, 0))],
        out_specs=row(D_MODEL),
        out_shape=jax.ShapeDtypeStruct((N_ROWS, D_MODEL), F32),
        scratch_shapes=[pltpu.VMEM((ROW_TILE, D_MODEL), F32)],
        compiler_params=_params("parallel", "arbitrary", "arbitrary"),
    )(u, gates, h, wg, wu, wd)


def kernel(x, meta_tokens, mixer_norm, ffn_norm, attn_w_qkv, attn_q_norm, attn_k_norm, attn_sinks, attn_w_o, rwkv_mix, rwkv_w0, rwkv_w1, rwkv_w2, rwkv_a0, rwkv_a1, rwkv_a2, rwkv_g1, rwkv_g2, rwkv_k_k, rwkv_k_a, rwkv_r_k, rwkv_w_r, rwkv_w_k, rwkv_w_v, rwkv_w_o, rwkv_gn_w, rwkv_gn_b, ffn_w_gate, ffn_w_up, ffn_w_down, moe_router, moe_w_gate, moe_w_up, moe_w_down):
    bf = lambda a: a.astype(BF16)
    vec = lambda a: a.reshape(1, -1).astype(F32)
    meta = jnp.broadcast_to(meta_tokens.astype(F32)[None], (BATCH, N_META, D_MODEL))
    h = jnp.concatenate([x, jnp.zeros((BATCH, PAD, D_MODEL), F32), meta], axis=1)

    qkv = _qkv(h.reshape(N_ROWS, D_MODEL), vec(mixer_norm[0]), bf(attn_w_qkv[0]))
    h = _attention(h, qkv.reshape(BATCH, TP, QKV_DIM), attn_sinks[0].astype(F32),
                   vec(attn_q_norm[0]), vec(attn_k_norm[0]), bf(attn_w_o[0]))
    h = _ffn(h.reshape(N_ROWS, D_MODEL), vec(ffn_norm[0]), bf(ffn_w_gate[0]), bf(ffn_w_up[0]),
             bf(ffn_w_down[0]))

    r, ld, k, v, kk, a, g = _rwkv_proj(
        h, vec(mixer_norm[1]), rwkv_mix[0], vec(rwkv_w0[0]), vec(rwkv_a0[0]), vec(rwkv_k_k[0]),
        vec(rwkv_k_a[0]), bf(rwkv_w_r[0]), bf(rwkv_w_k[0]), bf(rwkv_w_v[0]), bf(rwkv_w1[0]),
        bf(rwkv_w2[0]), bf(rwkv_a1[0]), bf(rwkv_a2[0]), bf(rwkv_g1[0]), bf(rwkv_g2[0]))
    b3 = lambda t: t.reshape(BATCH, TP, D_MODEL)
    rp, op, gm, ht, bonus, p_last = _scan_prep(b3(r), b3(ld), b3(k), b3(v), b3(kk), b3(a),
                                               vec(rwkv_r_k[0]))
    y = _scan(rp, op, gm, ht, bonus, p_last.reshape(BATCH, N_CHUNKS, 1, D_MODEL),
              vec(rwkv_gn_w[0]), vec(rwkv_gn_b[0]))
    h, u, gates = _rwkv_out(y.reshape(N_ROWS, D_MODEL), g, h, bf(rwkv_w_o[0]), vec(ffn_norm[1]),
                            moe_router[0].astype(F32))
    h = _moe(u, gates, h, bf(moe_w_gate[0]), bf(moe_w_up[0]), bf(moe_w_down[0]))
    return h.reshape(BATCH, TP, D_MODEL)[:, :SEQ]
```

```python
import functools

import jax
import jax.numpy as jnp
from jax import lax
from jax.experimental import pallas as pl
from jax.experimental.pallas import tpu as pltpu

F32 = jnp.float32
BF16 = jnp.bfloat16

D_MODEL = 1024
BATCH = 2
SEQ = 8192
N_META = 16
BLOCK = 128
PAD = BLOCK - N_META
TP = SEQ + BLOCK
N_ROWS = BATCH * TP
N_BLOCKS = TP // BLOCK
META_BLOCK = N_BLOCKS - 1
HEAD_DIM = 64
N_Q_HEADS = 16
N_KV_HEADS = 4
Q_PER_KV = 4
QKV_DIM = (N_Q_HEADS + 2 * N_KV_HEADS) * HEAD_DIM
RWKV_HEADS = 16
RWKV_HEAD = 64
D_FF = 3584
N_EXPERTS = 8
NORM_EPS = 1e-5
GN_EPS = 64e-5
CHUNK = 64
N_CHUNKS = TP // CHUNK
META_CHUNK0 = SEQ // CHUNK
HEADS_PER_GROUP = 4
GW = HEADS_PER_GROUP * RWKV_HEAD
N_GROUPS = RWKV_HEADS // HEADS_PER_GROUP

ROW_TILE = 640
PROJ_TILE = 320
FF_TILE = 512
N_TOKENS = BATCH * SEQ
TOP_K = 2
TOKEN_BLOCK = 512
N_TOKEN_BLOCKS = N_TOKENS // TOKEN_BLOCK
EXPERT_TILE = 1024
SUB_TILE = 256
MAX_EXPERT_TILES = TOP_K * N_TOKENS // EXPERT_TILE + N_EXPERTS
MAX_SORTED_ROWS = MAX_EXPERT_TILES * EXPERT_TILE
MAX_SUB_TILES = MAX_SORTED_ROWS // SUB_TILE
MAX_PAIRS = MAX_SUB_TILES + N_EXPERTS * N_TOKEN_BLOCKS
VMEM_LIMIT = 56 * 1024 * 1024
NEG = -1e30

NT_DIMS = (((1,), (1,)), ((), ()))
TN_DIMS = (((0,), (0,)), ((), ()))


def _params(*sem):
    return pltpu.CompilerParams(dimension_semantics=sem, vmem_limit_bytes=VMEM_LIMIT)


def _rms(x, gain):
    return x * lax.rsqrt(jnp.mean(x * x, axis=-1, keepdims=True) + NORM_EPS) * gain


def _sigmoid(x):
    return 1.0 / (1.0 + jnp.exp(-x))


def _dot(a, b):
    return jnp.dot(a, b, preferred_element_type=F32)


def _qkv_kernel(h_ref, g_ref, w_ref, o_ref):
    u = _rms(h_ref[...], g_ref[...])
    o_ref[...] = _dot(u.astype(BF16), w_ref[...]).astype(BF16)


def _qkv(h, gain, w):
    return pl.pallas_call(
        _qkv_kernel,
        grid=(N_ROWS // ROW_TILE,),
        in_specs=[pl.BlockSpec((ROW_TILE, D_MODEL), lambda i: (i, 0)),
                  pl.BlockSpec((1, D_MODEL), lambda i: (0, 0)),
                  pl.BlockSpec((D_MODEL, QKV_DIM), lambda i: (0, 0))],
        out_specs=pl.BlockSpec((ROW_TILE, QKV_DIM), lambda i: (i, 0)),
        out_shape=jax.ShapeDtypeStruct((N_ROWS, QKV_DIM), BF16),
        compiler_params=_params("parallel"),
    )(h, gain, w)


def _attn_kernel(sink_ref, h_ref, q_ref, kc_ref, vc_ref, kp_ref, vp_ref, km_ref, vm_ref,
                 qg_ref, kg_ref, wo_ref, o_ref, o_scr):
    j = pl.program_id(1)
    n_keys = N_META + 2 * BLOCK
    rowi = lax.broadcasted_iota(jnp.int32, (BLOCK, n_keys), 0)
    col = lax.broadcasted_iota(jnp.int32, (BLOCK, n_keys), 1)
    far = 4 * BLOCK
    is_real = j < META_BLOCK
    meta_off = jnp.where(is_real, -far, PAD)
    prev_off = jnp.where(jnp.logical_and(j >= 1, is_real), 0, far)
    cur_off = jnp.where(is_real, 0, far)
    meta_ok = jnp.logical_and(col < N_META, col + meta_off <= rowi)
    prev_ok = jnp.logical_and(jnp.logical_and(col >= N_META, col < N_META + BLOCK),
                              col - N_META > rowi + prev_off)
    cur_ok = jnp.logical_and(col >= N_META + BLOCK, col - (N_META + BLOCK) + cur_off <= rowi)
    ok = jnp.logical_or(jnp.logical_or(meta_ok, prev_ok), cur_ok)
    ok4 = jnp.concatenate([ok] * Q_PER_KV, axis=0)

    q_all = q_ref[0].astype(F32)
    kc, vc = kc_ref[0].astype(F32), vc_ref[0].astype(F32)
    kp, vp = kp_ref[0].astype(F32), vp_ref[0].astype(F32)
    km, vm = km_ref[0, PAD:, :].astype(F32), vm_ref[0, PAD:, :].astype(F32)
    qg, kg = qg_ref[...], kg_ref[...]
    scale = HEAD_DIM ** -0.5
    for kv in range(N_KV_HEADS):
        ks = slice(kv * HEAD_DIM, (kv + 1) * HEAD_DIM)
        kcat = jnp.concatenate([km[:, ks], kp[:, ks], kc[:, ks]], axis=0)
        vcat = jnp.concatenate([vm[:, ks], vp[:, ks], vc[:, ks]], axis=0).astype(BF16)
        kcat = _rms(kcat, kg).astype(BF16)
        heads = range(kv * Q_PER_KV, (kv + 1) * Q_PER_KV)
        q4 = jnp.concatenate([q_all[:, hd * HEAD_DIM:(hd + 1) * HEAD_DIM] for hd in heads], axis=0)
        q4 = (_rms(q4, qg) * scale).astype(BF16)
        s = lax.dot_general(q4, kcat, NT_DIMS, preferred_element_type=F32)
        s = jnp.where(ok4, s, NEG)
        sink = jnp.concatenate([jnp.full((BLOCK, 1), sink_ref[hd], F32) for hd in heads], axis=0)
        m = jnp.maximum(jnp.max(s, axis=-1, keepdims=True), sink)
        e = jnp.exp(s - m)
        den = jnp.sum(e, axis=-1, keepdims=True) + jnp.exp(sink - m)
        o4 = _dot(e.astype(BF16), vcat) / den
        for g, hd in enumerate(heads):
            o_scr[:, hd * HEAD_DIM:(hd + 1) * HEAD_DIM] = o4[g * BLOCK:(g + 1) * BLOCK, :]
    o_ref[0] = h_ref[0] + _dot(o_scr[...].astype(BF16), wo_ref[...])


def _attention(h, qkv, sinks, q_gain, k_gain, w_o):
    kcol, vcol = N_Q_HEADS * HEAD_DIM // 256, N_Q_HEADS * HEAD_DIM // 256 + 1
    kvw = N_KV_HEADS * HEAD_DIM
    prev = lambda j: jnp.maximum(j - 1, 0)
    return pl.pallas_call(
        _attn_kernel,
        grid=(BATCH, N_BLOCKS),
        in_specs=[pl.BlockSpec(memory_space=pltpu.SMEM),
                  pl.BlockSpec((1, BLOCK, D_MODEL), lambda b, j: (b, j, 0)),
                  pl.BlockSpec((1, BLOCK, D_MODEL), lambda b, j: (b, j, 0)),
                  pl.BlockSpec((1, BLOCK, kvw), lambda b, j: (b, j, kcol)),
                  pl.BlockSpec((1, BLOCK, kvw), lambda b, j: (b, j, vcol)),
                  pl.BlockSpec((1, BLOCK, kvw), lambda b, j: (b, prev(j), kcol)),
                  pl.BlockSpec((1, BLOCK, kvw), lambda b, j: (b, prev(j), vcol)),
                  pl.BlockSpec((1, BLOCK, kvw), lambda b, j: (b, META_BLOCK, kcol)),
                  pl.BlockSpec((1, BLOCK, kvw), lambda b, j: (b, META_BLOCK, vcol)),
                  pl.BlockSpec((1, HEAD_DIM), lambda b, j: (0, 0)),
                  pl.BlockSpec((1, HEAD_DIM), lambda b, j: (0, 0)),
                  pl.BlockSpec((D_MODEL, D_MODEL), lambda b, j: (0, 0))],
        out_specs=pl.BlockSpec((1, BLOCK, D_MODEL), lambda b, j: (b, j, 0)),
        out_shape=jax.ShapeDtypeStruct((BATCH, TP, D_MODEL), F32),
        scratch_shapes=[pltpu.VMEM((BLOCK, D_MODEL), F32)],
        compiler_params=_params("parallel", "parallel"),
    )(sinks, h, qkv, qkv, qkv, qkv, qkv, qkv, qkv, q_gain, k_gain, w_o)


def _ffn_kernel(h_ref, g_ref, wg_ref, wu_ref, wd_ref, o_ref, u_scr, acc):
    f = pl.program_id(1)

    @pl.when(f == 0)
    def _():
        u_scr[...] = _rms(h_ref[...], g_ref[...]).astype(BF16)
        acc[...] = jnp.zeros_like(acc)

    u = u_scr[...]
    a = _dot(u, wg_ref[...])
    b = _dot(u, wu_ref[...])
    acc[...] += _dot((a * _sigmoid(a) * b).astype(BF16), wd_ref[...])

    @pl.when(f == pl.num_programs(1) - 1)
    def _():
        o_ref[...] = h_ref[...] + acc[...]


def _ffn(h, gain, wg, wu, wd):
    return pl.pallas_call(
        _ffn_kernel,
        grid=(N_ROWS // ROW_TILE, D_FF // FF_TILE),
        in_specs=[pl.BlockSpec((ROW_TILE, D_MODEL), lambda i, f: (i, 0)),
                  pl.BlockSpec((1, D_MODEL), lambda i, f: (0, 0)),
                  pl.BlockSpec((D_MODEL, FF_TILE), lambda i, f: (0, f)),
                  pl.BlockSpec((D_MODEL, FF_TILE), lambda i, f: (0, f)),
                  pl.BlockSpec((FF_TILE, D_MODEL), lambda i, f: (f, 0))],
        out_specs=pl.BlockSpec((ROW_TILE, D_MODEL), lambda i, f: (i, 0)),
        out_shape=jax.ShapeDtypeStruct((N_ROWS, D_MODEL), F32),
        scratch_shapes=[pltpu.VMEM((ROW_TILE, D_MODEL), BF16), pltpu.VMEM((ROW_TILE, D_MODEL), F32)],
        compiler_params=_params("parallel", "arbitrary"),
    )(h, gain, wg, wu, wd)


def _rwkv_proj_kernel(h_ref, hp_ref, g_ref, mix_ref, w0_ref, a0_ref, kk_ref, ka_ref,
                      wr_ref, wk_ref, wv_ref, w1_ref, w2_ref, a1_ref, a2_ref, g1_ref, g2_ref,
                      r_out, ld_out, k_out, v_out, kk_out, a_out, g_out):
    i = pl.program_id(0)
    tiles_per_batch = TP // PROJ_TILE
    r0 = (i % tiles_per_batch) * PROJ_TILE
    local = lax.broadcasted_iota(jnp.int32, (PROJ_TILE, 1), 0)
    lrow = local + r0
    gain = g_ref[...]
    is_pad = jnp.logical_and(lrow >= SEQ, lrow < SEQ + PAD)
    u = jnp.where(is_pad, 0.0, _rms(h_ref[...], gain))
    u_prev_tile = _rms(hp_ref[7:8, :], gain)
    xprev = pltpu.roll(u, 1, 0)
    xprev = jnp.where(local == 0, u_prev_tile, xprev)
    xprev = jnp.where(lrow == SEQ, 0.0, xprev)
    xx = xprev - u
    mix = mix_ref[...]
    lerp = lambda n: (u + xx * mix[n:n + 1, :]).astype(BF16)
    xr, xw, xk, xv, xa, xg = [lerp(n) for n in range(6)]
    r = _dot(xr, wr_ref[...])
    k = _dot(xk, wk_ref[...])
    v = _dot(xv, wv_ref[...])
    lw = _dot(jnp.tanh(_dot(xw, w1_ref[...])).astype(BF16), w2_ref[...])
    z = -(w0_ref[...] + lw)
    softplus = jnp.maximum(z, 0.0) + jnp.log(1.0 + jnp.exp(-jnp.abs(z)))
    w = -softplus - 0.5
    a = _sigmoid(a0_ref[...] + _dot(_dot(xa, a1_ref[...]).astype(BF16), a2_ref[...]))
    g = _dot(_sigmoid(_dot(xg, g1_ref[...])).astype(BF16), g2_ref[...])
    r_out[...] = r
    ld_out[...] = -jnp.exp(w)
    kk_out[...] = k * kk_ref[...]
    k_out[...] = k * (1.0 + (a - 1.0) * ka_ref[...])
    v_out[...] = v
    a_out[...] = a
    g_out[...] = g


def _rwkv_proj(h, gain, mix, w0, a0, k_k, k_a, w_r, w_k, w_v, w1, w2, a1, a2, g1, g2):
    tiles_per_batch = TP // PROJ_TILE
    rows8 = PROJ_TILE // 8

    def prev_map(i):
        b = i // tiles_per_batch
        first = (i % tiles_per_batch) == 0
        return (jnp.where(first, (b * TP + TP - 8) // 8, i * rows8 - 1), 0)

    row = pl.BlockSpec((PROJ_TILE, D_MODEL), lambda i: (i, 0))
    full = lambda a: pl.BlockSpec(a.shape, lambda i: (0,) * a.ndim)
    smalls = (gain, mix, w0, a0, k_k, k_a, w_r, w_k, w_v, w1, w2, a1, a2, g1, g2)
    return pl.pallas_call(
        _rwkv_proj_kernel,
        grid=(N_ROWS // PROJ_TILE,),
        in_specs=[row, pl.BlockSpec((8, D_MODEL), prev_map)] + [full(a) for a in smalls],
        out_specs=[row] * 7,
        out_shape=[jax.ShapeDtypeStruct((N_ROWS, D_MODEL), F32)] * 7,
        compiler_params=_params("parallel"),
    )(h, h, *smalls)


def _group_masks():
    ri = lax.broadcasted_iota(jnp.int32, (GW, GW), 0) // RWKV_HEAD
    ci = lax.broadcasted_iota(jnp.int32, (GW, GW), 1) // RWKV_HEAD
    return jnp.where(ri == ci, 1.0, 0.0).astype(F32)


def _bd(x, mask):
    return (jnp.concatenate([x] * HEADS_PER_GROUP, axis=0) * mask).astype(BF16)


def _diag_blocks(full, mask):
    m = full * mask
    n = RWKV_HEAD
    return (m[0:n] + m[n:2 * n]) + (m[2 * n:3 * n] + m[3 * n:4 * n])


def _head_sum(x, mask_bf):
    hi = x.astype(BF16)
    lo = (x - hi.astype(F32)).astype(BF16)
    return _dot(hi, mask_bf) + _dot(lo, mask_bf)


def _scan_prep_kernel(r_ref, ld_ref, k_ref, v_ref, kk_ref, a_ref, rk_ref,
                      rp_out, op_out, g_out, ht_out, bonus_out, pl_out):
    L = CHUNK
    groups = range(N_GROUPS)
    gsl = [slice(g * GW, (g + 1) * GW) for g in groups]
    mask = _group_masks()
    mask_bf = mask.astype(BF16)
    ri = lax.broadcasted_iota(jnp.int32, (L, GW), 0)
    ci = lax.broadcasted_iota(jnp.int32, (L, GW), 1) % RWKV_HEAD
    incl = ci <= ri
    strict = ci < ri
    eye = jnp.where(ci == ri, 1.0, 0.0).astype(F32)
    t_r = lax.broadcasted_iota(jnp.int32, (L, L), 0)
    t_c = lax.broadcasted_iota(jnp.int32, (L, L), 1)
    ld = ld_ref[0]
    cs = jnp.dot(jnp.where(t_c <= t_r, 1.0, 0.0).astype(F32), ld, precision=lax.Precision.HIGHEST,
                 preferred_element_type=F32)
    p_all = jnp.exp(cs)
    pprev_all = jnp.exp(cs - ld)
    pinv_all = jnp.exp(-cs)
    pl_all = p_all[L - 1:L, :]
    pl_out[0] = pl_all
    r_all, k_all, v_all, kk_all, a_all = r_ref[0], k_ref[0], v_ref[0], kk_ref[0], a_ref[0]
    rk_all = rk_ref[...]
    kk2 = kk_all * kk_all
    rkk = r_all * k_all * rk_all
    rt_all = r_all * p_all
    kt_all = k_all * pinv_all

    at, rt, bt, kt, v = [], [], [], [], []
    for g in groups:
        sl = gsl[g]
        nrm = jnp.sqrt(_head_sum(kk2[:, sl], mask_bf))
        kk = kk_all[:, sl] / jnp.maximum(nrm, 1e-12)
        at.append(-kk * pprev_all[:, sl])
        bt.append(kk * a_all[:, sl] * pinv_all[:, sl])
        rt.append(rt_all[:, sl])
        kt.append(kt_all[:, sl])
        v.append(v_all[:, sl])
        bonus_out[0, :, sl] = _head_sum(rkk[:, sl], mask_bf) * v_all[:, sl]

    a_ab, a_ak, a_rb, a_rk = [], [], [], []
    for g in groups:
        lhs = jnp.concatenate([at[g], rt[g]], axis=0).astype(BF16)
        rhs = jnp.concatenate([_bd(bt[g], mask), _bd(kt[g], mask)], axis=0)
        big = lax.dot_general(lhs, rhs, NT_DIMS, preferred_element_type=F32)
        a_ab.append(jnp.where(strict, big[:L, :GW], 0.0))
        a_ak.append(jnp.where(strict, big[:L, GW:], 0.0))
        a_rb.append(jnp.where(incl, big[L:, :GW], 0.0))
        a_rk.append(jnp.where(incl, big[L:, GW:], 0.0))

    x = [_dot(a_ab[g].astype(BF16), _bd(a_ab[g], mask)) for g in groups]
    inv = [eye + a_ab[g] for g in groups]
    for step in range(5):
        for g in groups:
            rhs = _bd(x[g], mask)
            if step < 4:
                res = _dot(jnp.concatenate([x[g], inv[g]], axis=0).astype(BF16), rhs)
                x[g] = res[:L]
                inv[g] = inv[g] + res[L:]
            else:
                inv[g] = inv[g] + _dot(inv[g].astype(BF16), rhs)

    av = [_dot(jnp.concatenate([a_ak[g], a_rk[g]], axis=0).astype(BF16), _bd(v[g], mask)) for g in groups]
    wu = [_dot(inv[g].astype(BF16), jnp.concatenate([_bd(at[g], mask), _bd(av[g][:L], mask)], axis=1))
          for g in groups]
    aw = [_dot(a_rb[g].astype(BF16),
               jnp.concatenate([_bd(wu[g][:, :GW], mask), _bd(wu[g][:, GW:], mask)], axis=1))
          for g in groups]
    for g in groups:
        sl = gsl[g]
        rp_out[0, :, sl] = rt[g] + aw[g][:, :GW]
        op_out[0, :, sl] = av[g][L:] + aw[g][:, GW:]
        plg = pl_all[:, sl]
        bh = (bt[g] * plg).astype(BF16)
        kh = (kt[g] * plg).astype(BF16)
        w_b, u0_b = wu[g][:, :GW].astype(BF16), wu[g][:, GW:].astype(BF16)
        gfull = lax.dot_general(bh, w_b, TN_DIMS, preferred_element_type=F32)
        g_out[0, :, sl] = _diag_blocks(gfull, mask)
        hfull = lax.dot_general(jnp.concatenate([u0_b, v[g].astype(BF16)], axis=0),
                                jnp.concatenate([bh, kh], axis=0), TN_DIMS, preferred_element_type=F32)
        ht_out[0, :, sl] = _diag_blocks(hfull, mask)


def _scan_prep(r, ld, k, v, kk, a, r_k):
    blk = pl.BlockSpec((1, CHUNK, D_MODEL), lambda b, c: (b, c, 0))
    big = jax.ShapeDtypeStruct((BATCH, TP, D_MODEL), F32)
    return pl.pallas_call(
        _scan_prep_kernel,
        grid=(BATCH, N_CHUNKS),
        in_specs=[blk] * 6 + [pl.BlockSpec((1, D_MODEL), lambda b, c: (0, 0))],
        out_specs=[blk] * 5 + [pl.BlockSpec((1, 1, D_MODEL), lambda b, c: (b * N_CHUNKS + c, 0, 0))],
        out_shape=[big] * 5 + [jax.ShapeDtypeStruct((BATCH * N_CHUNKS, 1, D_MODEL), F32)],
        compiler_params=_params("parallel", "parallel"),
    )(r, ld, k, v, kk, a, r_k)


def _scan_kernel(rp_ref, op_ref, g_ref, ht_ref, bonus_ref, pl_ref, gw_ref, gb_ref, y_ref, s_scr):
    c = pl.program_id(0)

    @pl.when(c == 0)
    def _():
        s_scr[...] = jnp.zeros_like(s_scr)

    mask = _group_masks()
    mask_bf = mask.astype(BF16)
    units = [(b, slice(g * GW, (g + 1) * GW)) for b in range(BATCH) for g in range(N_GROUPS)]
    s0 = [s_scr[b, :, sl] for b, sl in units]
    o = [lax.dot_general(rp_ref[b, :, sl].astype(BF16), _bd(s0[n], mask), NT_DIMS,
                         preferred_element_type=F32) + op_ref[b, :, sl]
         for n, (b, sl) in enumerate(units)]
    sg = [lax.dot_general(s0[n].astype(BF16), _bd(g_ref[b, :, sl], mask), NT_DIMS,
                          preferred_element_type=F32)
          for n, (b, sl) in enumerate(units)]
    for n, (b, sl) in enumerate(units):
        s_scr[b, :, sl] = s0[n] * pl_ref[b, 0, :, sl] + sg[n] + ht_ref[b, :, sl]
    inv_n = 1.0 / RWKV_HEAD
    mu = [_head_sum(o[n], mask_bf) * inv_n for n in range(len(units))]
    d = [o[n] - mu[n] for n in range(len(units))]
    var = [_head_sum(d[n] * d[n], mask_bf) * inv_n for n in range(len(units))]
    for n, (b, sl) in enumerate(units):
        y_ref[b, :, sl] = (d[n] * lax.rsqrt(var[n] + GN_EPS) * gw_ref[:, sl] + gb_ref[:, sl]
                           + bonus_ref[b, :, sl])


def _scan(rp, op, g, ht, bonus, p_last, gn_w, gn_b):
    phys = lambda c: (c + META_CHUNK0) % N_CHUNKS
    blk = pl.BlockSpec((BATCH, CHUNK, D_MODEL), lambda c: (0, phys(c), 0))
    vec = pl.BlockSpec((1, D_MODEL), lambda c: (0, 0))
    return pl.pallas_call(
        _scan_kernel,
        grid=(N_CHUNKS,),
        in_specs=[blk] * 5 + [pl.BlockSpec((BATCH, 1, 1, D_MODEL), lambda c: (0, phys(c), 0, 0)), vec, vec],
        out_specs=blk,
        out_shape=jax.ShapeDtypeStruct((BATCH, TP, D_MODEL), F32),
        scratch_shapes=[pltpu.VMEM((BATCH, RWKV_HEAD, D_MODEL), F32)],
        compiler_params=_params("arbitrary"),
    )(rp, op, g, ht, bonus, p_last, gn_w, gn_b)


def _rwkv_out_kernel(y_ref, g_ref, h_ref, wo_ref, gain_ref, wr_ref, h_out, u_out, route_out, cnt_out,
                     carry):
    @pl.when(jnp.logical_and(pl.program_id(0) == 0, pl.program_id(1) == 0))
    def _():
        carry[...] = jnp.zeros_like(carry)

    h = h_ref[0] + _dot((y_ref[0] * g_ref[0]).astype(BF16), wo_ref[...])
    h_out[0] = h
    u = _rms(h, gain_ref[...])
    u_out[0] = u.astype(BF16)
    logits = jnp.dot(u, wr_ref[...], precision=lax.Precision.HIGHEST, preferred_element_type=F32)
    e = jnp.exp(logits - jnp.max(logits, axis=-1, keepdims=True))
    probs = e / jnp.sum(e, axis=-1, keepdims=True)
    idx = lax.broadcasted_iota(jnp.int32, probs.shape, 1).astype(F32)
    m1 = jnp.max(probs, axis=-1, keepdims=True)
    i1 = jnp.min(jnp.where(probs == m1, idx, float(N_EXPERTS)), axis=-1, keepdims=True)
    sel1 = idx == i1
    rest = jnp.where(sel1, -1.0, probs)
    m2 = jnp.max(rest, axis=-1, keepdims=True)
    i2 = jnp.min(jnp.where(rest == m2, idx, float(N_EXPERTS)), axis=-1, keepdims=True)
    sel2 = idx == i2
    onehot = jnp.where(jnp.logical_or(sel1, sel2), 1.0, 0.0).astype(F32)
    tr = lax.broadcasted_iota(jnp.int32, (TOKEN_BLOCK, TOKEN_BLOCK), 0)
    tc = lax.broadcasted_iota(jnp.int32, (TOKEN_BLOCK, TOKEN_BLOCK), 1)
    earlier = _dot(jnp.where(tc < tr, 1.0, 0.0).astype(BF16), onehot.astype(BF16)) + carry[...]
    rank1 = jnp.sum(jnp.where(sel1, earlier, 0.0), axis=-1, keepdims=True)
    rank2 = jnp.sum(jnp.where(sel2, earlier, 0.0), axis=-1, keepdims=True)
    den = m1 + m2
    fields = (i1, i2, rank1, rank2, m1 / den, m2 / den)
    route = jnp.zeros(probs.shape, F32)
    for n, val in enumerate(fields):
        route = jnp.where(idx == float(n), val, route)
    route_out[0] = route
    tile_cnt = jnp.sum(onehot, axis=0, keepdims=True)
    cnt_out[0] = tile_cnt
    carry[...] += tile_cnt


def _rwkv_out(y, g, h, w_o, gain, w_router):
    blocks = SEQ // TOKEN_BLOCK
    row = pl.BlockSpec((1, TOKEN_BLOCK, D_MODEL), lambda b, i: (b, i, 0))
    return pl.pallas_call(
        _rwkv_out_kernel,
        grid=(BATCH, blocks),
        in_specs=[row, row, row,
                  pl.BlockSpec((D_MODEL, D_MODEL), lambda b, i: (0, 0)),
                  pl.BlockSpec((1, D_MODEL), lambda b, i: (0, 0)),
                  pl.BlockSpec((D_MODEL, N_EXPERTS), lambda b, i: (0, 0))],
        out_specs=[row, row,
                   pl.BlockSpec((1, TOKEN_BLOCK, N_EXPERTS), lambda b, i: (b, i, 0)),
                   pl.BlockSpec((1, 1, N_EXPERTS), lambda b, i: (b * blocks + i, 0, 0))],
        out_shape=[jax.ShapeDtypeStruct((BATCH, SEQ, D_MODEL), F32),
                   jax.ShapeDtypeStruct((BATCH, SEQ, D_MODEL), BF16),
                   jax.ShapeDtypeStruct((BATCH, SEQ, N_EXPERTS), F32),
                   jax.ShapeDtypeStruct((BATCH * blocks, 1, N_EXPERTS), F32)],
        scratch_shapes=[pltpu.VMEM((1, N_EXPERTS), F32)],
        compiler_params=_params("arbitrary", "arbitrary"),
    )(y, g, h, w_o, gain, w_router)


def _routing_tables(route, cnt):
    i32 = jnp.int32
    route = route.reshape(N_TOKENS, N_EXPERTS)
    expert = route[:, 0:2].astype(i32)
    rank = route[:, 2:4].astype(i32)
    gate = route[:, 4:6]
    cnt = cnt.reshape(N_TOKEN_BLOCKS, N_EXPERTS).astype(i32)
    tiles_e = (jnp.sum(cnt, axis=0) + EXPERT_TILE - 1) // EXPERT_TILE
    tile_end = jnp.cumsum(tiles_e)
    n_used = tile_end[-1]
    start_row = (tile_end - tiles_e) * EXPERT_TILE
    pos = jnp.take(start_row, expert) + rank
    tile_expert = jnp.minimum(
        jnp.sum(jnp.arange(MAX_EXPERT_TILES, dtype=i32)[:, None] >= tile_end[None, :], axis=1),
        N_EXPERTS - 1).astype(i32)
    before = jnp.concatenate([jnp.zeros((1, N_EXPERTS), i32), jnp.cumsum(cnt, axis=0)], axis=0)
    sub = jnp.arange(MAX_SUB_TILES, dtype=i32)
    sub_expert = jnp.take(tile_expert, sub * SUB_TILE // EXPERT_TILE)
    sub_used = sub < n_used * (EXPERT_TILE // SUB_TILE)
    base = jnp.take(start_row, sub_expert)[None, :]
    lo = base + jnp.take(before[:-1], sub_expert, axis=1)
    hi = base + jnp.take(before[1:], sub_expert, axis=1)
    hit = ((hi > lo) & (lo < (sub + 1)[None, :] * SUB_TILE) & (hi > sub[None, :] * SUB_TILE)
           & sub_used[None, :])

    def pairs(mat, width):
        n = jnp.sum(mat.astype(i32))
        flat = jnp.nonzero(mat.reshape(-1), size=MAX_PAIRS, fill_value=0)[0].astype(i32)
        flat = jnp.where(jnp.arange(MAX_PAIRS) < n, flat, jnp.take(flat, jnp.maximum(n - 1, 0)))
        return flat // width, flat % width, n.reshape(1).astype(i32)

    by_sub = hit.T
    empty = sub_used & ~jnp.any(by_sub, axis=1)
    by_sub = by_sub.at[:, 0].set(by_sub[:, 0] | empty)
    g_sub, g_blk, g_n = pairs(by_sub, N_TOKEN_BLOCKS)
    c_blk, c_sub, c_n = pairs(hit, MAX_SUB_TILES)
    return dict(pos=pos, gate=gate, tile_expert=tile_expert, n_used=n_used.reshape(1).astype(i32),
                g_sub=g_sub, g_blk=g_blk, g_n=g_n, c_blk=c_blk, c_sub=c_sub, c_n=c_n)


def _gather_kernel(sub_ref, blk_ref, n_ref, pos_ref, gate_ref, u_ref, x_out, gs_out):
    s = pl.program_id(0)
    q = sub_ref[s]
    first = jnp.logical_or(s == 0, sub_ref[jnp.maximum(s - 1, 0)] != q)

    @pl.when(s < n_ref[0])
    def _():
        rows = q * SUB_TILE + lax.broadcasted_iota(jnp.int32, (SUB_TILE, 1), 0)
        hit0 = pos_ref[0:1, :] == rows
        hit1 = pos_ref[1:2, :] == rows
        onehot = jnp.where(jnp.logical_or(hit0, hit1), 1.0, 0.0).astype(BF16)
        xs = _dot(onehot, u_ref[...]).astype(BF16)
        gs = jnp.sum(jnp.where(hit0, gate_ref[0:1, :], 0.0) + jnp.where(hit1, gate_ref[1:2, :], 0.0),
                     axis=-1, keepdims=True)

        @pl.when(first)
        def _():
            x_out[...] = xs
            gs_out[...] = gs

        @pl.when(jnp.logical_not(first))
        def _():
            x_out[...] += xs
            gs_out[...] += gs


def _gather(tab, pos_rows, gate_rows, u):
    grid_spec = pltpu.PrefetchScalarGridSpec(
        num_scalar_prefetch=3, grid=(MAX_PAIRS,),
        in_specs=[pl.BlockSpec((2, TOKEN_BLOCK), lambda s, sub, blk, n: (0, blk[s])),
                  pl.BlockSpec((2, TOKEN_BLOCK), lambda s, sub, blk, n: (0, blk[s])),
                  pl.BlockSpec((TOKEN_BLOCK, D_MODEL), lambda s, sub, blk, n: (blk[s], 0))],
        out_specs=[pl.BlockSpec((SUB_TILE, D_MODEL), lambda s, sub, blk, n: (sub[s], 0)),
                   pl.BlockSpec((SUB_TILE, 1), lambda s, sub, blk, n: (sub[s], 0))])
    return pl.pallas_call(
        _gather_kernel, grid_spec=grid_spec,
        out_shape=[jax.ShapeDtypeStruct((MAX_SORTED_ROWS, D_MODEL), BF16),
                   jax.ShapeDtypeStruct((MAX_SORTED_ROWS, 1), F32)],
        compiler_params=_params("arbitrary"),
    )(tab["g_sub"], tab["g_blk"], tab["g_n"], pos_rows, gate_rows, u)


def _expert_kernel(te_ref, nu_ref, x_ref, gs_ref, wg_ref, wu_ref, wd_ref, y_ref, acc):
    i = pl.program_id(0)
    f = pl.program_id(1)

    @pl.when(i < nu_ref[0])
    def _():
        @pl.when(f == 0)
        def _():
            acc[...] = jnp.zeros_like(acc)

        x = x_ref[...]
        a = _dot(x, wg_ref[0].astype(BF16))
        b = _dot(x, wu_ref[0].astype(BF16))
        acc[...] += _dot((a * _sigmoid(a) * b).astype(BF16), wd_ref[0].astype(BF16))

        @pl.when(f == pl.num_programs(1) - 1)
        def _():
            y_ref[...] = (acc[...] * gs_ref[...]).astype(BF16)


def _experts(tab, x_sorted, gate_sorted, wg, wu, wd):
    n_ff = D_FF // FF_TILE
    tile = lambda i, nu: jnp.minimum(i, nu[0] - 1)
    ff = lambda i, f, nu: jnp.where(i < nu[0], f, n_ff - 1)
    grid_spec = pltpu.PrefetchScalarGridSpec(
        num_scalar_prefetch=2, grid=(MAX_EXPERT_TILES, n_ff),
        in_specs=[pl.BlockSpec((EXPERT_TILE, D_MODEL), lambda i, f, te, nu: (tile(i, nu), 0)),
                  pl.BlockSpec((EXPERT_TILE, 1), lambda i, f, te, nu: (tile(i, nu), 0)),
                  pl.BlockSpec((1, D_MODEL, FF_TILE), lambda i, f, te, nu: (te[tile(i, nu)], 0, ff(i, f, nu))),
                  pl.BlockSpec((1, D_MODEL, FF_TILE), lambda i, f, te, nu: (te[tile(i, nu)], 0, ff(i, f, nu))),
                  pl.BlockSpec((1, FF_TILE, D_MODEL), lambda i, f, te, nu: (te[tile(i, nu)], ff(i, f, nu), 0))],
        out_specs=pl.BlockSpec((EXPERT_TILE, D_MODEL), lambda i, f, te, nu: (tile(i, nu), 0)),
        scratch_shapes=[pltpu.VMEM((EXPERT_TILE, D_MODEL), F32)])
    return pl.pallas_call(
        _expert_kernel, grid_spec=grid_spec,
        out_shape=jax.ShapeDtypeStruct((MAX_SORTED_ROWS, D_MODEL), BF16),
        compiler_params=_params("arbitrary", "arbitrary"),
    )(tab["tile_expert"], tab["n_used"], x_sorted, gate_sorted, wg, wu, wd)


def _combine_kernel(blk_ref, sub_ref, n_ref, pos_ref, y_ref, h_ref, o_ref):
    s = pl.program_id(0)
    b = blk_ref[s]
    first = jnp.logical_or(s == 0, blk_ref[jnp.maximum(s - 1, 0)] != b)

    @pl.when(s < n_ref[0])
    def _():
        cols = sub_ref[s] * SUB_TILE + lax.broadcasted_iota(jnp.int32, (1, SUB_TILE), 1)
        hit = jnp.logical_or(pos_ref[:, 0:1] == cols, pos_ref[:, 1:2] == cols)
        part = _dot(jnp.where(hit, 1.0, 0.0).astype(BF16), y_ref[...])

        @pl.when(first)
        def _():
            o_ref[...] = h_ref[...] + part

        @pl.when(jnp.logical_not(first))
        def _():
            o_ref[...] += part


def _combine(tab, pos_cols, y_sorted, h):
    grid_spec = pltpu.PrefetchScalarGridSpec(
        num_scalar_prefetch=3, grid=(MAX_PAIRS,),
        in_specs=[pl.BlockSpec((TOKEN_BLOCK, 2), lambda s, blk, sub, n: (blk[s], 0)),
                  pl.BlockSpec((SUB_TILE, D_MODEL), lambda s, blk, sub, n: (sub[s], 0)),
                  pl.BlockSpec((TOKEN_BLOCK, D_MODEL), lambda s, blk, sub, n: (blk[s], 0))],
        out_specs=pl.BlockSpec((TOKEN_BLOCK, D_MODEL), lambda s, blk, sub, n: (blk[s], 0)))
    return pl.pallas_call(
        _combine_kernel, grid_spec=grid_spec,
        out_shape=jax.ShapeDtypeStruct((N_TOKENS, D_MODEL), F32),
        compiler_params=_params("arbitrary"),
    )(tab["c_blk"], tab["c_sub"], tab["c_n"], pos_cols, y_sorted, h)


def kernel(x, meta_tokens, mixer_norm, ffn_norm, attn_w_qkv, attn_q_norm, attn_k_norm, attn_sinks, attn_w_o, rwkv_mix, rwkv_w0, rwkv_w1, rwkv_w2, rwkv_a0, rwkv_a1, rwkv_a2, rwkv_g1, rwkv_g2, rwkv_k_k, rwkv_k_a, rwkv_r_k, rwkv_w_r, rwkv_w_k, rwkv_w_v, rwkv_w_o, rwkv_gn_w, rwkv_gn_b, ffn_w_gate, ffn_w_up, ffn_w_down, moe_router, moe_w_gate, moe_w_up, moe_w_down):
    bf = lambda a: a.astype(BF16)
    vec = lambda a: a.reshape(1, -1).astype(F32)
    meta = jnp.broadcast_to(meta_tokens.astype(F32)[None], (BATCH, N_META, D_MODEL))
    h = jnp.concatenate([x, jnp.zeros((BATCH, PAD, D_MODEL), F32), meta], axis=1)

    qkv = _qkv(h.reshape(N_ROWS, D_MODEL), vec(mixer_norm[0]), bf(attn_w_qkv[0]))
    h = _attention(h, qkv.reshape(BATCH, TP, QKV_DIM), attn_sinks[0].astype(F32),
                   vec(attn_q_norm[0]), vec(attn_k_norm[0]), bf(attn_w_o[0]))
    h = _ffn(h.reshape(N_ROWS, D_MODEL), vec(ffn_norm[0]), bf(ffn_w_gate[0]), bf(ffn_w_up[0]),
             bf(ffn_w_down[0]))

    r, ld, k, v, kk, a, g = _rwkv_proj(
        h, vec(mixer_norm[1]), rwkv_mix[0], vec(rwkv_w0[0]), vec(rwkv_a0[0]), vec(rwkv_k_k[0]),
        vec(rwkv_k_a[0]), bf(rwkv_w_r[0]), bf(rwkv_w_k[0]), bf(rwkv_w_v[0]), bf(rwkv_w1[0]),
        bf(rwkv_w2[0]), bf(rwkv_a1[0]), bf(rwkv_a2[0]), bf(rwkv_g1[0]), bf(rwkv_g2[0]))
    b3 = lambda t: t.reshape(BATCH, TP, D_MODEL)
    rp, op, gm, ht, bonus, p_last = _scan_prep(b3(r), b3(ld), b3(k), b3(v), b3(kk), b3(a),
                                               vec(rwkv_r_k[0]))
    y = _scan(rp, op, gm, ht, bonus, p_last.reshape(BATCH, N_CHUNKS, 1, D_MODEL),
              vec(rwkv_gn_w[0]), vec(rwkv_gn_b[0]))
    h, u, route, cnt = _rwkv_out(y, b3(g), b3(h), bf(rwkv_w_o[0]), vec(ffn_norm[1]),
                                 moe_router[0].astype(F32))
    tab = _routing_tables(route, cnt)
    x_sorted, gate_sorted = _gather(tab, tab["pos"].T, tab["gate"].T, u.reshape(N_TOKENS, D_MODEL))
    y_sorted = _experts(tab, x_sorted, gate_sorted, moe_w_gate[0], moe_w_up[0], moe_w_down[0])
    out = _combine(tab, tab["pos"], y_sorted, h.reshape(N_TOKENS, D_MODEL))
    return out.reshape(BATCH, SEQ, D_MODEL)
```

```python
import functools

import jax
import jax.numpy as jnp
from jax import lax
from jax.experimental import pallas as pl
from jax.experimental.pallas import tpu as pltpu

F32 = jnp.float32
BF16 = jnp.bfloat16

D_MODEL = 1024
BATCH = 2
SEQ = 8192
N_META = 16
BLOCK = 128
PAD = BLOCK - N_META
TP = SEQ + BLOCK
N_ROWS = BATCH * TP
N_BLOCKS = TP // BLOCK
META_BLOCK = N_BLOCKS - 1
HEAD_DIM = 64
N_Q_HEADS = 16
N_KV_HEADS = 4
Q_PER_KV = 4
QKV_DIM = (N_Q_HEADS + 2 * N_KV_HEADS) * HEAD_DIM
RWKV_HEADS = 16
RWKV_HEAD = 64
D_FF = 3584
N_EXPERTS = 8
NORM_EPS = 1e-5
GN_EPS = 64e-5
CHUNK = 64
N_CHUNKS = TP // CHUNK
META_CHUNK0 = SEQ // CHUNK
HEADS_PER_GROUP = 4
GW = HEADS_PER_GROUP * RWKV_HEAD
N_GROUPS = RWKV_HEADS // HEADS_PER_GROUP
N_SCAN_IN = 6
N_SCAN_MID = 5

ROW_TILE = 640
FFN_ROW_TILE = 1280
PROJ_TILE = 320
FF_TILE = 512
N_TOKENS = BATCH * SEQ
TOP_K = 2
TOKEN_BLOCK = 512
N_TOKEN_BLOCKS = N_TOKENS // TOKEN_BLOCK
EXPERT_TILE = 1024
SUB_TILE = 256
MAX_EXPERT_TILES = TOP_K * N_TOKENS // EXPERT_TILE + N_EXPERTS
MAX_SORTED_ROWS = MAX_EXPERT_TILES * EXPERT_TILE
MAX_SUB_TILES = MAX_SORTED_ROWS // SUB_TILE
MAX_PAIRS = MAX_SUB_TILES + N_EXPERTS * N_TOKEN_BLOCKS
VMEM_LIMIT = 56 * 1024 * 1024
NEG = -1e30

NT_DIMS = (((1,), (1,)), ((), ()))
TN_DIMS = (((0,), (0,)), ((), ()))


def _params(*sem):
    return pltpu.CompilerParams(dimension_semantics=sem, vmem_limit_bytes=VMEM_LIMIT)


def _rms(x, gain):
    return x * lax.rsqrt(jnp.mean(x * x, axis=-1, keepdims=True) + NORM_EPS) * gain


def _sigmoid(x):
    return 1.0 / (1.0 + jnp.exp(-x))


def _dot(a, b):
    return jnp.dot(a, b, preferred_element_type=F32)


def _qkv_kernel(h_ref, g_ref, w_ref, o_ref):
    u = _rms(h_ref[...], g_ref[...])
    o_ref[...] = _dot(u.astype(BF16), w_ref[...]).astype(BF16)


def _qkv(h, gain, w):
    return pl.pallas_call(
        _qkv_kernel,
        grid=(N_ROWS // ROW_TILE,),
        in_specs=[pl.BlockSpec((ROW_TILE, D_MODEL), lambda i: (i, 0)),
                  pl.BlockSpec((1, D_MODEL), lambda i: (0, 0)),
                  pl.BlockSpec((D_MODEL, QKV_DIM), lambda i: (0, 0))],
        out_specs=pl.BlockSpec((ROW_TILE, QKV_DIM), lambda i: (i, 0)),
        out_shape=jax.ShapeDtypeStruct((N_ROWS, QKV_DIM), BF16),
        compiler_params=_params("parallel"),
    )(h, gain, w)


def _attn_kernel(sink_ref, h_ref, q_ref, kc_ref, vc_ref, kp_ref, vp_ref, km_ref, vm_ref,
                 qg_ref, kg_ref, wo_ref, o_ref, o_scr):
    j = pl.program_id(1)
    n_keys = 3 * BLOCK
    rowi = lax.broadcasted_iota(jnp.int32, (BLOCK, n_keys), 0)
    col = lax.broadcasted_iota(jnp.int32, (BLOCK, n_keys), 1)
    far = 4 * BLOCK
    is_real = j < META_BLOCK
    meta_off = jnp.where(is_real, -far, 0)
    prev_off = jnp.where(jnp.logical_and(j >= 1, is_real), 0, far)
    cur_off = jnp.where(is_real, 0, far)
    meta_ok = jnp.logical_and(jnp.logical_and(col >= PAD, col < BLOCK), col + meta_off <= rowi)
    prev_ok = jnp.logical_and(jnp.logical_and(col >= BLOCK, col < 2 * BLOCK),
                              col - BLOCK > rowi + prev_off)
    cur_ok = jnp.logical_and(col >= 2 * BLOCK, col - 2 * BLOCK + cur_off <= rowi)
    ok = jnp.logical_or(jnp.logical_or(meta_ok, prev_ok), cur_ok)
    ok4 = jnp.concatenate([ok] * Q_PER_KV, axis=1)

    groups = range(N_KV_HEADS)
    mask_bf = _group_masks().astype(BF16)
    rb = lax.broadcasted_iota(jnp.int32, (Q_PER_KV * n_keys, GW), 0) // n_keys
    lb = lax.broadcasted_iota(jnp.int32, (Q_PER_KV * n_keys, GW), 1) // HEAD_DIM
    block_mask = jnp.where(rb == lb, 1.0, 0.0).astype(BF16)
    sr = lax.broadcasted_iota(jnp.int32, (GW, GW), 0)
    sc = lax.broadcasted_iota(jnp.int32, (GW, GW), 1)
    lane_head = lax.broadcasted_iota(jnp.int32, (1, GW), 1) // HEAD_DIM
    inv_d = 1.0 / HEAD_DIM
    scale = HEAD_DIM ** -0.5

    kall = jnp.concatenate([km_ref[0], kp_ref[0], kc_ref[0]], axis=0).astype(F32)
    vall = jnp.concatenate([vm_ref[0], vp_ref[0], vc_ref[0]], axis=0)
    kss = _dot((kall * kall).astype(BF16), mask_bf)
    kn = (kall * lax.rsqrt(kss * inv_d + NORM_EPS) * kg_ref[...]).astype(BF16)
    q_all = q_ref[0].astype(F32)
    qn = []
    for g in groups:
        qg = q_all[:, g * GW:(g + 1) * GW]
        qss = _dot((qg * qg).astype(BF16), mask_bf)
        qn.append((qg * lax.rsqrt(qss * inv_d + NORM_EPS) * (qg_ref[...] * scale)).astype(BF16))
    sel = [jnp.where(sr == g * HEAD_DIM + sc % HEAD_DIM, 1.0, 0.0).astype(BF16) for g in groups]
    krep = [_dot(kn, sel[g]).astype(BF16) for g in groups]
    vrep = [_dot(vall, sel[g]).astype(BF16) for g in groups]
    bdk = [jnp.concatenate([krep[g]] * Q_PER_KV, axis=0) * block_mask for g in groups]
    rhs = [jnp.concatenate([jnp.concatenate([vrep[g]] * Q_PER_KV, axis=0) * block_mask, block_mask], axis=1)
           for g in groups]
    s = [jnp.where(ok4, lax.dot_general(qn[g], bdk[g], NT_DIMS, preferred_element_type=F32), NEG)
         for g in groups]
    p, sink_den = [], []
    for g in groups:
        parts = []
        sd = jnp.zeros((BLOCK, GW), F32)
        for hh in range(Q_PER_KV):
            seg = s[g][:, hh * n_keys:(hh + 1) * n_keys]
            sink = sink_ref[g * Q_PER_KV + hh]
            m = jnp.maximum(jnp.max(seg, axis=-1, keepdims=True), sink)
            parts.append(jnp.exp(seg - m).astype(BF16))
            sd = sd + jnp.exp(sink - m) * jnp.where(lane_head == hh, 1.0, 0.0)
        p.append(jnp.concatenate(parts, axis=1))
        sink_den.append(sd)
    ov = [_dot(p[g], rhs[g]) for g in groups]
    for g in groups:
        o_scr[:, g * GW:(g + 1) * GW] = ov[g][:, :GW] / (ov[g][:, GW:] + sink_den[g])
    o_ref[0] = h_ref[0] + _dot(o_scr[...].astype(BF16), wo_ref[...])


def _attention(h, qkv, sinks, q_gain, k_gain, w_o):
    kcol, vcol = N_Q_HEADS * HEAD_DIM // 256, N_Q_HEADS * HEAD_DIM // 256 + 1
    kvw = N_KV_HEADS * HEAD_DIM
    prev = lambda j: jnp.maximum(j - 1, 0)
    return pl.pallas_call(
        _attn_kernel,
        grid=(BATCH, N_BLOCKS),
        in_specs=[pl.BlockSpec(memory_space=pltpu.SMEM),
                  pl.BlockSpec((1, BLOCK, D_MODEL), lambda b, j: (b, j, 0)),
                  pl.BlockSpec((1, BLOCK, D_MODEL), lambda b, j: (b, j, 0)),
                  pl.BlockSpec((1, BLOCK, kvw), lambda b, j: (b, j, kcol)),
                  pl.BlockSpec((1, BLOCK, kvw), lambda b, j: (b, j, vcol)),
                  pl.BlockSpec((1, BLOCK, kvw), lambda b, j: (b, prev(j), kcol)),
                  pl.BlockSpec((1, BLOCK, kvw), lambda b, j: (b, prev(j), vcol)),
                  pl.BlockSpec((1, BLOCK, kvw), lambda b, j: (b, META_BLOCK, kcol)),
                  pl.BlockSpec((1, BLOCK, kvw), lambda b, j: (b, META_BLOCK, vcol)),
                  pl.BlockSpec((1, GW), lambda b, j: (0, 0)),
                  pl.BlockSpec((1, GW), lambda b, j: (0, 0)),
                  pl.BlockSpec((D_MODEL, D_MODEL), lambda b, j: (0, 0))],
        out_specs=pl.BlockSpec((1, BLOCK, D_MODEL), lambda b, j: (b, j, 0)),
        out_shape=jax.ShapeDtypeStruct((BATCH, TP, D_MODEL), F32),
        scratch_shapes=[pltpu.VMEM((BLOCK, D_MODEL), F32)],
        compiler_params=_params("parallel", "parallel"),
    )(sinks, h, qkv, qkv, qkv, qkv, qkv, qkv, qkv, q_gain, k_gain, w_o)


def _ffn_kernel(h_ref, g_ref, wg_ref, wu_ref, wd_ref, o_ref, u_scr, acc):
    f = pl.program_id(1)

    @pl.when(f == 0)
    def _():
        u_scr[...] = _rms(h_ref[...], g_ref[...]).astype(BF16)
        acc[...] = jnp.zeros_like(acc)

    u = u_scr[...]
    a = _dot(u, wg_ref[...])
    b = _dot(u, wu_ref[...])
    acc[...] += _dot((a * _sigmoid(a) * b).astype(BF16), wd_ref[...])

    @pl.when(f == pl.num_programs(1) - 1)
    def _():
        o_ref[...] = h_ref[...] + acc[...]


def _ffn(h, gain, wg, wu, wd):
    return pl.pallas_call(
        _ffn_kernel,
        grid=(N_ROWS // FFN_ROW_TILE, D_FF // FF_TILE),
        in_specs=[pl.BlockSpec((FFN_ROW_TILE, D_MODEL), lambda i, f: (i, 0)),
                  pl.BlockSpec((1, D_MODEL), lambda i, f: (0, 0)),
                  pl.BlockSpec((D_MODEL, FF_TILE), lambda i, f: (0, f)),
                  pl.BlockSpec((D_MODEL, FF_TILE), lambda i, f: (0, f)),
                  pl.BlockSpec((FF_TILE, D_MODEL), lambda i, f: (f, 0))],
        out_specs=pl.BlockSpec((FFN_ROW_TILE, D_MODEL), lambda i, f: (i, 0)),
        out_shape=jax.ShapeDtypeStruct((N_ROWS, D_MODEL), F32),
        scratch_shapes=[pltpu.VMEM((FFN_ROW_TILE, D_MODEL), BF16), pltpu.VMEM((FFN_ROW_TILE, D_MODEL), F32)],
        compiler_params=_params("parallel", "arbitrary"),
    )(h, gain, wg, wu, wd)


def _rwkv_proj_kernel(h_ref, hp_ref, g_ref, mix_ref, w0_ref, a0_ref, kk_ref, ka_ref,
                      wr_ref, wk_ref, wv_ref, w1_ref, w2_ref, a1_ref, a2_ref, g1_ref, g2_ref,
                      rkv_out, g_out):
    i = pl.program_id(0)
    tiles_per_batch = TP // PROJ_TILE
    r0 = (i % tiles_per_batch) * PROJ_TILE
    local = lax.broadcasted_iota(jnp.int32, (PROJ_TILE, 1), 0)
    lrow = local + r0
    gain = g_ref[...]
    is_pad = jnp.logical_and(lrow >= SEQ, lrow < SEQ + PAD)
    u = jnp.where(is_pad, 0.0, _rms(h_ref[...], gain))
    u_prev_tile = _rms(hp_ref[7:8, :], gain)
    xprev = pltpu.roll(u, 1, 0)
    xprev = jnp.where(local == 0, u_prev_tile, xprev)
    xprev = jnp.where(lrow == SEQ, 0.0, xprev)
    xx = xprev - u
    mix = mix_ref[...]
    lerp = lambda n: (u + xx * mix[n:n + 1, :]).astype(BF16)
    xr, xw, xk, xv, xa, xg = [lerp(n) for n in range(6)]
    r = _dot(xr, wr_ref[...])
    k = _dot(xk, wk_ref[...])
    v = _dot(xv, wv_ref[...])
    lw = _dot(jnp.tanh(_dot(xw, w1_ref[...])).astype(BF16), w2_ref[...])
    z = -(w0_ref[...] + lw)
    softplus = jnp.maximum(z, 0.0) + jnp.log(1.0 + jnp.exp(-jnp.abs(z)))
    w = -softplus - 0.5
    a = _sigmoid(a0_ref[...] + _dot(_dot(xa, a1_ref[...]).astype(BF16), a2_ref[...]))
    g = _dot(_sigmoid(_dot(xg, g1_ref[...])).astype(BF16), g2_ref[...])
    fields = (r, -jnp.exp(w),
              k * (1.0 + (a - 1.0) * ka_ref[...]), v, k * kk_ref[...], a)
    for n, val in enumerate(fields):
        rkv_out[:, n * D_MODEL:(n + 1) * D_MODEL] = val
    g_out[...] = g


def _rwkv_proj(h, gain, mix, w0, a0, k_k, k_a, w_r, w_k, w_v, w1, w2, a1, a2, g1, g2):
    tiles_per_batch = TP // PROJ_TILE
    rows8 = PROJ_TILE // 8

    def prev_map(i):
        b = i // tiles_per_batch
        first = (i % tiles_per_batch) == 0
        return (jnp.where(first, (b * TP + TP - 8) // 8, i * rows8 - 1), 0)

    row = pl.BlockSpec((PROJ_TILE, D_MODEL), lambda i: (i, 0))
    full = lambda a: pl.BlockSpec(a.shape, lambda i: (0,) * a.ndim)
    smalls = (gain, mix, w0, a0, k_k, k_a, w_r, w_k, w_v, w1, w2, a1, a2, g1, g2)
    return pl.pallas_call(
        _rwkv_proj_kernel,
        grid=(N_ROWS // PROJ_TILE,),
        in_specs=[row, pl.BlockSpec((8, D_MODEL), prev_map)] + [full(a) for a in smalls],
        out_specs=[pl.BlockSpec((PROJ_TILE, N_SCAN_IN * D_MODEL), lambda i: (i, 0)), row],
        out_shape=[jax.ShapeDtypeStruct((N_ROWS, N_SCAN_IN * D_MODEL), F32),
                   jax.ShapeDtypeStruct((N_ROWS, D_MODEL), F32)],
        compiler_params=_params("parallel"),
    )(h, h, *smalls)


def _group_masks():
    ri = lax.broadcasted_iota(jnp.int32, (GW, GW), 0) // RWKV_HEAD
    ci = lax.broadcasted_iota(jnp.int32, (GW, GW), 1) // RWKV_HEAD
    return jnp.where(ri == ci, 1.0, 0.0).astype(F32)


def _bd(x, mask):
    return (jnp.concatenate([x] * HEADS_PER_GROUP, axis=0) * mask).astype(BF16)


def _diag_blocks(full, mask):
    m = full * mask
    n = RWKV_HEAD
    return (m[0:n] + m[n:2 * n]) + (m[2 * n:3 * n] + m[3 * n:4 * n])


def _head_sum(x, mask_bf):
    hi = x.astype(BF16)
    lo = (x - hi.astype(F32)).astype(BF16)
    return _dot(hi, mask_bf) + _dot(lo, mask_bf)


def _scan_prep_kernel(x_ref, rk_ref, o_ref, pl_out):
    L, D = CHUNK, D_MODEL
    units = [(b, g) for b in range(BATCH) for g in range(N_GROUPS)]
    un = range(len(units))
    mask = _group_masks()
    mask_bf = mask.astype(BF16)
    ri = lax.broadcasted_iota(jnp.int32, (L, GW), 0)
    ci = lax.broadcasted_iota(jnp.int32, (L, GW), 1) % RWKV_HEAD
    incl = ci <= ri
    strict = ci < ri
    eye = jnp.where(ci == ri, 1.0, 0.0).astype(F32)
    t_r = lax.broadcasted_iota(jnp.int32, (L, L), 0)
    t_c = lax.broadcasted_iota(jnp.int32, (L, L), 1)
    tril = jnp.where(t_c <= t_r, 1.0, 0.0).astype(F32)
    rk_all = rk_ref[...]

    def field(b, n, g):
        return x_ref[b, :, n * D + g * GW:n * D + (g + 1) * GW]

    def put(b, n, g, val):
        o_ref[b, :, n * D + g * GW:n * D + (g + 1) * GW] = val

    at, rt, bt, kt, v, plast = [], [], [], [], [], []
    for b in range(BATCH):
        ld = x_ref[b, :, D:2 * D]
        cs = jnp.dot(tril, ld, precision=lax.Precision.HIGHEST,
                     preferred_element_type=F32)
        p_all = jnp.exp(cs)
        pprev_all = jnp.exp(cs - ld)
        pinv_all = jnp.exp(-cs)
        pl_all = p_all[L - 1:L, :]
        pl_out[b, 0] = pl_all
        for g in range(N_GROUPS):
            sl = slice(g * GW, (g + 1) * GW)
            r, k, vv, kk, a = field(b, 0, g), field(b, 2, g), field(b, 3, g), field(b, 4, g), field(b, 5, g)
            nrm = jnp.sqrt(_head_sum(kk * kk, mask_bf))
            kk = kk / jnp.maximum(nrm, 1e-12)
            at.append(-kk * pprev_all[:, sl])
            bt.append(kk * a * pinv_all[:, sl])
            rt.append(r * p_all[:, sl])
            kt.append(k * pinv_all[:, sl])
            v.append(vv)
            plast.append(pl_all[:, sl])
            put(b, 4, g, _head_sum(r * k * rk_all[:, sl], mask_bf) * vv)

    a_ab, a_ak, a_rb, a_rk = [], [], [], []
    for n in un:
        lhs = jnp.concatenate([at[n], rt[n]], axis=0).astype(BF16)
        rhs = jnp.concatenate([_bd(bt[n], mask), _bd(kt[n], mask)], axis=0)
        big = lax.dot_general(lhs, rhs, NT_DIMS, preferred_element_type=F32)
        a_ab.append(jnp.where(strict, big[:L, :GW], 0.0))
        a_ak.append(jnp.where(strict, big[:L, GW:], 0.0))
        a_rb.append(jnp.where(incl, big[L:, :GW], 0.0))
        a_rk.append(jnp.where(incl, big[L:, GW:], 0.0))

    x = [_dot(a_ab[n].astype(BF16), _bd(a_ab[n], mask)) for n in un]
    inv = [eye + a_ab[n] for n in un]
    for step in range(5):
        for n in un:
            rhs = _bd(x[n], mask)
            if step < 4:
                res = _dot(jnp.concatenate([x[n], inv[n]], axis=0).astype(BF16), rhs)
                x[n] = res[:L]
                inv[n] = inv[n] + res[L:]
            else:
                inv[n] = inv[n] + _dot(inv[n].astype(BF16), rhs)

    av = [_dot(jnp.concatenate([a_ak[n], a_rk[n]], axis=0).astype(BF16), _bd(v[n], mask)) for n in un]
    wu = [_dot(inv[n].astype(BF16), jnp.concatenate([_bd(at[n], mask), _bd(av[n][:L], mask)], axis=1))
          for n in un]
    aw = [_dot(a_rb[n].astype(BF16),
               jnp.concatenate([_bd(wu[n][:, :GW], mask), _bd(wu[n][:, GW:], mask)], axis=1))
          for n in un]
    for n, (b, g) in enumerate(units):
        put(b, 0, g, rt[n] + aw[n][:, :GW])
        put(b, 1, g, av[n][L:] + aw[n][:, GW:])
        bh = (bt[n] * plast[n]).astype(BF16)
        kh = (kt[n] * plast[n]).astype(BF16)
        w_b, u0_b = wu[n][:, :GW].astype(BF16), wu[n][:, GW:].astype(BF16)
        gfull = lax.dot_general(bh, w_b, TN_DIMS, preferred_element_type=F32)
        put(b, 2, g, _diag_blocks(gfull, mask))
        hfull = lax.dot_general(jnp.concatenate([u0_b, v[n].astype(BF16)], axis=0),
                                jnp.concatenate([bh, kh], axis=0), TN_DIMS, preferred_element_type=F32)
        put(b, 3, g, _diag_blocks(hfull, mask))


def _scan_prep(rkv, r_k):
    return pl.pallas_call(
        _scan_prep_kernel,
        grid=(N_CHUNKS,),
        in_specs=[pl.BlockSpec((BATCH, CHUNK, N_SCAN_IN * D_MODEL), lambda c: (0, c, 0)),
                  pl.BlockSpec((1, D_MODEL), lambda c: (0, 0))],
        out_specs=[pl.BlockSpec((BATCH, CHUNK, N_SCAN_MID * D_MODEL), lambda c: (0, c, 0)),
                   pl.BlockSpec((BATCH, 1, 1, D_MODEL), lambda c: (0, c, 0, 0))],
        out_shape=[jax.ShapeDtypeStruct((BATCH, TP, N_SCAN_MID * D_MODEL), F32),
                   jax.ShapeDtypeStruct((BATCH, N_CHUNKS, 1, D_MODEL), F32)],
        compiler_params=_params("parallel"),
    )(rkv, r_k)


def _scan_kernel(x_ref, pl_ref, gw_ref, gb_ref, y_ref, s_scr):
    c = pl.program_id(0)
    D = D_MODEL

    @pl.when(c == 0)
    def _():
        s_scr[...] = jnp.zeros_like(s_scr)

    mask = _group_masks()
    mask_bf = mask.astype(BF16)
    units = [(b, slice(g * GW, (g + 1) * GW)) for b in range(BATCH) for g in range(N_GROUPS)]
    field = lambda b, n, sl: x_ref[b, :, n * D + sl.start:n * D + sl.stop]
    s0 = [s_scr[b, :, sl] for b, sl in units]
    o = [lax.dot_general(field(b, 0, sl).astype(BF16), _bd(s0[n], mask), NT_DIMS,
                         preferred_element_type=F32) + field(b, 1, sl)
         for n, (b, sl) in enumerate(units)]
    sg = [lax.dot_general(s0[n].astype(BF16), _bd(field(b, 2, sl), mask), NT_DIMS,
                          preferred_element_type=F32)
          for n, (b, sl) in enumerate(units)]
    for n, (b, sl) in enumerate(units):
        s_scr[b, :, sl] = s0[n] * pl_ref[b, 0, :, sl] + sg[n] + field(b, 3, sl)
    inv_n = 1.0 / RWKV_HEAD
    mu = [_head_sum(o[n], mask_bf) * inv_n for n in range(len(units))]
    d = [o[n] - mu[n] for n in range(len(units))]
    var = [_head_sum(d[n] * d[n], mask_bf) * inv_n for n in range(len(units))]
    for n, (b, sl) in enumerate(units):
        y_ref[b, :, sl] = (d[n] * lax.rsqrt(var[n] + GN_EPS) * gw_ref[:, sl] + gb_ref[:, sl]
                           + field(b, 4, sl))


def _scan(mid, p_last, gn_w, gn_b):
    phys = lambda c: (c + META_CHUNK0) % N_CHUNKS
    vec = pl.BlockSpec((1, D_MODEL), lambda c: (0, 0))
    return pl.pallas_call(
        _scan_kernel,
        grid=(N_CHUNKS,),
        in_specs=[pl.BlockSpec((BATCH, CHUNK, N_SCAN_MID * D_MODEL), lambda c: (0, phys(c), 0)),
                  pl.BlockSpec((BATCH, 1, 1, D_MODEL), lambda c: (0, phys(c), 0, 0)), vec, vec],
        out_specs=pl.BlockSpec((BATCH, CHUNK, D_MODEL), lambda c: (0, phys(c), 0)),
        out_shape=jax.ShapeDtypeStruct((BATCH, TP, D_MODEL), F32),
        scratch_shapes=[pltpu.VMEM((BATCH, RWKV_HEAD, D_MODEL), F32)],
        compiler_params=_params("arbitrary"),
    )(mid, p_last, gn_w, gn_b)


def _rwkv_out_kernel(y_ref, g_ref, h_ref, wo_ref, gain_ref, wr_ref, h_out, u_out, route_out, cnt_out,
                     carry):
    @pl.when(jnp.logical_and(pl.program_id(0) == 0, pl.program_id(1) == 0))
    def _():
        carry[...] = jnp.zeros_like(carry)

    h = h_ref[0] + _dot((y_ref[0] * g_ref[0]).astype(BF16), wo_ref[...])
    h_out[0] = h
    u = _rms(h, gain_ref[...])
    u_out[0] = u.astype(BF16)
    logits = jnp.dot(u, wr_ref[...], precision=lax.Precision.HIGHEST, preferred_element_type=F32)
    e = jnp.exp(logits - jnp.max(logits, axis=-1, keepdims=True))
    probs = e / jnp.sum(e, axis=-1, keepdims=True)
    idx = lax.broadcasted_iota(jnp.int32, probs.shape, 1).astype(F32)
    m1 = jnp.max(probs, axis=-1, keepdims=True)
    i1 = jnp.min(jnp.where(probs == m1, idx, float(N_EXPERTS)), axis=-1, keepdims=True)
    sel1 = idx == i1
    rest = jnp.where(sel1, -1.0, probs)
    m2 = jnp.max(rest, axis=-1, keepdims=True)
    i2 = jnp.min(jnp.where(rest == m2, idx, float(N_EXPERTS)), axis=-1, keepdims=True)
    sel2 = idx == i2
    onehot = jnp.where(jnp.logical_or(sel1, sel2), 1.0, 0.0).astype(F32)
    tr = lax.broadcasted_iota(jnp.int32, (TOKEN_BLOCK, TOKEN_BLOCK), 0)
    tc = lax.broadcasted_iota(jnp.int32, (TOKEN_BLOCK, TOKEN_BLOCK), 1)
    earlier = _dot(jnp.where(tc < tr, 1.0, 0.0).astype(BF16), onehot.astype(BF16)) + carry[...]
    rank1 = jnp.sum(jnp.where(sel1, earlier, 0.0), axis=-1, keepdims=True)
    rank2 = jnp.sum(jnp.where(sel2, earlier, 0.0), axis=-1, keepdims=True)
    den = m1 + m2
    fields = (i1, i2, rank1, rank2, m1 / den, m2 / den)
    route = jnp.zeros(probs.shape, F32)
    for n, val in enumerate(fields):
        route = jnp.where(idx == float(n), val, route)
    route_out[0] = route
    tile_cnt = jnp.sum(onehot, axis=0, keepdims=True)
    cnt_out[0] = tile_cnt
    carry[...] += tile_cnt


def _rwkv_out(y, g, h, w_o, gain, w_router):
    blocks = SEQ // TOKEN_BLOCK
    row = pl.BlockSpec((1, TOKEN_BLOCK, D_MODEL), lambda b, i: (b, i, 0))
    return pl.pallas_call(
        _rwkv_out_kernel,
        grid=(BATCH, blocks),
        in_specs=[row, row, row,
                  pl.BlockSpec((D_MODEL, D_MODEL), lambda b, i: (0, 0)),
                  pl.BlockSpec((1, D_MODEL), lambda b, i: (0, 0)),
                  pl.BlockSpec((D_MODEL, N_EXPERTS), lambda b, i: (0, 0))],
        out_specs=[row, row,
                   pl.BlockSpec((1, TOKEN_BLOCK, N_EXPERTS), lambda b, i: (b, i, 0)),
                   pl.BlockSpec((1, 1, N_EXPERTS), lambda b, i: (b * blocks + i, 0, 0))],
        out_shape=[jax.ShapeDtypeStruct((BATCH, SEQ, D_MODEL), F32),
                   jax.ShapeDtypeStruct((BATCH, SEQ, D_MODEL), BF16),
                   jax.ShapeDtypeStruct((BATCH, SEQ, N_EXPERTS), F32),
                   jax.ShapeDtypeStruct((BATCH * blocks, 1, N_EXPERTS), F32)],
        scratch_shapes=[pltpu.VMEM((1, N_EXPERTS), F32)],
        compiler_params=_params("arbitrary", "arbitrary"),
    )(y, g, h, w_o, gain, w_router)


def _routing_tables(route, cnt):
    i32 = jnp.int32
    route = route.reshape(N_TOKENS, N_EXPERTS)
    expert = route[:, 0:2].astype(i32)
    rank = route[:, 2:4].astype(i32)
    gate = route[:, 4:6]
    cnt = cnt.reshape(N_TOKEN_BLOCKS, N_EXPERTS).astype(i32)
    counts = jnp.sum(cnt, axis=0)
    tiles_e = (counts + EXPERT_TILE - 1) // EXPERT_TILE
    tile_end = jnp.cumsum(tiles_e)
    n_used = tile_end[-1]
    start_row = (tile_end - tiles_e) * EXPERT_TILE
    pos = jnp.take(start_row, expert) + rank
    tiles = jnp.arange(MAX_EXPERT_TILES, dtype=i32)
    tile_expert = jnp.minimum(jnp.sum(tiles[:, None] >= tile_end[None, :], axis=1), N_EXPERTS - 1).astype(i32)
    tile_rows = jnp.take(counts, tile_expert) - (tiles - jnp.take(tile_end - tiles_e, tile_expert)) * EXPERT_TILE
    tile_subs = jnp.where(tiles < n_used, (jnp.clip(tile_rows, 0, EXPERT_TILE) + SUB_TILE - 1) // SUB_TILE, 0)
    before = jnp.concatenate([jnp.zeros((1, N_EXPERTS), i32), jnp.cumsum(cnt, axis=0)], axis=0)
    sub = jnp.arange(MAX_SUB_TILES, dtype=i32)
    sub_expert = jnp.take(tile_expert, sub * SUB_TILE // EXPERT_TILE)
    sub_used = sub < n_used * (EXPERT_TILE // SUB_TILE)
    base = jnp.take(start_row, sub_expert)[None, :]
    lo = base + jnp.take(before[:-1], sub_expert, axis=1)
    hi = base + jnp.take(before[1:], sub_expert, axis=1)
    hit = ((hi > lo) & (lo < (sub + 1)[None, :] * SUB_TILE) & (hi > sub[None, :] * SUB_TILE)
           & sub_used[None, :])

    def pairs(mat, width):
        n = jnp.sum(mat.astype(i32))
        flat = jnp.nonzero(mat.reshape(-1), size=MAX_PAIRS, fill_value=0)[0].astype(i32)
        flat = jnp.where(jnp.arange(MAX_PAIRS) < n, flat, jnp.take(flat, jnp.maximum(n - 1, 0)))
        return flat // width, flat % width, n.reshape(1).astype(i32)

    by_sub = hit.T
    empty = sub_used & ~jnp.any(by_sub, axis=1)
    by_sub = by_sub.at[:, 0].set(by_sub[:, 0] | empty)
    g_sub, g_blk, g_n = pairs(by_sub, N_TOKEN_BLOCKS)
    c_blk, c_sub, c_n = pairs(hit, MAX_SUB_TILES)
    return dict(pos=pos, gate=gate, tile_expert=tile_expert, n_used=n_used.reshape(1).astype(i32),
                tile_subs=tile_subs.astype(i32),
                g_sub=g_sub, g_blk=g_blk, g_n=g_n, c_blk=c_blk, c_sub=c_sub, c_n=c_n)


def _gather_kernel(sub_ref, blk_ref, n_ref, pos_ref, gate_ref, u_ref, x_out, gs_out):
    s = pl.program_id(0)
    q = sub_ref[s]
    first = jnp.logical_or(s == 0, sub_ref[jnp.maximum(s - 1, 0)] != q)

    @pl.when(s < n_ref[0])
    def _():
        rows = q * SUB_TILE + lax.broadcasted_iota(jnp.int32, (SUB_TILE, 1), 0)
        hit0 = pos_ref[0:1, :] == rows
        hit1 = pos_ref[1:2, :] == rows
        onehot = jnp.where(jnp.logical_or(hit0, hit1), 1.0, 0.0).astype(BF16)
        xs = _dot(onehot, u_ref[...]).astype(BF16)
        gs = jnp.sum(jnp.where(hit0, gate_ref[0:1, :], 0.0) + jnp.where(hit1, gate_ref[1:2, :], 0.0),
                     axis=-1, keepdims=True)

        @pl.when(first)
        def _():
            x_out[...] = xs
            gs_out[...] = gs

        @pl.when(jnp.logical_not(first))
        def _():
            x_out[...] += xs
            gs_out[...] += gs


def _gather(tab, pos_rows, gate_rows, u):
    grid_spec = pltpu.PrefetchScalarGridSpec(
        num_scalar_prefetch=3, grid=(MAX_PAIRS,),
        in_specs=[pl.BlockSpec((2, TOKEN_BLOCK), lambda s, sub, blk, n: (0, blk[s])),
                  pl.BlockSpec((2, TOKEN_BLOCK), lambda s, sub, blk, n: (0, blk[s])),
                  pl.BlockSpec((TOKEN_BLOCK, D_MODEL), lambda s, sub, blk, n: (blk[s], 0))],
        out_specs=[pl.BlockSpec((SUB_TILE, D_MODEL), lambda s, sub, blk, n: (sub[s], 0)),
                   pl.BlockSpec((SUB_TILE, 1), lambda s, sub, blk, n: (sub[s], 0))])
    return pl.pallas_call(
        _gather_kernel, grid_spec=grid_spec,
        out_shape=[jax.ShapeDtypeStruct((MAX_SORTED_ROWS, D_MODEL), BF16),
                   jax.ShapeDtypeStruct((MAX_SORTED_ROWS, 1), F32)],
        compiler_params=_params("arbitrary"),
    )(tab["g_sub"], tab["g_blk"], tab["g_n"], pos_rows, gate_rows, u)


def _expert_kernel(te_ref, nu_ref, ns_ref, x_ref, gs_ref, wg_ref, wu_ref, wd_ref, y_ref, acc):
    i = pl.program_id(0)
    f = pl.program_id(1)
    last = f == pl.num_programs(1) - 1
    n_sub = ns_ref[i]
    subs = EXPERT_TILE // SUB_TILE

    @pl.when(n_sub > 0)
    def _():
        wg = wg_ref[0].astype(BF16)
        wu = wu_ref[0].astype(BF16)
        wd = wd_ref[0].astype(BF16)

        def block(rows):
            x = x_ref[rows, :]
            a = _dot(x, wg)
            b = _dot(x, wu)
            part = _dot((a * _sigmoid(a) * b).astype(BF16), wd)

            @pl.when(f == 0)
            def _():
                acc[rows, :] = part

            @pl.when(f > 0)
            def _():
                acc[rows, :] += part

            @pl.when(last)
            def _():
                y_ref[rows, :] = (acc[rows, :] * gs_ref[rows, :]).astype(BF16)

        @pl.when(n_sub == subs)
        def _():
            block(slice(0, EXPERT_TILE))

        @pl.when(n_sub < subs)
        def _():
            for k in range(subs):
                rows = slice(k * SUB_TILE, (k + 1) * SUB_TILE)
                if k < subs - 1:
                    pl.when(k < n_sub)(functools.partial(block, rows))

                @pl.when(jnp.logical_and(last, k >= n_sub))
                def _():
                    y_ref[rows, :] = jnp.zeros((SUB_TILE, D_MODEL), BF16)


def _experts(tab, x_sorted, gate_sorted, wg, wu, wd):
    n_ff = D_FF // FF_TILE
    tile = lambda i, nu: jnp.minimum(i, nu[0] - 1)
    ff = lambda i, f, nu: jnp.where(i < nu[0], f, n_ff - 1)
    grid_spec = pltpu.PrefetchScalarGridSpec(
        num_scalar_prefetch=3, grid=(MAX_EXPERT_TILES, n_ff),
        in_specs=[pl.BlockSpec((EXPERT_TILE, D_MODEL), lambda i, f, te, nu, ns: (tile(i, nu), 0)),
                  pl.BlockSpec((EXPERT_TILE, 1), lambda i, f, te, nu, ns: (tile(i, nu), 0)),
                  pl.BlockSpec((1, D_MODEL, FF_TILE),
                               lambda i, f, te, nu, ns: (te[tile(i, nu)], 0, ff(i, f, nu))),
                  pl.BlockSpec((1, D_MODEL, FF_TILE),
                               lambda i, f, te, nu, ns: (te[tile(i, nu)], 0, ff(i, f, nu))),
                  pl.BlockSpec((1, FF_TILE, D_MODEL),
                               lambda i, f, te, nu, ns: (te[tile(i, nu)], ff(i, f, nu), 0))],
        out_specs=pl.BlockSpec((EXPERT_TILE, D_MODEL), lambda i, f, te, nu, ns: (tile(i, nu), 0)),
        scratch_shapes=[pltpu.VMEM((EXPERT_TILE, D_MODEL), F32)])
    return pl.pallas_call(
        _expert_kernel, grid_spec=grid_spec,
        out_shape=jax.ShapeDtypeStruct((MAX_SORTED_ROWS, D_MODEL), BF16),
        compiler_params=_params("arbitrary", "arbitrary"),
    )(tab["tile_expert"], tab["n_used"], tab["tile_subs"], x_sorted, gate_sorted, wg, wu, wd)


def _combine_kernel(blk_ref, sub_ref, n_ref, pos_ref, y_ref, h_ref, o_ref):
    s = pl.program_id(0)
    b = blk_ref[s]
    first = jnp.logical_or(s == 0, blk_ref[jnp.maximum(s - 1, 0)] != b)

    @pl.when(s < n_ref[0])
    def _():
        cols = sub_ref[s] * SUB_TILE + lax.broadcasted_iota(jnp.int32, (1, SUB_TILE), 1)
        hit = jnp.logical_or(pos_ref[:, 0:1] == cols, pos_ref[:, 1:2] == cols)
        part = _dot(jnp.where(hit, 1.0, 0.0).astype(BF16), y_ref[...])

        @pl.when(first)
        def _():
            o_ref[...] = h_ref[...] + part

        @pl.when(jnp.logical_not(first))
        def _():
            o_ref[...] += part


def _combine(tab, pos_cols, y_sorted, h):
    grid_spec = pltpu.PrefetchScalarGridSpec(
        num_scalar_prefetch=3, grid=(MAX_PAIRS,),
        in_specs=[pl.BlockSpec((TOKEN_BLOCK, 2), lambda s, blk, sub, n: (blk[s], 0)),
                  pl.BlockSpec((SUB_TILE, D_MODEL), lambda s, blk, sub, n: (sub[s], 0)),
                  pl.BlockSpec((TOKEN_BLOCK, D_MODEL), lambda s, blk, sub, n: (blk[s], 0))],
        out_specs=pl.BlockSpec((TOKEN_BLOCK, D_MODEL), lambda s, blk, sub, n: (blk[s], 0)))
    return pl.pallas_call(
        _combine_kernel, grid_spec=grid_spec,
        out_shape=jax.ShapeDtypeStruct((N_TOKENS, D_MODEL), F32),
        compiler_params=_params("arbitrary"),
    )(tab["c_blk"], tab["c_sub"], tab["c_n"], pos_cols, y_sorted, h)


def kernel(x, meta_tokens, mixer_norm, ffn_norm, attn_w_qkv, attn_q_norm, attn_k_norm, attn_sinks, attn_w_o, rwkv_mix, rwkv_w0, rwkv_w1, rwkv_w2, rwkv_a0, rwkv_a1, rwkv_a2, rwkv_g1, rwkv_g2, rwkv_k_k, rwkv_k_a, rwkv_r_k, rwkv_w_r, rwkv_w_k, rwkv_w_v, rwkv_w_o, rwkv_gn_w, rwkv_gn_b, ffn_w_gate, ffn_w_up, ffn_w_down, moe_router, moe_w_gate, moe_w_up, moe_w_down):
    bf = lambda a: a.astype(BF16)
    vec = lambda a: a.reshape(1, -1).astype(F32)
    meta = jnp.broadcast_to(meta_tokens.astype(F32)[None], (BATCH, N_META, D_MODEL))
    h = jnp.concatenate([x, jnp.zeros((BATCH, PAD, D_MODEL), F32), meta], axis=1)

    qkv = _qkv(h.reshape(N_ROWS, D_MODEL), vec(mixer_norm[0]), bf(attn_w_qkv[0]))
    h = _attention(h, qkv.reshape(BATCH, TP, QKV_DIM), attn_sinks[0].astype(F32),
                   jnp.tile(vec(attn_q_norm[0]), (1, Q_PER_KV)), jnp.tile(vec(attn_k_norm[0]), (1, N_KV_HEADS)),
                   bf(attn_w_o[0]))
    h = _ffn(h.reshape(N_ROWS, D_MODEL), vec(ffn_norm[0]), bf(ffn_w_gate[0]), bf(ffn_w_up[0]),
             bf(ffn_w_down[0]))

    rkv, g = _rwkv_proj(
        h, vec(mixer_norm[1]), rwkv_mix[0], vec(rwkv_w0[0]), vec(rwkv_a0[0]), vec(rwkv_k_k[0]),
        vec(rwkv_k_a[0]), bf(rwkv_w_r[0]), bf(rwkv_w_k[0]), bf(rwkv_w_v[0]), bf(rwkv_w1[0]),
        bf(rwkv_w2[0]), bf(rwkv_a1[0]), bf(rwkv_a2[0]), bf(rwkv_g1[0]), bf(rwkv_g2[0]))
    b3 = lambda t: t.reshape(BATCH, TP, -1)
    mid, p_last = _scan_prep(b3(rkv), vec(rwkv_r_k[0]))
    y = _scan(mid, p_last, vec(rwkv_gn_w[0]), vec(rwkv_gn_b[0]))
    h, u, route, cnt = _rwkv_out(y, b3(g), b3(h), bf(rwkv_w_o[0]), vec(ffn_norm[1]),
                                 moe_router[0].astype(F32))
    tab = _routing_tables(route, cnt)
    x_sorted, gate_sorted = _gather(tab, tab["pos"].T, tab["gate"].T, u.reshape(N_TOKENS, D_MODEL))
    y_sorted = _experts(tab, x_sorted, gate_sorted, moe_w_gate[0], moe_w_up[0], moe_w_down[0])
    out = _combine(tab, tab["pos"], y_sorted, h.reshape(N_TOKENS, D_MODEL))
    return out.reshape(BATCH, SEQ, D_MODEL)
```

```python
import functools

import jax
import jax.numpy as jnp
from jax import lax
from jax.experimental import pallas as pl
from jax.experimental.pallas import tpu as pltpu
from jax.experimental.pallas import tpu_sc as plsc

F32 = jnp.float32
BF16 = jnp.bfloat16

D_MODEL = 1024
BATCH = 2
SEQ = 8192
N_META = 16
BLOCK = 128
PAD = BLOCK - N_META
TP = SEQ + BLOCK
N_ROWS = BATCH * TP
N_BLOCKS = TP // BLOCK
META_BLOCK = N_BLOCKS - 1
HEAD_DIM = 64
N_Q_HEADS = 16
N_KV_HEADS = 4
Q_PER_KV = 4
QKV_DIM = (N_Q_HEADS + 2 * N_KV_HEADS) * HEAD_DIM
RWKV_HEADS = 16
RWKV_HEAD = 64
D_FF = 3584
N_EXPERTS = 8
NORM_EPS = 1e-5
GN_EPS = 64e-5
CHUNK = 64
N_CHUNKS = TP // CHUNK
META_CHUNK0 = SEQ // CHUNK
HEADS_PER_GROUP = 4
GW = HEADS_PER_GROUP * RWKV_HEAD
N_GROUPS = RWKV_HEADS // HEADS_PER_GROUP
N_SCAN_IN = 6
N_SCAN_MID = 5

ROW_TILE = 640
FFN_ROW_TILE = 1280
PROJ_TILE = 320
FF_TILE = 512
N_TOKENS = BATCH * SEQ
TOP_K = 2
TOKEN_BLOCK = 512
N_TOKEN_BLOCKS = N_TOKENS // TOKEN_BLOCK
EXPERT_TILE = 1024
SUB_TILE = 256
MAX_EXPERT_TILES = TOP_K * N_TOKENS // EXPERT_TILE + N_EXPERTS
MAX_SORTED_ROWS = MAX_EXPERT_TILES * EXPERT_TILE
COL_CHUNK = 256
N_COL_CHUNKS = D_MODEL // COL_CHUNK
GATHER_ROWS = 128
VMEM_LIMIT = 56 * 1024 * 1024
NEG = -1e30

NT_DIMS = (((1,), (1,)), ((), ()))
TN_DIMS = (((0,), (0,)), ((), ()))


def _params(*sem):
    return pltpu.CompilerParams(dimension_semantics=sem, vmem_limit_bytes=VMEM_LIMIT)


def _rms(x, gain):
    return x * lax.rsqrt(jnp.mean(x * x, axis=-1, keepdims=True) + NORM_EPS) * gain


def _sigmoid(x):
    return 1.0 / (1.0 + jnp.exp(-x))


def _dot(a, b):
    return jnp.dot(a, b, preferred_element_type=F32)


def _qkv_kernel(h_ref, g_ref, w_ref, o_ref):
    u = _rms(h_ref[...], g_ref[...])
    o_ref[...] = _dot(u.astype(BF16), w_ref[...]).astype(BF16)


def _qkv(h, gain, w):
    return pl.pallas_call(
        _qkv_kernel,
        grid=(N_ROWS // ROW_TILE,),
        in_specs=[pl.BlockSpec((ROW_TILE, D_MODEL), lambda i: (i, 0)),
                  pl.BlockSpec((1, D_MODEL), lambda i: (0, 0)),
                  pl.BlockSpec((D_MODEL, QKV_DIM), lambda i: (0, 0))],
        out_specs=pl.BlockSpec((ROW_TILE, QKV_DIM), lambda i: (i, 0)),
        out_shape=jax.ShapeDtypeStruct((N_ROWS, QKV_DIM), BF16),
        compiler_params=_params("parallel"),
    )(h, gain, w)


def _attn_kernel(sink_ref, h_ref, q_ref, kc_ref, vc_ref, kp_ref, vp_ref, km_ref, vm_ref,
                 qg_ref, kg_ref, wo_ref, o_ref, o_scr):
    j = pl.program_id(1)
    n_keys = 3 * BLOCK
    rowi = lax.broadcasted_iota(jnp.int32, (BLOCK, n_keys), 0)
    col = lax.broadcasted_iota(jnp.int32, (BLOCK, n_keys), 1)
    far = 4 * BLOCK
    is_real = j < META_BLOCK
    meta_off = jnp.where(is_real, -far, 0)
    prev_off = jnp.where(jnp.logical_and(j >= 1, is_real), 0, far)
    cur_off = jnp.where(is_real, 0, far)
    meta_ok = jnp.logical_and(jnp.logical_and(col >= PAD, col < BLOCK), col + meta_off <= rowi)
    prev_ok = jnp.logical_and(jnp.logical_and(col >= BLOCK, col < 2 * BLOCK),
                              col - BLOCK > rowi + prev_off)
    cur_ok = jnp.logical_and(col >= 2 * BLOCK, col - 2 * BLOCK + cur_off <= rowi)
    ok = jnp.logical_or(jnp.logical_or(meta_ok, prev_ok), cur_ok)
    ok4 = jnp.concatenate([ok] * Q_PER_KV, axis=1)

    groups = range(N_KV_HEADS)
    mask_bf = _group_masks().astype(BF16)
    rb = lax.broadcasted_iota(jnp.int32, (Q_PER_KV * n_keys, GW), 0) // n_keys
    lb = lax.broadcasted_iota(jnp.int32, (Q_PER_KV * n_keys, GW), 1) // HEAD_DIM
    block_mask = jnp.where(rb == lb, 1.0, 0.0).astype(BF16)
    sr = lax.broadcasted_iota(jnp.int32, (GW, GW), 0)
    sc = lax.broadcasted_iota(jnp.int32, (GW, GW), 1)
    lane_head = lax.broadcasted_iota(jnp.int32, (1, GW), 1) // HEAD_DIM
    inv_d = 1.0 / HEAD_DIM
    scale = HEAD_DIM ** -0.5

    kall = jnp.concatenate([km_ref[0], kp_ref[0], kc_ref[0]], axis=0).astype(F32)
    vall = jnp.concatenate([vm_ref[0], vp_ref[0], vc_ref[0]], axis=0)
    kss = _dot((kall * kall).astype(BF16), mask_bf)
    kn = (kall * lax.rsqrt(kss * inv_d + NORM_EPS) * kg_ref[...]).astype(BF16)
    q_all = q_ref[0].astype(F32)
    qn = []
    for g in groups:
        qg = q_all[:, g * GW:(g + 1) * GW]
        qss = _dot((qg * qg).astype(BF16), mask_bf)
        qn.append((qg * lax.rsqrt(qss * inv_d + NORM_EPS) * (qg_ref[...] * scale)).astype(BF16))
    sel = [jnp.where(sr == g * HEAD_DIM + sc % HEAD_DIM, 1.0, 0.0).astype(BF16) for g in groups]
    krep = [_dot(kn, sel[g]).astype(BF16) for g in groups]
    vrep = [_dot(vall, sel[g]).astype(BF16) for g in groups]
    bdk = [jnp.concatenate([krep[g]] * Q_PER_KV, axis=0) * block_mask for g in groups]
    rhs = [jnp.concatenate([jnp.concatenate([vrep[g]] * Q_PER_KV, axis=0) * block_mask, block_mask], axis=1)
           for g in groups]
    s = [jnp.where(ok4, lax.dot_general(qn[g], bdk[g], NT_DIMS, preferred_element_type=F32), NEG)
         for g in groups]
    p, sink_den = [], []
    for g in groups:
        parts = []
        sd = jnp.zeros((BLOCK, GW), F32)
        for hh in range(Q_PER_KV):
            seg = s[g][:, hh * n_keys:(hh + 1) * n_keys]
            sink = sink_ref[g * Q_PER_KV + hh]
            m = jnp.maximum(jnp.max(seg, axis=-1, keepdims=True), sink)
            parts.append(jnp.exp(seg - m).astype(BF16))
            sd = sd + jnp.exp(sink - m) * jnp.where(lane_head == hh, 1.0, 0.0)
        p.append(jnp.concatenate(parts, axis=1))
        sink_den.append(sd)
    ov = [_dot(p[g], rhs[g]) for g in groups]
    for g in groups:
        o_scr[:, g * GW:(g + 1) * GW] = ov[g][:, :GW] / (ov[g][:, GW:] + sink_den[g])
    o_ref[0] = h_ref[0] + _dot(o_scr[...].astype(BF16), wo_ref[...])


def _attention(h, qkv, sinks, q_gain, k_gain, w_o):
    kcol, vcol = N_Q_HEADS * HEAD_DIM // 256, N_Q_HEADS * HEAD_DIM // 256 + 1
    kvw = N_KV_HEADS * HEAD_DIM
    prev = lambda j: jnp.maximum(j - 1, 0)
    return pl.pallas_call(
        _attn_kernel,
        grid=(BATCH, N_BLOCKS),
        in_specs=[pl.BlockSpec(memory_space=pltpu.SMEM),
                  pl.BlockSpec((1, BLOCK, D_MODEL), lambda b, j: (b, j, 0)),
                  pl.BlockSpec((1, BLOCK, D_MODEL), lambda b, j: (b, j, 0)),
                  pl.BlockSpec((1, BLOCK, kvw), lambda b, j: (b, j, kcol)),
                  pl.BlockSpec((1, BLOCK, kvw), lambda b, j: (b, j, vcol)),
                  pl.BlockSpec((1, BLOCK, kvw), lambda b, j: (b, prev(j), kcol)),
                  pl.BlockSpec((1, BLOCK, kvw), lambda b, j: (b, prev(j), vcol)),
                  pl.BlockSpec((1, BLOCK, kvw), lambda b, j: (b, META_BLOCK, kcol)),
                  pl.BlockSpec((1, BLOCK, kvw), lambda b, j: (b, META_BLOCK, vcol)),
                  pl.BlockSpec((1, GW), lambda b, j: (0, 0)),
                  pl.BlockSpec((1, GW), lambda b, j: (0, 0)),
                  pl.BlockSpec((D_MODEL, D_MODEL), lambda b, j: (0, 0))],
        out_specs=pl.BlockSpec((1, BLOCK, D_MODEL), lambda b, j: (b, j, 0)),
        out_shape=jax.ShapeDtypeStruct((BATCH, TP, D_MODEL), F32),
        scratch_shapes=[pltpu.VMEM((BLOCK, D_MODEL), F32)],
        compiler_params=_params("parallel", "parallel"),
    )(sinks, h, qkv, qkv, qkv, qkv, qkv, qkv, qkv, q_gain, k_gain, w_o)


def _ffn_kernel(h_ref, g_ref, wg_ref, wu_ref, wd_ref, o_ref, u_scr, acc):
    f = pl.program_id(1)

    @pl.when(f == 0)
    def _():
        u_scr[...] = _rms(h_ref[...], g_ref[...]).astype(BF16)
        acc[...] = jnp.zeros_like(acc)

    u = u_scr[...]
    a = _dot(u, wg_ref[...])
    b = _dot(u, wu_ref[...])
    acc[...] += _dot((a * _sigmoid(a) * b).astype(BF16), wd_ref[...])

    @pl.when(f == pl.num_programs(1) - 1)
    def _():
        o_ref[...] = h_ref[...] + acc[...]


def _ffn(h, gain, wg, wu, wd):
    return pl.pallas_call(
        _ffn_kernel,
        grid=(N_ROWS // FFN_ROW_TILE, D_FF // FF_TILE),
        in_specs=[pl.BlockSpec((FFN_ROW_TILE, D_MODEL), lambda i, f: (i, 0)),
                  pl.BlockSpec((1, D_MODEL), lambda i, f: (0, 0)),
                  pl.BlockSpec((D_MODEL, FF_TILE), lambda i, f: (0, f)),
                  pl.BlockSpec((D_MODEL, FF_TILE), lambda i, f: (0, f)),
                  pl.BlockSpec((FF_TILE, D_MODEL), lambda i, f: (f, 0))],
        out_specs=pl.BlockSpec((FFN_ROW_TILE, D_MODEL), lambda i, f: (i, 0)),
        out_shape=jax.ShapeDtypeStruct((N_ROWS, D_MODEL), F32),
        scratch_shapes=[pltpu.VMEM((FFN_ROW_TILE, D_MODEL), BF16), pltpu.VMEM((FFN_ROW_TILE, D_MODEL), F32)],
        compiler_params=_params("parallel", "arbitrary"),
    )(h, gain, wg, wu, wd)


def _rwkv_proj_kernel(h_ref, hp_ref, g_ref, mix_ref, w0_ref, a0_ref, kk_ref, ka_ref,
                      wr_ref, wk_ref, wv_ref, w1_ref, w2_ref, a1_ref, a2_ref, g1_ref, g2_ref,
                      rkv_out, g_out):
    i = pl.program_id(0)
    tiles_per_batch = TP // PROJ_TILE
    r0 = (i % tiles_per_batch) * PROJ_TILE
    local = lax.broadcasted_iota(jnp.int32, (PROJ_TILE, 1), 0)
    lrow = local + r0
    gain = g_ref[...]
    is_pad = jnp.logical_and(lrow >= SEQ, lrow < SEQ + PAD)
    u = jnp.where(is_pad, 0.0, _rms(h_ref[...], gain))
    u_prev_tile = _rms(hp_ref[7:8, :], gain)
    xprev = pltpu.roll(u, 1, 0)
    xprev = jnp.where(local == 0, u_prev_tile, xprev)
    xprev = jnp.where(lrow == SEQ, 0.0, xprev)
    xx = xprev - u
    mix = mix_ref[...]
    lerp = lambda n: (u + xx * mix[n:n + 1, :]).astype(BF16)
    xr, xw, xk, xv, xa, xg = [lerp(n) for n in range(6)]
    r = _dot(xr, wr_ref[...])
    k = _dot(xk, wk_ref[...])
    v = _dot(xv, wv_ref[...])
    lw = _dot(jnp.tanh(_dot(xw, w1_ref[...])).astype(BF16), w2_ref[...])
    z = -(w0_ref[...] + lw)
    softplus = jnp.maximum(z, 0.0) + jnp.log(1.0 + jnp.exp(-jnp.abs(z)))
    w = -softplus - 0.5
    a = _sigmoid(a0_ref[...] + _dot(_dot(xa, a1_ref[...]).astype(BF16), a2_ref[...]))
    g = _dot(_sigmoid(_dot(xg, g1_ref[...])).astype(BF16), g2_ref[...])
    fields = (r, -jnp.exp(w),
              k * (1.0 + (a - 1.0) * ka_ref[...]), v, k * kk_ref[...], a)
    for n, val in enumerate(fields):
        rkv_out[:, n * D_MODEL:(n + 1) * D_MODEL] = val
    g_out[...] = g


def _rwkv_proj(h, gain, mix, w0, a0, k_k, k_a, w_r, w_k, w_v, w1, w2, a1, a2, g1, g2):
    tiles_per_batch = TP // PROJ_TILE
    rows8 = PROJ_TILE // 8

    def prev_map(i):
        b = i // tiles_per_batch
        first = (i % tiles_per_batch) == 0
        return (jnp.where(first, (b * TP + TP - 8) // 8, i * rows8 - 1), 0)

    row = pl.BlockSpec((PROJ_TILE, D_MODEL), lambda i: (i, 0))
    full = lambda a: pl.BlockSpec(a.shape, lambda i: (0,) * a.ndim)
    smalls = (gain, mix, w0, a0, k_k, k_a, w_r, w_k, w_v, w1, w2, a1, a2, g1, g2)
    return pl.pallas_call(
        _rwkv_proj_kernel,
        grid=(N_ROWS // PROJ_TILE,),
        in_specs=[row, pl.BlockSpec((8, D_MODEL), prev_map)] + [full(a) for a in smalls],
        out_specs=[pl.BlockSpec((PROJ_TILE, N_SCAN_IN * D_MODEL), lambda i: (i, 0)), row],
        out_shape=[jax.ShapeDtypeStruct((N_ROWS, N_SCAN_IN * D_MODEL), F32),
                   jax.ShapeDtypeStruct((N_ROWS, D_MODEL), F32)],
        compiler_params=_params("parallel"),
    )(h, h, *smalls)


def _group_masks():
    ri = lax.broadcasted_iota(jnp.int32, (GW, GW), 0) // RWKV_HEAD
    ci = lax.broadcasted_iota(jnp.int32, (GW, GW), 1) // RWKV_HEAD
    return jnp.where(ri == ci, 1.0, 0.0).astype(F32)


def _bd(x, mask):
    return (jnp.concatenate([x] * HEADS_PER_GROUP, axis=0) * mask).astype(BF16)


def _diag_blocks(full, mask):
    m = full * mask
    n = RWKV_HEAD
    return (m[0:n] + m[n:2 * n]) + (m[2 * n:3 * n] + m[3 * n:4 * n])


def _head_sum(x, mask_bf):
    hi = x.astype(BF16)
    lo = (x - hi.astype(F32)).astype(BF16)
    return _dot(hi, mask_bf) + _dot(lo, mask_bf)


def _scan_prep_kernel(x_ref, rk_ref, o_ref, pl_out):
    L, D = CHUNK, D_MODEL
    units = [(b, g) for b in range(BATCH) for g in range(N_GROUPS)]
    un = range(len(units))
    mask = _group_masks()
    mask_bf = mask.astype(BF16)
    ri = lax.broadcasted_iota(jnp.int32, (L, GW), 0)
    ci = lax.broadcasted_iota(jnp.int32, (L, GW), 1) % RWKV_HEAD
    incl = ci <= ri
    strict = ci < ri
    eye = jnp.where(ci == ri, 1.0, 0.0).astype(F32)
    t_r = lax.broadcasted_iota(jnp.int32, (L, L), 0)
    t_c = lax.broadcasted_iota(jnp.int32, (L, L), 1)
    tril = jnp.where(t_c <= t_r, 1.0, 0.0).astype(F32)
    rk_all = rk_ref[...]

    def field(b, n, g):
        return x_ref[b, :, n * D + g * GW:n * D + (g + 1) * GW]

    def put(b, n, g, val):
        o_ref[b, :, n * D + g * GW:n * D + (g + 1) * GW] = val

    at, rt, bt, kt, v, plast = [], [], [], [], [], []
    for b in range(BATCH):
        ld = x_ref[b, :, D:2 * D]
        cs = jnp.dot(tril, ld, precision=lax.Precision.HIGHEST,
                     preferred_element_type=F32)
        p_all = jnp.exp(cs)
        pprev_all = jnp.exp(cs - ld)
        pinv_all = jnp.exp(-cs)
        pl_all = p_all[L - 1:L, :]
        pl_out[b, 0] = pl_all
        for g in range(N_GROUPS):
            sl = slice(g * GW, (g + 1) * GW)
            r, k, vv, kk, a = field(b, 0, g), field(b, 2, g), field(b, 3, g), field(b, 4, g), field(b, 5, g)
            nrm = jnp.sqrt(_head_sum(kk * kk, mask_bf))
            kk = kk / jnp.maximum(nrm, 1e-12)
            at.append(-kk * pprev_all[:, sl])
            bt.append(kk * a * pinv_all[:, sl])
            rt.append(r * p_all[:, sl])
            kt.append(k * pinv_all[:, sl])
            v.append(vv)
            plast.append(pl_all[:, sl])
            put(b, 4, g, _head_sum(r * k * rk_all[:, sl], mask_bf) * vv)

    a_ab, a_ak, a_rb, a_rk = [], [], [], []
    for n in un:
        lhs = jnp.concatenate([at[n], rt[n]], axis=0).astype(BF16)
        rhs = jnp.concatenate([_bd(bt[n], mask), _bd(kt[n], mask)], axis=0)
        big = lax.dot_general(lhs, rhs, NT_DIMS, preferred_element_type=F32)
        a_ab.append(jnp.where(strict, big[:L, :GW], 0.0))
        a_ak.append(jnp.where(strict, big[:L, GW:], 0.0))
        a_rb.append(jnp.where(incl, big[L:, :GW], 0.0))
        a_rk.append(jnp.where(incl, big[L:, GW:], 0.0))

    x = [_dot(a_ab[n].astype(BF16), _bd(a_ab[n], mask)) for n in un]
    inv = [eye + a_ab[n] for n in un]
    for step in range(5):
        for n in un:
            rhs = _bd(x[n], mask)
            if step < 4:
                res = _dot(jnp.concatenate([x[n], inv[n]], axis=0).astype(BF16), rhs)
                x[n] = res[:L]
                inv[n] = inv[n] + res[L:]
            else:
                inv[n] = inv[n] + _dot(inv[n].astype(BF16), rhs)

    av = [_dot(jnp.concatenate([a_ak[n], a_rk[n]], axis=0).astype(BF16), _bd(v[n], mask)) for n in un]
    wu = [_dot(inv[n].astype(BF16), jnp.concatenate([_bd(at[n], mask), _bd(av[n][:L], mask)], axis=1))
          for n in un]
    aw = [_dot(a_rb[n].astype(BF16),
               jnp.concatenate([_bd(wu[n][:, :GW], mask), _bd(wu[n][:, GW:], mask)], axis=1))
          for n in un]
    for n, (b, g) in enumerate(units):
        put(b, 0, g, rt[n] + aw[n][:, :GW])
        put(b, 1, g, av[n][L:] + aw[n][:, GW:])
        bh = (bt[n] * plast[n]).astype(BF16)
        kh = (kt[n] * plast[n]).astype(BF16)
        w_b, u0_b = wu[n][:, :GW].astype(BF16), wu[n][:, GW:].astype(BF16)
        gfull = lax.dot_general(bh, w_b, TN_DIMS, preferred_element_type=F32)
        put(b, 2, g, _diag_blocks(gfull, mask))
        hfull = lax.dot_general(jnp.concatenate([u0_b, v[n].astype(BF16)], axis=0),
                                jnp.concatenate([bh, kh], axis=0), TN_DIMS, preferred_element_type=F32)
        put(b, 3, g, _diag_blocks(hfull, mask))


def _scan_prep(rkv, r_k):
    return pl.pallas_call(
        _scan_prep_kernel,
        grid=(N_CHUNKS,),
        in_specs=[pl.BlockSpec((BATCH, CHUNK, N_SCAN_IN * D_MODEL), lambda c: (0, c, 0)),
                  pl.BlockSpec((1, D_MODEL), lambda c: (0, 0))],
        out_specs=[pl.BlockSpec((BATCH, CHUNK, N_SCAN_MID * D_MODEL), lambda c: (0, c, 0)),
                   pl.BlockSpec((BATCH, 1, 1, D_MODEL), lambda c: (0, c, 0, 0))],
        out_shape=[jax.ShapeDtypeStruct((BATCH, TP, N_SCAN_MID * D_MODEL), F32),
                   jax.ShapeDtypeStruct((BATCH, N_CHUNKS, 1, D_MODEL), F32)],
        compiler_params=_params("parallel"),
    )(rkv, r_k)


def _scan_kernel(x_ref, pl_ref, gw_ref, gb_ref, y_ref, s_scr):
    c = pl.program_id(0)
    D = D_MODEL

    @pl.when(c == 0)
    def _():
        s_scr[...] = jnp.zeros_like(s_scr)

    mask = _group_masks()
    mask_bf = mask.astype(BF16)
    units = [(b, slice(g * GW, (g + 1) * GW)) for b in range(BATCH) for g in range(N_GROUPS)]
    field = lambda b, n, sl: x_ref[b, :, n * D + sl.start:n * D + sl.stop]
    s0 = [s_scr[b, :, sl] for b, sl in units]
    o = [lax.dot_general(field(b, 0, sl).astype(BF16), _bd(s0[n], mask), NT_DIMS,
                         preferred_element_type=F32) + field(b, 1, sl)
         for n, (b, sl) in enumerate(units)]
    sg = [lax.dot_general(s0[n].astype(BF16), _bd(field(b, 2, sl), mask), NT_DIMS,
                          preferred_element_type=F32)
          for n, (b, sl) in enumerate(units)]
    for n, (b, sl) in enumerate(units):
        s_scr[b, :, sl] = s0[n] * pl_ref[b, 0, :, sl] + sg[n] + field(b, 3, sl)
    inv_n = 1.0 / RWKV_HEAD
    mu = [_head_sum(o[n], mask_bf) * inv_n for n in range(len(units))]
    d = [o[n] - mu[n] for n in range(len(units))]
    var = [_head_sum(d[n] * d[n], mask_bf) * inv_n for n in range(len(units))]
    for n, (b, sl) in enumerate(units):
        y_ref[b, :, sl] = (d[n] * lax.rsqrt(var[n] + GN_EPS) * gw_ref[:, sl] + gb_ref[:, sl]
                           + field(b, 4, sl))


def _scan(mid, p_last, gn_w, gn_b):
    phys = lambda c: (c + META_CHUNK0) % N_CHUNKS
    vec = pl.BlockSpec((1, D_MODEL), lambda c: (0, 0))
    return pl.pallas_call(
        _scan_kernel,
        grid=(N_CHUNKS,),
        in_specs=[pl.BlockSpec((BATCH, CHUNK, N_SCAN_MID * D_MODEL), lambda c: (0, phys(c), 0)),
                  pl.BlockSpec((BATCH, 1, 1, D_MODEL), lambda c: (0, phys(c), 0, 0)), vec, vec],
        out_specs=pl.BlockSpec((BATCH, CHUNK, D_MODEL), lambda c: (0, phys(c), 0)),
        out_shape=jax.ShapeDtypeStruct((BATCH, TP, D_MODEL), F32),
        scratch_shapes=[pltpu.VMEM((BATCH, RWKV_HEAD, D_MODEL), F32)],
        compiler_params=_params("arbitrary"),
    )(mid, p_last, gn_w, gn_b)


def _rwkv_out_kernel(y_ref, g_ref, h_ref, wo_ref, gain_ref, wr_ref, h_out, u_out, route_out, cnt_out,
                     carry):
    @pl.when(jnp.logical_and(pl.program_id(0) == 0, pl.program_id(1) == 0))
    def _():
        carry[...] = jnp.zeros_like(carry)

    h = h_ref[0] + _dot((y_ref[0] * g_ref[0]).astype(BF16), wo_ref[...])
    h_out[0] = h
    u = _rms(h, gain_ref[...])
    for c in range(N_COL_CHUNKS):
        u_out[0, 0, c] = u[:, c * COL_CHUNK:(c + 1) * COL_CHUNK]
    logits = jnp.dot(u, wr_ref[...], precision=lax.Precision.HIGHEST, preferred_element_type=F32)
    e = jnp.exp(logits - jnp.max(logits, axis=-1, keepdims=True))
    probs = e / jnp.sum(e, axis=-1, keepdims=True)
    idx = lax.broadcasted_iota(jnp.int32, probs.shape, 1).astype(F32)
    m1 = jnp.max(probs, axis=-1, keepdims=True)
    i1 = jnp.min(jnp.where(probs == m1, idx, float(N_EXPERTS)), axis=-1, keepdims=True)
    sel1 = idx == i1
    rest = jnp.where(sel1, -1.0, probs)
    m2 = jnp.max(rest, axis=-1, keepdims=True)
    i2 = jnp.min(jnp.where(rest == m2, idx, float(N_EXPERTS)), axis=-1, keepdims=True)
    sel2 = idx == i2
    onehot = jnp.where(jnp.logical_or(sel1, sel2), 1.0, 0.0).astype(F32)
    tr = lax.broadcasted_iota(jnp.int32, (TOKEN_BLOCK, TOKEN_BLOCK), 0)
    tc = lax.broadcasted_iota(jnp.int32, (TOKEN_BLOCK, TOKEN_BLOCK), 1)
    earlier = _dot(jnp.where(tc < tr, 1.0, 0.0).astype(BF16), onehot.astype(BF16)) + carry[...]
    rank1 = jnp.sum(jnp.where(sel1, earlier, 0.0), axis=-1, keepdims=True)
    rank2 = jnp.sum(jnp.where(sel2, earlier, 0.0), axis=-1, keepdims=True)
    den = m1 + m2
    fields = (i1, i2, rank1, rank2, m1 / den, m2 / den)
    route = jnp.zeros(probs.shape, F32)
    for n, val in enumerate(fields):
        route = jnp.where(idx == float(n), val, route)
    route_out[0] = route
    tile_cnt = jnp.sum(onehot, axis=0, keepdims=True)
    cnt_out[0] = tile_cnt
    carry[...] += tile_cnt


def _rwkv_out(y, g, h, w_o, gain, w_router):
    blocks = SEQ // TOKEN_BLOCK
    row = pl.BlockSpec((1, TOKEN_BLOCK, D_MODEL), lambda b, i: (b, i, 0))
    return pl.pallas_call(
        _rwkv_out_kernel,
        grid=(BATCH, blocks),
        in_specs=[row, row, row,
                  pl.BlockSpec((D_MODEL, D_MODEL), lambda b, i: (0, 0)),
                  pl.BlockSpec((1, D_MODEL), lambda b, i: (0, 0)),
                  pl.BlockSpec((D_MODEL, N_EXPERTS), lambda b, i: (0, 0))],
        out_specs=[row, pl.BlockSpec((1, 1, N_COL_CHUNKS, TOKEN_BLOCK, COL_CHUNK), lambda b, i: (b, i, 0, 0, 0)),
                   pl.BlockSpec((1, TOKEN_BLOCK, N_EXPERTS), lambda b, i: (b, i, 0)),
                   pl.BlockSpec((1, 1, N_EXPERTS), lambda b, i: (b * blocks + i, 0, 0))],
        out_shape=[jax.ShapeDtypeStruct((BATCH, SEQ, D_MODEL), F32),
                   jax.ShapeDtypeStruct((BATCH, blocks, N_COL_CHUNKS, TOKEN_BLOCK, COL_CHUNK), F32),
                   jax.ShapeDtypeStruct((BATCH, SEQ, N_EXPERTS), F32),
                   jax.ShapeDtypeStruct((BATCH * blocks, 1, N_EXPERTS), F32)],
        scratch_shapes=[pltpu.VMEM((1, N_EXPERTS), F32)],
        compiler_params=_params("arbitrary", "arbitrary"),
    )(y, g, h, w_o, gain, w_router)


def _routing_tables(route, cnt):
    i32 = jnp.int32
    route = route.reshape(N_TOKENS, N_EXPERTS)
    expert = route[:, 0:2].astype(i32)
    rank = route[:, 2:4].astype(i32)
    gate = route[:, 4:6]
    counts = jnp.sum(cnt.reshape(-1, N_EXPERTS).astype(i32), axis=0)
    tiles_e = (counts + EXPERT_TILE - 1) // EXPERT_TILE
    tile_end = jnp.cumsum(tiles_e)
    n_used = tile_end[-1]
    start_row = (tile_end - tiles_e) * EXPERT_TILE
    pos = (jnp.take(start_row, expert) + rank).T.reshape(-1)
    tokens = jnp.tile(jnp.arange(N_TOKENS, dtype=i32), TOP_K)
    row_token = jnp.zeros((MAX_SORTED_ROWS,), i32).at[pos].set(tokens)
    tiles = jnp.arange(MAX_EXPERT_TILES, dtype=i32)
    tile_expert = jnp.minimum(jnp.sum(tiles[:, None] >= tile_end[None, :], axis=1), N_EXPERTS - 1).astype(i32)
    tile_rows = jnp.take(counts, tile_expert) - (tiles - jnp.take(tile_end - tiles_e, tile_expert)) * EXPERT_TILE
    tile_subs = jnp.where(tiles < n_used, (jnp.clip(tile_rows, 0, EXPERT_TILE) + SUB_TILE - 1) // SUB_TILE, 0)
    chunk = jnp.arange(N_COL_CHUNKS, dtype=i32)[None, :, None]

    def piece(rows, tile_rows):
        return ((rows // tile_rows) * N_COL_CHUNKS + chunk) * tile_rows + rows % tile_rows

    x_src = piece(row_token.reshape(MAX_EXPERT_TILES, 1, EXPERT_TILE), TOKEN_BLOCK).reshape(-1)
    y_src = piece(pos.reshape(TOP_K * N_TOKEN_BLOCKS, 1, TOKEN_BLOCK), EXPERT_TILE).reshape(-1)
    return dict(x_src=x_src, y_src=y_src, gate=gate, tile_expert=tile_expert,
                n_used=n_used.reshape(1).astype(i32), tile_subs=tile_subs.astype(i32))


def _row_gather(x, indices):
    m = indices.shape[0]
    mesh = plsc.VectorSubcoreMesh(core_axis_name="c", subcore_axis_name="s")

    @pl.kernel(out_type=jax.ShapeDtypeStruct((m, COL_CHUNK), x.dtype), mesh=mesh)
    def gather(x_hbm, i_hbm, o_hbm):
        def body(i_vmem, o_vmem):
            pltpu.sync_copy(x_hbm.at[i_vmem.at[0]], o_vmem)

        pltpu.emit_pipeline(
            body, grid=(m // GATHER_ROWS,),
            in_specs=[pl.BlockSpec((1, GATHER_ROWS), lambda i: (0, i))],
            out_specs=[pl.BlockSpec((GATHER_ROWS, COL_CHUNK), lambda i: (i, 0))],
            core_axis_name=("c", "s"),
            dimension_semantics=(pltpu.PARALLEL,),
        )(i_hbm, o_hbm)

    return gather(x, indices.reshape(1, m))


def _expert_kernel(te_ref, nu_ref, ns_ref, x_ref, wg_ref, wu_ref, wd_ref, y_ref, xb):
    i = pl.program_id(0)
    f = pl.program_id(1)
    n_sub = ns_ref[i]
    subs = EXPERT_TILE // SUB_TILE

    @pl.when(n_sub > 0)
    def _():
        @pl.when(f == 0)
        def _():
            for c in range(N_COL_CHUNKS):
                xb[:, c * COL_CHUNK:(c + 1) * COL_CHUNK] = x_ref[c * EXPERT_TILE:(c + 1) * EXPERT_TILE, :].astype(BF16)
            y_ref[...] = jnp.zeros_like(y_ref)

        wg = wg_ref[0].astype(BF16)
        wu = wu_ref[0].astype(BF16)
        wd = wd_ref[0].astype(BF16)

        def block(rows):
            x = xb[rows, :]
            a = _dot(x, wg)
            b = _dot(x, wu)
            part = _dot((a * _sigmoid(a) * b).astype(BF16), wd)
            for c in range(N_COL_CHUNKS):
                y_ref[c * EXPERT_TILE + rows.start:c * EXPERT_TILE + rows.stop, :] += part[:, c * COL_CHUNK:(c + 1) * COL_CHUNK]

        @pl.when(n_sub == subs)
        def _():
            block(slice(0, EXPERT_TILE))

        @pl.when(n_sub < subs)
        def _():
            for k in range(subs - 1):
                pl.when(k < n_sub)(functools.partial(block, slice(k * SUB_TILE, (k + 1) * SUB_TILE)))


def _experts(tab, x_sorted, wg, wu, wd):
    n_ff = D_FF // FF_TILE
    tile = lambda i, nu: jnp.minimum(i, nu[0] - 1)
    ff = lambda i, f, nu: jnp.where(i < nu[0], f, n_ff - 1)
    grid_spec = pltpu.PrefetchScalarGridSpec(
        num_scalar_prefetch=3, grid=(MAX_EXPERT_TILES, n_ff),
        in_specs=[pl.BlockSpec((N_COL_CHUNKS * EXPERT_TILE, COL_CHUNK), lambda i, f, te, nu, ns: (tile(i, nu), 0)),
                  pl.BlockSpec((1, D_MODEL, FF_TILE),
                               lambda i, f, te, nu, ns: (te[tile(i, nu)], 0, ff(i, f, nu))),
                  pl.BlockSpec((1, D_MODEL, FF_TILE),
                               lambda i, f, te, nu, ns: (te[tile(i, nu)], 0, ff(i, f, nu))),
                  pl.BlockSpec((1, FF_TILE, D_MODEL),
                               lambda i, f, te, nu, ns: (te[tile(i, nu)], ff(i, f, nu), 0))],
        out_specs=pl.BlockSpec((N_COL_CHUNKS * EXPERT_TILE, COL_CHUNK), lambda i, f, te, nu, ns: (tile(i, nu), 0)),
        scratch_shapes=[pltpu.VMEM((EXPERT_TILE, D_MODEL), BF16)])
    return pl.pallas_call(
        _expert_kernel, grid_spec=grid_spec,
        out_shape=jax.ShapeDtypeStruct((N_COL_CHUNKS * MAX_SORTED_ROWS, COL_CHUNK), F32),
        compiler_params=_params("arbitrary", "arbitrary"),
    )(tab["tile_expert"], tab["n_used"], tab["tile_subs"], x_sorted, wg, wu, wd)


def _combine_kernel(h_ref, y_ref, g_ref, o_ref):
    g = g_ref[...]
    for c in range(N_COL_CHUNKS):
        cols = slice(c * COL_CHUNK, (c + 1) * COL_CHUNK)
        o_ref[:, cols] = h_ref[:, cols] + g[:, 0:1] * y_ref[0, 0, c] + g[:, 1:2] * y_ref[1, 0, c]


def _combine(h, y_pair, gate):
    row = pl.BlockSpec((TOKEN_BLOCK, D_MODEL), lambda i: (i, 0))
    return pl.pallas_call(
        _combine_kernel,
        grid=(N_TOKEN_BLOCKS,),
        in_specs=[row, pl.BlockSpec((TOP_K, 1, N_COL_CHUNKS, TOKEN_BLOCK, COL_CHUNK), lambda i: (0, i, 0, 0, 0)),
                  pl.BlockSpec((TOKEN_BLOCK, TOP_K), lambda i: (i, 0))],
        out_specs=row,
        out_shape=jax.ShapeDtypeStruct((N_TOKENS, D_MODEL), F32),
        compiler_params=_params("parallel"),
    )(h, y_pair, gate)


def kernel(x, meta_tokens, mixer_norm, ffn_norm, attn_w_qkv, attn_q_norm, attn_k_norm, attn_sinks, attn_w_o, rwkv_mix, rwkv_w0, rwkv_w1, rwkv_w2, rwkv_a0, rwkv_a1, rwkv_a2, rwkv_g1, rwkv_g2, rwkv_k_k, rwkv_k_a, rwkv_r_k, rwkv_w_r, rwkv_w_k, rwkv_w_v, rwkv_w_o, rwkv_gn_w, rwkv_gn_b, ffn_w_gate, ffn_w_up, ffn_w_down, moe_router, moe_w_gate, moe_w_up, moe_w_down):
    bf = lambda a: a.astype(BF16)
    vec = lambda a: a.reshape(1, -1).astype(F32)
    meta = jnp.broadcast_to(meta_tokens.astype(F32)[None], (BATCH, N_META, D_MODEL))
    h = jnp.concatenate([x, jnp.zeros((BATCH, PAD, D_MODEL), F32), meta], axis=1)

    qkv = _qkv(h.reshape(N_ROWS, D_MODEL), vec(mixer_norm[0]), bf(attn_w_qkv[0]))
    h = _attention(h, qkv.reshape(BATCH, TP, QKV_DIM), attn_sinks[0].astype(F32),
                   jnp.tile(vec(attn_q_norm[0]), (1, Q_PER_KV)), jnp.tile(vec(attn_k_norm[0]), (1, N_KV_HEADS)),
                   bf(attn_w_o[0]))
    h = _ffn(h.reshape(N_ROWS, D_MODEL), vec(ffn_norm[0]), bf(ffn_w_gate[0]), bf(ffn_w_up[0]),
             bf(ffn_w_down[0]))

    rkv, g = _rwkv_proj(
        h, vec(mixer_norm[1]), rwkv_mix[0], vec(rwkv_w0[0]), vec(rwkv_a0[0]), vec(rwkv_k_k[0]),
        vec(rwkv_k_a[0]), bf(rwkv_w_r[0]), bf(rwkv_w_k[0]), bf(rwkv_w_v[0]), bf(rwkv_w1[0]),
        bf(rwkv_w2[0]), bf(rwkv_a1[0]), bf(rwkv_a2[0]), bf(rwkv_g1[0]), bf(rwkv_g2[0]))
    b3 = lambda t: t.reshape(BATCH, TP, -1)
    mid, p_last = _scan_prep(b3(rkv), vec(rwkv_r_k[0]))
    y = _scan(mid, p_last, vec(rwkv_gn_w[0]), vec(rwkv_gn_b[0]))
    h, u, route, cnt = _rwkv_out(y, b3(g), b3(h), bf(rwkv_w_o[0]), vec(ffn_norm[1]),
                                 moe_router[0].astype(F32))
    tab = _routing_tables(route, cnt)
    x_sorted = _row_gather(u.reshape(-1, COL_CHUNK), tab["x_src"])
    y_sorted = _experts(tab, x_sorted, moe_w_gate[0], moe_w_up[0], moe_w_down[0])
    y_pair = _row_gather(y_sorted, tab["y_src"]).reshape(TOP_K, N_TOKEN_BLOCKS, N_COL_CHUNKS, TOKEN_BLOCK, COL_CHUNK)
    out = _combine(h.reshape(N_TOKENS, D_MODEL), y_pair, tab["gate"])
    return out.reshape(BATCH, SEQ, D_MODEL)
```

```python
import functools

import jax
import jax.numpy as jnp
from jax import lax
from jax.experimental import pallas as pl
from jax.experimental.pallas import tpu as pltpu
from jax.experimental.pallas import tpu_sc as plsc

F32 = jnp.float32
BF16 = jnp.bfloat16

D_MODEL = 1024
BATCH = 2
SEQ = 8192
N_META = 16
BLOCK = 128
PAD = BLOCK - N_META
TP = SEQ + BLOCK
N_ROWS = BATCH * TP
N_BLOCKS = TP // BLOCK
META_BLOCK = N_BLOCKS - 1
HEAD_DIM = 64
N_Q_HEADS = 16
N_KV_HEADS = 4
Q_PER_KV = 4
QKV_DIM = (N_Q_HEADS + 2 * N_KV_HEADS) * HEAD_DIM
RWKV_HEADS = 16
RWKV_HEAD = 64
D_FF = 3584
N_EXPERTS = 8
NORM_EPS = 1e-5
GN_EPS = 64e-5
CHUNK = 64
N_CHUNKS = TP // CHUNK
META_CHUNK0 = SEQ // CHUNK
HEADS_PER_GROUP = 4
GW = HEADS_PER_GROUP * RWKV_HEAD
N_GROUPS = RWKV_HEADS // HEADS_PER_GROUP
N_SCAN_IN = 6
N_SCAN_MID = 5

ROW_TILE = 640
FFN_ROW_TILE = 1280
PROJ_TILE = 320
FF_TILE = 512
N_TOKENS = BATCH * SEQ
TOP_K = 2
TOKEN_BLOCK = 512
N_TOKEN_BLOCKS = N_TOKENS // TOKEN_BLOCK
EXPERT_TILE = 1024
SUB_TILE = 256
MAX_EXPERT_TILES = TOP_K * N_TOKENS // EXPERT_TILE + N_EXPERTS
MAX_SORTED_ROWS = MAX_EXPERT_TILES * EXPERT_TILE
COL_CHUNK = 256
N_COL_CHUNKS = D_MODEL // COL_CHUNK
GATHER_ROWS = 128
VMEM_LIMIT = 56 * 1024 * 1024
NEG = -1e30

NT_DIMS = (((1,), (1,)), ((), ()))
TN_DIMS = (((0,), (0,)), ((), ()))


def _params(*sem):
    return pltpu.CompilerParams(dimension_semantics=sem, vmem_limit_bytes=VMEM_LIMIT)


def _rms(x, gain):
    return x * lax.rsqrt(jnp.mean(x * x, axis=-1, keepdims=True) + NORM_EPS) * gain


def _sigmoid(x):
    return 1.0 / (1.0 + jnp.exp(-x))


def _dot(a, b):
    return jnp.dot(a, b, preferred_element_type=F32)


def _qkv_kernel(h_ref, g_ref, w_ref, o_ref):
    u = _rms(h_ref[...], g_ref[...])
    o_ref[...] = _dot(u.astype(BF16), w_ref[...]).astype(BF16)


def _qkv(h, gain, w):
    return pl.pallas_call(
        _qkv_kernel,
        grid=(N_ROWS // ROW_TILE,),
        in_specs=[pl.BlockSpec((ROW_TILE, D_MODEL), lambda i: (i, 0)),
                  pl.BlockSpec((1, D_MODEL), lambda i: (0, 0)),
                  pl.BlockSpec((D_MODEL, QKV_DIM), lambda i: (0, 0))],
        out_specs=pl.BlockSpec((ROW_TILE, QKV_DIM), lambda i: (i, 0)),
        out_shape=jax.ShapeDtypeStruct((N_ROWS, QKV_DIM), BF16),
        compiler_params=_params("parallel"),
    )(h, gain, w)


def _attn_kernel(sink_ref, h_ref, q_ref, kc_ref, vc_ref, kp_ref, vp_ref, km_ref, vm_ref,
                 qg_ref, kg_ref, wo_ref, o_ref, o_scr):
    j = pl.program_id(1)
    n_keys = 3 * BLOCK
    rowi = lax.broadcasted_iota(jnp.int32, (BLOCK, n_keys), 0)
    col = lax.broadcasted_iota(jnp.int32, (BLOCK, n_keys), 1)
    far = 4 * BLOCK
    is_real = j < META_BLOCK
    meta_off = jnp.where(is_real, -far, 0)
    prev_off = jnp.where(jnp.logical_and(j >= 1, is_real), 0, far)
    cur_off = jnp.where(is_real, 0, far)
    meta_ok = jnp.logical_and(jnp.logical_and(col >= PAD, col < BLOCK), col + meta_off <= rowi)
    prev_ok = jnp.logical_and(jnp.logical_and(col >= BLOCK, col < 2 * BLOCK),
                              col - BLOCK > rowi + prev_off)
    cur_ok = jnp.logical_and(col >= 2 * BLOCK, col - 2 * BLOCK + cur_off <= rowi)
    ok = jnp.logical_or(jnp.logical_or(meta_ok, prev_ok), cur_ok)
    ok4 = jnp.concatenate([ok] * Q_PER_KV, axis=1)

    groups = range(N_KV_HEADS)
    mask_bf = _group_masks().astype(BF16)
    rb = lax.broadcasted_iota(jnp.int32, (Q_PER_KV * n_keys, GW), 0) // n_keys
    lb = lax.broadcasted_iota(jnp.int32, (Q_PER_KV * n_keys, GW), 1) // HEAD_DIM
    block_mask = jnp.where(rb == lb, 1.0, 0.0).astype(BF16)
    sr = lax.broadcasted_iota(jnp.int32, (GW, GW), 0)
    sc = lax.broadcasted_iota(jnp.int32, (GW, GW), 1)
    lane_head = lax.broadcasted_iota(jnp.int32, (1, GW), 1) // HEAD_DIM
    inv_d = 1.0 / HEAD_DIM
    scale = HEAD_DIM ** -0.5

    kall = jnp.concatenate([km_ref[0], kp_ref[0], kc_ref[0]], axis=0).astype(F32)
    vall = jnp.concatenate([vm_ref[0], vp_ref[0], vc_ref[0]], axis=0)
    kss = _dot((kall * kall).astype(BF16), mask_bf)
    kn = (kall * lax.rsqrt(kss * inv_d + NORM_EPS) * kg_ref[...]).astype(BF16)
    q_all = q_ref[0].astype(F32)
    qn = []
    for g in groups:
        qg = q_all[:, g * GW:(g + 1) * GW]
        qss = _dot((qg * qg).astype(BF16), mask_bf)
        qn.append((qg * lax.rsqrt(qss * inv_d + NORM_EPS) * (qg_ref[...] * scale)).astype(BF16))
    sel = [jnp.where(sr == g * HEAD_DIM + sc % HEAD_DIM, 1.0, 0.0).astype(BF16) for g in groups]
    krep = [_dot(kn, sel[g]).astype(BF16) for g in groups]
    vrep = [_dot(vall, sel[g]).astype(BF16) for g in groups]
    bdk = [jnp.concatenate([krep[g]] * Q_PER_KV, axis=0) * block_mask for g in groups]
    rhs = [jnp.concatenate([jnp.concatenate([vrep[g]] * Q_PER_KV, axis=0) * block_mask, block_mask], axis=1)
           for g in groups]
    s = [jnp.where(ok4, lax.dot_general(qn[g], bdk[g], NT_DIMS, preferred_element_type=F32), NEG)
         for g in groups]
    p, sink_den = [], []
    for g in groups:
        parts = []
        sd = jnp.zeros((BLOCK, GW), F32)
        for hh in range(Q_PER_KV):
            seg = s[g][:, hh * n_keys:(hh + 1) * n_keys]
            sink = sink_ref[g * Q_PER_KV + hh]
            m = jnp.maximum(jnp.max(seg, axis=-1, keepdims=True), sink)
            parts.append(jnp.exp(seg - m).astype(BF16))
            sd = sd + jnp.exp(sink - m) * jnp.where(lane_head == hh, 1.0, 0.0)
        p.append(jnp.concatenate(parts, axis=1))
        sink_den.append(sd)
    ov = [_dot(p[g], rhs[g]) for g in groups]
    for g in groups:
        o_scr[:, g * GW:(g + 1) * GW] = ov[g][:, :GW] / (ov[g][:, GW:] + sink_den[g])
    o_ref[0] = h_ref[0] + _dot(o_scr[...].astype(BF16), wo_ref[...])


def _attention(h, qkv, sinks, q_gain, k_gain, w_o):
    kcol, vcol = N_Q_HEADS * HEAD_DIM // 256, N_Q_HEADS * HEAD_DIM // 256 + 1
    kvw = N_KV_HEADS * HEAD_DIM
    prev = lambda j: jnp.maximum(j - 1, 0)
    return pl.pallas_call(
        _attn_kernel,
        grid=(BATCH, N_BLOCKS),
        in_specs=[pl.BlockSpec(memory_space=pltpu.SMEM),
                  pl.BlockSpec((1, BLOCK, D_MODEL), lambda b, j: (b, j, 0)),
                  pl.BlockSpec((1, BLOCK, D_MODEL), lambda b, j: (b, j, 0)),
                  pl.BlockSpec((1, BLOCK, kvw), lambda b, j: (b, j, kcol)),
                  pl.BlockSpec((1, BLOCK, kvw), lambda b, j: (b, j, vcol)),
                  pl.BlockSpec((1, BLOCK, kvw), lambda b, j: (b, prev(j), kcol)),
                  pl.BlockSpec((1, BLOCK, kvw), lambda b, j: (b, prev(j), vcol)),
                  pl.BlockSpec((1, BLOCK, kvw), lambda b, j: (b, META_BLOCK, kcol)),
                  pl.BlockSpec((1, BLOCK, kvw), lambda b, j: (b, META_BLOCK, vcol)),
                  pl.BlockSpec((1, GW), lambda b, j: (0, 0)),
                  pl.BlockSpec((1, GW), lambda b, j: (0, 0)),
                  pl.BlockSpec((D_MODEL, D_MODEL), lambda b, j: (0, 0))],
        out_specs=pl.BlockSpec((1, BLOCK, D_MODEL), lambda b, j: (b, j, 0)),
        out_shape=jax.ShapeDtypeStruct((BATCH, TP, D_MODEL), F32),
        scratch_shapes=[pltpu.VMEM((BLOCK, D_MODEL), F32)],
        compiler_params=_params("parallel", "parallel"),
    )(sinks, h, qkv, qkv, qkv, qkv, qkv, qkv, qkv, q_gain, k_gain, w_o)


def _ffn_kernel(h_ref, g_ref, wg_ref, wu_ref, wd_ref, o_ref, u_scr, acc):
    f = pl.program_id(1)

    @pl.when(f == 0)
    def _():
        u_scr[...] = _rms(h_ref[...], g_ref[...]).astype(BF16)
        acc[...] = jnp.zeros_like(acc)

    u = u_scr[...]
    a = _dot(u, wg_ref[...])
    b = _dot(u, wu_ref[...])
    acc[...] += _dot((a * _sigmoid(a) * b).astype(BF16), wd_ref[...])

    @pl.when(f == pl.num_programs(1) - 1)
    def _():
        o_ref[...] = h_ref[...] + acc[...]


def _ffn(h, gain, wg, wu, wd):
    return pl.pallas_call(
        _ffn_kernel,
        grid=(N_ROWS // FFN_ROW_TILE, D_FF // FF_TILE),
        in_specs=[pl.BlockSpec((FFN_ROW_TILE, D_MODEL), lambda i, f: (i, 0)),
                  pl.BlockSpec((1, D_MODEL), lambda i, f: (0, 0)),
                  pl.BlockSpec((D_MODEL, FF_TILE), lambda i, f: (0, f)),
                  pl.BlockSpec((D_MODEL, FF_TILE), lambda i, f: (0, f)),
                  pl.BlockSpec((FF_TILE, D_MODEL), lambda i, f: (f, 0))],
        out_specs=pl.BlockSpec((FFN_ROW_TILE, D_MODEL), lambda i, f: (i, 0)),
        out_shape=jax.ShapeDtypeStruct((N_ROWS, D_MODEL), F32),
        scratch_shapes=[pltpu.VMEM((FFN_ROW_TILE, D_MODEL), BF16), pltpu.VMEM((FFN_ROW_TILE, D_MODEL), F32)],
        compiler_params=_params("parallel", "arbitrary"),
    )(h, gain, wg, wu, wd)


def _rwkv_proj_kernel(h_ref, hp_ref, g_ref, mix_ref, w0_ref, a0_ref, kk_ref, ka_ref,
                      wr_ref, wk_ref, wv_ref, w1_ref, w2_ref, a1_ref, a2_ref, g1_ref, g2_ref,
                      rkv_out, g_out):
    i = pl.program_id(0)
    tiles_per_batch = TP // PROJ_TILE
    r0 = (i % tiles_per_batch) * PROJ_TILE
    local = lax.broadcasted_iota(jnp.int32, (PROJ_TILE, 1), 0)
    lrow = local + r0
    gain = g_ref[...]
    is_pad = jnp.logical_and(lrow >= SEQ, lrow < SEQ + PAD)
    u = jnp.where(is_pad, 0.0, _rms(h_ref[...], gain))
    u_prev_tile = _rms(hp_ref[7:8, :], gain)
    xprev = pltpu.roll(u, 1, 0)
    xprev = jnp.where(local == 0, u_prev_tile, xprev)
    xprev = jnp.where(lrow == SEQ, 0.0, xprev)
    xx = xprev - u
    mix = mix_ref[...]
    lerp = lambda n: (u + xx * mix[n:n + 1, :]).astype(BF16)
    xr, xw, xk, xv, xa, xg = [lerp(n) for n in range(6)]
    r = _dot(xr, wr_ref[...])
    k = _dot(xk, wk_ref[...])
    v = _dot(xv, wv_ref[...])
    lw = _dot(jnp.tanh(_dot(xw, w1_ref[...])).astype(BF16), w2_ref[...])
    z = -(w0_ref[...] + lw)
    softplus = jnp.maximum(z, 0.0) + jnp.log(1.0 + jnp.exp(-jnp.abs(z)))
    w = -softplus - 0.5
    a = _sigmoid(a0_ref[...] + _dot(_dot(xa, a1_ref[...]).astype(BF16), a2_ref[...]))
    g = _dot(_sigmoid(_dot(xg, g1_ref[...])).astype(BF16), g2_ref[...])
    fields = (r, -jnp.exp(w),
              k * (1.0 + (a - 1.0) * ka_ref[...]), v, k * kk_ref[...], a)
    for n, val in enumerate(fields):
        rkv_out[:, n * D_MODEL:(n + 1) * D_MODEL] = val
    g_out[...] = g


def _rwkv_proj(h, gain, mix, w0, a0, k_k, k_a, w_r, w_k, w_v, w1, w2, a1, a2, g1, g2):
    tiles_per_batch = TP // PROJ_TILE
    rows8 = PROJ_TILE // 8

    def prev_map(i):
        b = i // tiles_per_batch
        first = (i % tiles_per_batch) == 0
        return (jnp.where(first, (b * TP + TP - 8) // 8, i * rows8 - 1), 0)

    row = pl.BlockSpec((PROJ_TILE, D_MODEL), lambda i: (i, 0))
    full = lambda a: pl.BlockSpec(a.shape, lambda i: (0,) * a.ndim)
    smalls = (gain, mix, w0, a0, k_k, k_a, w_r, w_k, w_v, w1, w2, a1, a2, g1, g2)
    return pl.pallas_call(
        _rwkv_proj_kernel,
        grid=(N_ROWS // PROJ_TILE,),
        in_specs=[row, pl.BlockSpec((8, D_MODEL), prev_map)] + [full(a) for a in smalls],
        out_specs=[pl.BlockSpec((PROJ_TILE, N_SCAN_IN * D_MODEL), lambda i: (i, 0)), row],
        out_shape=[jax.ShapeDtypeStruct((N_ROWS, N_SCAN_IN * D_MODEL), F32),
                   jax.ShapeDtypeStruct((N_ROWS, D_MODEL), F32)],
        compiler_params=_params("parallel"),
    )(h, h, *smalls)


def _group_masks():
    ri = lax.broadcasted_iota(jnp.int32, (GW, GW), 0) // RWKV_HEAD
    ci = lax.broadcasted_iota(jnp.int32, (GW, GW), 1) // RWKV_HEAD
    return jnp.where(ri == ci, 1.0, 0.0).astype(F32)


def _bd(x, mask):
    return jnp.concatenate([x.astype(BF16)] * HEADS_PER_GROUP, axis=0) * mask.astype(BF16)


def _diag_blocks(full, mask):
    m = full * mask
    n = RWKV_HEAD
    return (m[0:n] + m[n:2 * n]) + (m[2 * n:3 * n] + m[3 * n:4 * n])


def _head_sum(x, mask_bf):
    hi = x.astype(BF16)
    lo = (x - hi.astype(F32)).astype(BF16)
    return _dot(hi, mask_bf) + _dot(lo, mask_bf)


def _scan_prep_kernel(x_ref, rk_ref, o_ref, pl_out):
    L, D = CHUNK, D_MODEL
    units = [(b, g) for b in range(BATCH) for g in range(N_GROUPS)]
    un = range(len(units))
    mask = _group_masks()
    mask_bf = mask.astype(BF16)
    ri = lax.broadcasted_iota(jnp.int32, (L, GW), 0)
    ci = lax.broadcasted_iota(jnp.int32, (L, GW), 1) % RWKV_HEAD
    incl = ci <= ri
    strict = ci < ri
    eye = jnp.where(ci == ri, 1.0, 0.0).astype(F32)
    t_r = lax.broadcasted_iota(jnp.int32, (L, L), 0)
    t_c = lax.broadcasted_iota(jnp.int32, (L, L), 1)
    tril = jnp.where(t_c <= t_r, 1.0, 0.0).astype(BF16)
    rk_all = rk_ref[...]

    def field(b, n, g):
        return x_ref[b, :, n * D + g * GW:n * D + (g + 1) * GW]

    def put(b, n, g, val):
        o_ref[b, :, n * D + g * GW:n * D + (g + 1) * GW] = val

    at, rt, bt, kt, v, plast = [], [], [], [], [], []
    for b in range(BATCH):
        ld = x_ref[b, :, D:2 * D]
        hi = ld.astype(BF16)
        rest = ld - hi.astype(F32)
        mid = rest.astype(BF16)
        lo = (rest - mid.astype(F32)).astype(BF16)
        cs = _dot(tril, hi) + _dot(tril, mid) + _dot(tril, lo)
        p_all = jnp.exp(cs)
        pprev_all = jnp.exp(cs - ld)
        pinv_all = jnp.exp(-cs)
        pl_all = p_all[L - 1:L, :]
        pl_out[b, 0] = pl_all
        for g in range(N_GROUPS):
            sl = slice(g * GW, (g + 1) * GW)
            r, k, vv, kk, a = field(b, 0, g), field(b, 2, g), field(b, 3, g), field(b, 4, g), field(b, 5, g)
            nrm = jnp.sqrt(_dot((kk * kk).astype(BF16), mask_bf))
            kk = kk / jnp.maximum(nrm, 1e-12)
            at.append(-kk * pprev_all[:, sl])
            bt.append(kk * a * pinv_all[:, sl])
            rt.append(r * p_all[:, sl])
            kt.append(k * pinv_all[:, sl])
            v.append(vv)
            plast.append(pl_all[:, sl])
            put(b, 4, g, _dot((r * k * rk_all[:, sl]).astype(BF16), mask_bf) * vv)

    a_ab, a_ak, a_rb, a_rk = [], [], [], []
    for n in un:
        lhs = jnp.concatenate([at[n], rt[n]], axis=0).astype(BF16)
        rhs = jnp.concatenate([_bd(bt[n], mask), _bd(kt[n], mask)], axis=0)
        big = lax.dot_general(lhs, rhs, NT_DIMS, preferred_element_type=F32)
        a_ab.append(jnp.where(strict, big[:L, :GW], 0.0))
        a_ak.append(jnp.where(strict, big[:L, GW:], 0.0))
        a_rb.append(jnp.where(incl, big[L:, :GW], 0.0))
        a_rk.append(jnp.where(incl, big[L:, GW:], 0.0))

    x = [_dot(a_ab[n].astype(BF16), _bd(a_ab[n], mask)) for n in un]
    inv = [eye + a_ab[n] for n in un]
    for step in range(5):
        for n in un:
            rhs = _bd(x[n], mask)
            if step < 4:
                res = _dot(jnp.concatenate([x[n], inv[n]], axis=0).astype(BF16), rhs)
                x[n] = res[:L]
                inv[n] = inv[n] + res[L:]
            else:
                inv[n] = inv[n] + _dot(inv[n].astype(BF16), rhs)

    av = [_dot(jnp.concatenate([a_ak[n], a_rk[n]], axis=0).astype(BF16), _bd(v[n], mask)) for n in un]
    wu = [_dot(inv[n].astype(BF16), jnp.concatenate([_bd(at[n], mask), _bd(av[n][:L], mask)], axis=1))
          for n in un]
    aw = [_dot(a_rb[n].astype(BF16),
               jnp.concatenate([_bd(wu[n][:, :GW], mask), _bd(wu[n][:, GW:], mask)], axis=1))
          for n in un]
    for n, (b, g) in enumerate(units):
        put(b, 0, g, rt[n] + aw[n][:, :GW])
        put(b, 1, g, av[n][L:] + aw[n][:, GW:])
        bh = (bt[n] * plast[n]).astype(BF16)
        kh = (kt[n] * plast[n]).astype(BF16)
        w_b, u0_b = wu[n][:, :GW].astype(BF16), wu[n][:, GW:].astype(BF16)
        gfull = lax.dot_general(bh, w_b, TN_DIMS, preferred_element_type=F32)
        put(b, 2, g, _diag_blocks(gfull, mask))
        hfull = lax.dot_general(jnp.concatenate([u0_b, v[n].astype(BF16)], axis=0),
                                jnp.concatenate([bh, kh], axis=0), TN_DIMS, preferred_element_type=F32)
        put(b, 3, g, _diag_blocks(hfull, mask))


def _scan_prep(rkv, r_k):
    return pl.pallas_call(
        _scan_prep_kernel,
        grid=(N_CHUNKS,),
        in_specs=[pl.BlockSpec((BATCH, CHUNK, N_SCAN_IN * D_MODEL), lambda c: (0, c, 0)),
                  pl.BlockSpec((1, D_MODEL), lambda c: (0, 0))],
        out_specs=[pl.BlockSpec((BATCH, CHUNK, N_SCAN_MID * D_MODEL), lambda c: (0, c, 0)),
                   pl.BlockSpec((BATCH, 1, 1, D_MODEL), lambda c: (0, c, 0, 0))],
        out_shape=[jax.ShapeDtypeStruct((BATCH, TP, N_SCAN_MID * D_MODEL), F32),
                   jax.ShapeDtypeStruct((BATCH, N_CHUNKS, 1, D_MODEL), F32)],
        compiler_params=_params("parallel"),
    )(rkv, r_k)


def _scan_kernel(x_ref, pl_ref, gw_ref, gb_ref, y_ref, s_scr):
    c = pl.program_id(0)
    D = D_MODEL

    @pl.when(c == 0)
    def _():
        s_scr[...] = jnp.zeros_like(s_scr)

    mask = _group_masks()
    mask_bf = mask.astype(BF16)
    units = [(b, slice(g * GW, (g + 1) * GW)) for b in range(BATCH) for g in range(N_GROUPS)]
    field = lambda b, n, sl: x_ref[b, :, n * D + sl.start:n * D + sl.stop]
    s0 = [s_scr[b, :, sl] for b, sl in units]
    o = [lax.dot_general(field(b, 0, sl).astype(BF16), _bd(s0[n], mask), NT_DIMS,
                         preferred_element_type=F32) + field(b, 1, sl)
         for n, (b, sl) in enumerate(units)]
    sg = [lax.dot_general(s0[n].astype(BF16), _bd(field(b, 2, sl), mask), NT_DIMS,
                          preferred_element_type=F32)
          for n, (b, sl) in enumerate(units)]
    for n, (b, sl) in enumerate(units):
        s_scr[b, :, sl] = s0[n] * pl_ref[b, 0, :, sl] + sg[n] + field(b, 3, sl)
    inv_n = 1.0 / RWKV_HEAD
    mu = [_head_sum(o[n], mask_bf) * inv_n for n in range(len(units))]
    d = [o[n] - mu[n] for n in range(len(units))]
    var = [_head_sum(d[n] * d[n], mask_bf) * inv_n for n in range(len(units))]
    for n, (b, sl) in enumerate(units):
        y_ref[b, :, sl] = (d[n] * lax.rsqrt(var[n] + GN_EPS) * gw_ref[:, sl] + gb_ref[:, sl]
                           + field(b, 4, sl))


def _scan(mid, p_last, gn_w, gn_b):
    phys = lambda c: (c + META_CHUNK0) % N_CHUNKS
    vec = pl.BlockSpec((1, D_MODEL), lambda c: (0, 0))
    return pl.pallas_call(
        _scan_kernel,
        grid=(N_CHUNKS,),
        in_specs=[pl.BlockSpec((BATCH, CHUNK, N_SCAN_MID * D_MODEL), lambda c: (0, phys(c), 0)),
                  pl.BlockSpec((BATCH, 1, 1, D_MODEL), lambda c: (0, phys(c), 0, 0)), vec, vec],
        out_specs=pl.BlockSpec((BATCH, CHUNK, D_MODEL), lambda c: (0, phys(c), 0)),
        out_shape=jax.ShapeDtypeStruct((BATCH, TP, D_MODEL), F32),
        scratch_shapes=[pltpu.VMEM((BATCH, RWKV_HEAD, D_MODEL), F32)],
        compiler_params=_params("arbitrary"),
    )(mid, p_last, gn_w, gn_b)


def _rwkv_out_kernel(y_ref, g_ref, h_ref, wo_ref, gain_ref, wr_ref, h_out, u_out, route_out, cnt_out,
                     carry):
    @pl.when(jnp.logical_and(pl.program_id(0) == 0, pl.program_id(1) == 0))
    def _():
        carry[...] = jnp.zeros_like(carry)

    h = h_ref[0] + _dot((y_ref[0] * g_ref[0]).astype(BF16), wo_ref[...])
    h_out[0] = h
    u = _rms(h, gain_ref[...])
    for c in range(N_COL_CHUNKS):
        u_out[0, 0, c] = u[:, c * COL_CHUNK:(c + 1) * COL_CHUNK]
    u_hi = u.astype(BF16)
    u_lo = (u - u_hi.astype(F32)).astype(BF16)
    logits = _dot(jnp.concatenate([u_hi, u_lo, u_hi], axis=1), wr_ref[...])
    e = jnp.exp(logits - jnp.max(logits, axis=-1, keepdims=True))
    probs = e / jnp.sum(e, axis=-1, keepdims=True)
    idx = lax.broadcasted_iota(jnp.int32, probs.shape, 1).astype(F32)
    m1 = jnp.max(probs, axis=-1, keepdims=True)
    i1 = jnp.min(jnp.where(probs == m1, idx, float(N_EXPERTS)), axis=-1, keepdims=True)
    sel1 = idx == i1
    rest = jnp.where(sel1, -1.0, probs)
    m2 = jnp.max(rest, axis=-1, keepdims=True)
    i2 = jnp.min(jnp.where(rest == m2, idx, float(N_EXPERTS)), axis=-1, keepdims=True)
    sel2 = idx == i2
    onehot = jnp.where(jnp.logical_or(sel1, sel2), 1.0, 0.0).astype(F32)
    tr = lax.broadcasted_iota(jnp.int32, (TOKEN_BLOCK, TOKEN_BLOCK), 0)
    tc = lax.broadcasted_iota(jnp.int32, (TOKEN_BLOCK, TOKEN_BLOCK), 1)
    earlier = _dot(jnp.where(tc < tr, 1.0, 0.0).astype(BF16), onehot.astype(BF16)) + carry[...]
    rank1 = jnp.sum(jnp.where(sel1, earlier, 0.0), axis=-1, keepdims=True)
    rank2 = jnp.sum(jnp.where(sel2, earlier, 0.0), axis=-1, keepdims=True)
    den = m1 + m2
    fields = (i1, i2, rank1, rank2, m1 / den, m2 / den)
    route = jnp.zeros(probs.shape, F32)
    for n, val in enumerate(fields):
        route = jnp.where(idx == float(n), val, route)
    route_out[0] = route
    tile_cnt = jnp.sum(onehot, axis=0, keepdims=True)
    cnt_out[0] = tile_cnt
    carry[...] += tile_cnt


def _rwkv_out(y, g, h, w_o, gain, w_router):
    blocks = SEQ // TOKEN_BLOCK
    row = pl.BlockSpec((1, TOKEN_BLOCK, D_MODEL), lambda b, i: (b, i, 0))
    return pl.pallas_call(
        _rwkv_out_kernel,
        grid=(BATCH, blocks),
        in_specs=[row, row, row,
                  pl.BlockSpec((D_MODEL, D_MODEL), lambda b, i: (0, 0)),
                  pl.BlockSpec((1, D_MODEL), lambda b, i: (0, 0)),
                  pl.BlockSpec((3 * D_MODEL, N_EXPERTS), lambda b, i: (0, 0))],
        out_specs=[row, pl.BlockSpec((1, 1, N_COL_CHUNKS, TOKEN_BLOCK, COL_CHUNK), lambda b, i: (b, i, 0, 0, 0)),
                   pl.BlockSpec((1, TOKEN_BLOCK, N_EXPERTS), lambda b, i: (b, i, 0)),
                   pl.BlockSpec((1, 1, N_EXPERTS), lambda b, i: (b * blocks + i, 0, 0))],
        out_shape=[jax.ShapeDtypeStruct((BATCH, SEQ, D_MODEL), F32),
                   jax.ShapeDtypeStruct((BATCH, blocks, N_COL_CHUNKS, TOKEN_BLOCK, COL_CHUNK), F32),
                   jax.ShapeDtypeStruct((BATCH, SEQ, N_EXPERTS), F32),
                   jax.ShapeDtypeStruct((BATCH * blocks, 1, N_EXPERTS), F32)],
        scratch_shapes=[pltpu.VMEM((1, N_EXPERTS), F32)],
        compiler_params=_params("arbitrary", "arbitrary"),
    )(y, g, h, w_o, gain, w_router)


def _routing_tables(route, cnt):
    i32 = jnp.int32
    route = route.reshape(N_TOKENS, N_EXPERTS)
    expert = route[:, 0:2].astype(i32)
    rank = route[:, 2:4].astype(i32)
    gate = route[:, 4:6]
    counts = jnp.sum(cnt.reshape(-1, N_EXPERTS).astype(i32), axis=0)
    tiles_e = (counts + EXPERT_TILE - 1) // EXPERT_TILE
    tile_end = jnp.cumsum(tiles_e)
    n_used = tile_end[-1]
    start_row = (tile_end - tiles_e) * EXPERT_TILE
    pos = (jnp.take(start_row, expert) + rank).T
    tiles = jnp.arange(MAX_EXPERT_TILES, dtype=i32)
    tile_expert = jnp.minimum(jnp.sum(tiles[:, None] >= tile_end[None, :], axis=1), N_EXPERTS - 1).astype(i32)
    tile_rows = jnp.take(counts, tile_expert) - (tiles - jnp.take(tile_end - tiles_e, tile_expert)) * EXPERT_TILE
    tile_rows = jnp.where(tiles < n_used, jnp.clip(tile_rows, 0, EXPERT_TILE), 0)
    pos = pos.reshape(TOP_K * N_TOKEN_BLOCKS, 1, TOKEN_BLOCK)
    chunk = jnp.arange(N_COL_CHUNKS, dtype=i32)[None, :, None]
    sorted_piece = (((pos // EXPERT_TILE) * N_COL_CHUNKS + chunk) * EXPERT_TILE + pos % EXPERT_TILE).reshape(-1)
    return dict(sorted_piece=sorted_piece, gate=gate, tile_expert=tile_expert,
                n_used=n_used.reshape(1).astype(i32), tile_rows=tile_rows.astype(i32))


def _row_gather(x, indices):
    m = indices.shape[0]
    mesh = plsc.VectorSubcoreMesh(core_axis_name="c", subcore_axis_name="s")

    @pl.kernel(out_type=jax.ShapeDtypeStruct((m, COL_CHUNK), x.dtype), mesh=mesh)
    def gather(x_hbm, i_hbm, o_hbm):
        def body(i_vmem, o_vmem):
            pltpu.sync_copy(x_hbm.at[i_vmem.at[0]], o_vmem)

        pltpu.emit_pipeline(
            body, grid=(m // GATHER_ROWS,),
            in_specs=[pl.BlockSpec((1, GATHER_ROWS), lambda i: (0, i))],
            out_specs=[pl.BlockSpec((GATHER_ROWS, COL_CHUNK), lambda i: (i, 0))],
            core_axis_name=("c", "s"),
            dimension_semantics=(pltpu.PARALLEL,),
        )(i_hbm, o_hbm)

    return gather(x, indices.reshape(1, m))


def _row_scatter(x, indices, out_rows):
    m = indices.shape[0]
    x_blocks = x.shape[0] // GATHER_ROWS
    mesh = plsc.VectorSubcoreMesh(core_axis_name="c", subcore_axis_name="s")

    @pl.kernel(out_type=jax.ShapeDtypeStruct((out_rows, COL_CHUNK), x.dtype), mesh=mesh)
    def scatter(x_hbm, i_hbm, o_hbm):
        def body(x_vmem, i_vmem):
            pltpu.sync_copy(x_vmem, o_hbm.at[i_vmem.at[0]])

        pltpu.emit_pipeline(
            body, grid=(m // GATHER_ROWS,),
            in_specs=[pl.BlockSpec((GATHER_ROWS, COL_CHUNK), lambda i: (i % x_blocks, 0)),
                      pl.BlockSpec((1, GATHER_ROWS), lambda i: (0, i))],
            out_specs=[],
            core_axis_name=("c", "s"),
            dimension_semantics=(pltpu.PARALLEL,),
        )(x_hbm, i_hbm)

    return scatter(x, indices.reshape(1, m))


def _expert_kernel(te_ref, nu_ref, nr_ref, x_ref, wg_ref, wu_ref, wd_ref, y_ref, xb):
    i = pl.program_id(0)
    f = pl.program_id(1)
    n_rows = nr_ref[i]
    n_sub = (n_rows + SUB_TILE - 1) // SUB_TILE
    subs = EXPERT_TILE // SUB_TILE

    @pl.when(n_sub > 0)
    def _():
        @pl.when(f == 0)
        def _():
            real = lax.broadcasted_iota(jnp.int32, (EXPERT_TILE, 1), 0) < n_rows
            for c in range(N_COL_CHUNKS):
                piece = jnp.where(real, x_ref[c * EXPERT_TILE:(c + 1) * EXPERT_TILE, :], 0.0)
                xb[:, c * COL_CHUNK:(c + 1) * COL_CHUNK] = piece.astype(BF16)
            y_ref[...] = jnp.zeros_like(y_ref)

        wg = wg_ref[0].astype(BF16)
        wu = wu_ref[0].astype(BF16)
        wd = wd_ref[0].astype(BF16)

        def block(rows):
            x = xb[rows, :]
            a = _dot(x, wg)
            b = _dot(x, wu)
            part = _dot((a * _sigmoid(a) * b).astype(BF16), wd)
            for c in range(N_COL_CHUNKS):
                y_ref[c * EXPERT_TILE + rows.start:c * EXPERT_TILE + rows.stop, :] += part[:, c * COL_CHUNK:(c + 1) * COL_CHUNK]

        @pl.when(n_sub == subs)
        def _():
            block(slice(0, EXPERT_TILE))

        @pl.when(n_sub < subs)
        def _():
            for k in range(subs - 1):
                pl.when(k < n_sub)(functools.partial(block, slice(k * SUB_TILE, (k + 1) * SUB_TILE)))


def _experts(tab, x_sorted, wg, wu, wd):
    n_ff = D_FF // FF_TILE
    tile = lambda i, nu: jnp.minimum(i, nu[0] - 1)
    ff = lambda i, f, nu: jnp.where(i < nu[0], f, n_ff - 1)
    grid_spec = pltpu.PrefetchScalarGridSpec(
        num_scalar_prefetch=3, grid=(MAX_EXPERT_TILES, n_ff),
        in_specs=[pl.BlockSpec((N_COL_CHUNKS * EXPERT_TILE, COL_CHUNK), lambda i, f, te, nu, ns: (tile(i, nu), 0)),
                  pl.BlockSpec((1, D_MODEL, FF_TILE),
                               lambda i, f, te, nu, ns: (te[tile(i, nu)], 0, ff(i, f, nu))),
                  pl.BlockSpec((1, D_MODEL, FF_TILE),
                               lambda i, f, te, nu, ns: (te[tile(i, nu)], 0, ff(i, f, nu))),
                  pl.BlockSpec((1, FF_TILE, D_MODEL),
                               lambda i, f, te, nu, ns: (te[tile(i, nu)], ff(i, f, nu), 0))],
        out_specs=pl.BlockSpec((N_COL_CHUNKS * EXPERT_TILE, COL_CHUNK), lambda i, f, te, nu, ns: (tile(i, nu), 0)),
        scratch_shapes=[pltpu.VMEM((EXPERT_TILE, D_MODEL), BF16)])
    return pl.pallas_call(
        _expert_kernel, grid_spec=grid_spec,
        out_shape=jax.ShapeDtypeStruct((N_COL_CHUNKS * MAX_SORTED_ROWS, COL_CHUNK), F32),
        compiler_params=_params("arbitrary", "arbitrary"),
    )(tab["tile_expert"], tab["n_used"], tab["tile_rows"], x_sorted, wg, wu, wd)


def _combine_kernel(h_ref, y_ref, g_ref, o_ref):
    g = g_ref[...]
    for c in range(N_COL_CHUNKS):
        cols = slice(c * COL_CHUNK, (c + 1) * COL_CHUNK)
        o_ref[:, cols] = h_ref[:, cols] + g[:, 0:1] * y_ref[0, 0, c] + g[:, 1:2] * y_ref[1, 0, c]


def _combine(h, y_pair, gate):
    row = pl.BlockSpec((TOKEN_BLOCK, D_MODEL), lambda i: (i, 0))
    return pl.pallas_call(
        _combine_kernel,
        grid=(N_TOKEN_BLOCKS,),
        in_specs=[row, pl.BlockSpec((TOP_K, 1, N_COL_CHUNKS, TOKEN_BLOCK, COL_CHUNK), lambda i: (0, i, 0, 0, 0)),
                  pl.BlockSpec((TOKEN_BLOCK, TOP_K), lambda i: (i, 0))],
        out_specs=row,
        out_shape=jax.ShapeDtypeStruct((N_TOKENS, D_MODEL), F32),
        compiler_params=_params("parallel"),
    )(h, y_pair, gate)


def kernel(x, meta_tokens, mixer_norm, ffn_norm, attn_w_qkv, attn_q_norm, attn_k_norm, attn_sinks, attn_w_o, rwkv_mix, rwkv_w0, rwkv_w1, rwkv_w2, rwkv_a0, rwkv_a1, rwkv_a2, rwkv_g1, rwkv_g2, rwkv_k_k, rwkv_k_a, rwkv_r_k, rwkv_w_r, rwkv_w_k, rwkv_w_v, rwkv_w_o, rwkv_gn_w, rwkv_gn_b, ffn_w_gate, ffn_w_up, ffn_w_down, moe_router, moe_w_gate, moe_w_up, moe_w_down):
    bf = lambda a: a.astype(BF16)
    vec = lambda a: a.reshape(1, -1).astype(F32)
    meta = jnp.broadcast_to(meta_tokens.astype(F32)[None], (BATCH, N_META, D_MODEL))
    h = jnp.concatenate([x, jnp.zeros((BATCH, PAD, D_MODEL), F32), meta], axis=1)

    qkv = _qkv(h.reshape(N_ROWS, D_MODEL), vec(mixer_norm[0]), bf(attn_w_qkv[0]))
    h = _attention(h, qkv.reshape(BATCH, TP, QKV_DIM), attn_sinks[0].astype(F32),
                   jnp.tile(vec(attn_q_norm[0]), (1, Q_PER_KV)), jnp.tile(vec(attn_k_norm[0]), (1, N_KV_HEADS)),
                   bf(attn_w_o[0]))
    h = _ffn(h.reshape(N_ROWS, D_MODEL), vec(ffn_norm[0]), bf(ffn_w_gate[0]), bf(ffn_w_up[0]),
             bf(ffn_w_down[0]))

    rkv, g = _rwkv_proj(
        h, vec(mixer_norm[1]), rwkv_mix[0], vec(rwkv_w0[0]), vec(rwkv_a0[0]), vec(rwkv_k_k[0]),
        vec(rwkv_k_a[0]), bf(rwkv_w_r[0]), bf(rwkv_w_k[0]), bf(rwkv_w_v[0]), bf(rwkv_w1[0]),
        bf(rwkv_w2[0]), bf(rwkv_a1[0]), bf(rwkv_a2[0]), bf(rwkv_g1[0]), bf(rwkv_g2[0]))
    b3 = lambda t: t.reshape(BATCH, TP, -1)
    mid, p_last = _scan_prep(b3(rkv), vec(rwkv_r_k[0]))
    y = _scan(mid, p_last, vec(rwkv_gn_w[0]), vec(rwkv_gn_b[0]))
    w_router = moe_router[0].astype(F32)
    wr_hi = bf(w_router)
    wr_lo = bf(w_router - wr_hi.astype(F32))
    h, u, route, cnt = _rwkv_out(y, b3(g), b3(h), bf(rwkv_w_o[0]), vec(ffn_norm[1]),
                                 jnp.concatenate([wr_hi, wr_hi, wr_lo], axis=0))
    tab = _routing_tables(route, cnt)
    x_sorted = _row_scatter(u.reshape(-1, COL_CHUNK), tab["sorted_piece"], N_COL_CHUNKS * MAX_SORTED_ROWS)
    y_sorted = _experts(tab, x_sorted, moe_w_gate[0], moe_w_up[0], moe_w_down[0])
    y_pair = _row_gather(y_sorted, tab["sorted_piece"]).reshape(TOP_K, N_TOKEN_BLOCKS, N_COL_CHUNKS, TOKEN_BLOCK, COL_CHUNK)
    out = _combine(h.reshape(N_TOKENS, D_MODEL), y_pair, tab["gate"])
    return out.reshape(BATCH, SEQ, D_MODEL)
```

```python
import functools

import jax
import jax.numpy as jnp
from jax import lax
from jax.experimental import pallas as pl
from jax.experimental.pallas import tpu as pltpu
from jax.experimental.pallas import tpu_sc as plsc

F32 = jnp.float32
BF16 = jnp.bfloat16

D_MODEL = 1024
BATCH = 2
SEQ = 8192
N_META = 16
BLOCK = 128
PAD = BLOCK - N_META
TP = SEQ + BLOCK
N_ROWS = BATCH * TP
N_BLOCKS = TP // BLOCK
META_BLOCK = N_BLOCKS - 1
HEAD_DIM = 64
N_Q_HEADS = 16
N_KV_HEADS = 4
Q_PER_KV = 4
QKV_DIM = (N_Q_HEADS + 2 * N_KV_HEADS) * HEAD_DIM
RWKV_HEADS = 16
RWKV_HEAD = 64
D_FF = 3584
N_EXPERTS = 8
NORM_EPS = 1e-5
GN_EPS = 64e-5
CHUNK = 64
N_CHUNKS = TP // CHUNK
META_CHUNK0 = SEQ // CHUNK
HEADS_PER_GROUP = 4
GW = HEADS_PER_GROUP * RWKV_HEAD
N_GROUPS = RWKV_HEADS // HEADS_PER_GROUP
N_SCAN_IN = 6
N_SCAN_MID = 5

FFN_ROW_TILE = 1280
PROJ_TILE = 320
FF_TILE = 512
N_TOKENS = BATCH * SEQ
TOP_K = 2
TOKEN_BLOCK = 512
N_TOKEN_BLOCKS = N_TOKENS // TOKEN_BLOCK
EXPERT_TILE = 1024
SUB_TILE = 256
MAX_EXPERT_TILES = TOP_K * N_TOKENS // EXPERT_TILE + N_EXPERTS
MAX_SORTED_ROWS = MAX_EXPERT_TILES * EXPERT_TILE
COL_CHUNK = 256
N_COL_CHUNKS = D_MODEL // COL_CHUNK
GATHER_ROWS = 128
VMEM_LIMIT = 56 * 1024 * 1024
NEG = -1e30

NT_DIMS = (((1,), (1,)), ((), ()))
TN_DIMS = (((0,), (0,)), ((), ()))


def _params(*sem):
    return pltpu.CompilerParams(dimension_semantics=sem, vmem_limit_bytes=VMEM_LIMIT)


def _rms(x, gain):
    return x * lax.rsqrt(jnp.mean(x * x, axis=-1, keepdims=True) + NORM_EPS) * gain


def _sigmoid(x):
    return 1.0 / (1.0 + jnp.exp(-x))


def _dot(a, b):
    return jnp.dot(a, b, preferred_element_type=F32)


def _qkv_kernel(h_ref, g_ref, w_ref, o_ref):
    u = _rms(h_ref[...], g_ref[...])
    o_ref[...] = _dot(u.astype(BF16), w_ref[...]).astype(BF16)


def _qkv(h, gain, w, tile):
    rows = h.shape[0]
    return pl.pallas_call(
        _qkv_kernel,
        grid=(rows // tile,),
        in_specs=[pl.BlockSpec((tile, D_MODEL), lambda i: (i, 0)),
                  pl.BlockSpec((1, D_MODEL), lambda i: (0, 0)),
                  pl.BlockSpec((D_MODEL, QKV_DIM), lambda i: (0, 0))],
        out_specs=pl.BlockSpec((tile, QKV_DIM), lambda i: (i, 0)),
        out_shape=jax.ShapeDtypeStruct((rows, QKV_DIM), BF16),
        compiler_params=_params("parallel"),
    )(h, gain, w)


def _attn_kernel(sink_ref, x_ref, hm_ref, q_ref, kc_ref, vc_ref, kp_ref, vp_ref, qm_ref, km_ref, vm_ref,
                 qg_ref, kg_ref, wo_ref, o_ref, o_scr):
    j = pl.program_id(1)
    n_keys = 3 * BLOCK
    rowi = lax.broadcasted_iota(jnp.int32, (BLOCK, n_keys), 0)
    col = lax.broadcasted_iota(jnp.int32, (BLOCK, n_keys), 1)
    far = 4 * BLOCK
    is_real = j < META_BLOCK
    meta_off = jnp.where(is_real, -far, 0)
    prev_off = jnp.where(jnp.logical_and(j >= 1, is_real), 0, far)
    cur_off = jnp.where(is_real, 0, far)
    meta_ok = jnp.logical_and(jnp.logical_and(col >= PAD, col < BLOCK), col + meta_off <= rowi)
    prev_ok = jnp.logical_and(jnp.logical_and(col >= BLOCK, col < 2 * BLOCK),
                              col - BLOCK > rowi + prev_off)
    cur_ok = jnp.logical_and(col >= 2 * BLOCK, col - 2 * BLOCK + cur_off <= rowi)
    ok = jnp.logical_or(jnp.logical_or(meta_ok, prev_ok), cur_ok)
    ok4 = jnp.concatenate([ok] * Q_PER_KV, axis=1)

    groups = range(N_KV_HEADS)
    mask_bf = _group_masks().astype(BF16)
    rb = lax.broadcasted_iota(jnp.int32, (Q_PER_KV * n_keys, GW), 0) // n_keys
    lb = lax.broadcasted_iota(jnp.int32, (Q_PER_KV * n_keys, GW), 1) // HEAD_DIM
    block_mask = jnp.where(rb == lb, 1.0, 0.0).astype(BF16)
    sr = lax.broadcasted_iota(jnp.int32, (GW, GW), 0)
    sc = lax.broadcasted_iota(jnp.int32, (GW, GW), 1)
    lane_head = lax.broadcasted_iota(jnp.int32, (1, GW), 1) // HEAD_DIM
    inv_d = 1.0 / HEAD_DIM
    scale = HEAD_DIM ** -0.5

    real_rows = lax.broadcasted_iota(jnp.int32, (BLOCK, 1), 0) < jnp.where(is_real, BLOCK, 0)
    km, vm = km_ref[...], vm_ref[...]
    kcur = jnp.where(real_rows, kc_ref[0], km)
    vcur = jnp.where(real_rows, vc_ref[0], vm)
    kall = jnp.concatenate([km, kp_ref[0], kcur], axis=0).astype(F32)
    vall = jnp.concatenate([vm, vp_ref[0], vcur], axis=0)
    kss = _dot((kall * kall).astype(BF16), mask_bf)
    kn = (kall * lax.rsqrt(kss * inv_d + NORM_EPS) * kg_ref[...]).astype(BF16)
    q_all = jnp.where(real_rows, q_ref[0], qm_ref[...]).astype(F32)
    qn = []
    for g in groups:
        qg = q_all[:, g * GW:(g + 1) * GW]
        qss = _dot((qg * qg).astype(BF16), mask_bf)
        qn.append((qg * lax.rsqrt(qss * inv_d + NORM_EPS) * (qg_ref[...] * scale)).astype(BF16))
    sel = [jnp.where(sr == g * HEAD_DIM + sc % HEAD_DIM, 1.0, 0.0).astype(BF16) for g in groups]
    krep = [_dot(kn, sel[g]).astype(BF16) for g in groups]
    vrep = [_dot(vall, sel[g]).astype(BF16) for g in groups]
    bdk = [jnp.concatenate([krep[g]] * Q_PER_KV, axis=0) * block_mask for g in groups]
    rhs = [jnp.concatenate([jnp.concatenate([vrep[g]] * Q_PER_KV, axis=0) * block_mask, block_mask], axis=1)
           for g in groups]
    s = [jnp.where(ok4, lax.dot_general(qn[g], bdk[g], NT_DIMS, preferred_element_type=F32), NEG)
         for g in groups]
    p, sink_den = [], []
    for g in groups:
        parts = []
        sd = jnp.zeros((BLOCK, GW), F32)
        for hh in range(Q_PER_KV):
            seg = s[g][:, hh * n_keys:(hh + 1) * n_keys]
            sink = sink_ref[g * Q_PER_KV + hh]
            m = jnp.maximum(jnp.max(seg, axis=-1, keepdims=True), sink)
            parts.append(jnp.exp(seg - m).astype(BF16))
            sd = sd + jnp.exp(sink - m) * jnp.where(lane_head == hh, 1.0, 0.0)
        p.append(jnp.concatenate(parts, axis=1))
        sink_den.append(sd)
    ov = [_dot(p[g], rhs[g]) for g in groups]
    for g in groups:
        o_scr[:, g * GW:(g + 1) * GW] = ov[g][:, :GW] / (ov[g][:, GW:] + sink_den[g])
    h = jnp.where(real_rows, x_ref[0], hm_ref[...])
    o_ref[0] = h + _dot(o_scr[...].astype(BF16), wo_ref[...])


def _attention(x, h_meta, qkv, qkv_meta, sinks, q_gain, k_gain, w_o):
    kcol, vcol = N_Q_HEADS * HEAD_DIM // 256, N_Q_HEADS * HEAD_DIM // 256 + 1
    kvw = N_KV_HEADS * HEAD_DIM
    real = lambda j: jnp.minimum(j, META_BLOCK - 1)
    prev = lambda j: jnp.clip(j - 1, 0, META_BLOCK - 1)
    return pl.pallas_call(
        _attn_kernel,
        grid=(BATCH, N_BLOCKS),
        in_specs=[pl.BlockSpec(memory_space=pltpu.SMEM),
                  pl.BlockSpec((1, BLOCK, D_MODEL), lambda b, j: (b, real(j), 0)),
                  pl.BlockSpec((BLOCK, D_MODEL), lambda b, j: (0, 0)),
                  pl.BlockSpec((1, BLOCK, D_MODEL), lambda b, j: (b, real(j), 0)),
                  pl.BlockSpec((1, BLOCK, kvw), lambda b, j: (b, real(j), kcol)),
                  pl.BlockSpec((1, BLOCK, kvw), lambda b, j: (b, real(j), vcol)),
                  pl.BlockSpec((1, BLOCK, kvw), lambda b, j: (b, prev(j), kcol)),
                  pl.BlockSpec((1, BLOCK, kvw), lambda b, j: (b, prev(j), vcol)),
                  pl.BlockSpec((BLOCK, D_MODEL), lambda b, j: (0, 0)),
                  pl.BlockSpec((BLOCK, kvw), lambda b, j: (0, kcol)),
                  pl.BlockSpec((BLOCK, kvw), lambda b, j: (0, vcol)),
                  pl.BlockSpec((1, GW), lambda b, j: (0, 0)),
                  pl.BlockSpec((1, GW), lambda b, j: (0, 0)),
                  pl.BlockSpec((D_MODEL, D_MODEL), lambda b, j: (0, 0))],
        out_specs=pl.BlockSpec((1, BLOCK, D_MODEL), lambda b, j: (b, j, 0)),
        out_shape=jax.ShapeDtypeStruct((BATCH, TP, D_MODEL), F32),
        scratch_shapes=[pltpu.VMEM((BLOCK, D_MODEL), F32)],
        compiler_params=_params("parallel", "parallel"),
    )(sinks, x, h_meta, qkv, qkv, qkv, qkv, qkv, qkv_meta, qkv_meta, qkv_meta, q_gain, k_gain, w_o)


def _ffn_kernel(h_ref, g_ref, wg_ref, wu_ref, wd_ref, o_ref, u_scr, acc):
    f = pl.program_id(1)

    @pl.when(f == 0)
    def _():
        u_scr[...] = _rms(h_ref[...], g_ref[...]).astype(BF16)
        acc[...] = jnp.zeros_like(acc)

    u = u_scr[...]
    a = _dot(u, wg_ref[...])
    b = _dot(u, wu_ref[...])
    acc[...] += _dot((a * _sigmoid(a) * b).astype(BF16), wd_ref[...])

    @pl.when(f == pl.num_programs(1) - 1)
    def _():
        o_ref[...] = h_ref[...] + acc[...]


def _ffn(h, gain, wg, wu, wd):
    return pl.pallas_call(
        _ffn_kernel,
        grid=(N_ROWS // FFN_ROW_TILE, D_FF // FF_TILE),
        in_specs=[pl.BlockSpec((FFN_ROW_TILE, D_MODEL), lambda i, f: (i, 0)),
                  pl.BlockSpec((1, D_MODEL), lambda i, f: (0, 0)),
                  pl.BlockSpec((D_MODEL, FF_TILE), lambda i, f: (0, f)),
                  pl.BlockSpec((D_MODEL, FF_TILE), lambda i, f: (0, f)),
                  pl.BlockSpec((FF_TILE, D_MODEL), lambda i, f: (f, 0))],
        out_specs=pl.BlockSpec((FFN_ROW_TILE, D_MODEL), lambda i, f: (i, 0)),
        out_shape=jax.ShapeDtypeStruct((N_ROWS, D_MODEL), F32),
        scratch_shapes=[pltpu.VMEM((FFN_ROW_TILE, D_MODEL), BF16), pltpu.VMEM((FFN_ROW_TILE, D_MODEL), F32)],
        compiler_params=_params("parallel", "arbitrary"),
    )(h, gain, wg, wu, wd)


def _rwkv_proj_kernel(h_ref, hp_ref, g_ref, mix_ref, w0_ref, a0_ref, kk_ref, ka_ref,
                      wr_ref, wk_ref, wv_ref, w1_ref, w2_ref, a1_ref, a2_ref, g1_ref, g2_ref,
                      rkv_out, g_out):
    i = pl.program_id(0)
    tiles_per_batch = TP // PROJ_TILE
    r0 = (i % tiles_per_batch) * PROJ_TILE
    local = lax.broadcasted_iota(jnp.int32, (PROJ_TILE, 1), 0)
    lrow = local + r0
    gain = g_ref[...]
    is_pad = jnp.logical_and(lrow >= SEQ, lrow < SEQ + PAD)
    u = jnp.where(is_pad, 0.0, _rms(h_ref[...], gain))
    u_prev_tile = _rms(hp_ref[7:8, :], gain)
    xprev = pltpu.roll(u, 1, 0)
    xprev = jnp.where(local == 0, u_prev_tile, xprev)
    xprev = jnp.where(lrow == SEQ, 0.0, xprev)
    xx = xprev - u
    mix = mix_ref[...]
    lerp = lambda n: (u + xx * mix[n:n + 1, :]).astype(BF16)
    xr, xw, xk, xv, xa, xg = [lerp(n) for n in range(6)]
    r = _dot(xr, wr_ref[...])
    k = _dot(xk, wk_ref[...])
    v = _dot(xv, wv_ref[...])
    lw = _dot(jnp.tanh(_dot(xw, w1_ref[...])).astype(BF16), w2_ref[...])
    z = -(w0_ref[...] + lw)
    softplus = jnp.maximum(z, 0.0) + jnp.log(1.0 + jnp.exp(-jnp.abs(z)))
    w = -softplus - 0.5
    a = _sigmoid(a0_ref[...] + _dot(_dot(xa, a1_ref[...]).astype(BF16), a2_ref[...]))
    g = _dot(_sigmoid(_dot(xg, g1_ref[...])).astype(BF16), g2_ref[...])
    fields = (r, -jnp.exp(w),
              k * (1.0 + (a - 1.0) * ka_ref[...]), v, k * kk_ref[...], a)
    for n, val in enumerate(fields):
        rkv_out[:, n * D_MODEL:(n + 1) * D_MODEL] = val
    g_out[...] = g


def _rwkv_proj(h, gain, mix, w0, a0, k_k, k_a, w_r, w_k, w_v, w1, w2, a1, a2, g1, g2):
    tiles_per_batch = TP // PROJ_TILE
    rows8 = PROJ_TILE // 8

    def prev_map(i):
        b = i // tiles_per_batch
        first = (i % tiles_per_batch) == 0
        return (jnp.where(first, (b * TP + TP - 8) // 8, i * rows8 - 1), 0)

    row = pl.BlockSpec((PROJ_TILE, D_MODEL), lambda i: (i, 0))
    full = lambda a: pl.BlockSpec(a.shape, lambda i: (0,) * a.ndim)
    smalls = (gain, mix, w0, a0, k_k, k_a, w_r, w_k, w_v, w1, w2, a1, a2, g1, g2)
    return pl.pallas_call(
        _rwkv_proj_kernel,
        grid=(N_ROWS // PROJ_TILE,),
        in_specs=[row, pl.BlockSpec((8, D_MODEL), prev_map)] + [full(a) for a in smalls],
        out_specs=[pl.BlockSpec((PROJ_TILE, N_SCAN_IN * D_MODEL), lambda i: (i, 0)), row],
        out_shape=[jax.ShapeDtypeStruct((N_ROWS, N_SCAN_IN * D_MODEL), F32),
                   jax.ShapeDtypeStruct((N_ROWS, D_MODEL), F32)],
        compiler_params=_params("parallel"),
    )(h, h, *smalls)


def _group_masks():
    ri = lax.broadcasted_iota(jnp.int32, (GW, GW), 0) // RWKV_HEAD
    ci = lax.broadcasted_iota(jnp.int32, (GW, GW), 1) // RWKV_HEAD
    return jnp.where(ri == ci, 1.0, 0.0).astype(F32)


def _bd(x, mask):
    return jnp.concatenate([x.astype(BF16)] * HEADS_PER_GROUP, axis=0) * mask.astype(BF16)


def _diag_blocks(full, mask):
    m = full * mask
    n = RWKV_HEAD
    return (m[0:n] + m[n:2 * n]) + (m[2 * n:3 * n] + m[3 * n:4 * n])


def _head_sum(x, mask_bf):
    hi = x.astype(BF16)
    lo = (x - hi.astype(F32)).astype(BF16)
    return _dot(hi, mask_bf) + _dot(lo, mask_bf)


def _scan_prep_kernel(x_ref, rk_ref, o_ref, pl_out):
    L, D = CHUNK, D_MODEL
    units = [(b, g) for b in range(BATCH) for g in range(N_GROUPS)]
    un = range(len(units))
    mask = _group_masks()
    mask_bf = mask.astype(BF16)
    ri = lax.broadcasted_iota(jnp.int32, (L, GW), 0)
    ci = lax.broadcasted_iota(jnp.int32, (L, GW), 1) % RWKV_HEAD
    incl = ci <= ri
    strict = ci < ri
    eye = jnp.where(ci == ri, 1.0, 0.0).astype(F32)
    t_r = lax.broadcasted_iota(jnp.int32, (L, L), 0)
    t_c = lax.broadcasted_iota(jnp.int32, (L, L), 1)
    tril = jnp.where(t_c <= t_r, 1.0, 0.0).astype(BF16)
    rk_all = rk_ref[...]

    def field(b, n, g):
        return x_ref[b, :, n * D + g * GW:n * D + (g + 1) * GW]

    def put(b, n, g, val):
        o_ref[b, :, n * D + g * GW:n * D + (g + 1) * GW] = val

    at, rt, bt, kt, v, plast = [], [], [], [], [], []
    for b in range(BATCH):
        ld = x_ref[b, :, D:2 * D]
        hi = ld.astype(BF16)
        rest = ld - hi.astype(F32)
        mid = rest.astype(BF16)
        lo = (rest - mid.astype(F32)).astype(BF16)
        cs = _dot(tril, hi) + _dot(tril, mid) + _dot(tril, lo)
        p_all = jnp.exp(cs)
        pprev_all = jnp.exp(cs - ld)
        pinv_all = jnp.exp(-cs)
        pl_all = p_all[L - 1:L, :]
        pl_out[b, 0] = pl_all
        for g in range(N_GROUPS):
            sl = slice(g * GW, (g + 1) * GW)
            r, k, vv, kk, a = field(b, 0, g), field(b, 2, g), field(b, 3, g), field(b, 4, g), field(b, 5, g)
            nrm = jnp.sqrt(_dot((kk * kk).astype(BF16), mask_bf))
            kk = kk / jnp.maximum(nrm, 1e-12)
            at.append(-kk * pprev_all[:, sl])
            bt.append(kk * a * pinv_all[:, sl])
            rt.append(r * p_all[:, sl])
            kt.append(k * pinv_all[:, sl])
            v.append(vv)
            plast.append(pl_all[:, sl])
            put(b, 4, g, _dot((r * k * rk_all[:, sl]).astype(BF16), mask_bf) * vv)

    a_ab, a_ak, a_rb, a_rk = [], [], [], []
    for n in un:
        lhs = jnp.concatenate([at[n], rt[n]], axis=0).astype(BF16)
        rhs = jnp.concatenate([_bd(bt[n], mask), _bd(kt[n], mask)], axis=0)
        big = lax.dot_general(lhs, rhs, NT_DIMS, preferred_element_type=F32)
        a_ab.append(jnp.where(strict, big[:L, :GW], 0.0))
        a_ak.append(jnp.where(strict, big[:L, GW:], 0.0))
        a_rb.append(jnp.where(incl, big[L:, :GW], 0.0))
        a_rk.append(jnp.where(incl, big[L:, GW:], 0.0))

    x = [_dot(a_ab[n].astype(BF16), _bd(a_ab[n], mask)) for n in un]
    inv = [eye + a_ab[n] for n in un]
    for step in range(5):
        for n in un:
            rhs = _bd(x[n], mask)
            if step < 4:
                res = _dot(jnp.concatenate([x[n], inv[n]], axis=0).astype(BF16), rhs)
                x[n] = res[:L]
                inv[n] = inv[n] + res[L:]
            else:
                inv[n] = inv[n] + _dot(inv[n].astype(BF16), rhs)

    av = [_dot(jnp.concatenate([a_ak[n], a_rk[n]], axis=0).astype(BF16), _bd(v[n], mask)) for n in un]
    wu = [_dot(inv[n].astype(BF16), jnp.concatenate([_bd(at[n], mask), _bd(av[n][:L], mask)], axis=1))
          for n in un]
    aw = [_dot(a_rb[n].astype(BF16),
               jnp.concatenate([_bd(wu[n][:, :GW], mask), _bd(wu[n][:, GW:], mask)], axis=1))
          for n in un]
    for n, (b, g) in enumerate(units):
        put(b, 0, g, rt[n] + aw[n][:, :GW])
        put(b, 1, g, av[n][L:] + aw[n][:, GW:])
        bh = (bt[n] * plast[n]).astype(BF16)
        kh = (kt[n] * plast[n]).astype(BF16)
        w_b, u0_b = wu[n][:, :GW].astype(BF16), wu[n][:, GW:].astype(BF16)
        gfull = lax.dot_general(bh, w_b, TN_DIMS, preferred_element_type=F32)
        put(b, 2, g, _diag_blocks(gfull, mask))
        hfull = lax.dot_general(jnp.concatenate([u0_b, v[n].astype(BF16)], axis=0),
                                jnp.concatenate([bh, kh], axis=0), TN_DIMS, preferred_element_type=F32)
        put(b, 3, g, _diag_blocks(hfull, mask))


def _scan_prep(rkv, r_k):
    return pl.pallas_call(
        _scan_prep_kernel,
        grid=(N_CHUNKS,),
        in_specs=[pl.BlockSpec((BATCH, CHUNK, N_SCAN_IN * D_MODEL), lambda c: (0, c, 0)),
                  pl.BlockSpec((1, D_MODEL), lambda c: (0, 0))],
        out_specs=[pl.BlockSpec((BATCH, CHUNK, N_SCAN_MID * D_MODEL), lambda c: (0, c, 0)),
                   pl.BlockSpec((BATCH, 1, 1, D_MODEL), lambda c: (0, c, 0, 0))],
        out_shape=[jax.ShapeDtypeStruct((BATCH, TP, N_SCAN_MID * D_MODEL), F32),
                   jax.ShapeDtypeStruct((BATCH, N_CHUNKS, 1, D_MODEL), F32)],
        compiler_params=_params("parallel"),
    )(rkv, r_k)


def _scan_kernel(x_ref, pl_ref, y_ref, s_scr):
    c = pl.program_id(0)
    D = D_MODEL

    @pl.when(c == 0)
    def _():
        s_scr[...] = jnp.zeros_like(s_scr)

    mask = _group_masks()
    units = [(b, slice(g * GW, (g + 1) * GW)) for b in range(BATCH) for g in range(N_GROUPS)]
    field = lambda b, n, sl: x_ref[b, :, n * D + sl.start:n * D + sl.stop]
    s0 = [s_scr[b, :, sl] for b, sl in units]
    o = [lax.dot_general(field(b, 0, sl).astype(BF16), _bd(s0[n], mask), NT_DIMS,
                         preferred_element_type=F32) + field(b, 1, sl)
         for n, (b, sl) in enumerate(units)]
    sg = [lax.dot_general(s0[n].astype(BF16), _bd(field(b, 2, sl), mask), NT_DIMS,
                          preferred_element_type=F32)
          for n, (b, sl) in enumerate(units)]
    for n, (b, sl) in enumerate(units):
        s_scr[b, :, sl] = s0[n] * pl_ref[b, 0, :, sl] + sg[n] + field(b, 3, sl)
    for n, (b, sl) in enumerate(units):
        y_ref[b, :, sl] = o[n]


def _scan(mid, p_last):
    phys = lambda c: (c + META_CHUNK0) % N_CHUNKS
    return pl.pallas_call(
        _scan_kernel,
        grid=(N_CHUNKS,),
        in_specs=[pl.BlockSpec((BATCH, CHUNK, N_SCAN_MID * D_MODEL), lambda c: (0, phys(c), 0)),
                  pl.BlockSpec((BATCH, 1, 1, D_MODEL), lambda c: (0, phys(c), 0, 0))],
        out_specs=pl.BlockSpec((BATCH, CHUNK, D_MODEL), lambda c: (0, phys(c), 0)),
        out_shape=jax.ShapeDtypeStruct((BATCH, TP, D_MODEL), F32),
        scratch_shapes=[pltpu.VMEM((BATCH, RWKV_HEAD, D_MODEL), F32)],
        compiler_params=_params("arbitrary"),
    )(mid, p_last)


def _rwkv_out_kernel(o_ref, bonus_ref, g_ref, h_ref, gw_ref, gb_ref, wo_ref, gain_ref, wr_ref,
                     h_out, u_out, route_out, cnt_out, carry):
    @pl.when(jnp.logical_and(pl.program_id(0) == 0, pl.program_id(1) == 0))
    def _():
        carry[...] = jnp.zeros_like(carry)

    mask_bf = _group_masks().astype(BF16)
    inv_n = 1.0 / RWKV_HEAD
    o, bonus, gate = o_ref[0], bonus_ref[0], g_ref[0]
    parts = []
    for g in range(N_GROUPS):
        sl = slice(g * GW, (g + 1) * GW)
        og = o[:, sl]
        d = og - _head_sum(og, mask_bf) * inv_n
        var = _dot((d * d).astype(BF16), mask_bf) * inv_n
        yn = d * lax.rsqrt(var + GN_EPS) * gw_ref[:, sl] + gb_ref[:, sl] + bonus[:, sl]
        parts.append((yn * gate[:, sl]).astype(BF16))
    h = h_ref[0] + _dot(jnp.concatenate(parts, axis=1), wo_ref[...])
    h_out[0] = h
    u = _rms(h, gain_ref[...])
    for c in range(N_COL_CHUNKS):
        u_out[0, 0, c] = u[:, c * COL_CHUNK:(c + 1) * COL_CHUNK]
    u_hi = u.astype(BF16)
    u_lo = (u - u_hi.astype(F32)).astype(BF16)
    logits = lax.dot_general(wr_ref[...], jnp.concatenate([u_hi, u_lo, u_hi], axis=1), NT_DIMS,
                             preferred_element_type=F32)
    e = jnp.exp(logits - jnp.max(logits, axis=0, keepdims=True))
    probs = e / jnp.sum(e, axis=0, keepdims=True)
    idx = lax.broadcasted_iota(jnp.int32, probs.shape, 0).astype(F32)
    m1 = jnp.max(probs, axis=0, keepdims=True)
    i1 = jnp.min(jnp.where(probs == m1, idx, float(N_EXPERTS)), axis=0, keepdims=True)
    sel1 = idx == i1
    rest = jnp.where(sel1, -1.0, probs)
    m2 = jnp.max(rest, axis=0, keepdims=True)
    i2 = jnp.min(jnp.where(rest == m2, idx, float(N_EXPERTS)), axis=0, keepdims=True)
    sel2 = idx == i2
    onehot = jnp.where(jnp.logical_or(sel1, sel2), 1.0, 0.0).astype(F32)
    tr = lax.broadcasted_iota(jnp.int32, (TOKEN_BLOCK, TOKEN_BLOCK), 0)
    tc = lax.broadcasted_iota(jnp.int32, (TOKEN_BLOCK, TOKEN_BLOCK), 1)
    earlier = _dot(onehot.astype(BF16), jnp.where(tr < tc, 1.0, 0.0).astype(BF16)) + carry[...]
    rank1 = jnp.sum(jnp.where(sel1, earlier, 0.0), axis=0, keepdims=True)
    rank2 = jnp.sum(jnp.where(sel2, earlier, 0.0), axis=0, keepdims=True)
    den = m1 + m2
    fields = (i1, i2, rank1, rank2, m1 / den, m2 / den)
    route = jnp.zeros(probs.shape, F32)
    for n, val in enumerate(fields):
        route = jnp.where(idx == float(n), val, route)
    route_out[0] = route
    tile_cnt = jnp.sum(onehot, axis=1, keepdims=True)
    cnt_out[0] = tile_cnt
    carry[...] += tile_cnt


def _rwkv_out(o, mid, g, h, gn_w, gn_b, w_o, gain, w_router):
    blocks = SEQ // TOKEN_BLOCK
    row = pl.BlockSpec((1, TOKEN_BLOCK, D_MODEL), lambda b, i: (b, i, 0))
    vec = pl.BlockSpec((1, D_MODEL), lambda b, i: (0, 0))
    return pl.pallas_call(
        _rwkv_out_kernel,
        grid=(BATCH, blocks),
        in_specs=[row, pl.BlockSpec((1, TOKEN_BLOCK, D_MODEL), lambda b, i: (b, i, N_SCAN_MID - 1)),
                  row, row, vec, vec,
                  pl.BlockSpec((D_MODEL, D_MODEL), lambda b, i: (0, 0)), vec,
                  pl.BlockSpec((N_EXPERTS, 3 * D_MODEL), lambda b, i: (0, 0))],
        out_specs=[row, pl.BlockSpec((1, 1, N_COL_CHUNKS, TOKEN_BLOCK, COL_CHUNK), lambda b, i: (b, i, 0, 0, 0)),
                   pl.BlockSpec((1, N_EXPERTS, TOKEN_BLOCK), lambda b, i: (b, 0, i)),
                   pl.BlockSpec((1, N_EXPERTS, 1), lambda b, i: (b * blocks + i, 0, 0))],
        out_shape=[jax.ShapeDtypeStruct((BATCH, SEQ, D_MODEL), F32),
                   jax.ShapeDtypeStruct((BATCH, blocks, N_COL_CHUNKS, TOKEN_BLOCK, COL_CHUNK), F32),
                   jax.ShapeDtypeStruct((BATCH, N_EXPERTS, SEQ), F32),
                   jax.ShapeDtypeStruct((BATCH * blocks, N_EXPERTS, 1), F32)],
        scratch_shapes=[pltpu.VMEM((N_EXPERTS, 1), F32)],
        compiler_params=_params("arbitrary", "arbitrary"),
    )(o, mid, g, h, gn_w, gn_b, w_o, gain, w_router)


def _routing_tables(route, cnt):
    i32 = jnp.int32
    route = jnp.swapaxes(route, 0, 1).reshape(N_EXPERTS, N_TOKENS)
    expert = route[0:2].astype(i32)
    rank = route[2:4].astype(i32)
    gate = route[4:6].T
    counts = jnp.sum(cnt.reshape(-1, N_EXPERTS).astype(i32), axis=0)
    tiles_e = (counts + EXPERT_TILE - 1) // EXPERT_TILE
    tile_end = jnp.cumsum(tiles_e)
    n_used = tile_end[-1]
    start_row = (tile_end - tiles_e) * EXPERT_TILE
    pos = jnp.take(start_row, expert) + rank
    tiles = jnp.arange(MAX_EXPERT_TILES, dtype=i32)
    tile_expert = jnp.minimum(jnp.sum(tiles[:, None] >= tile_end[None, :], axis=1), N_EXPERTS - 1).astype(i32)
    tile_rows = jnp.take(counts, tile_expert) - (tiles - jnp.take(tile_end - tiles_e, tile_expert)) * EXPERT_TILE
    tile_rows = jnp.where(tiles < n_used, jnp.clip(tile_rows, 0, EXPERT_TILE), 0)
    pos = pos.reshape(TOP_K * N_TOKEN_BLOCKS, 1, TOKEN_BLOCK)
    chunk = jnp.arange(N_COL_CHUNKS, dtype=i32)[None, :, None]
    sorted_piece = (((pos // EXPERT_TILE) * N_COL_CHUNKS + chunk) * EXPERT_TILE + pos % EXPERT_TILE).reshape(-1)
    return dict(sorted_piece=sorted_piece, gate=gate, tile_expert=tile_expert,
                n_used=n_used.reshape(1).astype(i32), tile_rows=tile_rows.astype(i32))


def _row_gather(x, indices):
    m = indices.shape[0]
    mesh = plsc.VectorSubcoreMesh(core_axis_name="c", subcore_axis_name="s")

    @pl.kernel(out_type=jax.ShapeDtypeStruct((m, COL_CHUNK), x.dtype), mesh=mesh)
    def gather(x_hbm, i_hbm, o_hbm):
        def body(i_vmem, o_vmem):
            pltpu.sync_copy(x_hbm.at[i_vmem.at[0]], o_vmem)

        pltpu.emit_pipeline(
            body, grid=(m // GATHER_ROWS,),
            in_specs=[pl.BlockSpec((1, GATHER_ROWS), lambda i: (0, i))],
            out_specs=[pl.BlockSpec((GATHER_ROWS, COL_CHUNK), lambda i: (i, 0))],
            core_axis_name=("c", "s"),
            dimension_semantics=(pltpu.PARALLEL,),
        )(i_hbm, o_hbm)

    return gather(x, indices.reshape(1, m))


def _row_scatter(x, indices, out_rows):
    m = indices.shape[0]
    x_blocks = x.shape[0] // GATHER_ROWS
    mesh = plsc.VectorSubcoreMesh(core_axis_name="c", subcore_axis_name="s")

    @pl.kernel(out_type=jax.ShapeDtypeStruct((out_rows, COL_CHUNK), x.dtype), mesh=mesh)
    def scatter(x_hbm, i_hbm, o_hbm):
        def body(x_vmem, i_vmem):
            pltpu.sync_copy(x_vmem, o_hbm.at[i_vmem.at[0]])

        pltpu.emit_pipeline(
            body, grid=(m // GATHER_ROWS,),
            in_specs=[pl.BlockSpec((GATHER_ROWS, COL_CHUNK), lambda i: (i % x_blocks, 0)),
                      pl.BlockSpec((1, GATHER_ROWS), lambda i: (0, i))],
            out_specs=[],
            core_axis_name=("c", "s"),
            dimension_semantics=(pltpu.PARALLEL,),
        )(x_hbm, i_hbm)

    return scatter(x, indices.reshape(1, m))


def _expert_kernel(te_ref, nu_ref, nr_ref, x_ref, wg_ref, wu_ref, wd_ref, y_ref, xb):
    i = pl.program_id(0)
    f = pl.program_id(1)
    n_rows = nr_ref[i]
    n_sub = (n_rows + SUB_TILE - 1) // SUB_TILE
    subs = EXPERT_TILE // SUB_TILE

    @pl.when(n_sub > 0)
    def _():
        @pl.when(f == 0)
        def _():
            real = lax.broadcasted_iota(jnp.int32, (EXPERT_TILE, 1), 0) < n_rows
            for c in range(N_COL_CHUNKS):
                piece = jnp.where(real, x_ref[c * EXPERT_TILE:(c + 1) * EXPERT_TILE, :], 0.0)
                xb[:, c * COL_CHUNK:(c + 1) * COL_CHUNK] = piece.astype(BF16)
            y_ref[...] = jnp.zeros_like(y_ref)

        wg = wg_ref[0].astype(BF16)
        wu = wu_ref[0].astype(BF16)
        wd = wd_ref[0].astype(BF16)

        def block(rows):
            x = xb[rows, :]
            a = _dot(x, wg)
            b = _dot(x, wu)
            part = _dot((a * _sigmoid(a) * b).astype(BF16), wd)
            for c in range(N_COL_CHUNKS):
                y_ref[c * EXPERT_TILE + rows.start:c * EXPERT_TILE + rows.stop, :] += part[:, c * COL_CHUNK:(c + 1) * COL_CHUNK]

        @pl.when(n_sub == subs)
        def _():
            block(slice(0, EXPERT_TILE))

        @pl.when(n_sub < subs)
        def _():
            for k in range(subs - 1):
                pl.when(k < n_sub)(functools.partial(block, slice(k * SUB_TILE, (k + 1) * SUB_TILE)))


def _experts(tab, x_sorted, wg, wu, wd):
    n_ff = D_FF // FF_TILE
    tile = lambda i, nu: jnp.minimum(i, nu[0] - 1)
    ff = lambda i, f, nu: jnp.where(i < nu[0], f, n_ff - 1)
    grid_spec = pltpu.PrefetchScalarGridSpec(
        num_scalar_prefetch=3, grid=(MAX_EXPERT_TILES, n_ff),
        in_specs=[pl.BlockSpec((N_COL_CHUNKS * EXPERT_TILE, COL_CHUNK), lambda i, f, te, nu, ns: (tile(i, nu), 0)),
                  pl.BlockSpec((1, D_MODEL, FF_TILE),
                               lambda i, f, te, nu, ns: (te[tile(i, nu)], 0, ff(i, f, nu))),
                  pl.BlockSpec((1, D_MODEL, FF_TILE),
                               lambda i, f, te, nu, ns: (te[tile(i, nu)], 0, ff(i, f, nu))),
                  pl.BlockSpec((1, FF_TILE, D_MODEL),
                               lambda i, f, te, nu, ns: (te[tile(i, nu)], ff(i, f, nu), 0))],
        out_specs=pl.BlockSpec((N_COL_CHUNKS * EXPERT_TILE, COL_CHUNK), lambda i, f, te, nu, ns: (tile(i, nu), 0)),
        scratch_shapes=[pltpu.VMEM((EXPERT_TILE, D_MODEL), BF16)])
    return pl.pallas_call(
        _expert_kernel, grid_spec=grid_spec,
        out_shape=jax.ShapeDtypeStruct((N_COL_CHUNKS * MAX_SORTED_ROWS, COL_CHUNK), F32),
        compiler_params=_params("arbitrary", "arbitrary"),
    )(tab["tile_expert"], tab["n_used"], tab["tile_rows"], x_sorted, wg, wu, wd)


def _combine_kernel(h_ref, y_ref, g_ref, o_ref):
    g = g_ref[...]
    for c in range(N_COL_CHUNKS):
        cols = slice(c * COL_CHUNK, (c + 1) * COL_CHUNK)
        o_ref[:, cols] = h_ref[:, cols] + g[:, 0:1] * y_ref[0, 0, c] + g[:, 1:2] * y_ref[1, 0, c]


def _combine(h, y_pair, gate):
    row = pl.BlockSpec((TOKEN_BLOCK, D_MODEL), lambda i: (i, 0))
    return pl.pallas_call(
        _combine_kernel,
        grid=(N_TOKEN_BLOCKS,),
        in_specs=[row, pl.BlockSpec((TOP_K, 1, N_COL_CHUNKS, TOKEN_BLOCK, COL_CHUNK), lambda i: (0, i, 0, 0, 0)),
                  pl.BlockSpec((TOKEN_BLOCK, TOP_K), lambda i: (i, 0))],
        out_specs=row,
        out_shape=jax.ShapeDtypeStruct((N_TOKENS, D_MODEL), F32),
        compiler_params=_params("parallel"),
    )(h, y_pair, gate)


def kernel(x, meta_tokens, mixer_norm, ffn_norm, attn_w_qkv, attn_q_norm, attn_k_norm, attn_sinks, attn_w_o, rwkv_mix, rwkv_w0, rwkv_w1, rwkv_w2, rwkv_a0, rwkv_a1, rwkv_a2, rwkv_g1, rwkv_g2, rwkv_k_k, rwkv_k_a, rwkv_r_k, rwkv_w_r, rwkv_w_k, rwkv_w_v, rwkv_w_o, rwkv_gn_w, rwkv_gn_b, ffn_w_gate, ffn_w_up, ffn_w_down, moe_router, moe_w_gate, moe_w_up, moe_w_down):
    bf = lambda a: a.astype(BF16)
    vec = lambda a: a.reshape(1, -1).astype(F32)
    h_meta = jnp.concatenate([jnp.zeros((PAD, D_MODEL), F32), meta_tokens.astype(F32)], axis=0)

    w_qkv = bf(attn_w_qkv[0])
    qkv = _qkv(x.reshape(N_TOKENS, D_MODEL), vec(mixer_norm[0]), w_qkv, TOKEN_BLOCK)
    qkv_meta = _qkv(h_meta, vec(mixer_norm[0]), w_qkv, BLOCK)
    h = _attention(x, h_meta, qkv.reshape(BATCH, SEQ, QKV_DIM), qkv_meta, attn_sinks[0].astype(F32),
                   jnp.tile(vec(attn_q_norm[0]), (1, Q_PER_KV)), jnp.tile(vec(attn_k_norm[0]), (1, N_KV_HEADS)),
                   bf(attn_w_o[0]))
    h = _ffn(h.reshape(N_ROWS, D_MODEL), vec(ffn_norm[0]), bf(ffn_w_gate[0]), bf(ffn_w_up[0]),
             bf(ffn_w_down[0]))

    rkv, g = _rwkv_proj(
        h, vec(mixer_norm[1]), rwkv_mix[0], vec(rwkv_w0[0]), vec(rwkv_a0[0]), vec(rwkv_k_k[0]),
        vec(rwkv_k_a[0]), bf(rwkv_w_r[0]), bf(rwkv_w_k[0]), bf(rwkv_w_v[0]), bf(rwkv_w1[0]),
        bf(rwkv_w2[0]), bf(rwkv_a1[0]), bf(rwkv_a2[0]), bf(rwkv_g1[0]), bf(rwkv_g2[0]))
    b3 = lambda t: t.reshape(BATCH, TP, -1)
    mid, p_last = _scan_prep(b3(rkv), vec(rwkv_r_k[0]))
    o = _scan(mid, p_last)
    w_router = moe_router[0].astype(F32).T
    wr_hi = bf(w_router)
    wr_lo = bf(w_router - wr_hi.astype(F32))
    h, u, route, cnt = _rwkv_out(o, mid, b3(g), b3(h), vec(rwkv_gn_w[0]), vec(rwkv_gn_b[0]),
                                 bf(rwkv_w_o[0]), vec(ffn_norm[1]),
                                 jnp.concatenate([wr_hi, wr_hi, wr_lo], axis=1))
    tab = _routing_tables(route, cnt)
    x_sorted = _row_scatter(u.reshape(-1, COL_CHUNK), tab["sorted_piece"], N_COL_CHUNKS * MAX_SORTED_ROWS)
    y_sorted = _experts(tab, x_sorted, moe_w_gate[0], moe_w_up[0], moe_w_down[0])
    y_pair = _row_gather(y_sorted, tab["sorted_piece"]).reshape(TOP_K, N_TOKEN_BLOCKS, N_COL_CHUNKS, TOKEN_BLOCK, COL_CHUNK)
    out = _combine(h.reshape(N_TOKENS, D_MODEL), y_pair, tab["gate"])
    return out.reshape(BATCH, SEQ, D_MODEL)
```

```python
import functools

import jax
import jax.numpy as jnp
from jax import lax
from jax.experimental import pallas as pl
from jax.experimental.pallas import tpu as pltpu
from jax.experimental.pallas import tpu_sc as plsc

F32 = jnp.float32
BF16 = jnp.bfloat16

D_MODEL = 1024
BATCH = 2
SEQ = 8192
N_META = 16
BLOCK = 128
PAD = BLOCK - N_META
TP = SEQ + BLOCK
N_ROWS = BATCH * TP
N_BLOCKS = TP // BLOCK
META_BLOCK = N_BLOCKS - 1
HEAD_DIM = 64
N_Q_HEADS = 16
N_KV_HEADS = 4
Q_PER_KV = 4
QKV_DIM = (N_Q_HEADS + 2 * N_KV_HEADS) * HEAD_DIM
RWKV_HEADS = 16
RWKV_HEAD = 64
D_FF = 3584
N_EXPERTS = 8
NORM_EPS = 1e-5
GN_EPS = 64e-5
CHUNK = 64
N_CHUNKS = TP // CHUNK
META_CHUNK0 = SEQ // CHUNK
HEADS_PER_GROUP = 4
GW = HEADS_PER_GROUP * RWKV_HEAD
N_GROUPS = RWKV_HEADS // HEADS_PER_GROUP
N_SCAN_IN = 6

FFN_ROW_TILE = 1280
PROJ_TILE = 320
FF_TILE = 512
N_TOKENS = BATCH * SEQ
TOP_K = 2
TOKEN_BLOCK = 512
N_TOKEN_BLOCKS = N_TOKENS // TOKEN_BLOCK
EXPERT_TILE = 1024
SUB_TILE = 256
MAX_EXPERT_TILES = TOP_K * N_TOKENS // EXPERT_TILE + N_EXPERTS
MAX_SORTED_ROWS = MAX_EXPERT_TILES * EXPERT_TILE
COL_CHUNK = 256
N_COL_CHUNKS = D_MODEL // COL_CHUNK
GATHER_ROWS = 128
VMEM_LIMIT = 56 * 1024 * 1024
NEG = -1e30

NT_DIMS = (((1,), (1,)), ((), ()))
TN_DIMS = (((0,), (0,)), ((), ()))


def _params(*sem):
    return pltpu.CompilerParams(dimension_semantics=sem, vmem_limit_bytes=VMEM_LIMIT)


def _rms(x, gain):
    return x * lax.rsqrt(jnp.mean(x * x, axis=-1, keepdims=True) + NORM_EPS) * gain


def _sigmoid(x):
    return 1.0 / (1.0 + jnp.exp(-x))


def _dot(a, b):
    return jnp.dot(a, b, preferred_element_type=F32)


def _qkv_kernel(h_ref, g_ref, w_ref, o_ref):
    u = _rms(h_ref[...], g_ref[...])
    o_ref[...] = _dot(u.astype(BF16), w_ref[...]).astype(BF16)


def _qkv(h, gain, w, tile):
    rows = h.shape[0]
    return pl.pallas_call(
        _qkv_kernel,
        grid=(rows // tile,),
        in_specs=[pl.BlockSpec((tile, D_MODEL), lambda i: (i, 0)),
                  pl.BlockSpec((1, D_MODEL), lambda i: (0, 0)),
                  pl.BlockSpec((D_MODEL, QKV_DIM), lambda i: (0, 0))],
        out_specs=pl.BlockSpec((tile, QKV_DIM), lambda i: (i, 0)),
        out_shape=jax.ShapeDtypeStruct((rows, QKV_DIM), BF16),
        compiler_params=_params("parallel"),
    )(h, gain, w)


def _attn_kernel(sink_ref, x_ref, hm_ref, q_ref, kc_ref, vc_ref, kp_ref, vp_ref, qm_ref, km_ref, vm_ref,
                 qg_ref, kg_ref, wo_ref, o_ref, o_scr):
    j = pl.program_id(1)
    n_keys = 3 * BLOCK
    rowi = lax.broadcasted_iota(jnp.int32, (BLOCK, n_keys), 0)
    col = lax.broadcasted_iota(jnp.int32, (BLOCK, n_keys), 1)
    far = 4 * BLOCK
    is_real = j < META_BLOCK
    meta_off = jnp.where(is_real, -far, 0)
    prev_off = jnp.where(jnp.logical_and(j >= 1, is_real), 0, far)
    cur_off = jnp.where(is_real, 0, far)
    meta_ok = jnp.logical_and(jnp.logical_and(col >= PAD, col < BLOCK), col + meta_off <= rowi)
    prev_ok = jnp.logical_and(jnp.logical_and(col >= BLOCK, col < 2 * BLOCK),
                              col - BLOCK > rowi + prev_off)
    cur_ok = jnp.logical_and(col >= 2 * BLOCK, col - 2 * BLOCK + cur_off <= rowi)
    ok = jnp.logical_or(jnp.logical_or(meta_ok, prev_ok), cur_ok)
    ok4 = jnp.concatenate([ok] * Q_PER_KV, axis=1)

    groups = range(N_KV_HEADS)
    mask_bf = _group_masks().astype(BF16)
    rb = lax.broadcasted_iota(jnp.int32, (Q_PER_KV * n_keys, GW), 0) // n_keys
    lb = lax.broadcasted_iota(jnp.int32, (Q_PER_KV * n_keys, GW), 1) // HEAD_DIM
    block_mask = jnp.where(rb == lb, 1.0, 0.0).astype(BF16)
    sr = lax.broadcasted_iota(jnp.int32, (GW, GW), 0)
    sc = lax.broadcasted_iota(jnp.int32, (GW, GW), 1)
    lane_head = lax.broadcasted_iota(jnp.int32, (1, GW), 1) // HEAD_DIM
    inv_d = 1.0 / HEAD_DIM
    scale = HEAD_DIM ** -0.5

    real_rows = lax.broadcasted_iota(jnp.int32, (BLOCK, 1), 0) < jnp.where(is_real, BLOCK, 0)
    km, vm = km_ref[...], vm_ref[...]
    kcur = jnp.where(real_rows, kc_ref[0], km)
    vcur = jnp.where(real_rows, vc_ref[0], vm)
    kall = jnp.concatenate([km, kp_ref[0], kcur], axis=0).astype(F32)
    vall = jnp.concatenate([vm, vp_ref[0], vcur], axis=0)
    kss = _dot((kall * kall).astype(BF16), mask_bf)
    kn = (kall * lax.rsqrt(kss * inv_d + NORM_EPS) * kg_ref[...]).astype(BF16)
    q_all = jnp.where(real_rows, q_ref[0], qm_ref[...]).astype(F32)
    qn = []
    for g in groups:
        qg = q_all[:, g * GW:(g + 1) * GW]
        qss = _dot((qg * qg).astype(BF16), mask_bf)
        qn.append((qg * lax.rsqrt(qss * inv_d + NORM_EPS) * (qg_ref[...] * scale)).astype(BF16))
    sel = [jnp.where(sr == g * HEAD_DIM + sc % HEAD_DIM, 1.0, 0.0).astype(BF16) for g in groups]
    krep = [_dot(kn, sel[g]).astype(BF16) for g in groups]
    vrep = [_dot(vall, sel[g]).astype(BF16) for g in groups]
    bdk = [jnp.concatenate([krep[g]] * Q_PER_KV, axis=0) * block_mask for g in groups]
    rhs = [jnp.concatenate([jnp.concatenate([vrep[g]] * Q_PER_KV, axis=0) * block_mask, block_mask], axis=1)
           for g in groups]
    s = [jnp.where(ok4, lax.dot_general(qn[g], bdk[g], NT_DIMS, preferred_element_type=F32), NEG)
         for g in groups]
    p, sink_den = [], []
    for g in groups:
        parts = []
        sd = jnp.zeros((BLOCK, GW), F32)
        for hh in range(Q_PER_KV):
            seg = s[g][:, hh * n_keys:(hh + 1) * n_keys]
            sink = sink_ref[g * Q_PER_KV + hh]
            m = jnp.maximum(jnp.max(seg, axis=-1, keepdims=True), sink)
            parts.append(jnp.exp(seg - m).astype(BF16))
            sd = sd + jnp.exp(sink - m) * jnp.where(lane_head == hh, 1.0, 0.0)
        p.append(jnp.concatenate(parts, axis=1))
        sink_den.append(sd)
    ov = [_dot(p[g], rhs[g]) for g in groups]
    for g in groups:
        o_scr[:, g * GW:(g + 1) * GW] = ov[g][:, :GW] / (ov[g][:, GW:] + sink_den[g])
    h = jnp.where(real_rows, x_ref[0], hm_ref[...])
    o_ref[0] = h + _dot(o_scr[...].astype(BF16), wo_ref[...])


def _attention(x, h_meta, qkv, qkv_meta, sinks, q_gain, k_gain, w_o):
    kcol, vcol = N_Q_HEADS * HEAD_DIM // 256, N_Q_HEADS * HEAD_DIM // 256 + 1
    kvw = N_KV_HEADS * HEAD_DIM
    real = lambda j: jnp.minimum(j, META_BLOCK - 1)
    prev = lambda j: jnp.clip(j - 1, 0, META_BLOCK - 1)
    return pl.pallas_call(
        _attn_kernel,
        grid=(BATCH, N_BLOCKS),
        in_specs=[pl.BlockSpec(memory_space=pltpu.SMEM),
                  pl.BlockSpec((1, BLOCK, D_MODEL), lambda b, j: (b, real(j), 0)),
                  pl.BlockSpec((BLOCK, D_MODEL), lambda b, j: (0, 0)),
                  pl.BlockSpec((1, BLOCK, D_MODEL), lambda b, j: (b, real(j), 0)),
                  pl.BlockSpec((1, BLOCK, kvw), lambda b, j: (b, real(j), kcol)),
                  pl.BlockSpec((1, BLOCK, kvw), lambda b, j: (b, real(j), vcol)),
                  pl.BlockSpec((1, BLOCK, kvw), lambda b, j: (b, prev(j), kcol)),
                  pl.BlockSpec((1, BLOCK, kvw), lambda b, j: (b, prev(j), vcol)),
                  pl.BlockSpec((BLOCK, D_MODEL), lambda b, j: (0, 0)),
                  pl.BlockSpec((BLOCK, kvw), lambda b, j: (0, kcol)),
                  pl.BlockSpec((BLOCK, kvw), lambda b, j: (0, vcol)),
                  pl.BlockSpec((1, GW), lambda b, j: (0, 0)),
                  pl.BlockSpec((1, GW), lambda b, j: (0, 0)),
                  pl.BlockSpec((D_MODEL, D_MODEL), lambda b, j: (0, 0))],
        out_specs=pl.BlockSpec((1, BLOCK, D_MODEL), lambda b, j: (b, j, 0)),
        out_shape=jax.ShapeDtypeStruct((BATCH, TP, D_MODEL), F32),
        scratch_shapes=[pltpu.VMEM((BLOCK, D_MODEL), F32)],
        compiler_params=_params("parallel", "parallel"),
    )(sinks, x, h_meta, qkv, qkv, qkv, qkv, qkv, qkv_meta, qkv_meta, qkv_meta, q_gain, k_gain, w_o)


def _ffn_kernel(h_ref, g_ref, wg_ref, wu_ref, wd_ref, o_ref, u_scr, acc):
    f = pl.program_id(1)

    @pl.when(f == 0)
    def _():
        u_scr[...] = _rms(h_ref[...], g_ref[...]).astype(BF16)
        acc[...] = jnp.zeros_like(acc)

    u = u_scr[...]
    a = _dot(u, wg_ref[...])
    b = _dot(u, wu_ref[...])
    acc[...] += _dot((a * _sigmoid(a) * b).astype(BF16), wd_ref[...])

    @pl.when(f == pl.num_programs(1) - 1)
    def _():
        o_ref[...] = h_ref[...] + acc[...]


def _ffn(h, gain, wg, wu, wd):
    return pl.pallas_call(
        _ffn_kernel,
        grid=(N_ROWS // FFN_ROW_TILE, D_FF // FF_TILE),
        in_specs=[pl.BlockSpec((FFN_ROW_TILE, D_MODEL), lambda i, f: (i, 0)),
                  pl.BlockSpec((1, D_MODEL), lambda i, f: (0, 0)),
                  pl.BlockSpec((D_MODEL, FF_TILE), lambda i, f: (0, f)),
                  pl.BlockSpec((D_MODEL, FF_TILE), lambda i, f: (0, f)),
                  pl.BlockSpec((FF_TILE, D_MODEL), lambda i, f: (f, 0))],
        out_specs=pl.BlockSpec((FFN_ROW_TILE, D_MODEL), lambda i, f: (i, 0)),
        out_shape=jax.ShapeDtypeStruct((N_ROWS, D_MODEL), F32),
        scratch_shapes=[pltpu.VMEM((FFN_ROW_TILE, D_MODEL), BF16), pltpu.VMEM((FFN_ROW_TILE, D_MODEL), F32)],
        compiler_params=_params("parallel", "arbitrary"),
    )(h, gain, wg, wu, wd)


def _rwkv_proj_kernel(h_ref, hp_ref, g_ref, mix_ref, w0_ref, a0_ref, kk_ref, ka_ref,
                      wr_ref, wk_ref, wv_ref, w1_ref, w2_ref, a1_ref, a2_ref, g1_ref, g2_ref,
                      rkv_out, g_out):
    i = pl.program_id(0)
    tiles_per_batch = TP // PROJ_TILE
    r0 = (i % tiles_per_batch) * PROJ_TILE
    local = lax.broadcasted_iota(jnp.int32, (PROJ_TILE, 1), 0)
    lrow = local + r0
    gain = g_ref[...]
    is_pad = jnp.logical_and(lrow >= SEQ, lrow < SEQ + PAD)
    u = jnp.where(is_pad, 0.0, _rms(h_ref[...], gain))
    u_prev_tile = _rms(hp_ref[7:8, :], gain)
    xprev = pltpu.roll(u, 1, 0)
    xprev = jnp.where(local == 0, u_prev_tile, xprev)
    xprev = jnp.where(lrow == SEQ, 0.0, xprev)
    xx = xprev - u
    mix = mix_ref[...]
    lerp = lambda n: (u + xx * mix[n:n + 1, :]).astype(BF16)
    xr, xw, xk, xv, xa, xg = [lerp(n) for n in range(6)]
    r = _dot(xr, wr_ref[...])
    k = _dot(xk, wk_ref[...])
    v = _dot(xv, wv_ref[...])
    lw = _dot(jnp.tanh(_dot(xw, w1_ref[...])).astype(BF16), w2_ref[...])
    z = -(w0_ref[...] + lw)
    softplus = jnp.maximum(z, 0.0) + jnp.log(1.0 + jnp.exp(-jnp.abs(z)))
    w = -softplus - 0.5
    a = _sigmoid(a0_ref[...] + _dot(_dot(xa, a1_ref[...]).astype(BF16), a2_ref[...]))
    g = _dot(_sigmoid(_dot(xg, g1_ref[...])).astype(BF16), g2_ref[...])
    fields = (r, -jnp.exp(w),
              k * (1.0 + (a - 1.0) * ka_ref[...]), v, k * kk_ref[...], a)
    for n, val in enumerate(fields):
        rkv_out[:, n * D_MODEL:(n + 1) * D_MODEL] = val
    g_out[...] = g


def _rwkv_proj(h, gain, mix, w0, a0, k_k, k_a, w_r, w_k, w_v, w1, w2, a1, a2, g1, g2):
    tiles_per_batch = TP // PROJ_TILE
    rows8 = PROJ_TILE // 8

    def prev_map(i):
        b = i // tiles_per_batch
        first = (i % tiles_per_batch) == 0
        return (jnp.where(first, (b * TP + TP - 8) // 8, i * rows8 - 1), 0)

    row = pl.BlockSpec((PROJ_TILE, D_MODEL), lambda i: (i, 0))
    full = lambda a: pl.BlockSpec(a.shape, lambda i: (0,) * a.ndim)
    smalls = (gain, mix, w0, a0, k_k, k_a, w_r, w_k, w_v, w1, w2, a1, a2, g1, g2)
    return pl.pallas_call(
        _rwkv_proj_kernel,
        grid=(N_ROWS // PROJ_TILE,),
        in_specs=[row, pl.BlockSpec((8, D_MODEL), prev_map)] + [full(a) for a in smalls],
        out_specs=[pl.BlockSpec((PROJ_TILE, N_SCAN_IN * D_MODEL), lambda i: (i, 0)), row],
        out_shape=[jax.ShapeDtypeStruct((N_ROWS, N_SCAN_IN * D_MODEL), F32),
                   jax.ShapeDtypeStruct((N_ROWS, D_MODEL), F32)],
        compiler_params=_params("parallel"),
    )(h, h, *smalls)


def _group_masks():
    ri = lax.broadcasted_iota(jnp.int32, (GW, GW), 0) // RWKV_HEAD
    ci = lax.broadcasted_iota(jnp.int32, (GW, GW), 1) // RWKV_HEAD
    return jnp.where(ri == ci, 1.0, 0.0).astype(F32)


def _bd(x, mask):
    return jnp.concatenate([x.astype(BF16)] * HEADS_PER_GROUP, axis=0) * mask.astype(BF16)


def _diag_blocks(full, mask):
    m = full * mask
    n = RWKV_HEAD
    return (m[0:n] + m[n:2 * n]) + (m[2 * n:3 * n] + m[3 * n:4 * n])


def _head_sum(x, mask_bf):
    hi = x.astype(BF16)
    lo = (x - hi.astype(F32)).astype(BF16)
    return _dot(hi, mask_bf) + _dot(lo, mask_bf)


def _scan_prep_kernel(x_ref, rk_ref, mm_out, add_out, bonus_out, pl_out):
    L, D = CHUNK, D_MODEL
    units = [(b, g) for b in range(BATCH) for g in range(N_GROUPS)]
    un = range(len(units))
    mask = _group_masks()
    mask_bf = mask.astype(BF16)
    ri = lax.broadcasted_iota(jnp.int32, (L, GW), 0)
    ci = lax.broadcasted_iota(jnp.int32, (L, GW), 1) % RWKV_HEAD
    incl = ci <= ri
    strict = ci < ri
    eye = jnp.where(ci == ri, 1.0, 0.0).astype(F32)
    t_r = lax.broadcasted_iota(jnp.int32, (L, L), 0)
    t_c = lax.broadcasted_iota(jnp.int32, (L, L), 1)
    tril = jnp.where(t_c <= t_r, 1.0, 0.0).astype(BF16)
    rk_all = rk_ref[...]

    def field(b, n, g):
        return x_ref[b, :, n * D + g * GW:n * D + (g + 1) * GW]

    def put(ref, b, n, g, val):
        ref[b, :, n * D + g * GW:n * D + (g + 1) * GW] = val.astype(ref.dtype)

    at, rt, bt, kt, v, plast = [], [], [], [], [], []
    for b in range(BATCH):
        ld = x_ref[b, :, D:2 * D]
        hi = ld.astype(BF16)
        rest = ld - hi.astype(F32)
        mid = rest.astype(BF16)
        lo = (rest - mid.astype(F32)).astype(BF16)
        cs = _dot(tril, hi) + _dot(tril, mid) + _dot(tril, lo)
        p_all = jnp.exp(cs)
        pprev_all = jnp.exp(cs - ld)
        pinv_all = jnp.exp(-cs)
        pl_all = p_all[L - 1:L, :]
        pl_out[b, 0] = pl_all
        for g in range(N_GROUPS):
            sl = slice(g * GW, (g + 1) * GW)
            r, k, vv, kk, a = field(b, 0, g), field(b, 2, g), field(b, 3, g), field(b, 4, g), field(b, 5, g)
            nrm = jnp.sqrt(_dot((kk * kk).astype(BF16), mask_bf))
            kk = kk / jnp.maximum(nrm, 1e-12)
            at.append(-kk * pprev_all[:, sl])
            bt.append(kk * a * pinv_all[:, sl])
            rt.append(r * p_all[:, sl])
            kt.append(k * pinv_all[:, sl])
            v.append(vv)
            plast.append(pl_all[:, sl])
            put(bonus_out, b, 0, g, _dot((r * k * rk_all[:, sl]).astype(BF16), mask_bf) * vv)

    a_ab, a_ak, a_rb, a_rk = [], [], [], []
    for n in un:
        lhs = jnp.concatenate([at[n], rt[n]], axis=0).astype(BF16)
        rhs = jnp.concatenate([_bd(bt[n], mask), _bd(kt[n], mask)], axis=0)
        big = lax.dot_general(lhs, rhs, NT_DIMS, preferred_element_type=F32)
        a_ab.append(jnp.where(strict, big[:L, :GW], 0.0))
        a_ak.append(jnp.where(strict, big[:L, GW:], 0.0))
        a_rb.append(jnp.where(incl, big[L:, :GW], 0.0))
        a_rk.append(jnp.where(incl, big[L:, GW:], 0.0))

    x = [_dot(a_ab[n].astype(BF16), _bd(a_ab[n], mask)) for n in un]
    inv = [eye + a_ab[n] for n in un]
    for step in range(5):
        for n in un:
            rhs = _bd(x[n], mask)
            if step < 4:
                res = _dot(jnp.concatenate([x[n], inv[n]], axis=0).astype(BF16), rhs)
                x[n] = res[:L]
                inv[n] = inv[n] + res[L:]
            else:
                inv[n] = inv[n] + _dot(inv[n].astype(BF16), rhs)

    av = [_dot(jnp.concatenate([a_ak[n], a_rk[n]], axis=0).astype(BF16), _bd(v[n], mask)) for n in un]
    wu = [_dot(inv[n].astype(BF16), jnp.concatenate([_bd(at[n], mask), _bd(av[n][:L], mask)], axis=1))
          for n in un]
    aw = [_dot(a_rb[n].astype(BF16),
               jnp.concatenate([_bd(wu[n][:, :GW], mask), _bd(wu[n][:, GW:], mask)], axis=1))
          for n in un]
    for n, (b, g) in enumerate(units):
        put(mm_out, b, 0, g, rt[n] + aw[n][:, :GW])
        put(add_out, b, 0, g, av[n][L:] + aw[n][:, GW:])
        bh = (bt[n] * plast[n]).astype(BF16)
        kh = (kt[n] * plast[n]).astype(BF16)
        w_b, u0_b = wu[n][:, :GW].astype(BF16), wu[n][:, GW:].astype(BF16)
        gfull = lax.dot_general(bh, w_b, TN_DIMS, preferred_element_type=F32)
        put(mm_out, b, 1, g, _diag_blocks(gfull, mask))
        hfull = lax.dot_general(jnp.concatenate([u0_b, v[n].astype(BF16)], axis=0),
                                jnp.concatenate([bh, kh], axis=0), TN_DIMS, preferred_element_type=F32)
        put(add_out, b, 1, g, _diag_blocks(hfull, mask))


def _scan_prep(rkv, r_k):
    return pl.pallas_call(
        _scan_prep_kernel,
        grid=(N_CHUNKS,),
        in_specs=[pl.BlockSpec((BATCH, CHUNK, N_SCAN_IN * D_MODEL), lambda c: (0, c, 0)),
                  pl.BlockSpec((1, D_MODEL), lambda c: (0, 0))],
        out_specs=[pl.BlockSpec((BATCH, CHUNK, 2 * D_MODEL), lambda c: (0, c, 0)),
                   pl.BlockSpec((BATCH, CHUNK, 2 * D_MODEL), lambda c: (0, c, 0)),
                   pl.BlockSpec((BATCH, CHUNK, D_MODEL), lambda c: (0, c, 0)),
                   pl.BlockSpec((BATCH, 1, 1, D_MODEL), lambda c: (0, c, 0, 0))],
        out_shape=[jax.ShapeDtypeStruct((BATCH, TP, 2 * D_MODEL), BF16),
                   jax.ShapeDtypeStruct((BATCH, TP, 2 * D_MODEL), F32),
                   jax.ShapeDtypeStruct((BATCH, TP, D_MODEL), F32),
                   jax.ShapeDtypeStruct((BATCH, N_CHUNKS, 1, D_MODEL), F32)],
        compiler_params=_params("parallel"),
    )(rkv, r_k)


def _scan_kernel(mm_ref, add_ref, pl_ref, y_ref, s_scr):
    c = pl.program_id(0)
    D = D_MODEL

    @pl.when(c == 0)
    def _():
        s_scr[...] = jnp.zeros_like(s_scr)

    mask = _group_masks()
    units = [(b, slice(g * GW, (g + 1) * GW)) for b in range(BATCH) for g in range(N_GROUPS)]
    field = lambda ref, b, n, sl: ref[b, :, n * D + sl.start:n * D + sl.stop]
    s0 = [s_scr[b, :, sl] for b, sl in units]
    o = [lax.dot_general(field(mm_ref, b, 0, sl), _bd(s0[n], mask), NT_DIMS,
                         preferred_element_type=F32) + field(add_ref, b, 0, sl)
         for n, (b, sl) in enumerate(units)]
    sg = [lax.dot_general(s0[n].astype(BF16), _bd(field(mm_ref, b, 1, sl), mask), NT_DIMS,
                          preferred_element_type=F32)
          for n, (b, sl) in enumerate(units)]
    for n, (b, sl) in enumerate(units):
        s_scr[b, :, sl] = s0[n] * pl_ref[b, 0, :, sl] + sg[n] + field(add_ref, b, 1, sl)
    for n, (b, sl) in enumerate(units):
        y_ref[b, :, sl] = o[n]


def _scan(mm, add, p_last):
    phys = lambda c: (c + META_CHUNK0) % N_CHUNKS
    pair = pl.BlockSpec((BATCH, CHUNK, 2 * D_MODEL), lambda c: (0, phys(c), 0))
    return pl.pallas_call(
        _scan_kernel,
        grid=(N_CHUNKS,),
        in_specs=[pair, pair, pl.BlockSpec((BATCH, 1, 1, D_MODEL), lambda c: (0, phys(c), 0, 0))],
        out_specs=pl.BlockSpec((BATCH, CHUNK, D_MODEL), lambda c: (0, phys(c), 0)),
        out_shape=jax.ShapeDtypeStruct((BATCH, TP, D_MODEL), F32),
        scratch_shapes=[pltpu.VMEM((BATCH, RWKV_HEAD, D_MODEL), F32)],
        compiler_params=_params("arbitrary"),
    )(mm, add, p_last)


def _rwkv_out_kernel(o_ref, bonus_ref, g_ref, h_ref, gw_ref, gb_ref, wo_ref, gain_ref, wr_ref,
                     h_out, u_out, route_out, cnt_out, carry):
    @pl.when(jnp.logical_and(pl.program_id(0) == 0, pl.program_id(1) == 0))
    def _():
        carry[...] = jnp.zeros_like(carry)

    mask_bf = _group_masks().astype(BF16)
    inv_n = 1.0 / RWKV_HEAD
    o, bonus, gate = o_ref[0], bonus_ref[0], g_ref[0]
    parts = []
    for g in range(N_GROUPS):
        sl = slice(g * GW, (g + 1) * GW)
        og = o[:, sl]
        d = og - _head_sum(og, mask_bf) * inv_n
        var = _dot((d * d).astype(BF16), mask_bf) * inv_n
        yn = d * lax.rsqrt(var + GN_EPS) * gw_ref[:, sl] + gb_ref[:, sl] + bonus[:, sl]
        parts.append((yn * gate[:, sl]).astype(BF16))
    h = h_ref[0] + _dot(jnp.concatenate(parts, axis=1), wo_ref[...])
    h_out[0] = h
    u = _rms(h, gain_ref[...])
    for c in range(N_COL_CHUNKS):
        u_out[0, 0, c] = u[:, c * COL_CHUNK:(c + 1) * COL_CHUNK]
    u_hi = u.astype(BF16)
    u_lo = (u - u_hi.astype(F32)).astype(BF16)
    logits = lax.dot_general(wr_ref[...], jnp.concatenate([u_hi, u_lo, u_hi], axis=1), NT_DIMS,
                             preferred_element_type=F32)
    e = jnp.exp(logits - jnp.max(logits, axis=0, keepdims=True))
    probs = e / jnp.sum(e, axis=0, keepdims=True)
    idx = lax.broadcasted_iota(jnp.int32, probs.shape, 0).astype(F32)
    m1 = jnp.max(probs, axis=0, keepdims=True)
    i1 = jnp.min(jnp.where(probs == m1, idx, float(N_EXPERTS)), axis=0, keepdims=True)
    sel1 = idx == i1
    rest = jnp.where(sel1, -1.0, probs)
    m2 = jnp.max(rest, axis=0, keepdims=True)
    i2 = jnp.min(jnp.where(rest == m2, idx, float(N_EXPERTS)), axis=0, keepdims=True)
    sel2 = idx == i2
    onehot = jnp.where(jnp.logical_or(sel1, sel2), 1.0, 0.0).astype(F32)
    tr = lax.broadcasted_iota(jnp.int32, (TOKEN_BLOCK, TOKEN_BLOCK), 0)
    tc = lax.broadcasted_iota(jnp.int32, (TOKEN_BLOCK, TOKEN_BLOCK), 1)
    earlier = _dot(onehot.astype(BF16), jnp.where(tr < tc, 1.0, 0.0).astype(BF16)) + carry[...]
    rank1 = jnp.sum(jnp.where(sel1, earlier, 0.0), axis=0, keepdims=True)
    rank2 = jnp.sum(jnp.where(sel2, earlier, 0.0), axis=0, keepdims=True)
    den = m1 + m2
    fields = (i1, i2, rank1, rank2, m1 / den, m2 / den)
    route = jnp.zeros(probs.shape, F32)
    for n, val in enumerate(fields):
        route = jnp.where(idx == float(n), val, route)
    route_out[0] = route
    tile_cnt = jnp.sum(onehot, axis=1, keepdims=True)
    cnt_out[0] = tile_cnt
    carry[...] += tile_cnt


def _rwkv_out(o, bonus, g, h, gn_w, gn_b, w_o, gain, w_router):
    blocks = SEQ // TOKEN_BLOCK
    row = pl.BlockSpec((1, TOKEN_BLOCK, D_MODEL), lambda b, i: (b, i, 0))
    vec = pl.BlockSpec((1, D_MODEL), lambda b, i: (0, 0))
    return pl.pallas_call(
        _rwkv_out_kernel,
        grid=(BATCH, blocks),
        in_specs=[row, row, row, row, vec, vec,
                  pl.BlockSpec((D_MODEL, D_MODEL), lambda b, i: (0, 0)), vec,
                  pl.BlockSpec((N_EXPERTS, 3 * D_MODEL), lambda b, i: (0, 0))],
        out_specs=[row, pl.BlockSpec((1, 1, N_COL_CHUNKS, TOKEN_BLOCK, COL_CHUNK), lambda b, i: (b, i, 0, 0, 0)),
                   pl.BlockSpec((1, N_EXPERTS, TOKEN_BLOCK), lambda b, i: (b, 0, i)),
                   pl.BlockSpec((1, N_EXPERTS, 1), lambda b, i: (b * blocks + i, 0, 0))],
        out_shape=[jax.ShapeDtypeStruct((BATCH, SEQ, D_MODEL), F32),
                   jax.ShapeDtypeStruct((BATCH, blocks, N_COL_CHUNKS, TOKEN_BLOCK, COL_CHUNK), F32),
                   jax.ShapeDtypeStruct((BATCH, N_EXPERTS, SEQ), F32),
                   jax.ShapeDtypeStruct((BATCH * blocks, N_EXPERTS, 1), F32)],
        scratch_shapes=[pltpu.VMEM((N_EXPERTS, 1), F32)],
        compiler_params=_params("arbitrary", "arbitrary"),
    )(o, bonus, g, h, gn_w, gn_b, w_o, gain, w_router)


def _routing_tables(route, cnt):
    i32 = jnp.int32
    route = jnp.swapaxes(route, 0, 1).reshape(N_EXPERTS, N_TOKENS)
    expert = route[0:2].astype(i32)
    rank = route[2:4].astype(i32)
    gate = route[4:6].T
    counts = jnp.sum(cnt.reshape(-1, N_EXPERTS).astype(i32), axis=0)
    tiles_e = (counts + EXPERT_TILE - 1) // EXPERT_TILE
    tile_end = jnp.cumsum(tiles_e)
    n_used = tile_end[-1]
    start_row = (tile_end - tiles_e) * EXPERT_TILE
    group_start = sum(jnp.where(expert == e, start_row[e], 0) for e in range(N_EXPERTS))
    pos = group_start + rank
    tiles = jnp.arange(MAX_EXPERT_TILES, dtype=i32)
    tile_expert = jnp.minimum(jnp.sum(tiles[:, None] >= tile_end[None, :], axis=1), N_EXPERTS - 1).astype(i32)
    tile_rows = jnp.take(counts, tile_expert) - (tiles - jnp.take(tile_end - tiles_e, tile_expert)) * EXPERT_TILE
    tile_rows = jnp.where(tiles < n_used, jnp.clip(tile_rows, 0, EXPERT_TILE), 0)
    pos = pos.reshape(TOP_K * N_TOKEN_BLOCKS, 1, TOKEN_BLOCK)
    chunk = jnp.arange(N_COL_CHUNKS, dtype=i32)[None, :, None]
    sorted_piece = (((pos // EXPERT_TILE) * N_COL_CHUNKS + chunk) * EXPERT_TILE + pos % EXPERT_TILE).reshape(-1)
    return dict(sorted_piece=sorted_piece, gate=gate, tile_expert=tile_expert,
                n_used=n_used.reshape(1).astype(i32), tile_rows=tile_rows.astype(i32))


def _row_gather(x, indices):
    m = indices.shape[0]
    mesh = plsc.VectorSubcoreMesh(core_axis_name="c", subcore_axis_name="s")

    @pl.kernel(out_type=jax.ShapeDtypeStruct((m, COL_CHUNK), x.dtype), mesh=mesh)
    def gather(x_hbm, i_hbm, o_hbm):
        def body(i_vmem, o_vmem):
            pltpu.sync_copy(x_hbm.at[i_vmem.at[0]], o_vmem)

        pltpu.emit_pipeline(
            body, grid=(m // GATHER_ROWS,),
            in_specs=[pl.BlockSpec((1, GATHER_ROWS), lambda i: (0, i))],
            out_specs=[pl.BlockSpec((GATHER_ROWS, COL_CHUNK), lambda i: (i, 0))],
            core_axis_name=("c", "s"),
            dimension_semantics=(pltpu.PARALLEL,),
        )(i_hbm, o_hbm)

    return gather(x, indices.reshape(1, m))


def _row_scatter(x, indices, out_rows):
    m = indices.shape[0]
    x_blocks = x.shape[0] // GATHER_ROWS
    mesh = plsc.VectorSubcoreMesh(core_axis_name="c", subcore_axis_name="s")

    @pl.kernel(out_type=jax.ShapeDtypeStruct((out_rows, COL_CHUNK), x.dtype), mesh=mesh)
    def scatter(x_hbm, i_hbm, o_hbm):
        def body(x_vmem, i_vmem):
            pltpu.sync_copy(x_vmem, o_hbm.at[i_vmem.at[0]])

        pltpu.emit_pipeline(
            body, grid=(m // GATHER_ROWS,),
            in_specs=[pl.BlockSpec((GATHER_ROWS, COL_CHUNK), lambda i: (i % x_blocks, 0)),
                      pl.BlockSpec((1, GATHER_ROWS), lambda i: (0, i))],
            out_specs=[],
            core_axis_name=("c", "s"),
            dimension_semantics=(pltpu.PARALLEL,),
        )(x_hbm, i_hbm)

    return scatter(x, indices.reshape(1, m))


def _expert_kernel(te_ref, nu_ref, nr_ref, x_ref, wg_ref, wu_ref, wd_ref, y_ref, xb):
    i = pl.program_id(0)
    f = pl.program_id(1)
    n_rows = nr_ref[i]
    n_sub = (n_rows + SUB_TILE - 1) // SUB_TILE
    subs = EXPERT_TILE // SUB_TILE

    @pl.when(n_sub > 0)
    def _():
        @pl.when(f == 0)
        def _():
            real = lax.broadcasted_iota(jnp.int32, (EXPERT_TILE, 1), 0) < n_rows
            for c in range(N_COL_CHUNKS):
                piece = jnp.where(real, x_ref[c * EXPERT_TILE:(c + 1) * EXPERT_TILE, :], 0.0)
                xb[:, c * COL_CHUNK:(c + 1) * COL_CHUNK] = piece.astype(BF16)
            y_ref[...] = jnp.zeros_like(y_ref)

        wg = wg_ref[0].astype(BF16)
        wu = wu_ref[0].astype(BF16)
        wd = wd_ref[0].astype(BF16)

        def block(rows):
            x = xb[rows, :]
            a = _dot(x, wg)
            b = _dot(x, wu)
            part = _dot((a * _sigmoid(a) * b).astype(BF16), wd)
            for c in range(N_COL_CHUNKS):
                y_ref[c * EXPERT_TILE + rows.start:c * EXPERT_TILE + rows.stop, :] += part[:, c * COL_CHUNK:(c + 1) * COL_CHUNK]

        @pl.when(n_sub == subs)
        def _():
            block(slice(0, EXPERT_TILE))

        @pl.when(n_sub < subs)
        def _():
            for k in range(subs - 1):
                pl.when(k < n_sub)(functools.partial(block, slice(k * SUB_TILE, (k + 1) * SUB_TILE)))


def _experts(tab, x_sorted, wg, wu, wd):
    n_ff = D_FF // FF_TILE
    tile = lambda i, nu: jnp.minimum(i, nu[0] - 1)
    ff = lambda i, f, nu: jnp.where(i < nu[0], f, n_ff - 1)
    grid_spec = pltpu.PrefetchScalarGridSpec(
        num_scalar_prefetch=3, grid=(MAX_EXPERT_TILES, n_ff),
        in_specs=[pl.BlockSpec((N_COL_CHUNKS * EXPERT_TILE, COL_CHUNK), lambda i, f, te, nu, ns: (tile(i, nu), 0)),
                  pl.BlockSpec((1, D_MODEL, FF_TILE),
                               lambda i, f, te, nu, ns: (te[tile(i, nu)], 0, ff(i, f, nu))),
                  pl.BlockSpec((1, D_MODEL, FF_TILE),
                               lambda i, f, te, nu, ns: (te[tile(i, nu)], 0, ff(i, f, nu))),
                  pl.BlockSpec((1, FF_TILE, D_MODEL),
                               lambda i, f, te, nu, ns: (te[tile(i, nu)], ff(i, f, nu), 0))],
        out_specs=pl.BlockSpec((N_COL_CHUNKS * EXPERT_TILE, COL_CHUNK), lambda i, f, te, nu, ns: (tile(i, nu), 0)),
        scratch_shapes=[pltpu.VMEM((EXPERT_TILE, D_MODEL), BF16)])
    return pl.pallas_call(
        _expert_kernel, grid_spec=grid_spec,
        out_shape=jax.ShapeDtypeStruct((N_COL_CHUNKS * MAX_SORTED_ROWS, COL_CHUNK), F32),
        compiler_params=_params("arbitrary", "arbitrary"),
    )(tab["tile_expert"], tab["n_used"], tab["tile_rows"], x_sorted, wg, wu, wd)


def _combine_kernel(h_ref, y_ref, g_ref, o_ref):
    g = g_ref[...]
    for c in range(N_COL_CHUNKS):
        cols = slice(c * COL_CHUNK, (c + 1) * COL_CHUNK)
        o_ref[:, cols] = h_ref[:, cols] + g[:, 0:1] * y_ref[0, 0, c] + g[:, 1:2] * y_ref[1, 0, c]


def _combine(h, y_pair, gate):
    row = pl.BlockSpec((TOKEN_BLOCK, D_MODEL), lambda i: (i, 0))
    return pl.pallas_call(
        _combine_kernel,
        grid=(N_TOKEN_BLOCKS,),
        in_specs=[row, pl.BlockSpec((TOP_K, 1, N_COL_CHUNKS, TOKEN_BLOCK, COL_CHUNK), lambda i: (0, i, 0, 0, 0)),
                  pl.BlockSpec((TOKEN_BLOCK, TOP_K), lambda i: (i, 0))],
        out_specs=row,
        out_shape=jax.ShapeDtypeStruct((N_TOKENS, D_MODEL), F32),
        compiler_params=_params("parallel"),
    )(h, y_pair, gate)


def kernel(x, meta_tokens, mixer_norm, ffn_norm, attn_w_qkv, attn_q_norm, attn_k_norm, attn_sinks, attn_w_o, rwkv_mix, rwkv_w0, rwkv_w1, rwkv_w2, rwkv_a0, rwkv_a1, rwkv_a2, rwkv_g1, rwkv_g2, rwkv_k_k, rwkv_k_a, rwkv_r_k, rwkv_w_r, rwkv_w_k, rwkv_w_v, rwkv_w_o, rwkv_gn_w, rwkv_gn_b, ffn_w_gate, ffn_w_up, ffn_w_down, moe_router, moe_w_gate, moe_w_up, moe_w_down):
    bf = lambda a: a.astype(BF16)
    vec = lambda a: a.reshape(1, -1).astype(F32)
    h_meta = jnp.concatenate([jnp.zeros((PAD, D_MODEL), F32), meta_tokens.astype(F32)], axis=0)

    w_qkv = bf(attn_w_qkv[0])
    qkv = _qkv(x.reshape(N_TOKENS, D_MODEL), vec(mixer_norm[0]), w_qkv, TOKEN_BLOCK)
    qkv_meta = _qkv(h_meta, vec(mixer_norm[0]), w_qkv, BLOCK)
    h = _attention(x, h_meta, qkv.reshape(BATCH, SEQ, QKV_DIM), qkv_meta, attn_sinks[0].astype(F32),
                   jnp.tile(vec(attn_q_norm[0]), (1, Q_PER_KV)), jnp.tile(vec(attn_k_norm[0]), (1, N_KV_HEADS)),
                   bf(attn_w_o[0]))
    h = _ffn(h.reshape(N_ROWS, D_MODEL), vec(ffn_norm[0]), bf(ffn_w_gate[0]), bf(ffn_w_up[0]),
             bf(ffn_w_down[0]))

    rkv, g = _rwkv_proj(
        h, vec(mixer_norm[1]), rwkv_mix[0], vec(rwkv_w0[0]), vec(rwkv_a0[0]), vec(rwkv_k_k[0]),
        vec(rwkv_k_a[0]), bf(rwkv_w_r[0]), bf(rwkv_w_k[0]), bf(rwkv_w_v[0]), bf(rwkv_w1[0]),
        bf(rwkv_w2[0]), bf(rwkv_a1[0]), bf(rwkv_a2[0]), bf(rwkv_g1[0]), bf(rwkv_g2[0]))
    b3 = lambda t: t.reshape(BATCH, TP, -1)
    mm, add, bonus, p_last = _scan_prep(b3(rkv), vec(rwkv_r_k[0]))
    o = _scan(mm, add, p_last)
    w_router = moe_router[0].astype(F32).T
    wr_hi = bf(w_router)
    wr_lo = bf(w_router - wr_hi.astype(F32))
    h, u, route, cnt = _rwkv_out(o, bonus, b3(g), b3(h), vec(rwkv_gn_w[0]), vec(rwkv_gn_b[0]),
                                 bf(rwkv_w_o[0]), vec(ffn_norm[1]),
                                 jnp.concatenate([wr_hi, wr_hi, wr_lo], axis=1))
    tab = _routing_tables(route, cnt)
    x_sorted = _row_scatter(u.reshape(-1, COL_CHUNK), tab["sorted_piece"], N_COL_CHUNKS * MAX_SORTED_ROWS)
    y_sorted = _experts(tab, x_sorted, moe_w_gate[0], moe_w_up[0], moe_w_down[0])
    y_pair = _row_gather(y_sorted, tab["sorted_piece"]).reshape(TOP_K, N_TOKEN_BLOCKS, N_COL_CHUNKS, TOKEN_BLOCK, COL_CHUNK)
    out = _combine(h.reshape(N_TOKENS, D_MODEL), y_pair, tab["gate"])
    return out.reshape(BATCH, SEQ, D_MODEL)
```

```python
import functools

import jax
import jax.numpy as jnp
from jax import lax
from jax.experimental import pallas as pl
from jax.experimental.pallas import tpu as pltpu
from jax.experimental.pallas import tpu_sc as plsc

F32 = jnp.float32
BF16 = jnp.bfloat16

D_MODEL = 1024
BATCH = 2
SEQ = 8192
N_META = 16
BLOCK = 128
PAD = BLOCK - N_META
TP = SEQ + BLOCK
N_ROWS = BATCH * TP
N_BLOCKS = TP // BLOCK
META_BLOCK = N_BLOCKS - 1
HEAD_DIM = 64
N_Q_HEADS = 16
N_KV_HEADS = 4
Q_PER_KV = 4
QKV_DIM = (N_Q_HEADS + 2 * N_KV_HEADS) * HEAD_DIM
RWKV_HEADS = 16
RWKV_HEAD = 64
D_FF = 3584
N_EXPERTS = 8
NORM_EPS = 1e-5
GN_EPS = 64e-5
CHUNK = 64
N_CHUNKS = TP // CHUNK
META_CHUNK0 = SEQ // CHUNK
HEADS_PER_GROUP = 4
GW = HEADS_PER_GROUP * RWKV_HEAD
N_GROUPS = RWKV_HEADS // HEADS_PER_GROUP
N_SCAN_IN = 6

FFN_ROW_TILE = 1280
PROJ_TILE = 320
FF_TILE = 512
N_TOKENS = BATCH * SEQ
TOP_K = 2
TOKEN_BLOCK = 512
N_TOKEN_BLOCKS = N_TOKENS // TOKEN_BLOCK
EXPERT_TILE = 1024
SUB_TILE = 256
MAX_EXPERT_TILES = TOP_K * N_TOKENS // EXPERT_TILE + N_EXPERTS
MAX_SORTED_ROWS = MAX_EXPERT_TILES * EXPERT_TILE
COL_CHUNK = 256
N_COL_CHUNKS = D_MODEL // COL_CHUNK
PIECE_WORDS = COL_CHUNK // 2
U32 = jnp.uint32
GATHER_ROWS = 128
VMEM_LIMIT = 56 * 1024 * 1024
NEG = -1e30

NT_DIMS = (((1,), (1,)), ((), ()))
TN_DIMS = (((0,), (0,)), ((), ()))


def _params(*sem):
    return pltpu.CompilerParams(dimension_semantics=sem, vmem_limit_bytes=VMEM_LIMIT)


def _rms(x, gain):
    return x * lax.rsqrt(jnp.mean(x * x, axis=-1, keepdims=True) + NORM_EPS) * gain


def _sigmoid(x):
    return 1.0 / (1.0 + jnp.exp(-x))


def _dot(a, b):
    return jnp.dot(a, b, preferred_element_type=F32)


def _pack_piece(x):
    return pltpu.pack_elementwise([x[:, :PIECE_WORDS], x[:, PIECE_WORDS:]], packed_dtype=BF16)


def _unpack_piece(w):
    return tuple(pltpu.unpack_elementwise(w, index=i, packed_dtype=BF16, unpacked_dtype=F32) for i in (0, 1))


def _qkv_kernel(h_ref, g_ref, w_ref, o_ref):
    u = _rms(h_ref[...], g_ref[...])
    o_ref[...] = _dot(u.astype(BF16), w_ref[...]).astype(BF16)


def _qkv(h, gain, w, tile):
    rows = h.shape[0]
    return pl.pallas_call(
        _qkv_kernel,
        grid=(rows // tile,),
        in_specs=[pl.BlockSpec((tile, D_MODEL), lambda i: (i, 0)),
                  pl.BlockSpec((1, D_MODEL), lambda i: (0, 0)),
                  pl.BlockSpec((D_MODEL, QKV_DIM), lambda i: (0, 0))],
        out_specs=pl.BlockSpec((tile, QKV_DIM), lambda i: (i, 0)),
        out_shape=jax.ShapeDtypeStruct((rows, QKV_DIM), BF16),
        compiler_params=_params("parallel"),
    )(h, gain, w)


def _attn_kernel(sink_ref, x_ref, hm_ref, q_ref, kc_ref, vc_ref, kp_ref, vp_ref, qm_ref, km_ref, vm_ref,
                 qg_ref, kg_ref, wo_ref, o_ref, o_scr):
    j = pl.program_id(1)
    n_keys = 3 * BLOCK
    rowi = lax.broadcasted_iota(jnp.int32, (BLOCK, n_keys), 0)
    col = lax.broadcasted_iota(jnp.int32, (BLOCK, n_keys), 1)
    far = 4 * BLOCK
    is_real = j < META_BLOCK
    meta_off = jnp.where(is_real, -far, 0)
    prev_off = jnp.where(jnp.logical_and(j >= 1, is_real), 0, far)
    cur_off = jnp.where(is_real, 0, far)
    meta_ok = jnp.logical_and(jnp.logical_and(col >= PAD, col < BLOCK), col + meta_off <= rowi)
    prev_ok = jnp.logical_and(jnp.logical_and(col >= BLOCK, col < 2 * BLOCK),
                              col - BLOCK > rowi + prev_off)
    cur_ok = jnp.logical_and(col >= 2 * BLOCK, col - 2 * BLOCK + cur_off <= rowi)
    ok = jnp.logical_or(jnp.logical_or(meta_ok, prev_ok), cur_ok)
    ok4 = jnp.concatenate([ok] * Q_PER_KV, axis=1)

    groups = range(N_KV_HEADS)
    mask_bf = _group_masks().astype(BF16)
    rb = lax.broadcasted_iota(jnp.int32, (Q_PER_KV * n_keys, GW), 0) // n_keys
    lb = lax.broadcasted_iota(jnp.int32, (Q_PER_KV * n_keys, GW), 1) // HEAD_DIM
    block_mask = jnp.where(rb == lb, 1.0, 0.0).astype(BF16)
    sr = lax.broadcasted_iota(jnp.int32, (GW, GW), 0)
    sc = lax.broadcasted_iota(jnp.int32, (GW, GW), 1)
    lane_head = lax.broadcasted_iota(jnp.int32, (1, GW), 1) // HEAD_DIM
    inv_d = 1.0 / HEAD_DIM
    scale = HEAD_DIM ** -0.5

    real_rows = lax.broadcasted_iota(jnp.int32, (BLOCK, 1), 0) < jnp.where(is_real, BLOCK, 0)
    km, vm = km_ref[...], vm_ref[...]
    kcur = jnp.where(real_rows, kc_ref[0], km)
    vcur = jnp.where(real_rows, vc_ref[0], vm)
    kall = jnp.concatenate([km, kp_ref[0], kcur], axis=0).astype(F32)
    vall = jnp.concatenate([vm, vp_ref[0], vcur], axis=0)
    kss = _dot((kall * kall).astype(BF16), mask_bf)
    kn = (kall * lax.rsqrt(kss * inv_d + NORM_EPS) * kg_ref[...]).astype(BF16)
    q_all = jnp.where(real_rows, q_ref[0], qm_ref[...]).astype(F32)
    qn = []
    for g in groups:
        qg = q_all[:, g * GW:(g + 1) * GW]
        qss = _dot((qg * qg).astype(BF16), mask_bf)
        qn.append((qg * lax.rsqrt(qss * inv_d + NORM_EPS) * (qg_ref[...] * scale)).astype(BF16))
    sel = [jnp.where(sr == g * HEAD_DIM + sc % HEAD_DIM, 1.0, 0.0).astype(BF16) for g in groups]
    krep = [_dot(kn, sel[g]).astype(BF16) for g in groups]
    vrep = [_dot(vall, sel[g]).astype(BF16) for g in groups]
    bdk = [jnp.concatenate([krep[g]] * Q_PER_KV, axis=0) * block_mask for g in groups]
    rhs = [jnp.concatenate([jnp.concatenate([vrep[g]] * Q_PER_KV, axis=0) * block_mask, block_mask], axis=1)
           for g in groups]
    s = [jnp.where(ok4, lax.dot_general(qn[g], bdk[g], NT_DIMS, preferred_element_type=F32), NEG)
         for g in groups]
    p, sink_den = [], []
    for g in groups:
        parts = []
        sd = jnp.zeros((BLOCK, GW), F32)
        for hh in range(Q_PER_KV):
            seg = s[g][:, hh * n_keys:(hh + 1) * n_keys]
            sink = sink_ref[g * Q_PER_KV + hh]
            m = jnp.maximum(jnp.max(seg, axis=-1, keepdims=True), sink)
            parts.append(jnp.exp(seg - m).astype(BF16))
            sd = sd + jnp.exp(sink - m) * jnp.where(lane_head == hh, 1.0, 0.0)
        p.append(jnp.concatenate(parts, axis=1))
        sink_den.append(sd)
    ov = [_dot(p[g], rhs[g]) for g in groups]
    for g in groups:
        o_scr[:, g * GW:(g + 1) * GW] = ov[g][:, :GW] / (ov[g][:, GW:] + sink_den[g])
    h = jnp.where(real_rows, x_ref[0], hm_ref[...])
    o_ref[0] = h + _dot(o_scr[...].astype(BF16), wo_ref[...])


def _attention(x, h_meta, qkv, qkv_meta, sinks, q_gain, k_gain, w_o):
    kcol, vcol = N_Q_HEADS * HEAD_DIM // 256, N_Q_HEADS * HEAD_DIM // 256 + 1
    kvw = N_KV_HEADS * HEAD_DIM
    real = lambda j: jnp.minimum(j, META_BLOCK - 1)
    prev = lambda j: jnp.clip(j - 1, 0, META_BLOCK - 1)
    return pl.pallas_call(
        _attn_kernel,
        grid=(BATCH, N_BLOCKS),
        in_specs=[pl.BlockSpec(memory_space=pltpu.SMEM),
                  pl.BlockSpec((1, BLOCK, D_MODEL), lambda b, j: (b, real(j), 0)),
                  pl.BlockSpec((BLOCK, D_MODEL), lambda b, j: (0, 0)),
                  pl.BlockSpec((1, BLOCK, D_MODEL), lambda b, j: (b, real(j), 0)),
                  pl.BlockSpec((1, BLOCK, kvw), lambda b, j: (b, real(j), kcol)),
                  pl.BlockSpec((1, BLOCK, kvw), lambda b, j: (b, real(j), vcol)),
                  pl.BlockSpec((1, BLOCK, kvw), lambda b, j: (b, prev(j), kcol)),
                  pl.BlockSpec((1, BLOCK, kvw), lambda b, j: (b, prev(j), vcol)),
                  pl.BlockSpec((BLOCK, D_MODEL), lambda b, j: (0, 0)),
                  pl.BlockSpec((BLOCK, kvw), lambda b, j: (0, kcol)),
                  pl.BlockSpec((BLOCK, kvw), lambda b, j: (0, vcol)),
                  pl.BlockSpec((1, GW), lambda b, j: (0, 0)),
                  pl.BlockSpec((1, GW), lambda b, j: (0, 0)),
                  pl.BlockSpec((D_MODEL, D_MODEL), lambda b, j: (0, 0))],
        out_specs=pl.BlockSpec((1, BLOCK, D_MODEL), lambda b, j: (b, j, 0)),
        out_shape=jax.ShapeDtypeStruct((BATCH, TP, D_MODEL), F32),
        scratch_shapes=[pltpu.VMEM((BLOCK, D_MODEL), F32)],
        compiler_params=_params("parallel", "parallel"),
    )(sinks, x, h_meta, qkv, qkv, qkv, qkv, qkv, qkv_meta, qkv_meta, qkv_meta, q_gain, k_gain, w_o)


def _ffn_kernel(h_ref, g_ref, wg_ref, wu_ref, wd_ref, o_ref, u_scr, acc):
    f = pl.program_id(1)

    @pl.when(f == 0)
    def _():
        u_scr[...] = _rms(h_ref[...], g_ref[...]).astype(BF16)
        acc[...] = jnp.zeros_like(acc)

    u = u_scr[...]
    a = _dot(u, wg_ref[...])
    b = _dot(u, wu_ref[...])
    acc[...] += _dot((a * _sigmoid(a) * b).astype(BF16), wd_ref[...])

    @pl.when(f == pl.num_programs(1) - 1)
    def _():
        o_ref[...] = h_ref[...] + acc[...]


def _ffn(h, gain, wg, wu, wd):
    return pl.pallas_call(
        _ffn_kernel,
        grid=(N_ROWS // FFN_ROW_TILE, D_FF // FF_TILE),
        in_specs=[pl.BlockSpec((FFN_ROW_TILE, D_MODEL), lambda i, f: (i, 0)),
                  pl.BlockSpec((1, D_MODEL), lambda i, f: (0, 0)),
                  pl.BlockSpec((D_MODEL, FF_TILE), lambda i, f: (0, f)),
                  pl.BlockSpec((D_MODEL, FF_TILE), lambda i, f: (0, f)),
                  pl.BlockSpec((FF_TILE, D_MODEL), lambda i, f: (f, 0))],
        out_specs=pl.BlockSpec((FFN_ROW_TILE, D_MODEL), lambda i, f: (i, 0)),
        out_shape=jax.ShapeDtypeStruct((N_ROWS, D_MODEL), F32),
        scratch_shapes=[pltpu.VMEM((FFN_ROW_TILE, D_MODEL), BF16), pltpu.VMEM((FFN_ROW_TILE, D_MODEL), F32)],
        compiler_params=_params("parallel", "arbitrary"),
    )(h, gain, wg, wu, wd)


def _rwkv_proj_kernel(h_ref, hp_ref, g_ref, mix_ref, w0_ref, a0_ref, kk_ref, ka_ref,
                      wr_ref, wk_ref, wv_ref, w1_ref, w2_ref, a1_ref, a2_ref, g1_ref, g2_ref,
                      rkv_out, g_out):
    i = pl.program_id(0)
    tiles_per_batch = TP // PROJ_TILE
    r0 = (i % tiles_per_batch) * PROJ_TILE
    local = lax.broadcasted_iota(jnp.int32, (PROJ_TILE, 1), 0)
    lrow = local + r0
    gain = g_ref[...]
    is_pad = jnp.logical_and(lrow >= SEQ, lrow < SEQ + PAD)
    u = jnp.where(is_pad, 0.0, _rms(h_ref[...], gain))
    u_prev_tile = _rms(hp_ref[7:8, :], gain)
    xprev = pltpu.roll(u, 1, 0)
    xprev = jnp.where(local == 0, u_prev_tile, xprev)
    xprev = jnp.where(lrow == SEQ, 0.0, xprev)
    xx = xprev - u
    mix = mix_ref[...]
    lerp = lambda n: (u + xx * mix[n:n + 1, :]).astype(BF16)
    xr, xw, xk, xv, xa, xg = [lerp(n) for n in range(6)]
    r = _dot(xr, wr_ref[...])
    k = _dot(xk, wk_ref[...])
    v = _dot(xv, wv_ref[...])
    lw = _dot(jnp.tanh(_dot(xw, w1_ref[...])).astype(BF16), w2_ref[...])
    z = -(w0_ref[...] + lw)
    softplus = jnp.maximum(z, 0.0) + jnp.log(1.0 + jnp.exp(-jnp.abs(z)))
    w = -softplus - 0.5
    a = _sigmoid(a0_ref[...] + _dot(_dot(xa, a1_ref[...]).astype(BF16), a2_ref[...]))
    g = _dot(_sigmoid(_dot(xg, g1_ref[...])).astype(BF16), g2_ref[...])
    fields = (r, -jnp.exp(w),
              k * (1.0 + (a - 1.0) * ka_ref[...]), v, k * kk_ref[...], a)
    for n, val in enumerate(fields):
        rkv_out[:, n * D_MODEL:(n + 1) * D_MODEL] = val
    g_out[...] = g


def _rwkv_proj(h, gain, mix, w0, a0, k_k, k_a, w_r, w_k, w_v, w1, w2, a1, a2, g1, g2):
    tiles_per_batch = TP // PROJ_TILE
    rows8 = PROJ_TILE // 8

    def prev_map(i):
        b = i // tiles_per_batch
        first = (i % tiles_per_batch) == 0
        return (jnp.where(first, (b * TP + TP - 8) // 8, i * rows8 - 1), 0)

    row = pl.BlockSpec((PROJ_TILE, D_MODEL), lambda i: (i, 0))
    full = lambda a: pl.BlockSpec(a.shape, lambda i: (0,) * a.ndim)
    smalls = (gain, mix, w0, a0, k_k, k_a, w_r, w_k, w_v, w1, w2, a1, a2, g1, g2)
    return pl.pallas_call(
        _rwkv_proj_kernel,
        grid=(N_ROWS // PROJ_TILE,),
        in_specs=[row, pl.BlockSpec((8, D_MODEL), prev_map)] + [full(a) for a in smalls],
        out_specs=[pl.BlockSpec((PROJ_TILE, N_SCAN_IN * D_MODEL), lambda i: (i, 0)), row],
        out_shape=[jax.ShapeDtypeStruct((N_ROWS, N_SCAN_IN * D_MODEL), F32),
                   jax.ShapeDtypeStruct((N_ROWS, D_MODEL), F32)],
        compiler_params=_params("parallel"),
    )(h, h, *smalls)


def _group_masks():
    ri = lax.broadcasted_iota(jnp.int32, (GW, GW), 0) // RWKV_HEAD
    ci = lax.broadcasted_iota(jnp.int32, (GW, GW), 1) // RWKV_HEAD
    return jnp.where(ri == ci, 1.0, 0.0).astype(F32)


def _bd(x, mask):
    return jnp.concatenate([x.astype(BF16)] * HEADS_PER_GROUP, axis=0) * mask.astype(BF16)


def _diag_blocks(full, mask):
    m = full * mask
    n = RWKV_HEAD
    return (m[0:n] + m[n:2 * n]) + (m[2 * n:3 * n] + m[3 * n:4 * n])


def _head_sum(x, mask_bf):
    hi = x.astype(BF16)
    lo = (x - hi.astype(F32)).astype(BF16)
    return _dot(hi, mask_bf) + _dot(lo, mask_bf)


def _scan_prep_kernel(x_ref, rk_ref, mm_out, add_out, bonus_out, pl_out):
    L, D = CHUNK, D_MODEL
    units = [(b, g) for b in range(BATCH) for g in range(N_GROUPS)]
    un = range(len(units))
    mask = _group_masks()
    mask_bf = mask.astype(BF16)
    ri = lax.broadcasted_iota(jnp.int32, (L, GW), 0)
    ci = lax.broadcasted_iota(jnp.int32, (L, GW), 1) % RWKV_HEAD
    incl = ci <= ri
    strict = ci < ri
    eye = jnp.where(ci == ri, 1.0, 0.0).astype(F32)
    t_r = lax.broadcasted_iota(jnp.int32, (L, L), 0)
    t_c = lax.broadcasted_iota(jnp.int32, (L, L), 1)
    tril = jnp.where(t_c <= t_r, 1.0, 0.0).astype(BF16)
    rk_all = rk_ref[...]

    def field(b, n, g):
        return x_ref[b, :, n * D + g * GW:n * D + (g + 1) * GW]

    def put(ref, b, n, g, val):
        ref[b, :, n * D + g * GW:n * D + (g + 1) * GW] = val.astype(ref.dtype)

    at, rt, bt, kt, v, plast = [], [], [], [], [], []
    for b in range(BATCH):
        ld = x_ref[b, :, D:2 * D]
        hi = ld.astype(BF16)
        rest = ld - hi.astype(F32)
        mid = rest.astype(BF16)
        lo = (rest - mid.astype(F32)).astype(BF16)
        cs = _dot(tril, hi) + _dot(tril, mid) + _dot(tril, lo)
        p_all = jnp.exp(cs)
        pprev_all = jnp.exp(cs - ld)
        pinv_all = jnp.exp(-cs)
        pl_all = p_all[L - 1:L, :]
        pl_out[b, 0] = pl_all
        for g in range(N_GROUPS):
            sl = slice(g * GW, (g + 1) * GW)
            r, k, vv, kk, a = field(b, 0, g), field(b, 2, g), field(b, 3, g), field(b, 4, g), field(b, 5, g)
            nrm = jnp.sqrt(_dot((kk * kk).astype(BF16), mask_bf))
            kk = kk / jnp.maximum(nrm, 1e-12)
            at.append(-kk * pprev_all[:, sl])
            bt.append(kk * a * pinv_all[:, sl])
            rt.append(r * p_all[:, sl])
            kt.append(k * pinv_all[:, sl])
            v.append(vv)
            plast.append(pl_all[:, sl])
            put(bonus_out, b, 0, g, _dot((r * k * rk_all[:, sl]).astype(BF16), mask_bf) * vv)

    a_ab, a_ak, a_rb, a_rk = [], [], [], []
    for n in un:
        lhs = jnp.concatenate([at[n], rt[n]], axis=0).astype(BF16)
        rhs = jnp.concatenate([_bd(bt[n], mask), _bd(kt[n], mask)], axis=0)
        big = lax.dot_general(lhs, rhs, NT_DIMS, preferred_element_type=F32)
        a_ab.append(jnp.where(strict, big[:L, :GW], 0.0))
        a_ak.append(jnp.where(strict, big[:L, GW:], 0.0))
        a_rb.append(jnp.where(incl, big[L:, :GW], 0.0))
        a_rk.append(jnp.where(incl, big[L:, GW:], 0.0))

    x = [_dot(a_ab[n].astype(BF16), _bd(a_ab[n], mask)) for n in un]
    inv = [eye + a_ab[n] for n in un]
    for step in range(5):
        for n in un:
            rhs = _bd(x[n], mask)
            if step < 4:
                res = _dot(jnp.concatenate([x[n], inv[n]], axis=0).astype(BF16), rhs)
                x[n] = res[:L]
                inv[n] = inv[n] + res[L:]
            else:
                inv[n] = inv[n] + _dot(inv[n].astype(BF16), rhs)

    av = [_dot(jnp.concatenate([a_ak[n], a_rk[n]], axis=0).astype(BF16), _bd(v[n], mask)) for n in un]
    wu = [_dot(inv[n].astype(BF16), jnp.concatenate([_bd(at[n], mask), _bd(av[n][:L], mask)], axis=1))
          for n in un]
    aw = [_dot(a_rb[n].astype(BF16),
               jnp.concatenate([_bd(wu[n][:, :GW], mask), _bd(wu[n][:, GW:], mask)], axis=1))
          for n in un]
    for n, (b, g) in enumerate(units):
        put(mm_out, b, 0, g, rt[n] + aw[n][:, :GW])
        put(add_out, b, 0, g, av[n][L:] + aw[n][:, GW:])
        bh = (bt[n] * plast[n]).astype(BF16)
        kh = (kt[n] * plast[n]).astype(BF16)
        w_b, u0_b = wu[n][:, :GW].astype(BF16), wu[n][:, GW:].astype(BF16)
        gfull = lax.dot_general(bh, w_b, TN_DIMS, preferred_element_type=F32)
        put(mm_out, b, 1, g, _diag_blocks(gfull, mask))
        hfull = lax.dot_general(jnp.concatenate([u0_b, v[n].astype(BF16)], axis=0),
                                jnp.concatenate([bh, kh], axis=0), TN_DIMS, preferred_element_type=F32)
        put(add_out, b, 1, g, _diag_blocks(hfull, mask))


def _scan_prep(rkv, r_k):
    return pl.pallas_call(
        _scan_prep_kernel,
        grid=(N_CHUNKS,),
        in_specs=[pl.BlockSpec((BATCH, CHUNK, N_SCAN_IN * D_MODEL), lambda c: (0, c, 0)),
                  pl.BlockSpec((1, D_MODEL), lambda c: (0, 0))],
        out_specs=[pl.BlockSpec((BATCH, CHUNK, 2 * D_MODEL), lambda c: (0, c, 0)),
                   pl.BlockSpec((BATCH, CHUNK, 2 * D_MODEL), lambda c: (0, c, 0)),
                   pl.BlockSpec((BATCH, CHUNK, D_MODEL), lambda c: (0, c, 0)),
                   pl.BlockSpec((BATCH, 1, 1, D_MODEL), lambda c: (0, c, 0, 0))],
        out_shape=[jax.ShapeDtypeStruct((BATCH, TP, 2 * D_MODEL), BF16),
                   jax.ShapeDtypeStruct((BATCH, TP, 2 * D_MODEL), F32),
                   jax.ShapeDtypeStruct((BATCH, TP, D_MODEL), F32),
                   jax.ShapeDtypeStruct((BATCH, N_CHUNKS, 1, D_MODEL), F32)],
        compiler_params=_params("parallel"),
    )(rkv, r_k)


def _scan_kernel(mm_ref, add_ref, pl_ref, y_ref, s_scr):
    c = pl.program_id(0)
    D = D_MODEL

    @pl.when(c == 0)
    def _():
        s_scr[...] = jnp.zeros_like(s_scr)

    mask = _group_masks()
    units = [(b, slice(g * GW, (g + 1) * GW)) for b in range(BATCH) for g in range(N_GROUPS)]
    field = lambda ref, b, n, sl: ref[b, :, n * D + sl.start:n * D + sl.stop]
    s0 = [s_scr[b, :, sl] for b, sl in units]
    o = [lax.dot_general(field(mm_ref, b, 0, sl), _bd(s0[n], mask), NT_DIMS,
                         preferred_element_type=F32) + field(add_ref, b, 0, sl)
         for n, (b, sl) in enumerate(units)]
    sg = [lax.dot_general(s0[n].astype(BF16), _bd(field(mm_ref, b, 1, sl), mask), NT_DIMS,
                          preferred_element_type=F32)
          for n, (b, sl) in enumerate(units)]
    for n, (b, sl) in enumerate(units):
        s_scr[b, :, sl] = s0[n] * pl_ref[b, 0, :, sl] + sg[n] + field(add_ref, b, 1, sl)
    for n, (b, sl) in enumerate(units):
        y_ref[b, :, sl] = o[n]


def _scan(mm, add, p_last):
    phys = lambda c: (c + META_CHUNK0) % N_CHUNKS
    pair = pl.BlockSpec((BATCH, CHUNK, 2 * D_MODEL), lambda c: (0, phys(c), 0))
    return pl.pallas_call(
        _scan_kernel,
        grid=(N_CHUNKS,),
        in_specs=[pair, pair, pl.BlockSpec((BATCH, 1, 1, D_MODEL), lambda c: (0, phys(c), 0, 0))],
        out_specs=pl.BlockSpec((BATCH, CHUNK, D_MODEL), lambda c: (0, phys(c), 0)),
        out_shape=jax.ShapeDtypeStruct((BATCH, TP, D_MODEL), F32),
        scratch_shapes=[pltpu.VMEM((BATCH, RWKV_HEAD, D_MODEL), F32)],
        compiler_params=_params("arbitrary"),
    )(mm, add, p_last)


def _rwkv_out_kernel(o_ref, bonus_ref, g_ref, h_ref, gw_ref, gb_ref, wo_ref, gain_ref, wr_ref,
                     h_out, u_out, route_out, cnt_out, carry):
    @pl.when(jnp.logical_and(pl.program_id(0) == 0, pl.program_id(1) == 0))
    def _():
        carry[...] = jnp.zeros_like(carry)

    mask_bf = _group_masks().astype(BF16)
    inv_n = 1.0 / RWKV_HEAD
    o, bonus, gate = o_ref[0], bonus_ref[0], g_ref[0]
    parts = []
    for g in range(N_GROUPS):
        sl = slice(g * GW, (g + 1) * GW)
        og = o[:, sl]
        d = og - _head_sum(og, mask_bf) * inv_n
        var = _dot((d * d).astype(BF16), mask_bf) * inv_n
        yn = d * lax.rsqrt(var + GN_EPS) * gw_ref[:, sl] + gb_ref[:, sl] + bonus[:, sl]
        parts.append((yn * gate[:, sl]).astype(BF16))
    h = h_ref[0] + _dot(jnp.concatenate(parts, axis=1), wo_ref[...])
    h_out[0] = h
    u = _rms(h, gain_ref[...])
    for c in range(N_COL_CHUNKS):
        u_out[0, 0, c] = _pack_piece(u[:, c * COL_CHUNK:(c + 1) * COL_CHUNK])
    u_hi = u.astype(BF16)
    u_lo = (u - u_hi.astype(F32)).astype(BF16)
    logits = lax.dot_general(wr_ref[...], jnp.concatenate([u_hi, u_lo, u_hi], axis=1), NT_DIMS,
                             preferred_element_type=F32)
    e = jnp.exp(logits - jnp.max(logits, axis=0, keepdims=True))
    probs = e / jnp.sum(e, axis=0, keepdims=True)
    idx = lax.broadcasted_iota(jnp.int32, probs.shape, 0).astype(F32)
    m1 = jnp.max(probs, axis=0, keepdims=True)
    i1 = jnp.min(jnp.where(probs == m1, idx, float(N_EXPERTS)), axis=0, keepdims=True)
    sel1 = idx == i1
    rest = jnp.where(sel1, -1.0, probs)
    m2 = jnp.max(rest, axis=0, keepdims=True)
    i2 = jnp.min(jnp.where(rest == m2, idx, float(N_EXPERTS)), axis=0, keepdims=True)
    sel2 = idx == i2
    onehot = jnp.where(jnp.logical_or(sel1, sel2), 1.0, 0.0).astype(F32)
    tr = lax.broadcasted_iota(jnp.int32, (TOKEN_BLOCK, TOKEN_BLOCK), 0)
    tc = lax.broadcasted_iota(jnp.int32, (TOKEN_BLOCK, TOKEN_BLOCK), 1)
    earlier = _dot(onehot.astype(BF16), jnp.where(tr < tc, 1.0, 0.0).astype(BF16)) + carry[...]
    rank1 = jnp.sum(jnp.where(sel1, earlier, 0.0), axis=0, keepdims=True)
    rank2 = jnp.sum(jnp.where(sel2, earlier, 0.0), axis=0, keepdims=True)
    den = m1 + m2
    fields = (i1, i2, rank1, rank2, m1 / den, m2 / den)
    route = jnp.zeros(probs.shape, F32)
    for n, val in enumerate(fields):
        route = jnp.where(idx == float(n), val, route)
    route_out[0] = route
    tile_cnt = jnp.sum(onehot, axis=1, keepdims=True)
    cnt_out[0] = tile_cnt
    carry[...] += tile_cnt


def _rwkv_out(o, bonus, g, h, gn_w, gn_b, w_o, gain, w_router):
    blocks = SEQ // TOKEN_BLOCK
    row = pl.BlockSpec((1, TOKEN_BLOCK, D_MODEL), lambda b, i: (b, i, 0))
    vec = pl.BlockSpec((1, D_MODEL), lambda b, i: (0, 0))
    return pl.pallas_call(
        _rwkv_out_kernel,
        grid=(BATCH, blocks),
        in_specs=[row, row, row, row, vec, vec,
                  pl.BlockSpec((D_MODEL, D_MODEL), lambda b, i: (0, 0)), vec,
                  pl.BlockSpec((N_EXPERTS, 3 * D_MODEL), lambda b, i: (0, 0))],
        out_specs=[row, pl.BlockSpec((1, 1, N_COL_CHUNKS, TOKEN_BLOCK, PIECE_WORDS), lambda b, i: (b, i, 0, 0, 0)),
                   pl.BlockSpec((1, N_EXPERTS, TOKEN_BLOCK), lambda b, i: (b, 0, i)),
                   pl.BlockSpec((1, N_EXPERTS, 1), lambda b, i: (b * blocks + i, 0, 0))],
        out_shape=[jax.ShapeDtypeStruct((BATCH, SEQ, D_MODEL), F32),
                   jax.ShapeDtypeStruct((BATCH, blocks, N_COL_CHUNKS, TOKEN_BLOCK, PIECE_WORDS), U32),
                   jax.ShapeDtypeStruct((BATCH, N_EXPERTS, SEQ), F32),
                   jax.ShapeDtypeStruct((BATCH * blocks, N_EXPERTS, 1), F32)],
        scratch_shapes=[pltpu.VMEM((N_EXPERTS, 1), F32)],
        compiler_params=_params("arbitrary", "arbitrary"),
    )(o, bonus, g, h, gn_w, gn_b, w_o, gain, w_router)


def _routing_tables(route, cnt):
    i32 = jnp.int32
    route = jnp.swapaxes(route, 0, 1).reshape(N_EXPERTS, N_TOKENS)
    expert = route[0:2].astype(i32)
    rank = route[2:4].astype(i32)
    gate = route[4:6].T
    counts = jnp.sum(cnt.reshape(-1, N_EXPERTS).astype(i32), axis=0)
    tiles_e = (counts + EXPERT_TILE - 1) // EXPERT_TILE
    tile_end = jnp.cumsum(tiles_e)
    n_used = tile_end[-1]
    start_row = (tile_end - tiles_e) * EXPERT_TILE
    group_start = sum(jnp.where(expert == e, start_row[e], 0) for e in range(N_EXPERTS))
    pos = group_start + rank
    tiles = jnp.arange(MAX_EXPERT_TILES, dtype=i32)
    tile_expert = jnp.minimum(jnp.sum(tiles[:, None] >= tile_end[None, :], axis=1), N_EXPERTS - 1).astype(i32)
    tile_rows = jnp.take(counts, tile_expert) - (tiles - jnp.take(tile_end - tiles_e, tile_expert)) * EXPERT_TILE
    tile_rows = jnp.where(tiles < n_used, jnp.clip(tile_rows, 0, EXPERT_TILE), 0)
    pos = pos.reshape(TOP_K * N_TOKEN_BLOCKS, 1, TOKEN_BLOCK)
    chunk = jnp.arange(N_COL_CHUNKS, dtype=i32)[None, :, None]
    sorted_piece = (((pos // EXPERT_TILE) * N_COL_CHUNKS + chunk) * EXPERT_TILE + pos % EXPERT_TILE).reshape(-1)
    return dict(sorted_piece=sorted_piece, gate=gate, tile_expert=tile_expert,
                n_used=n_used.reshape(1).astype(i32), tile_rows=tile_rows.astype(i32))


def _row_gather(x, indices):
    m = indices.shape[0]
    mesh = plsc.VectorSubcoreMesh(core_axis_name="c", subcore_axis_name="s")

    @pl.kernel(out_type=jax.ShapeDtypeStruct((m, PIECE_WORDS), x.dtype), mesh=mesh)
    def gather(x_hbm, i_hbm, o_hbm):
        def body(i_vmem, o_vmem):
            pltpu.sync_copy(x_hbm.at[i_vmem.at[0]], o_vmem)

        pltpu.emit_pipeline(
            body, grid=(m // GATHER_ROWS,),
            in_specs=[pl.BlockSpec((1, GATHER_ROWS), lambda i: (0, i))],
            out_specs=[pl.BlockSpec((GATHER_ROWS, PIECE_WORDS), lambda i: (i, 0))],
            core_axis_name=("c", "s"),
            dimension_semantics=(pltpu.PARALLEL,),
        )(i_hbm, o_hbm)

    return gather(x, indices.reshape(1, m))


def _row_scatter(x, indices, out_rows):
    m = indices.shape[0]
    x_blocks = x.shape[0] // GATHER_ROWS
    mesh = plsc.VectorSubcoreMesh(core_axis_name="c", subcore_axis_name="s")

    @pl.kernel(out_type=jax.ShapeDtypeStruct((out_rows, PIECE_WORDS), x.dtype), mesh=mesh)
    def scatter(x_hbm, i_hbm, o_hbm):
        def body(x_vmem, i_vmem):
            pltpu.sync_copy(x_vmem, o_hbm.at[i_vmem.at[0]])

        pltpu.emit_pipeline(
            body, grid=(m // GATHER_ROWS,),
            in_specs=[pl.BlockSpec((GATHER_ROWS, PIECE_WORDS), lambda i: (i % x_blocks, 0)),
                      pl.BlockSpec((1, GATHER_ROWS), lambda i: (0, i))],
            out_specs=[],
            core_axis_name=("c", "s"),
            dimension_semantics=(pltpu.PARALLEL,),
        )(x_hbm, i_hbm)

    return scatter(x, indices.reshape(1, m))


def _expert_kernel(te_ref, nu_ref, nr_ref, x_ref, wg_ref, wu_ref, wd_ref, y_ref, xb, acc):
    i = pl.program_id(0)
    f = pl.program_id(1)
    n_rows = nr_ref[i]
    n_sub = (n_rows + SUB_TILE - 1) // SUB_TILE
    subs = EXPERT_TILE // SUB_TILE

    @pl.when(n_sub > 0)
    def _():
        @pl.when(f == 0)
        def _():
            real = lax.broadcasted_iota(jnp.int32, (EXPERT_TILE, 1), 0) < n_rows
            for c in range(N_COL_CHUNKS):
                halves = _unpack_piece(x_ref[c * EXPERT_TILE:(c + 1) * EXPERT_TILE, :])
                for n, half in enumerate(halves):
                    lo = c * COL_CHUNK + n * PIECE_WORDS
                    xb[:, lo:lo + PIECE_WORDS] = jnp.where(real, half, 0.0).astype(BF16)
            acc[...] = jnp.zeros_like(acc)

        wg = wg_ref[0].astype(BF16)
        wu = wu_ref[0].astype(BF16)
        wd = wd_ref[0].astype(BF16)

        def block(rows):
            x = xb[rows, :]
            a = _dot(x, wg)
            b = _dot(x, wu)
            acc[rows, :] += _dot((a * _sigmoid(a) * b).astype(BF16), wd)

        @pl.when(n_sub == subs)
        def _():
            block(slice(0, EXPERT_TILE))

        @pl.when(n_sub < subs)
        def _():
            for k in range(subs - 1):
                pl.when(k < n_sub)(functools.partial(block, slice(k * SUB_TILE, (k + 1) * SUB_TILE)))

        @pl.when(f == pl.num_programs(1) - 1)
        def _():
            for c in range(N_COL_CHUNKS):
                y_ref[c * EXPERT_TILE:(c + 1) * EXPERT_TILE, :] = _pack_piece(acc[:, c * COL_CHUNK:(c + 1) * COL_CHUNK])


def _experts(tab, x_sorted, wg, wu, wd):
    n_ff = D_FF // FF_TILE
    tile = lambda i, nu: jnp.minimum(i, nu[0] - 1)
    ff = lambda i, f, nu: jnp.where(i < nu[0], f, n_ff - 1)
    grid_spec = pltpu.PrefetchScalarGridSpec(
        num_scalar_prefetch=3, grid=(MAX_EXPERT_TILES, n_ff),
        in_specs=[pl.BlockSpec((N_COL_CHUNKS * EXPERT_TILE, PIECE_WORDS), lambda i, f, te, nu, ns: (tile(i, nu), 0)),
                  pl.BlockSpec((1, D_MODEL, FF_TILE),
                               lambda i, f, te, nu, ns: (te[tile(i, nu)], 0, ff(i, f, nu))),
                  pl.BlockSpec((1, D_MODEL, FF_TILE),
                               lambda i, f, te, nu, ns: (te[tile(i, nu)], 0, ff(i, f, nu))),
                  pl.BlockSpec((1, FF_TILE, D_MODEL),
                               lambda i, f, te, nu, ns: (te[tile(i, nu)], ff(i, f, nu), 0))],
        out_specs=pl.BlockSpec((N_COL_CHUNKS * EXPERT_TILE, PIECE_WORDS), lambda i, f, te, nu, ns: (tile(i, nu), 0)),
        scratch_shapes=[pltpu.VMEM((EXPERT_TILE, D_MODEL), BF16), pltpu.VMEM((EXPERT_TILE, D_MODEL), F32)])
    return pl.pallas_call(
        _expert_kernel, grid_spec=grid_spec,
        out_shape=jax.ShapeDtypeStruct((N_COL_CHUNKS * MAX_SORTED_ROWS, PIECE_WORDS), U32),
        compiler_params=_params("arbitrary", "arbitrary"),
    )(tab["tile_expert"], tab["n_used"], tab["tile_rows"], x_sorted, wg, wu, wd)


def _combine_kernel(h_ref, y_ref, g_ref, o_ref):
    g = g_ref[...]
    for c in range(N_COL_CHUNKS):
        first, second = _unpack_piece(y_ref[0, 0, c]), _unpack_piece(y_ref[1, 0, c])
        for n in range(2):
            cols = slice(c * COL_CHUNK + n * PIECE_WORDS, c * COL_CHUNK + (n + 1) * PIECE_WORDS)
            o_ref[:, cols] = h_ref[:, cols] + g[:, 0:1] * first[n] + g[:, 1:2] * second[n]


def _combine(h, y_pair, gate):
    row = pl.BlockSpec((TOKEN_BLOCK, D_MODEL), lambda i: (i, 0))
    return pl.pallas_call(
        _combine_kernel,
        grid=(N_TOKEN_BLOCKS,),
        in_specs=[row, pl.BlockSpec((TOP_K, 1, N_COL_CHUNKS, TOKEN_BLOCK, PIECE_WORDS), lambda i: (0, i, 0, 0, 0)),
                  pl.BlockSpec((TOKEN_BLOCK, TOP_K), lambda i: (i, 0))],
        out_specs=row,
        out_shape=jax.ShapeDtypeStruct((N_TOKENS, D_MODEL), F32),
        compiler_params=_params("parallel"),
    )(h, y_pair, gate)


def kernel(x, meta_tokens, mixer_norm, ffn_norm, attn_w_qkv, attn_q_norm, attn_k_norm, attn_sinks, attn_w_o, rwkv_mix, rwkv_w0, rwkv_w1, rwkv_w2, rwkv_a0, rwkv_a1, rwkv_a2, rwkv_g1, rwkv_g2, rwkv_k_k, rwkv_k_a, rwkv_r_k, rwkv_w_r, rwkv_w_k, rwkv_w_v, rwkv_w_o, rwkv_gn_w, rwkv_gn_b, ffn_w_gate, ffn_w_up, ffn_w_down, moe_router, moe_w_gate, moe_w_up, moe_w_down):
    bf = lambda a: a.astype(BF16)
    vec = lambda a: a.reshape(1, -1).astype(F32)
    h_meta = jnp.concatenate([jnp.zeros((PAD, D_MODEL), F32), meta_tokens.astype(F32)], axis=0)

    w_qkv = bf(attn_w_qkv[0])
    qkv = _qkv(x.reshape(N_TOKENS, D_MODEL), vec(mixer_norm[0]), w_qkv, TOKEN_BLOCK)
    qkv_meta = _qkv(h_meta, vec(mixer_norm[0]), w_qkv, BLOCK)
    h = _attention(x, h_meta, qkv.reshape(BATCH, SEQ, QKV_DIM), qkv_meta, attn_sinks[0].astype(F32),
                   jnp.tile(vec(attn_q_norm[0]), (1, Q_PER_KV)), jnp.tile(vec(attn_k_norm[0]), (1, N_KV_HEADS)),
                   bf(attn_w_o[0]))
    h = _ffn(h.reshape(N_ROWS, D_MODEL), vec(ffn_norm[0]), bf(ffn_w_gate[0]), bf(ffn_w_up[0]),
             bf(ffn_w_down[0]))

    rkv, g = _rwkv_proj(
        h, vec(mixer_norm[1]), rwkv_mix[0], vec(rwkv_w0[0]), vec(rwkv_a0[0]), vec(rwkv_k_k[0]),
        vec(rwkv_k_a[0]), bf(rwkv_w_r[0]), bf(rwkv_w_k[0]), bf(rwkv_w_v[0]), bf(rwkv_w1[0]),
        bf(rwkv_w2[0]), bf(rwkv_a1[0]), bf(rwkv_a2[0]), bf(rwkv_g1[0]), bf(rwkv_g2[0]))
    b3 = lambda t: t.reshape(BATCH, TP, -1)
    mm, add, bonus, p_last = _scan_prep(b3(rkv), vec(rwkv_r_k[0]))
    o = _scan(mm, add, p_last)
    w_router = moe_router[0].astype(F32).T
    wr_hi = bf(w_router)
    wr_lo = bf(w_router - wr_hi.astype(F32))
    h, u, route, cnt = _rwkv_out(o, bonus, b3(g), b3(h), vec(rwkv_gn_w[0]), vec(rwkv_gn_b[0]),
                                 bf(rwkv_w_o[0]), vec(ffn_norm[1]),
                                 jnp.concatenate([wr_hi, wr_hi, wr_lo], axis=1))
    tab = _routing_tables(route, cnt)
    x_sorted = _row_scatter(u.reshape(-1, PIECE_WORDS), tab["sorted_piece"], N_COL_CHUNKS * MAX_SORTED_ROWS)
    y_sorted = _experts(tab, x_sorted, moe_w_gate[0], moe_w_up[0], moe_w_down[0])
    y_pair = _row_gather(y_sorted, tab["sorted_piece"]).reshape(TOP_K, N_TOKEN_BLOCKS, N_COL_CHUNKS, TOKEN_BLOCK, PIECE_WORDS)
    out = _combine(h.reshape(N_TOKENS, D_MODEL), y_pair, tab["gate"])
    return out.reshape(BATCH, SEQ, D_MODEL)
```

```python
import functools

import jax
import jax.numpy as jnp
from jax import lax
from jax.experimental import pallas as pl
from jax.experimental.pallas import tpu as pltpu
from jax.experimental.pallas import tpu_sc as plsc

F32 = jnp.float32
BF16 = jnp.bfloat16

D_MODEL = 1024
BATCH = 2
SEQ = 8192
N_META = 16
BLOCK = 128
PAD = BLOCK - N_META
TP = SEQ + BLOCK
N_ROWS = BATCH * TP
N_BLOCKS = TP // BLOCK
META_BLOCK = N_BLOCKS - 1
HEAD_DIM = 64
N_Q_HEADS = 16
N_KV_HEADS = 4
Q_PER_KV = 4
QKV_DIM = (N_Q_HEADS + 2 * N_KV_HEADS) * HEAD_DIM
RWKV_HEADS = 16
RWKV_HEAD = 64
D_FF = 3584
N_EXPERTS = 8
NORM_EPS = 1e-5
GN_EPS = 64e-5
CHUNK = 64
N_CHUNKS = TP // CHUNK
META_CHUNK0 = SEQ // CHUNK
HEADS_PER_GROUP = 4
GW = HEADS_PER_GROUP * RWKV_HEAD
N_GROUPS = RWKV_HEADS // HEADS_PER_GROUP
N_SCAN_IN = 6

FFN_ROW_TILE = 1280
PROJ_TILE = 320
FF_TILE = 512
N_TOKENS = BATCH * SEQ
TOP_K = 2
TOKEN_BLOCK = 512
N_TOKEN_BLOCKS = N_TOKENS // TOKEN_BLOCK
EXPERT_TILE = 1024
SUB_TILE = 256
MAX_EXPERT_TILES = TOP_K * N_TOKENS // EXPERT_TILE + N_EXPERTS
MAX_SORTED_ROWS = MAX_EXPERT_TILES * EXPERT_TILE
COL_CHUNK = 256
N_COL_CHUNKS = D_MODEL // COL_CHUNK
PIECE_WORDS = COL_CHUNK // 2
U32 = jnp.uint32
GATHER_ROWS = 128
VMEM_LIMIT = 56 * 1024 * 1024
NEG = -1e30

NT_DIMS = (((1,), (1,)), ((), ()))
TN_DIMS = (((0,), (0,)), ((), ()))


def _params(*sem):
    return pltpu.CompilerParams(dimension_semantics=sem, vmem_limit_bytes=VMEM_LIMIT)


def _rms(x, gain):
    return x * lax.rsqrt(jnp.mean(x * x, axis=-1, keepdims=True) + NORM_EPS) * gain


def _sigmoid(x):
    return 1.0 / (1.0 + jnp.exp(-x))


def _dot(a, b):
    return jnp.dot(a, b, preferred_element_type=F32)


def _pack_piece(x):
    return pltpu.pack_elementwise([x[:, :PIECE_WORDS], x[:, PIECE_WORDS:]], packed_dtype=BF16)


def _unpack_piece(w):
    return tuple(pltpu.unpack_elementwise(w, index=i, packed_dtype=BF16, unpacked_dtype=F32) for i in (0, 1))


def _qkv_kernel(h_ref, g_ref, w_ref, o_ref):
    u = _rms(h_ref[...], g_ref[...])
    o_ref[...] = _dot(u.astype(BF16), w_ref[...]).astype(BF16)


def _qkv(h, gain, w, tile):
    rows = h.shape[0]
    return pl.pallas_call(
        _qkv_kernel,
        grid=(rows // tile,),
        in_specs=[pl.BlockSpec((tile, D_MODEL), lambda i: (i, 0)),
                  pl.BlockSpec((1, D_MODEL), lambda i: (0, 0)),
                  pl.BlockSpec((D_MODEL, QKV_DIM), lambda i: (0, 0))],
        out_specs=pl.BlockSpec((tile, QKV_DIM), lambda i: (i, 0)),
        out_shape=jax.ShapeDtypeStruct((rows, QKV_DIM), BF16),
        compiler_params=_params("parallel"),
    )(h, gain, w)


def _attn_kernel(sink_ref, x_ref, hm_ref, q_ref, kc_ref, vc_ref, kp_ref, vp_ref, qm_ref, km_ref, vm_ref,
                 qg_ref, kg_ref, wo_ref, o_ref, o_scr):
    j = pl.program_id(1)
    n_band = 2 * BLOCK
    slots = 2 * N_META
    far = 4 * BLOCK
    is_real = j < META_BLOCK
    prev_off = jnp.where(jnp.logical_and(j >= 1, is_real), 0, far)
    cur_off = jnp.where(is_real, 0, far)
    rowi = lax.broadcasted_iota(jnp.int32, (BLOCK, n_band), 0)
    col = lax.broadcasted_iota(jnp.int32, (BLOCK, n_band), 1)
    band_ok = jnp.logical_or(jnp.logical_and(col < BLOCK, col > rowi + prev_off),
                             jnp.logical_and(col >= BLOCK, col - BLOCK + cur_off <= rowi))
    band_ok4 = jnp.concatenate([band_ok] * Q_PER_KV, axis=1)
    meta_off = jnp.where(is_real, -far, PAD)
    mrow = lax.broadcasted_iota(jnp.int32, (BLOCK, Q_PER_KV * slots), 0)
    mlane = lax.broadcasted_iota(jnp.int32, (BLOCK, Q_PER_KV * slots), 1)
    mslot = mlane % slots
    meta_ok4 = jnp.logical_and(mslot < N_META, mslot + meta_off <= mrow)
    meta_lane_head = mlane // slots

    groups = range(N_KV_HEADS)
    mask_bf = _group_masks().astype(BF16)
    rb = lax.broadcasted_iota(jnp.int32, (Q_PER_KV * n_band, GW), 0) // n_band
    lb = lax.broadcasted_iota(jnp.int32, (Q_PER_KV * n_band, GW), 1) // HEAD_DIM
    band_mask = jnp.where(rb == lb, 1.0, 0.0).astype(BF16)
    rm = lax.broadcasted_iota(jnp.int32, (Q_PER_KV * slots, GW), 0) // slots
    lm = lax.broadcasted_iota(jnp.int32, (Q_PER_KV * slots, GW), 1) // HEAD_DIM
    meta_mask = jnp.where(rm == lm, 1.0, 0.0).astype(BF16)
    sr = lax.broadcasted_iota(jnp.int32, (GW, GW), 0)
    sc = lax.broadcasted_iota(jnp.int32, (GW, GW), 1)
    lane_head = lax.broadcasted_iota(jnp.int32, (1, GW), 1) // HEAD_DIM
    inv_d = 1.0 / HEAD_DIM
    scale = HEAD_DIM ** -0.5

    real_rows = lax.broadcasted_iota(jnp.int32, (BLOCK, 1), 0) < jnp.where(is_real, BLOCK, 0)
    km, vm = km_ref[...], vm_ref[...]
    kcur = jnp.where(real_rows, kc_ref[0], km)
    vcur = jnp.where(real_rows, vc_ref[0], vm)
    kall = jnp.concatenate([km[PAD:], kp_ref[0], kcur], axis=0).astype(F32)
    vall = jnp.concatenate([vm[PAD:], vp_ref[0], vcur], axis=0)
    kss = _dot((kall * kall).astype(BF16), mask_bf)
    kn = (kall * lax.rsqrt(kss * inv_d + NORM_EPS) * kg_ref[...]).astype(BF16)
    q_all = jnp.where(real_rows, q_ref[0], qm_ref[...]).astype(F32)
    qn = []
    for g in groups:
        qg = q_all[:, g * GW:(g + 1) * GW]
        qss = _dot((qg * qg).astype(BF16), mask_bf)
        qn.append((qg * lax.rsqrt(qss * inv_d + NORM_EPS) * (qg_ref[...] * scale)).astype(BF16))
    sel = [jnp.where(sr == g * HEAD_DIM + sc % HEAD_DIM, 1.0, 0.0).astype(BF16) for g in groups]
    krep = [_dot(kn, sel[g]).astype(BF16) for g in groups]
    vrep = [_dot(vall, sel[g]).astype(BF16) for g in groups]
    unused = jnp.zeros((slots - N_META, GW), BF16)

    def band_bd(rep):
        return jnp.concatenate([rep[N_META:]] * Q_PER_KV, axis=0) * band_mask

    def meta_bd(rep):
        return jnp.concatenate([rep[:N_META], unused] * Q_PER_KV, axis=0) * meta_mask

    s_band = [jnp.where(band_ok4, lax.dot_general(qn[g], band_bd(krep[g]), NT_DIMS,
                                                  preferred_element_type=F32), NEG) for g in groups]
    s_meta = [jnp.where(meta_ok4, lax.dot_general(qn[g], meta_bd(krep[g]), NT_DIMS,
                                                  preferred_element_type=F32), NEG) for g in groups]
    rhs_band = [jnp.concatenate([band_bd(vrep[g]), band_mask], axis=1) for g in groups]
    rhs_meta = [jnp.concatenate([meta_bd(vrep[g]), meta_mask], axis=1) for g in groups]
    ov, sink_den = [], []
    for g in groups:
        parts = []
        sd = jnp.zeros((BLOCK, GW), F32)
        m_lanes = jnp.zeros((BLOCK, Q_PER_KV * slots), F32)
        for hh in range(Q_PER_KV):
            seg = s_band[g][:, hh * n_band:(hh + 1) * n_band]
            own = meta_lane_head == hh
            sink = sink_ref[g * Q_PER_KV + hh]
            m = jnp.maximum(jnp.max(seg, axis=-1, keepdims=True),
                            jnp.max(jnp.where(own, s_meta[g], NEG), axis=-1, keepdims=True))
            m = jnp.maximum(m, sink)
            parts.append(jnp.exp(seg - m).astype(BF16))
            m_lanes = jnp.where(own, m, m_lanes)
            sd = sd + jnp.exp(sink - m) * jnp.where(lane_head == hh, 1.0, 0.0)
        p_band = jnp.concatenate(parts, axis=1)
        p_meta = jnp.exp(s_meta[g] - m_lanes).astype(BF16)
        ov.append(_dot(p_band, rhs_band[g]) + _dot(p_meta, rhs_meta[g]))
        sink_den.append(sd)
    for g in groups:
        o_scr[:, g * GW:(g + 1) * GW] = ov[g][:, :GW] / (ov[g][:, GW:] + sink_den[g])
    h = jnp.where(real_rows, x_ref[0], hm_ref[...])
    o_ref[0] = h + _dot(o_scr[...].astype(BF16), wo_ref[...])


def _attention(x, h_meta, qkv, qkv_meta, sinks, q_gain, k_gain, w_o):
    kcol, vcol = N_Q_HEADS * HEAD_DIM // 256, N_Q_HEADS * HEAD_DIM // 256 + 1
    kvw = N_KV_HEADS * HEAD_DIM
    real = lambda j: jnp.minimum(j, META_BLOCK - 1)
    prev = lambda j: jnp.clip(j - 1, 0, META_BLOCK - 1)
    return pl.pallas_call(
        _attn_kernel,
        grid=(BATCH, N_BLOCKS),
        in_specs=[pl.BlockSpec(memory_space=pltpu.SMEM),
                  pl.BlockSpec((1, BLOCK, D_MODEL), lambda b, j: (b, real(j), 0)),
                  pl.BlockSpec((BLOCK, D_MODEL), lambda b, j: (0, 0)),
                  pl.BlockSpec((1, BLOCK, D_MODEL), lambda b, j: (b, real(j), 0)),
                  pl.BlockSpec((1, BLOCK, kvw), lambda b, j: (b, real(j), kcol)),
                  pl.BlockSpec((1, BLOCK, kvw), lambda b, j: (b, real(j), vcol)),
                  pl.BlockSpec((1, BLOCK, kvw), lambda b, j: (b, prev(j), kcol)),
                  pl.BlockSpec((1, BLOCK, kvw), lambda b, j: (b, prev(j), vcol)),
                  pl.BlockSpec((BLOCK, D_MODEL), lambda b, j: (0, 0)),
                  pl.BlockSpec((BLOCK, kvw), lambda b, j: (0, kcol)),
                  pl.BlockSpec((BLOCK, kvw), lambda b, j: (0, vcol)),
                  pl.BlockSpec((1, GW), lambda b, j: (0, 0)),
                  pl.BlockSpec((1, GW), lambda b, j: (0, 0)),
                  pl.BlockSpec((D_MODEL, D_MODEL), lambda b, j: (0, 0))],
        out_specs=pl.BlockSpec((1, BLOCK, D_MODEL), lambda b, j: (b, j, 0)),
        out_shape=jax.ShapeDtypeStruct((BATCH, TP, D_MODEL), F32),
        scratch_shapes=[pltpu.VMEM((BLOCK, D_MODEL), F32)],
        compiler_params=_params("parallel", "parallel"),
    )(sinks, x, h_meta, qkv, qkv, qkv, qkv, qkv, qkv_meta, qkv_meta, qkv_meta, q_gain, k_gain, w_o)


def _ffn_kernel(h_ref, g_ref, wg_ref, wu_ref, wd_ref, o_ref, u_scr, acc):
    f = pl.program_id(1)

    @pl.when(f == 0)
    def _():
        u_scr[...] = _rms(h_ref[...], g_ref[...]).astype(BF16)
        acc[...] = jnp.zeros_like(acc)

    u = u_scr[...]
    a = _dot(u, wg_ref[...])
    b = _dot(u, wu_ref[...])
    acc[...] += _dot((a * _sigmoid(a) * b).astype(BF16), wd_ref[...])

    @pl.when(f == pl.num_programs(1) - 1)
    def _():
        o_ref[...] = h_ref[...] + acc[...]


def _ffn(h, gain, wg, wu, wd):
    return pl.pallas_call(
        _ffn_kernel,
        grid=(N_ROWS // FFN_ROW_TILE, D_FF // FF_TILE),
        in_specs=[pl.BlockSpec((FFN_ROW_TILE, D_MODEL), lambda i, f: (i, 0)),
                  pl.BlockSpec((1, D_MODEL), lambda i, f: (0, 0)),
                  pl.BlockSpec((D_MODEL, FF_TILE), lambda i, f: (0, f)),
                  pl.BlockSpec((D_MODEL, FF_TILE), lambda i, f: (0, f)),
                  pl.BlockSpec((FF_TILE, D_MODEL), lambda i, f: (f, 0))],
        out_specs=pl.BlockSpec((FFN_ROW_TILE, D_MODEL), lambda i, f: (i, 0)),
        out_shape=jax.ShapeDtypeStruct((N_ROWS, D_MODEL), F32),
        scratch_shapes=[pltpu.VMEM((FFN_ROW_TILE, D_MODEL), BF16), pltpu.VMEM((FFN_ROW_TILE, D_MODEL), F32)],
        compiler_params=_params("parallel", "arbitrary"),
    )(h, gain, wg, wu, wd)


def _rwkv_proj_kernel(h_ref, hp_ref, g_ref, mix_ref, w0_ref, a0_ref, kk_ref, ka_ref,
                      wr_ref, wk_ref, wv_ref, w1_ref, w2_ref, a1_ref, a2_ref, g1_ref, g2_ref,
                      rkv_out, g_out):
    i = pl.program_id(0)
    tiles_per_batch = TP // PROJ_TILE
    r0 = (i % tiles_per_batch) * PROJ_TILE
    local = lax.broadcasted_iota(jnp.int32, (PROJ_TILE, 1), 0)
    lrow = local + r0
    gain = g_ref[...]
    is_pad = jnp.logical_and(lrow >= SEQ, lrow < SEQ + PAD)
    u = jnp.where(is_pad, 0.0, _rms(h_ref[...], gain))
    u_prev_tile = _rms(hp_ref[7:8, :], gain)
    xprev = pltpu.roll(u, 1, 0)
    xprev = jnp.where(local == 0, u_prev_tile, xprev)
    xprev = jnp.where(lrow == SEQ, 0.0, xprev)
    xx = xprev - u
    mix = mix_ref[...]
    lerp = lambda n: (u + xx * mix[n:n + 1, :]).astype(BF16)
    xr, xw, xk, xv, xa, xg = [lerp(n) for n in range(6)]
    r = _dot(xr, wr_ref[...])
    k = _dot(xk, wk_ref[...])
    v = _dot(xv, wv_ref[...])
    lw = _dot(jnp.tanh(_dot(xw, w1_ref[...])).astype(BF16), w2_ref[...])
    z = -(w0_ref[...] + lw)
    softplus = jnp.maximum(z, 0.0) + jnp.log(1.0 + jnp.exp(-jnp.abs(z)))
    w = -softplus - 0.5
    a = _sigmoid(a0_ref[...] + _dot(_dot(xa, a1_ref[...]).astype(BF16), a2_ref[...]))
    g = _dot(_sigmoid(_dot(xg, g1_ref[...])).astype(BF16), g2_ref[...])
    fields = (r, -jnp.exp(w),
              k * (1.0 + (a - 1.0) * ka_ref[...]), v, k * kk_ref[...], a)
    for n, val in enumerate(fields):
        rkv_out[:, n * D_MODEL:(n + 1) * D_MODEL] = val
    g_out[...] = g


def _rwkv_proj(h, gain, mix, w0, a0, k_k, k_a, w_r, w_k, w_v, w1, w2, a1, a2, g1, g2):
    tiles_per_batch = TP // PROJ_TILE
    rows8 = PROJ_TILE // 8

    def prev_map(i):
        b = i // tiles_per_batch
        first = (i % tiles_per_batch) == 0
        return (jnp.where(first, (b * TP + TP - 8) // 8, i * rows8 - 1), 0)

    row = pl.BlockSpec((PROJ_TILE, D_MODEL), lambda i: (i, 0))
    full = lambda a: pl.BlockSpec(a.shape, lambda i: (0,) * a.ndim)
    smalls = (gain, mix, w0, a0, k_k, k_a, w_r, w_k, w_v, w1, w2, a1, a2, g1, g2)
    return pl.pallas_call(
        _rwkv_proj_kernel,
        grid=(N_ROWS // PROJ_TILE,),
        in_specs=[row, pl.BlockSpec((8, D_MODEL), prev_map)] + [full(a) for a in smalls],
        out_specs=[pl.BlockSpec((PROJ_TILE, N_SCAN_IN * D_MODEL), lambda i: (i, 0)), row],
        out_shape=[jax.ShapeDtypeStruct((N_ROWS, N_SCAN_IN * D_MODEL), F32),
                   jax.ShapeDtypeStruct((N_ROWS, D_MODEL), F32)],
        compiler_params=_params("parallel"),
    )(h, h, *smalls)


def _group_masks():
    ri = lax.broadcasted_iota(jnp.int32, (GW, GW), 0) // RWKV_HEAD
    ci = lax.broadcasted_iota(jnp.int32, (GW, GW), 1) // RWKV_HEAD
    return jnp.where(ri == ci, 1.0, 0.0).astype(F32)


def _bd(x, mask):
    return jnp.concatenate([x.astype(BF16)] * HEADS_PER_GROUP, axis=0) * mask.astype(BF16)


def _diag_blocks(full, mask):
    m = full * mask
    n = RWKV_HEAD
    return (m[0:n] + m[n:2 * n]) + (m[2 * n:3 * n] + m[3 * n:4 * n])


def _head_sum(x, mask_bf):
    hi = x.astype(BF16)
    lo = (x - hi.astype(F32)).astype(BF16)
    return _dot(hi, mask_bf) + _dot(lo, mask_bf)


def _scan_prep_kernel(x_ref, rk_ref, mm_out, add_out, bonus_out, pl_out):
    L, D = CHUNK, D_MODEL
    units = [(b, g) for b in range(BATCH) for g in range(N_GROUPS)]
    un = range(len(units))
    mask = _group_masks()
    mask_bf = mask.astype(BF16)
    ri = lax.broadcasted_iota(jnp.int32, (L, GW), 0)
    ci = lax.broadcasted_iota(jnp.int32, (L, GW), 1) % RWKV_HEAD
    incl = ci <= ri
    strict = ci < ri
    eye = jnp.where(ci == ri, 1.0, 0.0).astype(F32)
    t_r = lax.broadcasted_iota(jnp.int32, (L, L), 0)
    t_c = lax.broadcasted_iota(jnp.int32, (L, L), 1)
    tril = jnp.where(t_c <= t_r, 1.0, 0.0).astype(BF16)
    rk_all = rk_ref[...]

    def field(b, n, g):
        return x_ref[b, :, n * D + g * GW:n * D + (g + 1) * GW]

    def put(ref, b, n, g, val):
        ref[b, :, n * D + g * GW:n * D + (g + 1) * GW] = val.astype(ref.dtype)

    at, rt, bt, kt, v, plast = [], [], [], [], [], []
    for b in range(BATCH):
        ld = x_ref[b, :, D:2 * D]
        hi = ld.astype(BF16)
        rest = ld - hi.astype(F32)
        mid = rest.astype(BF16)
        lo = (rest - mid.astype(F32)).astype(BF16)
        cs = _dot(tril, hi) + _dot(tril, mid) + _dot(tril, lo)
        p_all = jnp.exp(cs)
        pprev_all = jnp.exp(cs - ld)
        pinv_all = jnp.exp(-cs)
        pl_all = p_all[L - 1:L, :]
        pl_out[b, 0] = pl_all
        for g in range(N_GROUPS):
            sl = slice(g * GW, (g + 1) * GW)
            r, k, vv, kk, a = field(b, 0, g), field(b, 2, g), field(b, 3, g), field(b, 4, g), field(b, 5, g)
            nrm = jnp.sqrt(_dot((kk * kk).astype(BF16), mask_bf))
            kk = kk / jnp.maximum(nrm, 1e-12)
            at.append(-kk * pprev_all[:, sl])
            bt.append(kk * a * pinv_all[:, sl])
            rt.append(r * p_all[:, sl])
            kt.append(k * pinv_all[:, sl])
            v.append(vv)
            plast.append(pl_all[:, sl])
            put(bonus_out, b, 0, g, _dot((r * k * rk_all[:, sl]).astype(BF16), mask_bf) * vv)

    a_ab, a_ak, a_rb, a_rk = [], [], [], []
    for n in un:
        lhs = jnp.concatenate([at[n], rt[n]], axis=0).astype(BF16)
        rhs = jnp.concatenate([_bd(bt[n], mask), _bd(kt[n], mask)], axis=0)
        big = lax.dot_general(lhs, rhs, NT_DIMS, preferred_element_type=F32)
        a_ab.append(jnp.where(strict, big[:L, :GW], 0.0))
        a_ak.append(jnp.where(strict, big[:L, GW:], 0.0))
        a_rb.append(jnp.where(incl, big[L:, :GW], 0.0))
        a_rk.append(jnp.where(incl, big[L:, GW:], 0.0))

    x = [_dot(a_ab[n].astype(BF16), _bd(a_ab[n], mask)) for n in un]
    inv = [eye + a_ab[n] for n in un]
    for step in range(5):
        for n in un:
            rhs = _bd(x[n], mask)
            if step < 4:
                res = _dot(jnp.concatenate([x[n], inv[n]], axis=0).astype(BF16), rhs)
                x[n] = res[:L]
                inv[n] = inv[n] + res[L:]
            else:
                inv[n] = inv[n] + _dot(inv[n].astype(BF16), rhs)

    av = [_dot(jnp.concatenate([a_ak[n], a_rk[n]], axis=0).astype(BF16), _bd(v[n], mask)) for n in un]
    wu = [_dot(inv[n].astype(BF16), jnp.concatenate([_bd(at[n], mask), _bd(av[n][:L], mask)], axis=1))
          for n in un]
    aw = [_dot(a_rb[n].astype(BF16),
               jnp.concatenate([_bd(wu[n][:, :GW], mask), _bd(wu[n][:, GW:], mask)], axis=1))
          for n in un]
    for n, (b, g) in enumerate(units):
        put(mm_out, b, 0, g, rt[n] + aw[n][:, :GW])
        put(add_out, b, 0, g, av[n][L:] + aw[n][:, GW:])
        bh = (bt[n] * plast[n]).astype(BF16)
        kh = (kt[n] * plast[n]).astype(BF16)
        w_b, u0_b = wu[n][:, :GW].astype(BF16), wu[n][:, GW:].astype(BF16)
        gfull = lax.dot_general(bh, w_b, TN_DIMS, preferred_element_type=F32)
        put(mm_out, b, 1, g, _diag_blocks(gfull, mask))
        hfull = lax.dot_general(jnp.concatenate([u0_b, v[n].astype(BF16)], axis=0),
                                jnp.concatenate([bh, kh], axis=0), TN_DIMS, preferred_element_type=F32)
        put(add_out, b, 1, g, _diag_blocks(hfull, mask))


def _scan_prep(rkv, r_k):
    return pl.pallas_call(
        _scan_prep_kernel,
        grid=(N_CHUNKS,),
        in_specs=[pl.BlockSpec((BATCH, CHUNK, N_SCAN_IN * D_MODEL), lambda c: (0, c, 0)),
                  pl.BlockSpec((1, D_MODEL), lambda c: (0, 0))],
        out_specs=[pl.BlockSpec((BATCH, CHUNK, 2 * D_MODEL), lambda c: (0, c, 0)),
                   pl.BlockSpec((BATCH, CHUNK, 2 * D_MODEL), lambda c: (0, c, 0)),
                   pl.BlockSpec((BATCH, CHUNK, D_MODEL), lambda c: (0, c, 0)),
                   pl.BlockSpec((BATCH, 1, 1, D_MODEL), lambda c: (0, c, 0, 0))],
        out_shape=[jax.ShapeDtypeStruct((BATCH, TP, 2 * D_MODEL), BF16),
                   jax.ShapeDtypeStruct((BATCH, TP, 2 * D_MODEL), F32),
                   jax.ShapeDtypeStruct((BATCH, TP, D_MODEL), F32),
                   jax.ShapeDtypeStruct((BATCH, N_CHUNKS, 1, D_MODEL), F32)],
        compiler_params=_params("parallel"),
    )(rkv, r_k)


def _scan_kernel(mm_ref, add_ref, pl_ref, y_ref, s_scr):
    c = pl.program_id(0)
    D = D_MODEL

    @pl.when(c == 0)
    def _():
        s_scr[...] = jnp.zeros_like(s_scr)

    mask = _group_masks()
    units = [(b, slice(g * GW, (g + 1) * GW)) for b in range(BATCH) for g in range(N_GROUPS)]
    field = lambda ref, b, n, sl: ref[b, :, n * D + sl.start:n * D + sl.stop]
    s0 = [s_scr[b, :, sl] for b, sl in units]
    o = [lax.dot_general(field(mm_ref, b, 0, sl), _bd(s0[n], mask), NT_DIMS,
                         preferred_element_type=F32) + field(add_ref, b, 0, sl)
         for n, (b, sl) in enumerate(units)]
    sg = [lax.dot_general(s0[n].astype(BF16), _bd(field(mm_ref, b, 1, sl), mask), NT_DIMS,
                          preferred_element_type=F32)
          for n, (b, sl) in enumerate(units)]
    for n, (b, sl) in enumerate(units):
        s_scr[b, :, sl] = s0[n] * pl_ref[b, 0, :, sl] + sg[n] + field(add_ref, b, 1, sl)
    for n, (b, sl) in enumerate(units):
        y_ref[b, :, sl] = o[n]


def _scan(mm, add, p_last):
    phys = lambda c: (c + META_CHUNK0) % N_CHUNKS
    pair = pl.BlockSpec((BATCH, CHUNK, 2 * D_MODEL), lambda c: (0, phys(c), 0))
    return pl.pallas_call(
        _scan_kernel,
        grid=(N_CHUNKS,),
        in_specs=[pair, pair, pl.BlockSpec((BATCH, 1, 1, D_MODEL), lambda c: (0, phys(c), 0, 0))],
        out_specs=pl.BlockSpec((BATCH, CHUNK, D_MODEL), lambda c: (0, phys(c), 0)),
        out_shape=jax.ShapeDtypeStruct((BATCH, TP, D_MODEL), F32),
        scratch_shapes=[pltpu.VMEM((BATCH, RWKV_HEAD, D_MODEL), F32)],
        compiler_params=_params("arbitrary"),
    )(mm, add, p_last)


def _rwkv_out_kernel(o_ref, bonus_ref, g_ref, h_ref, gw_ref, gb_ref, wo_ref, gain_ref, wr_ref,
                     h_out, u_out, route_out, cnt_out, carry):
    @pl.when(jnp.logical_and(pl.program_id(0) == 0, pl.program_id(1) == 0))
    def _():
        carry[...] = jnp.zeros_like(carry)

    mask_bf = _group_masks().astype(BF16)
    inv_n = 1.0 / RWKV_HEAD
    o, bonus, gate = o_ref[0], bonus_ref[0], g_ref[0]
    parts = []
    for g in range(N_GROUPS):
        sl = slice(g * GW, (g + 1) * GW)
        og = o[:, sl]
        d = og - _head_sum(og, mask_bf) * inv_n
        var = _dot((d * d).astype(BF16), mask_bf) * inv_n
        yn = d * lax.rsqrt(var + GN_EPS) * gw_ref[:, sl] + gb_ref[:, sl] + bonus[:, sl]
        parts.append((yn * gate[:, sl]).astype(BF16))
    h = h_ref[0] + _dot(jnp.concatenate(parts, axis=1), wo_ref[...])
    h_out[0] = h
    u = _rms(h, gain_ref[...])
    for c in range(N_COL_CHUNKS):
        u_out[0, 0, c] = _pack_piece(u[:, c * COL_CHUNK:(c + 1) * COL_CHUNK])
    u_hi = u.astype(BF16)
    u_lo = (u - u_hi.astype(F32)).astype(BF16)
    logits = lax.dot_general(wr_ref[...], jnp.concatenate([u_hi, u_lo, u_hi], axis=1), NT_DIMS,
                             preferred_element_type=F32)
    e = jnp.exp(logits - jnp.max(logits, axis=0, keepdims=True))
    probs = e / jnp.sum(e, axis=0, keepdims=True)
    idx = lax.broadcasted_iota(jnp.int32, probs.shape, 0).astype(F32)
    m1 = jnp.max(probs, axis=0, keepdims=True)
    i1 = jnp.min(jnp.where(probs == m1, idx, float(N_EXPERTS)), axis=0, keepdims=True)
    sel1 = idx == i1
    rest = jnp.where(sel1, -1.0, probs)
    m2 = jnp.max(rest, axis=0, keepdims=True)
    i2 = jnp.min(jnp.where(rest == m2, idx, float(N_EXPERTS)), axis=0, keepdims=True)
    sel2 = idx == i2
    onehot = jnp.where(jnp.logical_or(sel1, sel2), 1.0, 0.0).astype(F32)
    tr = lax.broadcasted_iota(jnp.int32, (TOKEN_BLOCK, TOKEN_BLOCK), 0)
    tc = lax.broadcasted_iota(jnp.int32, (TOKEN_BLOCK, TOKEN_BLOCK), 1)
    earlier = _dot(onehot.astype(BF16), jnp.where(tr < tc, 1.0, 0.0).astype(BF16)) + carry[...]
    rank1 = jnp.sum(jnp.where(sel1, earlier, 0.0), axis=0, keepdims=True)
    rank2 = jnp.sum(jnp.where(sel2, earlier, 0.0), axis=0, keepdims=True)
    den = m1 + m2
    fields = (i1, i2, rank1, rank2, m1 / den, m2 / den)
    route = jnp.zeros(probs.shape, F32)
    for n, val in enumerate(fields):
        route = jnp.where(idx == float(n), val, route)
    route_out[0] = route
    tile_cnt = jnp.sum(onehot, axis=1, keepdims=True)
    cnt_out[0] = tile_cnt
    carry[...] += tile_cnt


def _rwkv_out(o, bonus, g, h, gn_w, gn_b, w_o, gain, w_router):
    blocks = SEQ // TOKEN_BLOCK
    row = pl.BlockSpec((1, TOKEN_BLOCK, D_MODEL), lambda b, i: (b, i, 0))
    vec = pl.BlockSpec((1, D_MODEL), lambda b, i: (0, 0))
    return pl.pallas_call(
        _rwkv_out_kernel,
        grid=(BATCH, blocks),
        in_specs=[row, row, row, row, vec, vec,
                  pl.BlockSpec((D_MODEL, D_MODEL), lambda b, i: (0, 0)), vec,
                  pl.BlockSpec((N_EXPERTS, 3 * D_MODEL), lambda b, i: (0, 0))],
        out_specs=[row, pl.BlockSpec((1, 1, N_COL_CHUNKS, TOKEN_BLOCK, PIECE_WORDS), lambda b, i: (b, i, 0, 0, 0)),
                   pl.BlockSpec((1, N_EXPERTS, TOKEN_BLOCK), lambda b, i: (b, 0, i)),
                   pl.BlockSpec((1, N_EXPERTS, 1), lambda b, i: (b * blocks + i, 0, 0))],
        out_shape=[jax.ShapeDtypeStruct((BATCH, SEQ, D_MODEL), F32),
                   jax.ShapeDtypeStruct((BATCH, blocks, N_COL_CHUNKS, TOKEN_BLOCK, PIECE_WORDS), U32),
                   jax.ShapeDtypeStruct((BATCH, N_EXPERTS, SEQ), F32),
                   jax.ShapeDtypeStruct((BATCH * blocks, N_EXPERTS, 1), F32)],
        scratch_shapes=[pltpu.VMEM((N_EXPERTS, 1), F32)],
        compiler_params=_params("arbitrary", "arbitrary"),
    )(o, bonus, g, h, gn_w, gn_b, w_o, gain, w_router)


def _routing_tables(route, cnt):
    i32 = jnp.int32
    route = jnp.swapaxes(route, 0, 1).reshape(N_EXPERTS, N_TOKENS)
    expert = route[0:2].astype(i32)
    rank = route[2:4].astype(i32)
    gate = route[4:6].T
    counts = jnp.sum(cnt.reshape(-1, N_EXPERTS).astype(i32), axis=0)
    tiles_e = (counts + EXPERT_TILE - 1) // EXPERT_TILE
    tile_end = jnp.cumsum(tiles_e)
    n_used = tile_end[-1]
    start_row = (tile_end - tiles_e) * EXPERT_TILE
    group_start = sum(jnp.where(expert == e, start_row[e], 0) for e in range(N_EXPERTS))
    pos = group_start + rank
    tiles = jnp.arange(MAX_EXPERT_TILES, dtype=i32)
    tile_expert = jnp.minimum(jnp.sum(tiles[:, None] >= tile_end[None, :], axis=1), N_EXPERTS - 1).astype(i32)
    tile_rows = jnp.take(counts, tile_expert) - (tiles - jnp.take(tile_end - tiles_e, tile_expert)) * EXPERT_TILE
    tile_rows = jnp.where(tiles < n_used, jnp.clip(tile_rows, 0, EXPERT_TILE), 0)
    pos = pos.reshape(TOP_K * N_TOKEN_BLOCKS, 1, TOKEN_BLOCK)
    chunk = jnp.arange(N_COL_CHUNKS, dtype=i32)[None, :, None]
    sorted_piece = (((pos // EXPERT_TILE) * N_COL_CHUNKS + chunk) * EXPERT_TILE + pos % EXPERT_TILE).reshape(-1)
    return dict(sorted_piece=sorted_piece, gate=gate, tile_expert=tile_expert,
                n_used=n_used.reshape(1).astype(i32), tile_rows=tile_rows.astype(i32))


def _row_gather(x, indices):
    m = indices.shape[0]
    mesh = plsc.VectorSubcoreMesh(core_axis_name="c", subcore_axis_name="s")

    @pl.kernel(out_type=jax.ShapeDtypeStruct((m, PIECE_WORDS), x.dtype), mesh=mesh)
    def gather(x_hbm, i_hbm, o_hbm):
        def body(i_vmem, o_vmem):
            pltpu.sync_copy(x_hbm.at[i_vmem.at[0]], o_vmem)

        pltpu.emit_pipeline(
            body, grid=(m // GATHER_ROWS,),
            in_specs=[pl.BlockSpec((1, GATHER_ROWS), lambda i: (0, i))],
            out_specs=[pl.BlockSpec((GATHER_ROWS, PIECE_WORDS), lambda i: (i, 0))],
            core_axis_name=("c", "s"),
            dimension_semantics=(pltpu.PARALLEL,),
        )(i_hbm, o_hbm)

    return gather(x, indices.reshape(1, m))


def _row_scatter(x, indices, out_rows):
    m = indices.shape[0]
    x_blocks = x.shape[0] // GATHER_ROWS
    mesh = plsc.VectorSubcoreMesh(core_axis_name="c", subcore_axis_name="s")

    @pl.kernel(out_type=jax.ShapeDtypeStruct((out_rows, PIECE_WORDS), x.dtype), mesh=mesh)
    def scatter(x_hbm, i_hbm, o_hbm):
        def body(x_vmem, i_vmem):
            pltpu.sync_copy(x_vmem, o_hbm.at[i_vmem.at[0]])

        pltpu.emit_pipeline(
            body, grid=(m // GATHER_ROWS,),
            in_specs=[pl.BlockSpec((GATHER_ROWS, PIECE_WORDS), lambda i: (i % x_blocks, 0)),
                      pl.BlockSpec((1, GATHER_ROWS), lambda i: (0, i))],
            out_specs=[],
            core_axis_name=("c", "s"),
            dimension_semantics=(pltpu.PARALLEL,),
        )(x_hbm, i_hbm)

    return scatter(x, indices.reshape(1, m))


def _expert_kernel(te_ref, nu_ref, nr_ref, x_ref, wg_ref, wu_ref, wd_ref, y_ref, xb, acc):
    i = pl.program_id(0)
    f = pl.program_id(1)
    n_rows = nr_ref[i]
    n_sub = (n_rows + SUB_TILE - 1) // SUB_TILE
    subs = EXPERT_TILE // SUB_TILE

    @pl.when(n_sub > 0)
    def _():
        @pl.when(f == 0)
        def _():
            real = lax.broadcasted_iota(jnp.int32, (EXPERT_TILE, 1), 0) < n_rows
            for c in range(N_COL_CHUNKS):
                halves = _unpack_piece(x_ref[c * EXPERT_TILE:(c + 1) * EXPERT_TILE, :])
                for n, half in enumerate(halves):
                    lo = c * COL_CHUNK + n * PIECE_WORDS
                    xb[:, lo:lo + PIECE_WORDS] = jnp.where(real, half, 0.0).astype(BF16)
            acc[...] = jnp.zeros_like(acc)

        wg = wg_ref[0].astype(BF16)
        wu = wu_ref[0].astype(BF16)
        wd = wd_ref[0].astype(BF16)

        def block(rows):
            x = xb[rows, :]
            a = _dot(x, wg)
            b = _dot(x, wu)
            acc[rows, :] += _dot((a * _sigmoid(a) * b).astype(BF16), wd)

        @pl.when(n_sub == subs)
        def _():
            block(slice(0, EXPERT_TILE))

        @pl.when(n_sub < subs)
        def _():
            for k in range(subs - 1):
                pl.when(k < n_sub)(functools.partial(block, slice(k * SUB_TILE, (k + 1) * SUB_TILE)))

        @pl.when(f == pl.num_programs(1) - 1)
        def _():
            for c in range(N_COL_CHUNKS):
                y_ref[c * EXPERT_TILE:(c + 1) * EXPERT_TILE, :] = _pack_piece(acc[:, c * COL_CHUNK:(c + 1) * COL_CHUNK])


def _experts(tab, x_sorted, wg, wu, wd):
    n_ff = D_FF // FF_TILE
    tile = lambda i, nu: jnp.minimum(i, nu[0] - 1)
    ff = lambda i, f, nu: jnp.where(i < nu[0], f, n_ff - 1)
    grid_spec = pltpu.PrefetchScalarGridSpec(
        num_scalar_prefetch=3, grid=(MAX_EXPERT_TILES, n_ff),
        in_specs=[pl.BlockSpec((N_COL_CHUNKS * EXPERT_TILE, PIECE_WORDS), lambda i, f, te, nu, ns: (tile(i, nu), 0)),
                  pl.BlockSpec((1, D_MODEL, FF_TILE),
                               lambda i, f, te, nu, ns: (te[tile(i, nu)], 0, ff(i, f, nu))),
                  pl.BlockSpec((1, D_MODEL, FF_TILE),
                               lambda i, f, te, nu, ns: (te[tile(i, nu)], 0, ff(i, f, nu))),
                  pl.BlockSpec((1, FF_TILE, D_MODEL),
                               lambda i, f, te, nu, ns: (te[tile(i, nu)], ff(i, f, nu), 0))],
        out_specs=pl.BlockSpec((N_COL_CHUNKS * EXPERT_TILE, PIECE_WORDS), lambda i, f, te, nu, ns: (tile(i, nu), 0)),
        scratch_shapes=[pltpu.VMEM((EXPERT_TILE, D_MODEL), BF16), pltpu.VMEM((EXPERT_TILE, D_MODEL), F32)])
    return pl.pallas_call(
        _expert_kernel, grid_spec=grid_spec,
        out_shape=jax.ShapeDtypeStruct((N_COL_CHUNKS * MAX_SORTED_ROWS, PIECE_WORDS), U32),
        compiler_params=_params("arbitrary", "arbitrary"),
    )(tab["tile_expert"], tab["n_used"], tab["tile_rows"], x_sorted, wg, wu, wd)


def _combine_kernel(h_ref, y_ref, g_ref, o_ref):
    g = g_ref[...]
    for c in range(N_COL_CHUNKS):
        first, second = _unpack_piece(y_ref[0, 0, c]), _unpack_piece(y_ref[1, 0, c])
        for n in range(2):
            cols = slice(c * COL_CHUNK + n * PIECE_WORDS, c * COL_CHUNK + (n + 1) * PIECE_WORDS)
            o_ref[:, cols] = h_ref[:, cols] + g[:, 0:1] * first[n] + g[:, 1:2] * second[n]


def _combine(h, y_pair, gate):
    row = pl.BlockSpec((TOKEN_BLOCK, D_MODEL), lambda i: (i, 0))
    return pl.pallas_call(
        _combine_kernel,
        grid=(N_TOKEN_BLOCKS,),
        in_specs=[row, pl.BlockSpec((TOP_K, 1, N_COL_CHUNKS, TOKEN_BLOCK, PIECE_WORDS), lambda i: (0, i, 0, 0, 0)),
                  pl.BlockSpec((TOKEN_BLOCK, TOP_K), lambda i: (i, 0))],
        out_specs=row,
        out_shape=jax.ShapeDtypeStruct((N_TOKENS, D_MODEL), F32),
        compiler_params=_params("parallel"),
    )(h, y_pair, gate)


def kernel(x, meta_tokens, mixer_norm, ffn_norm, attn_w_qkv, attn_q_norm, attn_k_norm, attn_sinks, attn_w_o, rwkv_mix, rwkv_w0, rwkv_w1, rwkv_w2, rwkv_a0, rwkv_a1, rwkv_a2, rwkv_g1, rwkv_g2, rwkv_k_k, rwkv_k_a, rwkv_r_k, rwkv_w_r, rwkv_w_k, rwkv_w_v, rwkv_w_o, rwkv_gn_w, rwkv_gn_b, ffn_w_gate, ffn_w_up, ffn_w_down, moe_router, moe_w_gate, moe_w_up, moe_w_down):
    bf = lambda a: a.astype(BF16)
    vec = lambda a: a.reshape(1, -1).astype(F32)
    h_meta = jnp.concatenate([jnp.zeros((PAD, D_MODEL), F32), meta_tokens.astype(F32)], axis=0)

    w_qkv = bf(attn_w_qkv[0])
    qkv = _qkv(x.reshape(N_TOKENS, D_MODEL), vec(mixer_norm[0]), w_qkv, TOKEN_BLOCK)
    qkv_meta = _qkv(h_meta, vec(mixer_norm[0]), w_qkv, BLOCK)
    h = _attention(x, h_meta, qkv.reshape(BATCH, SEQ, QKV_DIM), qkv_meta, attn_sinks[0].astype(F32),
                   jnp.tile(vec(attn_q_norm[0]), (1, Q_PER_KV)), jnp.tile(vec(attn_k_norm[0]), (1, N_KV_HEADS)),
                   bf(attn_w_o[0]))
    h = _ffn(h.reshape(N_ROWS, D_MODEL), vec(ffn_norm[0]), bf(ffn_w_gate[0]), bf(ffn_w_up[0]),
             bf(ffn_w_down[0]))

    rkv, g = _rwkv_proj(
        h, vec(mixer_norm[1]), rwkv_mix[0], vec(rwkv_w0[0]), vec(rwkv_a0[0]), vec(rwkv_k_k[0]),
        vec(rwkv_k_a[0]), bf(rwkv_w_r[0]), bf(rwkv_w_k[0]), bf(rwkv_w_v[0]), bf(rwkv_w1[0]),
        bf(rwkv_w2[0]), bf(rwkv_a1[0]), bf(rwkv_a2[0]), bf(rwkv_g1[0]), bf(rwkv_g2[0]))
    b3 = lambda t: t.reshape(BATCH, TP, -1)
    mm, add, bonus, p_last = _scan_prep(b3(rkv), vec(rwkv_r_k[0]))
    o = _scan(mm, add, p_last)
    w_router = moe_router[0].astype(F32).T
    wr_hi = bf(w_router)
    wr_lo = bf(w_router - wr_hi.astype(F32))
    h, u, route, cnt = _rwkv_out(o, bonus, b3(g), b3(h), vec(rwkv_gn_w[0]), vec(rwkv_gn_b[0]),
                                 bf(rwkv_w_o[0]), vec(ffn_norm[1]),
                                 jnp.concatenate([wr_hi, wr_hi, wr_lo], axis=1))
    tab = _routing_tables(route, cnt)
    x_sorted = _row_scatter(u.reshape(-1, PIECE_WORDS), tab["sorted_piece"], N_COL_CHUNKS * MAX_SORTED_ROWS)
    y_sorted = _experts(tab, x_sorted, moe_w_gate[0], moe_w_up[0], moe_w_down[0])
    y_pair = _row_gather(y_sorted, tab["sorted_piece"]).reshape(TOP_K, N_TOKEN_BLOCKS, N_COL_CHUNKS, TOKEN_BLOCK, PIECE_WORDS)
    out = _combine(h.reshape(N_TOKENS, D_MODEL), y_pair, tab["gate"])
    return out.reshape(BATCH, SEQ, D_MODEL)
```

```python
import functools

import jax
import jax.numpy as jnp
from jax import lax
from jax.experimental import pallas as pl
from jax.experimental.pallas import tpu as pltpu
from jax.experimental.pallas import tpu_sc as plsc

F32 = jnp.float32
BF16 = jnp.bfloat16

D_MODEL = 1024
BATCH = 2
SEQ = 8192
N_META = 16
BLOCK = 128
PAD = BLOCK - N_META
TP = SEQ + BLOCK
N_ROWS = BATCH * TP
N_BLOCKS = TP // BLOCK
META_BLOCK = N_BLOCKS - 1
HEAD_DIM = 64
N_Q_HEADS = 16
N_KV_HEADS = 4
Q_PER_KV = 4
QKV_DIM = (N_Q_HEADS + 2 * N_KV_HEADS) * HEAD_DIM
RWKV_HEADS = 16
RWKV_HEAD = 64
D_FF = 3584
N_EXPERTS = 8
NORM_EPS = 1e-5
GN_EPS = 64e-5
CHUNK = 64
N_CHUNKS = TP // CHUNK
META_CHUNK0 = SEQ // CHUNK
SCAN_CHUNKS = 2
HEADS_PER_GROUP = 4
GW = HEADS_PER_GROUP * RWKV_HEAD
N_GROUPS = RWKV_HEADS // HEADS_PER_GROUP
N_SCAN_IN = 6

FFN_ROW_TILE = 1664
PROJ_TILE = 320
FF_TILE = 512
N_TOKENS = BATCH * SEQ
TOP_K = 2
TOKEN_BLOCK = 512
N_TOKEN_BLOCKS = N_TOKENS // TOKEN_BLOCK
EXPERT_TILE = 1024
SUB_TILE = 256
MAX_EXPERT_TILES = TOP_K * N_TOKENS // EXPERT_TILE + N_EXPERTS
MAX_SORTED_ROWS = MAX_EXPERT_TILES * EXPERT_TILE
COL_CHUNK = 256
N_COL_CHUNKS = D_MODEL // COL_CHUNK
PIECE_WORDS = COL_CHUNK // 2
U32 = jnp.uint32
GATHER_ROWS = 128
VMEM_LIMIT = 56 * 1024 * 1024
NEG = -1e30

NT_DIMS = (((1,), (1,)), ((), ()))
TN_DIMS = (((0,), (0,)), ((), ()))


def _params(*sem):
    return pltpu.CompilerParams(dimension_semantics=sem, vmem_limit_bytes=VMEM_LIMIT)


def _rms(x, gain):
    return x * lax.rsqrt(jnp.mean(x * x, axis=-1, keepdims=True) + NORM_EPS) * gain


def _sigmoid(x):
    return 1.0 / (1.0 + jnp.exp(-x))


def _dot(a, b):
    return jnp.dot(a, b, preferred_element_type=F32)


def _pack_piece(x):
    return pltpu.pack_elementwise([x[:, :PIECE_WORDS], x[:, PIECE_WORDS:]], packed_dtype=BF16)


def _unpack_piece(w):
    return tuple(pltpu.unpack_elementwise(w, index=i, packed_dtype=BF16, unpacked_dtype=F32) for i in (0, 1))


def _qkv_kernel(h_ref, g_ref, w_ref, o_ref):
    u = _rms(h_ref[...], g_ref[...])
    o_ref[...] = _dot(u.astype(BF16), w_ref[...]).astype(BF16)


def _qkv(h, gain, w, tile):
    rows = h.shape[0]
    return pl.pallas_call(
        _qkv_kernel,
        grid=(rows // tile,),
        in_specs=[pl.BlockSpec((tile, D_MODEL), lambda i: (i, 0)),
                  pl.BlockSpec((1, D_MODEL), lambda i: (0, 0)),
                  pl.BlockSpec((D_MODEL, QKV_DIM), lambda i: (0, 0))],
        out_specs=pl.BlockSpec((tile, QKV_DIM), lambda i: (i, 0)),
        out_shape=jax.ShapeDtypeStruct((rows, QKV_DIM), BF16),
        compiler_params=_params("parallel"),
    )(h, gain, w)


def _attn_kernel(sink_ref, x_ref, hm_ref, q_ref, kc_ref, vc_ref, kp_ref, vp_ref, qm_ref, km_ref, vm_ref,
                 qg_ref, kg_ref, wo_ref, o_ref, o_scr):
    j = pl.program_id(1)
    n_band = 2 * BLOCK
    slots = 2 * N_META
    far = 4 * BLOCK
    is_real = j < META_BLOCK
    prev_off = jnp.where(jnp.logical_and(j >= 1, is_real), 0, far)
    cur_off = jnp.where(is_real, 0, far)
    rowi = lax.broadcasted_iota(jnp.int32, (BLOCK, n_band), 0)
    col = lax.broadcasted_iota(jnp.int32, (BLOCK, n_band), 1)
    band_ok = jnp.logical_or(jnp.logical_and(col < BLOCK, col > rowi + prev_off),
                             jnp.logical_and(col >= BLOCK, col - BLOCK + cur_off <= rowi))
    band_ok4 = jnp.concatenate([band_ok] * Q_PER_KV, axis=1)
    meta_off = jnp.where(is_real, -far, PAD)
    mrow = lax.broadcasted_iota(jnp.int32, (BLOCK, Q_PER_KV * slots), 0)
    mlane = lax.broadcasted_iota(jnp.int32, (BLOCK, Q_PER_KV * slots), 1)
    mslot = mlane % slots
    meta_ok4 = jnp.logical_and(mslot < N_META, mslot + meta_off <= mrow)
    meta_lane_head = mlane // slots

    groups = range(N_KV_HEADS)
    mask_bf = _group_masks().astype(BF16)
    rb = lax.broadcasted_iota(jnp.int32, (Q_PER_KV * n_band, GW), 0) // n_band
    lb = lax.broadcasted_iota(jnp.int32, (Q_PER_KV * n_band, GW), 1) // HEAD_DIM
    band_mask = jnp.where(rb == lb, 1.0, 0.0).astype(BF16)
    rm = lax.broadcasted_iota(jnp.int32, (Q_PER_KV * slots, GW), 0) // slots
    lm = lax.broadcasted_iota(jnp.int32, (Q_PER_KV * slots, GW), 1) // HEAD_DIM
    meta_mask = jnp.where(rm == lm, 1.0, 0.0).astype(BF16)
    sr = lax.broadcasted_iota(jnp.int32, (GW, GW), 0)
    sc = lax.broadcasted_iota(jnp.int32, (GW, GW), 1)
    lane_head = lax.broadcasted_iota(jnp.int32, (1, GW), 1) // HEAD_DIM
    inv_d = 1.0 / HEAD_DIM
    scale = HEAD_DIM ** -0.5

    real_rows = lax.broadcasted_iota(jnp.int32, (BLOCK, 1), 0) < jnp.where(is_real, BLOCK, 0)
    km, vm = km_ref[...], vm_ref[...]
    kcur = jnp.where(real_rows, kc_ref[0], km)
    vcur = jnp.where(real_rows, vc_ref[0], vm)
    kall = jnp.concatenate([km[PAD:], kp_ref[0], kcur], axis=0).astype(F32)
    vall = jnp.concatenate([vm[PAD:], vp_ref[0], vcur], axis=0)
    kss = _dot((kall * kall).astype(BF16), mask_bf)
    kn = (kall * lax.rsqrt(kss * inv_d + NORM_EPS) * kg_ref[...]).astype(BF16)
    q_all = jnp.where(real_rows, q_ref[0], qm_ref[...]).astype(F32)
    qn = []
    for g in groups:
        qg = q_all[:, g * GW:(g + 1) * GW]
        qss = _dot((qg * qg).astype(BF16), mask_bf)
        qn.append((qg * lax.rsqrt(qss * inv_d + NORM_EPS) * (qg_ref[...] * scale)).astype(BF16))
    sel = [jnp.where(sr == g * HEAD_DIM + sc % HEAD_DIM, 1.0, 0.0).astype(BF16) for g in groups]
    krep = [_dot(kn, sel[g]).astype(BF16) for g in groups]
    vrep = [_dot(vall, sel[g]).astype(BF16) for g in groups]
    unused = jnp.zeros((slots - N_META, GW), BF16)

    def band_bd(rep):
        return jnp.concatenate([rep[N_META:]] * Q_PER_KV, axis=0) * band_mask

    def meta_bd(rep):
        return jnp.concatenate([rep[:N_META], unused] * Q_PER_KV, axis=0) * meta_mask

    s_band = [jnp.where(band_ok4, lax.dot_general(qn[g], band_bd(krep[g]), NT_DIMS,
                                                  preferred_element_type=F32), NEG) for g in groups]
    s_meta = [jnp.where(meta_ok4, lax.dot_general(qn[g], meta_bd(krep[g]), NT_DIMS,
                                                  preferred_element_type=F32), NEG) for g in groups]
    rhs_band = [jnp.concatenate([band_bd(vrep[g]), band_mask], axis=1) for g in groups]
    rhs_meta = [jnp.concatenate([meta_bd(vrep[g]), meta_mask], axis=1) for g in groups]
    ov, sink_den = [], []
    for g in groups:
        parts = []
        sd = jnp.zeros((BLOCK, GW), F32)
        m_lanes = jnp.zeros((BLOCK, Q_PER_KV * slots), F32)
        for hh in range(Q_PER_KV):
            seg = s_band[g][:, hh * n_band:(hh + 1) * n_band]
            own = meta_lane_head == hh
            sink = sink_ref[g * Q_PER_KV + hh]
            m = jnp.maximum(jnp.max(seg, axis=-1, keepdims=True),
                            jnp.max(jnp.where(own, s_meta[g], NEG), axis=-1, keepdims=True))
            m = jnp.maximum(m, sink)
            parts.append(jnp.exp(seg - m).astype(BF16))
            m_lanes = jnp.where(own, m, m_lanes)
            sd = sd + jnp.exp(sink - m) * jnp.where(lane_head == hh, 1.0, 0.0)
        p_band = jnp.concatenate(parts, axis=1)
        p_meta = jnp.exp(s_meta[g] - m_lanes).astype(BF16)
        ov.append(_dot(p_band, rhs_band[g]) + _dot(p_meta, rhs_meta[g]))
        sink_den.append(sd)
    for g in groups:
        o_scr[:, g * GW:(g + 1) * GW] = ov[g][:, :GW] / (ov[g][:, GW:] + sink_den[g])
    h = jnp.where(real_rows, x_ref[0], hm_ref[...])
    o_ref[0] = h + _dot(o_scr[...].astype(BF16), wo_ref[...])


def _attention(x, h_meta, qkv, qkv_meta, sinks, q_gain, k_gain, w_o):
    kcol, vcol = N_Q_HEADS * HEAD_DIM // 256, N_Q_HEADS * HEAD_DIM // 256 + 1
    kvw = N_KV_HEADS * HEAD_DIM
    real = lambda j: jnp.minimum(j, META_BLOCK - 1)
    prev = lambda j: jnp.clip(j - 1, 0, META_BLOCK - 1)
    return pl.pallas_call(
        _attn_kernel,
        grid=(BATCH, N_BLOCKS),
        in_specs=[pl.BlockSpec(memory_space=pltpu.SMEM),
                  pl.BlockSpec((1, BLOCK, D_MODEL), lambda b, j: (b, real(j), 0)),
                  pl.BlockSpec((BLOCK, D_MODEL), lambda b, j: (0, 0)),
                  pl.BlockSpec((1, BLOCK, D_MODEL), lambda b, j: (b, real(j), 0)),
                  pl.BlockSpec((1, BLOCK, kvw), lambda b, j: (b, real(j), kcol)),
                  pl.BlockSpec((1, BLOCK, kvw), lambda b, j: (b, real(j), vcol)),
                  pl.BlockSpec((1, BLOCK, kvw), lambda b, j: (b, prev(j), kcol)),
                  pl.BlockSpec((1, BLOCK, kvw), lambda b, j: (b, prev(j), vcol)),
                  pl.BlockSpec((BLOCK, D_MODEL), lambda b, j: (0, 0)),
                  pl.BlockSpec((BLOCK, kvw), lambda b, j: (0, kcol)),
                  pl.BlockSpec((BLOCK, kvw), lambda b, j: (0, vcol)),
                  pl.BlockSpec((1, GW), lambda b, j: (0, 0)),
                  pl.BlockSpec((1, GW), lambda b, j: (0, 0)),
                  pl.BlockSpec((D_MODEL, D_MODEL), lambda b, j: (0, 0))],
        out_specs=pl.BlockSpec((1, BLOCK, D_MODEL), lambda b, j: (b, j, 0)),
        out_shape=jax.ShapeDtypeStruct((BATCH, TP, D_MODEL), F32),
        scratch_shapes=[pltpu.VMEM((BLOCK, D_MODEL), F32)],
        compiler_params=_params("parallel", "parallel"),
    )(sinks, x, h_meta, qkv, qkv, qkv, qkv, qkv, qkv_meta, qkv_meta, qkv_meta, q_gain, k_gain, w_o)


def _ffn_kernel(h_ref, g_ref, wg_ref, wu_ref, wd_ref, o_ref, u_scr, acc):
    f = pl.program_id(1)

    @pl.when(f == 0)
    def _():
        u_scr[...] = _rms(h_ref[...], g_ref[...]).astype(BF16)
        acc[...] = jnp.zeros_like(acc)

    u = u_scr[...]
    a = _dot(u, wg_ref[...])
    b = _dot(u, wu_ref[...])
    acc[...] += _dot((a * _sigmoid(a) * b).astype(BF16), wd_ref[...])

    @pl.when(f == pl.num_programs(1) - 1)
    def _():
        o_ref[...] = h_ref[...] + acc[...]


def _ffn(h, gain, wg, wu, wd):
    return pl.pallas_call(
        _ffn_kernel,
        grid=(N_ROWS // FFN_ROW_TILE, D_FF // FF_TILE),
        in_specs=[pl.BlockSpec((FFN_ROW_TILE, D_MODEL), lambda i, f: (i, 0)),
                  pl.BlockSpec((1, D_MODEL), lambda i, f: (0, 0)),
                  pl.BlockSpec((D_MODEL, FF_TILE), lambda i, f: (0, f)),
                  pl.BlockSpec((D_MODEL, FF_TILE), lambda i, f: (0, f)),
                  pl.BlockSpec((FF_TILE, D_MODEL), lambda i, f: (f, 0))],
        out_specs=pl.BlockSpec((FFN_ROW_TILE, D_MODEL), lambda i, f: (i, 0)),
        out_shape=jax.ShapeDtypeStruct((N_ROWS, D_MODEL), F32),
        scratch_shapes=[pltpu.VMEM((FFN_ROW_TILE, D_MODEL), BF16), pltpu.VMEM((FFN_ROW_TILE, D_MODEL), F32)],
        compiler_params=_params("parallel", "arbitrary"),
    )(h, gain, wg, wu, wd)


def _rwkv_proj_kernel(h_ref, hp_ref, g_ref, mix_ref, w0_ref, a0_ref, kk_ref, ka_ref,
                      wr_ref, wk_ref, wv_ref, w1_ref, w2_ref, a1_ref, a2_ref, g1_ref, g2_ref,
                      rkv_out, g_out):
    i = pl.program_id(0)
    tiles_per_batch = TP // PROJ_TILE
    r0 = (i % tiles_per_batch) * PROJ_TILE
    local = lax.broadcasted_iota(jnp.int32, (PROJ_TILE, 1), 0)
    lrow = local + r0
    gain = g_ref[...]
    is_pad = jnp.logical_and(lrow >= SEQ, lrow < SEQ + PAD)
    u = jnp.where(is_pad, 0.0, _rms(h_ref[...], gain))
    u_prev_tile = _rms(hp_ref[7:8, :], gain)
    xprev = pltpu.roll(u, 1, 0)
    xprev = jnp.where(local == 0, u_prev_tile, xprev)
    xprev = jnp.where(lrow == SEQ, 0.0, xprev)
    xx = xprev - u
    mix = mix_ref[...]
    lerp = lambda n: (u + xx * mix[n:n + 1, :]).astype(BF16)
    xr, xw, xk, xv, xa, xg = [lerp(n) for n in range(6)]
    r = _dot(xr, wr_ref[...])
    k = _dot(xk, wk_ref[...])
    v = _dot(xv, wv_ref[...])
    lw = _dot(jnp.tanh(_dot(xw, w1_ref[...])).astype(BF16), w2_ref[...])
    z = -(w0_ref[...] + lw)
    softplus = jnp.maximum(z, 0.0) + jnp.log(1.0 + jnp.exp(-jnp.abs(z)))
    w = -softplus - 0.5
    a = _sigmoid(a0_ref[...] + _dot(_dot(xa, a1_ref[...]).astype(BF16), a2_ref[...]))
    g = _dot(_sigmoid(_dot(xg, g1_ref[...])).astype(BF16), g2_ref[...])
    fields = (r, -jnp.exp(w),
              k * (1.0 + (a - 1.0) * ka_ref[...]), v, k * kk_ref[...], a)
    for n, val in enumerate(fields):
        rkv_out[:, n * D_MODEL:(n + 1) * D_MODEL] = val
    g_out[...] = g


def _rwkv_proj(h, gain, mix, w0, a0, k_k, k_a, w_r, w_k, w_v, w1, w2, a1, a2, g1, g2):
    tiles_per_batch = TP // PROJ_TILE
    rows8 = PROJ_TILE // 8

    def prev_map(i):
        b = i // tiles_per_batch
        first = (i % tiles_per_batch) == 0
        return (jnp.where(first, (b * TP + TP - 8) // 8, i * rows8 - 1), 0)

    row = pl.BlockSpec((PROJ_TILE, D_MODEL), lambda i: (i, 0))
    full = lambda a: pl.BlockSpec(a.shape, lambda i: (0,) * a.ndim)
    smalls = (gain, mix, w0, a0, k_k, k_a, w_r, w_k, w_v, w1, w2, a1, a2, g1, g2)
    return pl.pallas_call(
        _rwkv_proj_kernel,
        grid=(N_ROWS // PROJ_TILE,),
        in_specs=[row, pl.BlockSpec((8, D_MODEL), prev_map)] + [full(a) for a in smalls],
        out_specs=[pl.BlockSpec((PROJ_TILE, N_SCAN_IN * D_MODEL), lambda i: (i, 0)), row],
        out_shape=[jax.ShapeDtypeStruct((N_ROWS, N_SCAN_IN * D_MODEL), F32),
                   jax.ShapeDtypeStruct((N_ROWS, D_MODEL), F32)],
        compiler_params=_params("parallel"),
    )(h, h, *smalls)


def _group_masks():
    ri = lax.broadcasted_iota(jnp.int32, (GW, GW), 0) // RWKV_HEAD
    ci = lax.broadcasted_iota(jnp.int32, (GW, GW), 1) // RWKV_HEAD
    return jnp.where(ri == ci, 1.0, 0.0).astype(F32)


def _bd(x, mask):
    return jnp.concatenate([x.astype(BF16)] * HEADS_PER_GROUP, axis=0) * mask.astype(BF16)


def _diag_blocks(full, mask):
    m = full * mask
    n = RWKV_HEAD
    return (m[0:n] + m[n:2 * n]) + (m[2 * n:3 * n] + m[3 * n:4 * n])


def _head_sum(x, mask_bf):
    hi = x.astype(BF16)
    lo = (x - hi.astype(F32)).astype(BF16)
    return _dot(hi, mask_bf) + _dot(lo, mask_bf)


def _scan_prep_kernel(x_ref, rk_ref, mm_out, add_out, bonus_out, pl_out):
    L, D = CHUNK, D_MODEL
    units = [(b, g) for b in range(BATCH) for g in range(N_GROUPS)]
    un = range(len(units))
    mask = _group_masks()
    mask_bf = mask.astype(BF16)
    ri = lax.broadcasted_iota(jnp.int32, (L, GW), 0)
    ci = lax.broadcasted_iota(jnp.int32, (L, GW), 1) % RWKV_HEAD
    incl = ci <= ri
    strict = ci < ri
    eye = jnp.where(ci == ri, 1.0, 0.0).astype(F32)
    t_r = lax.broadcasted_iota(jnp.int32, (L, L), 0)
    t_c = lax.broadcasted_iota(jnp.int32, (L, L), 1)
    tril = jnp.where(t_c <= t_r, 1.0, 0.0).astype(BF16)
    rk_all = rk_ref[...]

    def field(b, n, g):
        return x_ref[b, :, n * D + g * GW:n * D + (g + 1) * GW]

    def put(ref, b, n, g, val):
        ref[b, :, n * D + g * GW:n * D + (g + 1) * GW] = val.astype(ref.dtype)

    at, rt, bt, kt, v, plast = [], [], [], [], [], []
    for b in range(BATCH):
        ld = x_ref[b, :, D:2 * D]
        hi = ld.astype(BF16)
        rest = ld - hi.astype(F32)
        mid = rest.astype(BF16)
        lo = (rest - mid.astype(F32)).astype(BF16)
        cs = _dot(tril, hi) + _dot(tril, mid) + _dot(tril, lo)
        p_all = jnp.exp(cs)
        pprev_all = jnp.exp(cs - ld)
        pinv_all = jnp.exp(-cs)
        pl_all = p_all[L - 1:L, :]
        pl_out[b, 0] = pl_all
        for g in range(N_GROUPS):
            sl = slice(g * GW, (g + 1) * GW)
            r, k, vv, kk, a = field(b, 0, g), field(b, 2, g), field(b, 3, g), field(b, 4, g), field(b, 5, g)
            nrm = jnp.sqrt(_dot((kk * kk).astype(BF16), mask_bf))
            kk = kk / jnp.maximum(nrm, 1e-12)
            at.append(-kk * pprev_all[:, sl])
            bt.append(kk * a * pinv_all[:, sl])
            rt.append(r * p_all[:, sl])
            kt.append(k * pinv_all[:, sl])
            v.append(vv)
            plast.append(pl_all[:, sl])
            put(bonus_out, b, 0, g, _dot((r * k * rk_all[:, sl]).astype(BF16), mask_bf) * vv)

    a_ab, a_ak, a_rb, a_rk = [], [], [], []
    for n in un:
        lhs = jnp.concatenate([at[n], rt[n]], axis=0).astype(BF16)
        rhs = jnp.concatenate([_bd(bt[n], mask), _bd(kt[n], mask)], axis=0)
        big = lax.dot_general(lhs, rhs, NT_DIMS, preferred_element_type=F32)
        a_ab.append(jnp.where(strict, big[:L, :GW], 0.0))
        a_ak.append(jnp.where(strict, big[:L, GW:], 0.0))
        a_rb.append(jnp.where(incl, big[L:, :GW], 0.0))
        a_rk.append(jnp.where(incl, big[L:, GW:], 0.0))

    x = [_dot(a_ab[n].astype(BF16), _bd(a_ab[n], mask)) for n in un]
    inv = [eye + a_ab[n] for n in un]
    for step in range(5):
        for n in un:
            rhs = _bd(x[n], mask)
            if step < 4:
                res = _dot(jnp.concatenate([x[n], inv[n]], axis=0).astype(BF16), rhs)
                x[n] = res[:L]
                inv[n] = inv[n] + res[L:]
            else:
                inv[n] = inv[n] + _dot(inv[n].astype(BF16), rhs)

    av = [_dot(jnp.concatenate([a_ak[n], a_rk[n]], axis=0).astype(BF16), _bd(v[n], mask)) for n in un]
    wu = [_dot(inv[n].astype(BF16), jnp.concatenate([_bd(at[n], mask), _bd(av[n][:L], mask)], axis=1))
          for n in un]
    aw = [_dot(a_rb[n].astype(BF16),
               jnp.concatenate([_bd(wu[n][:, :GW], mask), _bd(wu[n][:, GW:], mask)], axis=1))
          for n in un]
    for n, (b, g) in enumerate(units):
        put(mm_out, b, 0, g, rt[n] + aw[n][:, :GW])
        put(add_out, b, 0, g, av[n][L:] + aw[n][:, GW:])
        bh = (bt[n] * plast[n]).astype(BF16)
        kh = (kt[n] * plast[n]).astype(BF16)
        w_b, u0_b = wu[n][:, :GW].astype(BF16), wu[n][:, GW:].astype(BF16)
        gfull = lax.dot_general(bh, w_b, TN_DIMS, preferred_element_type=F32)
        put(mm_out, b, 1, g, _diag_blocks(gfull, mask))
        hfull = lax.dot_general(jnp.concatenate([u0_b, v[n].astype(BF16)], axis=0),
                                jnp.concatenate([bh, kh], axis=0), TN_DIMS, preferred_element_type=F32)
        put(add_out, b, 1, g, _diag_blocks(hfull, mask))


def _scan_prep(rkv, r_k):
    return pl.pallas_call(
        _scan_prep_kernel,
        grid=(N_CHUNKS,),
        in_specs=[pl.BlockSpec((BATCH, CHUNK, N_SCAN_IN * D_MODEL), lambda c: (0, c, 0)),
                  pl.BlockSpec((1, D_MODEL), lambda c: (0, 0))],
        out_specs=[pl.BlockSpec((BATCH, CHUNK, 2 * D_MODEL), lambda c: (0, c, 0)),
                   pl.BlockSpec((BATCH, CHUNK, 2 * D_MODEL), lambda c: (0, c, 0)),
                   pl.BlockSpec((BATCH, CHUNK, D_MODEL), lambda c: (0, c, 0)),
                   pl.BlockSpec((BATCH, 1, 1, D_MODEL), lambda c: (0, c, 0, 0))],
        out_shape=[jax.ShapeDtypeStruct((BATCH, TP, 2 * D_MODEL), BF16),
                   jax.ShapeDtypeStruct((BATCH, TP, 2 * D_MODEL), F32),
                   jax.ShapeDtypeStruct((BATCH, TP, D_MODEL), F32),
                   jax.ShapeDtypeStruct((BATCH, N_CHUNKS, 1, D_MODEL), F32)],
        compiler_params=_params("parallel"),
    )(rkv, r_k)


def _scan_kernel(mm_ref, add_ref, pl_ref, y_ref, s_scr):
    c = pl.program_id(0)
    D = D_MODEL

    @pl.when(c == 0)
    def _():
        s_scr[...] = jnp.zeros_like(s_scr)

    mask = _group_masks()
    units = [(b, slice(g * GW, (g + 1) * GW)) for b in range(BATCH) for g in range(N_GROUPS)]
    s = [s_scr[b, :, sl] for b, sl in units]
    for k in range(SCAN_CHUNKS):
        rows = slice(k * CHUNK, (k + 1) * CHUNK)
        field = lambda ref, b, n, sl, rows=rows: ref[b, rows, n * D + sl.start:n * D + sl.stop]
        o = [lax.dot_general(field(mm_ref, b, 0, sl), _bd(s[n], mask), NT_DIMS,
                             preferred_element_type=F32) + field(add_ref, b, 0, sl)
             for n, (b, sl) in enumerate(units)]
        sg = [lax.dot_general(s[n].astype(BF16), _bd(field(mm_ref, b, 1, sl), mask), NT_DIMS,
                              preferred_element_type=F32)
              for n, (b, sl) in enumerate(units)]
        s = [s[n] * pl_ref[b, k, :, sl] + sg[n] + field(add_ref, b, 1, sl)
             for n, (b, sl) in enumerate(units)]
        for n, (b, sl) in enumerate(units):
            y_ref[b, rows, sl] = o[n]
    for n, (b, sl) in enumerate(units):
        s_scr[b, :, sl] = s[n]


def _scan(mm, add, p_last):
    steps = N_CHUNKS // SCAN_CHUNKS
    first = META_CHUNK0 // SCAN_CHUNKS
    phys = lambda c: (c + first) % steps
    pair = pl.BlockSpec((BATCH, SCAN_CHUNKS * CHUNK, 2 * D_MODEL), lambda c: (0, phys(c), 0))
    return pl.pallas_call(
        _scan_kernel,
        grid=(steps,),
        in_specs=[pair, pair, pl.BlockSpec((BATCH, SCAN_CHUNKS, 1, D_MODEL), lambda c: (0, phys(c), 0, 0))],
        out_specs=pl.BlockSpec((BATCH, SCAN_CHUNKS * CHUNK, D_MODEL), lambda c: (0, phys(c), 0)),
        out_shape=jax.ShapeDtypeStruct((BATCH, TP, D_MODEL), F32),
        scratch_shapes=[pltpu.VMEM((BATCH, RWKV_HEAD, D_MODEL), F32)],
        compiler_params=_params("arbitrary"),
    )(mm, add, p_last)


def _rwkv_out_kernel(o_ref, bonus_ref, g_ref, h_ref, gw_ref, gb_ref, wo_ref, gain_ref, wr_ref,
                     h_out, u_out, route_out, cnt_out, carry):
    @pl.when(jnp.logical_and(pl.program_id(0) == 0, pl.program_id(1) == 0))
    def _():
        carry[...] = jnp.zeros_like(carry)

    mask_bf = _group_masks().astype(BF16)
    inv_n = 1.0 / RWKV_HEAD
    o, bonus, gate = o_ref[0], bonus_ref[0], g_ref[0]
    parts = []
    for g in range(N_GROUPS):
        sl = slice(g * GW, (g + 1) * GW)
        og = o[:, sl]
        d = og - _head_sum(og, mask_bf) * inv_n
        var = _dot((d * d).astype(BF16), mask_bf) * inv_n
        yn = d * lax.rsqrt(var + GN_EPS) * gw_ref[:, sl] + gb_ref[:, sl] + bonus[:, sl]
        parts.append((yn * gate[:, sl]).astype(BF16))
    h = h_ref[0] + _dot(jnp.concatenate(parts, axis=1), wo_ref[...])
    h_out[0] = h
    u = _rms(h, gain_ref[...])
    for c in range(N_COL_CHUNKS):
        u_out[0, 0, c] = _pack_piece(u[:, c * COL_CHUNK:(c + 1) * COL_CHUNK])
    u_hi = u.astype(BF16)
    u_lo = (u - u_hi.astype(F32)).astype(BF16)
    logits = lax.dot_general(wr_ref[...], jnp.concatenate([u_hi, u_lo, u_hi], axis=1), NT_DIMS,
                             preferred_element_type=F32)
    e = jnp.exp(logits - jnp.max(logits, axis=0, keepdims=True))
    probs = e / jnp.sum(e, axis=0, keepdims=True)
    idx = lax.broadcasted_iota(jnp.int32, probs.shape, 0).astype(F32)
    m1 = jnp.max(probs, axis=0, keepdims=True)
    i1 = jnp.min(jnp.where(probs == m1, idx, float(N_EXPERTS)), axis=0, keepdims=True)
    sel1 = idx == i1
    rest = jnp.where(sel1, -1.0, probs)
    m2 = jnp.max(rest, axis=0, keepdims=True)
    i2 = jnp.min(jnp.where(rest == m2, idx, float(N_EXPERTS)), axis=0, keepdims=True)
    sel2 = idx == i2
    onehot = jnp.where(jnp.logical_or(sel1, sel2), 1.0, 0.0).astype(F32)
    tr = lax.broadcasted_iota(jnp.int32, (TOKEN_BLOCK, TOKEN_BLOCK), 0)
    tc = lax.broadcasted_iota(jnp.int32, (TOKEN_BLOCK, TOKEN_BLOCK), 1)
    earlier = _dot(onehot.astype(BF16), jnp.where(tr < tc, 1.0, 0.0).astype(BF16)) + carry[...]
    rank1 = jnp.sum(jnp.where(sel1, earlier, 0.0), axis=0, keepdims=True)
    rank2 = jnp.sum(jnp.where(sel2, earlier, 0.0), axis=0, keepdims=True)
    den = m1 + m2
    fields = (i1, i2, rank1, rank2, m1 / den, m2 / den)
    route = jnp.zeros(probs.shape, F32)
    for n, val in enumerate(fields):
        route = jnp.where(idx == float(n), val, route)
    route_out[0] = route
    tile_cnt = jnp.sum(onehot, axis=1, keepdims=True)
    cnt_out[0] = tile_cnt
    carry[...] += tile_cnt


def _rwkv_out(o, bonus, g, h, gn_w, gn_b, w_o, gain, w_router):
    blocks = SEQ // TOKEN_BLOCK
    row = pl.BlockSpec((1, TOKEN_BLOCK, D_MODEL), lambda b, i: (b, i, 0))
    vec = pl.BlockSpec((1, D_MODEL), lambda b, i: (0, 0))
    return pl.pallas_call(
        _rwkv_out_kernel,
        grid=(BATCH, blocks),
        in_specs=[row, row, row, row, vec, vec,
                  pl.BlockSpec((D_MODEL, D_MODEL), lambda b, i: (0, 0)), vec,
                  pl.BlockSpec((N_EXPERTS, 3 * D_MODEL), lambda b, i: (0, 0))],
        out_specs=[row, pl.BlockSpec((1, 1, N_COL_CHUNKS, TOKEN_BLOCK, PIECE_WORDS), lambda b, i: (b, i, 0, 0, 0)),
                   pl.BlockSpec((1, N_EXPERTS, TOKEN_BLOCK), lambda b, i: (b, 0, i)),
                   pl.BlockSpec((1, N_EXPERTS, 1), lambda b, i: (b * blocks + i, 0, 0))],
        out_shape=[jax.ShapeDtypeStruct((BATCH, SEQ, D_MODEL), F32),
                   jax.ShapeDtypeStruct((BATCH, blocks, N_COL_CHUNKS, TOKEN_BLOCK, PIECE_WORDS), U32),
                   jax.ShapeDtypeStruct((BATCH, N_EXPERTS, SEQ), F32),
                   jax.ShapeDtypeStruct((BATCH * blocks, N_EXPERTS, 1), F32)],
        scratch_shapes=[pltpu.VMEM((N_EXPERTS, 1), F32)],
        compiler_params=_params("arbitrary", "arbitrary"),
    )(o, bonus, g, h, gn_w, gn_b, w_o, gain, w_router)


def _routing_tables(route, cnt):
    i32 = jnp.int32
    route = jnp.swapaxes(route, 0, 1).reshape(N_EXPERTS, N_TOKENS)
    expert = route[0:2].astype(i32)
    rank = route[2:4].astype(i32)
    gate = route[4:6].T
    counts = jnp.sum(cnt.reshape(-1, N_EXPERTS).astype(i32), axis=0)
    tiles_e = (counts + EXPERT_TILE - 1) // EXPERT_TILE
    tile_end = jnp.cumsum(tiles_e)
    n_used = tile_end[-1]
    start_row = (tile_end - tiles_e) * EXPERT_TILE
    group_start = sum(jnp.where(expert == e, start_row[e], 0) for e in range(N_EXPERTS))
    pos = group_start + rank
    tiles = jnp.arange(MAX_EXPERT_TILES, dtype=i32)
    tile_expert = jnp.minimum(jnp.sum(tiles[:, None] >= tile_end[None, :], axis=1), N_EXPERTS - 1).astype(i32)
    tile_rows = jnp.take(counts, tile_expert) - (tiles - jnp.take(tile_end - tiles_e, tile_expert)) * EXPERT_TILE
    tile_rows = jnp.where(tiles < n_used, jnp.clip(tile_rows, 0, EXPERT_TILE), 0)
    pos = pos.reshape(TOP_K * N_TOKEN_BLOCKS, 1, TOKEN_BLOCK)
    chunk = jnp.arange(N_COL_CHUNKS, dtype=i32)[None, :, None]
    sorted_piece = (((pos // EXPERT_TILE) * N_COL_CHUNKS + chunk) * EXPERT_TILE + pos % EXPERT_TILE).reshape(-1)
    return dict(sorted_piece=sorted_piece, gate=gate, tile_expert=tile_expert,
                n_used=n_used.reshape(1).astype(i32), tile_rows=tile_rows.astype(i32))


def _row_gather(x, indices):
    m = indices.shape[0]
    mesh = plsc.VectorSubcoreMesh(core_axis_name="c", subcore_axis_name="s")

    @pl.kernel(out_type=jax.ShapeDtypeStruct((m, PIECE_WORDS), x.dtype), mesh=mesh)
    def gather(x_hbm, i_hbm, o_hbm):
        def body(i_vmem, o_vmem):
            pltpu.sync_copy(x_hbm.at[i_vmem.at[0]], o_vmem)

        pltpu.emit_pipeline(
            body, grid=(m // GATHER_ROWS,),
            in_specs=[pl.BlockSpec((1, GATHER_ROWS), lambda i: (0, i))],
            out_specs=[pl.BlockSpec((GATHER_ROWS, PIECE_WORDS), lambda i: (i, 0))],
            core_axis_name=("c", "s"),
            dimension_semantics=(pltpu.PARALLEL,),
        )(i_hbm, o_hbm)

    return gather(x, indices.reshape(1, m))


def _row_scatter(x, indices, out_rows):
    m = indices.shape[0]
    x_blocks = x.shape[0] // GATHER_ROWS
    mesh = plsc.VectorSubcoreMesh(core_axis_name="c", subcore_axis_name="s")

    @pl.kernel(out_type=jax.ShapeDtypeStruct((out_rows, PIECE_WORDS), x.dtype), mesh=mesh)
    def scatter(x_hbm, i_hbm, o_hbm):
        def body(x_vmem, i_vmem):
            pltpu.sync_copy(x_vmem, o_hbm.at[i_vmem.at[0]])

        pltpu.emit_pipeline(
            body, grid=(m // GATHER_ROWS,),
            in_specs=[pl.BlockSpec((GATHER_ROWS, PIECE_WORDS), lambda i: (i % x_blocks, 0)),
                      pl.BlockSpec((1, GATHER_ROWS), lambda i: (0, i))],
            out_specs=[],
            core_axis_name=("c", "s"),
            dimension_semantics=(pltpu.PARALLEL,),
        )(x_hbm, i_hbm)

    return scatter(x, indices.reshape(1, m))


def _expert_kernel(te_ref, nu_ref, nr_ref, x_ref, wg_ref, wu_ref, wd_ref, y_ref, xb, acc):
    i = pl.program_id(0)
    f = pl.program_id(1)
    n_rows = nr_ref[i]
    n_sub = (n_rows + SUB_TILE - 1) // SUB_TILE
    subs = EXPERT_TILE // SUB_TILE

    @pl.when(n_sub > 0)
    def _():
        @pl.when(f == 0)
        def _():
            real = lax.broadcasted_iota(jnp.int32, (EXPERT_TILE, 1), 0) < n_rows
            for c in range(N_COL_CHUNKS):
                halves = _unpack_piece(x_ref[c * EXPERT_TILE:(c + 1) * EXPERT_TILE, :])
                for n, half in enumerate(halves):
                    lo = c * COL_CHUNK + n * PIECE_WORDS
                    xb[:, lo:lo + PIECE_WORDS] = jnp.where(real, half, 0.0).astype(BF16)
            acc[...] = jnp.zeros_like(acc)

        wg = wg_ref[0].astype(BF16)
        wu = wu_ref[0].astype(BF16)
        wd = wd_ref[0].astype(BF16)

        def block(rows):
            x = xb[rows, :]
            a = _dot(x, wg)
            b = _dot(x, wu)
            acc[rows, :] += _dot((a * _sigmoid(a) * b).astype(BF16), wd)

        @pl.when(n_sub == subs)
        def _():
            block(slice(0, EXPERT_TILE))

        @pl.when(n_sub < subs)
        def _():
            for k in range(subs - 1):
                pl.when(k < n_sub)(functools.partial(block, slice(k * SUB_TILE, (k + 1) * SUB_TILE)))

        @pl.when(f == pl.num_programs(1) - 1)
        def _():
            for c in range(N_COL_CHUNKS):
                y_ref[c * EXPERT_TILE:(c + 1) * EXPERT_TILE, :] = _pack_piece(acc[:, c * COL_CHUNK:(c + 1) * COL_CHUNK])


def _experts(tab, x_sorted, wg, wu, wd):
    n_ff = D_FF // FF_TILE
    tile = lambda i, nu: jnp.minimum(i, nu[0] - 1)
    ff = lambda i, f, nu: jnp.where(i < nu[0], f, n_ff - 1)
    grid_spec = pltpu.PrefetchScalarGridSpec(
        num_scalar_prefetch=3, grid=(MAX_EXPERT_TILES, n_ff),
        in_specs=[pl.BlockSpec((N_COL_CHUNKS * EXPERT_TILE, PIECE_WORDS), lambda i, f, te, nu, ns: (tile(i, nu), 0)),
                  pl.BlockSpec((1, D_MODEL, FF_TILE),
                               lambda i, f, te, nu, ns: (te[tile(i, nu)], 0, ff(i, f, nu))),
                  pl.BlockSpec((1, D_MODEL, FF_TILE),
                               lambda i, f, te, nu, ns: (te[tile(i, nu)], 0, ff(i, f, nu))),
                  pl.BlockSpec((1, FF_TILE, D_MODEL),
                               lambda i, f, te, nu, ns: (te[tile(i, nu)], ff(i, f, nu), 0))],
        out_specs=pl.BlockSpec((N_COL_CHUNKS * EXPERT_TILE, PIECE_WORDS), lambda i, f, te, nu, ns: (tile(i, nu), 0)),
        scratch_shapes=[pltpu.VMEM((EXPERT_TILE, D_MODEL), BF16), pltpu.VMEM((EXPERT_TILE, D_MODEL), F32)])
    return pl.pallas_call(
        _expert_kernel, grid_spec=grid_spec,
        out_shape=jax.ShapeDtypeStruct((N_COL_CHUNKS * MAX_SORTED_ROWS, PIECE_WORDS), U32),
        compiler_params=_params("arbitrary", "arbitrary"),
    )(tab["tile_expert"], tab["n_used"], tab["tile_rows"], x_sorted, wg, wu, wd)


def _combine_kernel(h_ref, y_ref, g_ref, o_ref):
    g = g_ref[...]
    for c in range(N_COL_CHUNKS):
        first, second = _unpack_piece(y_ref[0, 0, c]), _unpack_piece(y_ref[1, 0, c])
        for n in range(2):
            cols = slice(c * COL_CHUNK + n * PIECE_WORDS, c * COL_CHUNK + (n + 1) * PIECE_WORDS)
            o_ref[:, cols] = h_ref[:, cols] + g[:, 0:1] * first[n] + g[:, 1:2] * second[n]


def _combine(h, y_pair, gate):
    row = pl.BlockSpec((TOKEN_BLOCK, D_MODEL), lambda i: (i, 0))
    return pl.pallas_call(
        _combine_kernel,
        grid=(N_TOKEN_BLOCKS,),
        in_specs=[row, pl.BlockSpec((TOP_K, 1, N_COL_CHUNKS, TOKEN_BLOCK, PIECE_WORDS), lambda i: (0, i, 0, 0, 0)),
                  pl.BlockSpec((TOKEN_BLOCK, TOP_K), lambda i: (i, 0))],
        out_specs=row,
        out_shape=jax.ShapeDtypeStruct((N_TOKENS, D_MODEL), F32),
        compiler_params=_params("parallel"),
    )(h, y_pair, gate)


def kernel(x, meta_tokens, mixer_norm, ffn_norm, attn_w_qkv, attn_q_norm, attn_k_norm, attn_sinks, attn_w_o, rwkv_mix, rwkv_w0, rwkv_w1, rwkv_w2, rwkv_a0, rwkv_a1, rwkv_a2, rwkv_g1, rwkv_g2, rwkv_k_k, rwkv_k_a, rwkv_r_k, rwkv_w_r, rwkv_w_k, rwkv_w_v, rwkv_w_o, rwkv_gn_w, rwkv_gn_b, ffn_w_gate, ffn_w_up, ffn_w_down, moe_router, moe_w_gate, moe_w_up, moe_w_down):
    bf = lambda a: a.astype(BF16)
    vec = lambda a: a.reshape(1, -1).astype(F32)
    h_meta = jnp.concatenate([jnp.zeros((PAD, D_MODEL), F32), meta_tokens.astype(F32)], axis=0)

    w_qkv = bf(attn_w_qkv[0])
    qkv = _qkv(x.reshape(N_TOKENS, D_MODEL), vec(mixer_norm[0]), w_qkv, TOKEN_BLOCK)
    qkv_meta = _qkv(h_meta, vec(mixer_norm[0]), w_qkv, BLOCK)
    h = _attention(x, h_meta, qkv.reshape(BATCH, SEQ, QKV_DIM), qkv_meta, attn_sinks[0].astype(F32),
                   jnp.tile(vec(attn_q_norm[0]), (1, Q_PER_KV)), jnp.tile(vec(attn_k_norm[0]), (1, N_KV_HEADS)),
                   bf(attn_w_o[0]))
    h = _ffn(h.reshape(N_ROWS, D_MODEL), vec(ffn_norm[0]), bf(ffn_w_gate[0]), bf(ffn_w_up[0]),
             bf(ffn_w_down[0]))

    rkv, g = _rwkv_proj(
        h, vec(mixer_norm[1]), rwkv_mix[0], vec(rwkv_w0[0]), vec(rwkv_a0[0]), vec(rwkv_k_k[0]),
        vec(rwkv_k_a[0]), bf(rwkv_w_r[0]), bf(rwkv_w_k[0]), bf(rwkv_w_v[0]), bf(rwkv_w1[0]),
        bf(rwkv_w2[0]), bf(rwkv_a1[0]), bf(rwkv_a2[0]), bf(rwkv_g1[0]), bf(rwkv_g2[0]))
    b3 = lambda t: t.reshape(BATCH, TP, -1)
    mm, add, bonus, p_last = _scan_prep(b3(rkv), vec(rwkv_r_k[0]))
    o = _scan(mm, add, p_last)
    w_router = moe_router[0].astype(F32).T
    wr_hi = bf(w_router)
    wr_lo = bf(w_router - wr_hi.astype(F32))
    h, u, route, cnt = _rwkv_out(o, bonus, b3(g), b3(h), vec(rwkv_gn_w[0]), vec(rwkv_gn_b[0]),
                                 bf(rwkv_w_o[0]), vec(ffn_norm[1]),
                                 jnp.concatenate([wr_hi, wr_hi, wr_lo], axis=1))
    tab = _routing_tables(route, cnt)
    x_sorted = _row_scatter(u.reshape(-1, PIECE_WORDS), tab["sorted_piece"], N_COL_CHUNKS * MAX_SORTED_ROWS)
    y_sorted = _experts(tab, x_sorted, moe_w_gate[0], moe_w_up[0], moe_w_down[0])
    y_pair = _row_gather(y_sorted, tab["sorted_piece"]).reshape(TOP_K, N_TOKEN_BLOCKS, N_COL_CHUNKS, TOKEN_BLOCK, PIECE_WORDS)
    out = _combine(h.reshape(N_TOKENS, D_MODEL), y_pair, tab["gate"])
    return out.reshape(BATCH, SEQ, D_MODEL)
```

```python
import functools

import jax
import jax.numpy as jnp
from jax import lax
from jax.experimental import pallas as pl
from jax.experimental.pallas import tpu as pltpu
from jax.experimental.pallas import tpu_sc as plsc

F32 = jnp.float32
BF16 = jnp.bfloat16

D_MODEL = 1024
BATCH = 2
SEQ = 8192
N_META = 16
BLOCK = 128
PAD = BLOCK - N_META
TP = SEQ + BLOCK
N_ROWS = BATCH * TP
N_BLOCKS = TP // BLOCK
META_BLOCK = N_BLOCKS - 1
HEAD_DIM = 64
N_Q_HEADS = 16
N_KV_HEADS = 4
Q_PER_KV = 4
QKV_DIM = (N_Q_HEADS + 2 * N_KV_HEADS) * HEAD_DIM
RWKV_HEADS = 16
RWKV_HEAD = 64
D_FF = 3584
N_EXPERTS = 8
NORM_EPS = 1e-5
GN_EPS = 64e-5
CHUNK = 64
N_CHUNKS = TP // CHUNK
META_CHUNK0 = SEQ // CHUNK
SCAN_CHUNKS = 2
HEADS_PER_GROUP = 4
GW = HEADS_PER_GROUP * RWKV_HEAD
N_GROUPS = RWKV_HEADS // HEADS_PER_GROUP
N_SCAN_IN = 6

FFN_ROW_TILE = 1664
PROJ_TILE = 320
FF_TILE = 512
N_TOKENS = BATCH * SEQ
TOP_K = 2
TOKEN_BLOCK = 512
N_TOKEN_BLOCKS = N_TOKENS // TOKEN_BLOCK
EXPERT_TILE = 1024
SUB_TILE = 256
MAX_EXPERT_TILES = TOP_K * N_TOKENS // EXPERT_TILE + N_EXPERTS
TILE_CAP = EXPERT_TILE + SUB_TILE
MAX_SORTED_ROWS = MAX_EXPERT_TILES * TILE_CAP
COL_CHUNK = 256
N_COL_CHUNKS = D_MODEL // COL_CHUNK
PIECE_WORDS = COL_CHUNK // 2
U32 = jnp.uint32
GATHER_ROWS = 128
VMEM_LIMIT = 56 * 1024 * 1024
NEG = -1e30

NT_DIMS = (((1,), (1,)), ((), ()))
TN_DIMS = (((0,), (0,)), ((), ()))


def _params(*sem):
    return pltpu.CompilerParams(dimension_semantics=sem, vmem_limit_bytes=VMEM_LIMIT)


def _rms(x, gain):
    return x * lax.rsqrt(jnp.mean(x * x, axis=-1, keepdims=True) + NORM_EPS) * gain


def _sigmoid(x):
    return 1.0 / (1.0 + jnp.exp(-x))


def _dot(a, b):
    return jnp.dot(a, b, preferred_element_type=F32)


def _pack_piece(x):
    return pltpu.pack_elementwise([x[:, :PIECE_WORDS], x[:, PIECE_WORDS:]], packed_dtype=BF16)


def _unpack_piece(w):
    return tuple(pltpu.unpack_elementwise(w, index=i, packed_dtype=BF16, unpacked_dtype=F32) for i in (0, 1))


def _qkv_kernel(h_ref, g_ref, w_ref, o_ref):
    u = _rms(h_ref[...], g_ref[...])
    o_ref[...] = _dot(u.astype(BF16), w_ref[...]).astype(BF16)


def _qkv(h, gain, w, tile):
    rows = h.shape[0]
    return pl.pallas_call(
        _qkv_kernel,
        grid=(rows // tile,),
        in_specs=[pl.BlockSpec((tile, D_MODEL), lambda i: (i, 0)),
                  pl.BlockSpec((1, D_MODEL), lambda i: (0, 0)),
                  pl.BlockSpec((D_MODEL, QKV_DIM), lambda i: (0, 0))],
        out_specs=pl.BlockSpec((tile, QKV_DIM), lambda i: (i, 0)),
        out_shape=jax.ShapeDtypeStruct((rows, QKV_DIM), BF16),
        compiler_params=_params("parallel"),
    )(h, gain, w)


def _attn_kernel(sink_ref, x_ref, hm_ref, q_ref, kc_ref, vc_ref, kp_ref, vp_ref, qm_ref, km_ref, vm_ref,
                 qg_ref, kg_ref, wo_ref, o_ref, o_scr):
    j = pl.program_id(1)
    n_band = 2 * BLOCK
    slots = 2 * N_META
    far = 4 * BLOCK
    is_real = j < META_BLOCK
    prev_off = jnp.where(jnp.logical_and(j >= 1, is_real), 0, far)
    cur_off = jnp.where(is_real, 0, far)
    rowi = lax.broadcasted_iota(jnp.int32, (BLOCK, n_band), 0)
    col = lax.broadcasted_iota(jnp.int32, (BLOCK, n_band), 1)
    band_ok = jnp.logical_or(jnp.logical_and(col < BLOCK, col > rowi + prev_off),
                             jnp.logical_and(col >= BLOCK, col - BLOCK + cur_off <= rowi))
    band_ok4 = jnp.concatenate([band_ok] * Q_PER_KV, axis=1)
    meta_off = jnp.where(is_real, -far, PAD)
    mrow = lax.broadcasted_iota(jnp.int32, (BLOCK, Q_PER_KV * slots), 0)
    mlane = lax.broadcasted_iota(jnp.int32, (BLOCK, Q_PER_KV * slots), 1)
    mslot = mlane % slots
    meta_ok4 = jnp.logical_and(mslot < N_META, mslot + meta_off <= mrow)
    meta_lane_head = mlane // slots

    groups = range(N_KV_HEADS)
    mask_bf = _group_masks().astype(BF16)
    rb = lax.broadcasted_iota(jnp.int32, (Q_PER_KV * n_band, GW), 0) // n_band
    lb = lax.broadcasted_iota(jnp.int32, (Q_PER_KV * n_band, GW), 1) // HEAD_DIM
    band_mask = jnp.where(rb == lb, 1.0, 0.0).astype(BF16)
    rm = lax.broadcasted_iota(jnp.int32, (Q_PER_KV * slots, GW), 0) // slots
    lm = lax.broadcasted_iota(jnp.int32, (Q_PER_KV * slots, GW), 1) // HEAD_DIM
    meta_mask = jnp.where(rm == lm, 1.0, 0.0).astype(BF16)
    sr = lax.broadcasted_iota(jnp.int32, (GW, GW), 0)
    sc = lax.broadcasted_iota(jnp.int32, (GW, GW), 1)
    lane_head = lax.broadcasted_iota(jnp.int32, (1, GW), 1) // HEAD_DIM
    inv_d = 1.0 / HEAD_DIM
    scale = HEAD_DIM ** -0.5

    real_rows = lax.broadcasted_iota(jnp.int32, (BLOCK, 1), 0) < jnp.where(is_real, BLOCK, 0)
    km, vm = km_ref[...], vm_ref[...]
    kcur = jnp.where(real_rows, kc_ref[0], km)
    vcur = jnp.where(real_rows, vc_ref[0], vm)
    kall = jnp.concatenate([km[PAD:], kp_ref[0], kcur], axis=0).astype(F32)
    vall = jnp.concatenate([vm[PAD:], vp_ref[0], vcur], axis=0)
    kss = _dot((kall * kall).astype(BF16), mask_bf)
    kn = (kall * lax.rsqrt(kss * inv_d + NORM_EPS) * kg_ref[...]).astype(BF16)
    q_all = jnp.where(real_rows, q_ref[0], qm_ref[...]).astype(F32)
    qn = []
    for g in groups:
        qg = q_all[:, g * GW:(g + 1) * GW]
        qss = _dot((qg * qg).astype(BF16), mask_bf)
        qn.append((qg * lax.rsqrt(qss * inv_d + NORM_EPS) * (qg_ref[...] * scale)).astype(BF16))
    sel = [jnp.where(sr == g * HEAD_DIM + sc % HEAD_DIM, 1.0, 0.0).astype(BF16) for g in groups]
    krep = [_dot(kn, sel[g]).astype(BF16) for g in groups]
    vrep = [_dot(vall, sel[g]).astype(BF16) for g in groups]
    unused = jnp.zeros((slots - N_META, GW), BF16)

    def band_bd(rep):
        return jnp.concatenate([rep[N_META:]] * Q_PER_KV, axis=0) * band_mask

    def meta_bd(rep):
        return jnp.concatenate([rep[:N_META], unused] * Q_PER_KV, axis=0) * meta_mask

    s_band = [jnp.where(band_ok4, lax.dot_general(qn[g], band_bd(krep[g]), NT_DIMS,
                                                  preferred_element_type=F32), NEG) for g in groups]
    s_meta = [jnp.where(meta_ok4, lax.dot_general(qn[g], meta_bd(krep[g]), NT_DIMS,
                                                  preferred_element_type=F32), NEG) for g in groups]
    rhs_band = [jnp.concatenate([band_bd(vrep[g]), band_mask], axis=1) for g in groups]
    rhs_meta = [jnp.concatenate([meta_bd(vrep[g]), meta_mask], axis=1) for g in groups]
    ov, sink_den = [], []
    for g in groups:
        parts = []
        sd = jnp.zeros((BLOCK, GW), F32)
        m_lanes = jnp.zeros((BLOCK, Q_PER_KV * slots), F32)
        for hh in range(Q_PER_KV):
            seg = s_band[g][:, hh * n_band:(hh + 1) * n_band]
            own = meta_lane_head == hh
            sink = sink_ref[g * Q_PER_KV + hh]
            m = jnp.maximum(jnp.max(seg, axis=-1, keepdims=True),
                            jnp.max(jnp.where(own, s_meta[g], NEG), axis=-1, keepdims=True))
            m = jnp.maximum(m, sink)
            parts.append(jnp.exp(seg - m).astype(BF16))
            m_lanes = jnp.where(own, m, m_lanes)
            sd = sd + jnp.exp(sink - m) * jnp.where(lane_head == hh, 1.0, 0.0)
        p_band = jnp.concatenate(parts, axis=1)
        p_meta = jnp.exp(s_meta[g] - m_lanes).astype(BF16)
        ov.append(_dot(p_band, rhs_band[g]) + _dot(p_meta, rhs_meta[g]))
        sink_den.append(sd)
    for g in groups:
        o_scr[:, g * GW:(g + 1) * GW] = ov[g][:, :GW] / (ov[g][:, GW:] + sink_den[g])
    h = jnp.where(real_rows, x_ref[0], hm_ref[...])
    o_ref[0] = h + _dot(o_scr[...].astype(BF16), wo_ref[...])


def _attention(x, h_meta, qkv, qkv_meta, sinks, q_gain, k_gain, w_o):
    kcol, vcol = N_Q_HEADS * HEAD_DIM // 256, N_Q_HEADS * HEAD_DIM // 256 + 1
    kvw = N_KV_HEADS * HEAD_DIM
    real = lambda j: jnp.minimum(j, META_BLOCK - 1)
    prev = lambda j: jnp.clip(j - 1, 0, META_BLOCK - 1)
    return pl.pallas_call(
        _attn_kernel,
        grid=(BATCH, N_BLOCKS),
        in_specs=[pl.BlockSpec(memory_space=pltpu.SMEM),
                  pl.BlockSpec((1, BLOCK, D_MODEL), lambda b, j: (b, real(j), 0)),
                  pl.BlockSpec((BLOCK, D_MODEL), lambda b, j: (0, 0)),
                  pl.BlockSpec((1, BLOCK, D_MODEL), lambda b, j: (b, real(j), 0)),
                  pl.BlockSpec((1, BLOCK, kvw), lambda b, j: (b, real(j), kcol)),
                  pl.BlockSpec((1, BLOCK, kvw), lambda b, j: (b, real(j), vcol)),
                  pl.BlockSpec((1, BLOCK, kvw), lambda b, j: (b, prev(j), kcol)),
                  pl.BlockSpec((1, BLOCK, kvw), lambda b, j: (b, prev(j), vcol)),
                  pl.BlockSpec((BLOCK, D_MODEL), lambda b, j: (0, 0)),
                  pl.BlockSpec((BLOCK, kvw), lambda b, j: (0, kcol)),
                  pl.BlockSpec((BLOCK, kvw), lambda b, j: (0, vcol)),
                  pl.BlockSpec((1, GW), lambda b, j: (0, 0)),
                  pl.BlockSpec((1, GW), lambda b, j: (0, 0)),
                  pl.BlockSpec((D_MODEL, D_MODEL), lambda b, j: (0, 0))],
        out_specs=pl.BlockSpec((1, BLOCK, D_MODEL), lambda b, j: (b, j, 0)),
        out_shape=jax.ShapeDtypeStruct((BATCH, TP, D_MODEL), F32),
        scratch_shapes=[pltpu.VMEM((BLOCK, D_MODEL), F32)],
        compiler_params=_params("parallel", "parallel"),
    )(sinks, x, h_meta, qkv, qkv, qkv, qkv, qkv, qkv_meta, qkv_meta, qkv_meta, q_gain, k_gain, w_o)


def _ffn_kernel(h_ref, g_ref, wg_ref, wu_ref, wd_ref, o_ref, u_scr, acc):
    f = pl.program_id(1)

    @pl.when(f == 0)
    def _():
        u_scr[...] = _rms(h_ref[...], g_ref[...]).astype(BF16)
        acc[...] = jnp.zeros_like(acc)

    u = u_scr[...]
    a = _dot(u, wg_ref[...])
    b = _dot(u, wu_ref[...])
    acc[...] += _dot((a * _sigmoid(a) * b).astype(BF16), wd_ref[...])

    @pl.when(f == pl.num_programs(1) - 1)
    def _():
        o_ref[...] = h_ref[...] + acc[...]


def _ffn(h, gain, wg, wu, wd):
    return pl.pallas_call(
        _ffn_kernel,
        grid=(N_ROWS // FFN_ROW_TILE, D_FF // FF_TILE),
        in_specs=[pl.BlockSpec((FFN_ROW_TILE, D_MODEL), lambda i, f: (i, 0)),
                  pl.BlockSpec((1, D_MODEL), lambda i, f: (0, 0)),
                  pl.BlockSpec((D_MODEL, FF_TILE), lambda i, f: (0, f)),
                  pl.BlockSpec((D_MODEL, FF_TILE), lambda i, f: (0, f)),
                  pl.BlockSpec((FF_TILE, D_MODEL), lambda i, f: (f, 0))],
        out_specs=pl.BlockSpec((FFN_ROW_TILE, D_MODEL), lambda i, f: (i, 0)),
        out_shape=jax.ShapeDtypeStruct((N_ROWS, D_MODEL), F32),
        scratch_shapes=[pltpu.VMEM((FFN_ROW_TILE, D_MODEL), BF16), pltpu.VMEM((FFN_ROW_TILE, D_MODEL), F32)],
        compiler_params=_params("parallel", "arbitrary"),
    )(h, gain, wg, wu, wd)


def _rwkv_proj_kernel(h_ref, hp_ref, g_ref, mix_ref, w0_ref, a0_ref, kk_ref, ka_ref,
                      wr_ref, wk_ref, wv_ref, w1_ref, w2_ref, a1_ref, a2_ref, g1_ref, g2_ref,
                      rkv_out, g_out):
    i = pl.program_id(0)
    tiles_per_batch = TP // PROJ_TILE
    r0 = (i % tiles_per_batch) * PROJ_TILE
    local = lax.broadcasted_iota(jnp.int32, (PROJ_TILE, 1), 0)
    lrow = local + r0
    gain = g_ref[...]
    is_pad = jnp.logical_and(lrow >= SEQ, lrow < SEQ + PAD)
    u = jnp.where(is_pad, 0.0, _rms(h_ref[...], gain))
    u_prev_tile = _rms(hp_ref[7:8, :], gain)
    xprev = pltpu.roll(u, 1, 0)
    xprev = jnp.where(local == 0, u_prev_tile, xprev)
    xprev = jnp.where(lrow == SEQ, 0.0, xprev)
    xx = xprev - u
    mix = mix_ref[...]
    lerp = lambda n: (u + xx * mix[n:n + 1, :]).astype(BF16)
    xr, xw, xk, xv, xa, xg = [lerp(n) for n in range(6)]
    r = _dot(xr, wr_ref[...])
    k = _dot(xk, wk_ref[...])
    v = _dot(xv, wv_ref[...])
    lw = _dot(jnp.tanh(_dot(xw, w1_ref[...])).astype(BF16), w2_ref[...])
    z = -(w0_ref[...] + lw)
    softplus = jnp.maximum(z, 0.0) + jnp.log(1.0 + jnp.exp(-jnp.abs(z)))
    w = -softplus - 0.5
    a = _sigmoid(a0_ref[...] + _dot(_dot(xa, a1_ref[...]).astype(BF16), a2_ref[...]))
    g = _dot(_sigmoid(_dot(xg, g1_ref[...])).astype(BF16), g2_ref[...])
    fields = (r, -jnp.exp(w),
              k * (1.0 + (a - 1.0) * ka_ref[...]), v, k * kk_ref[...], a)
    for n, val in enumerate(fields):
        rkv_out[:, n * D_MODEL:(n + 1) * D_MODEL] = val
    g_out[...] = g


def _rwkv_proj(h, gain, mix, w0, a0, k_k, k_a, w_r, w_k, w_v, w1, w2, a1, a2, g1, g2):
    tiles_per_batch = TP // PROJ_TILE
    rows8 = PROJ_TILE // 8

    def prev_map(i):
        b = i // tiles_per_batch
        first = (i % tiles_per_batch) == 0
        return (jnp.where(first, (b * TP + TP - 8) // 8, i * rows8 - 1), 0)

    row = pl.BlockSpec((PROJ_TILE, D_MODEL), lambda i: (i, 0))
    full = lambda a: pl.BlockSpec(a.shape, lambda i: (0,) * a.ndim)
    smalls = (gain, mix, w0, a0, k_k, k_a, w_r, w_k, w_v, w1, w2, a1, a2, g1, g2)
    return pl.pallas_call(
        _rwkv_proj_kernel,
        grid=(N_ROWS // PROJ_TILE,),
        in_specs=[row, pl.BlockSpec((8, D_MODEL), prev_map)] + [full(a) for a in smalls],
        out_specs=[pl.BlockSpec((PROJ_TILE, N_SCAN_IN * D_MODEL), lambda i: (i, 0)), row],
        out_shape=[jax.ShapeDtypeStruct((N_ROWS, N_SCAN_IN * D_MODEL), F32),
                   jax.ShapeDtypeStruct((N_ROWS, D_MODEL), F32)],
        compiler_params=_params("parallel"),
    )(h, h, *smalls)


def _group_masks():
    ri = lax.broadcasted_iota(jnp.int32, (GW, GW), 0) // RWKV_HEAD
    ci = lax.broadcasted_iota(jnp.int32, (GW, GW), 1) // RWKV_HEAD
    return jnp.where(ri == ci, 1.0, 0.0).astype(F32)


def _bd(x, mask):
    return jnp.concatenate([x.astype(BF16)] * HEADS_PER_GROUP, axis=0) * mask.astype(BF16)


def _diag_blocks(full, mask):
    m = full * mask
    n = RWKV_HEAD
    return (m[0:n] + m[n:2 * n]) + (m[2 * n:3 * n] + m[3 * n:4 * n])


def _head_sum(x, mask_bf):
    hi = x.astype(BF16)
    lo = (x - hi.astype(F32)).astype(BF16)
    return _dot(hi, mask_bf) + _dot(lo, mask_bf)


def _scan_prep_kernel(x_ref, rk_ref, mm_out, add_out, bonus_out, pl_out):
    L, D = CHUNK, D_MODEL
    units = [(b, g) for b in range(BATCH) for g in range(N_GROUPS)]
    un = range(len(units))
    mask = _group_masks()
    mask_bf = mask.astype(BF16)
    ri = lax.broadcasted_iota(jnp.int32, (L, GW), 0)
    ci = lax.broadcasted_iota(jnp.int32, (L, GW), 1) % RWKV_HEAD
    incl = ci <= ri
    strict = ci < ri
    eye = jnp.where(ci == ri, 1.0, 0.0).astype(F32)
    t_r = lax.broadcasted_iota(jnp.int32, (L, L), 0)
    t_c = lax.broadcasted_iota(jnp.int32, (L, L), 1)
    tril = jnp.where(t_c <= t_r, 1.0, 0.0).astype(BF16)
    rk_all = rk_ref[...]

    def field(b, n, g):
        return x_ref[b, :, n * D + g * GW:n * D + (g + 1) * GW]

    def put(ref, b, n, g, val):
        ref[b, :, n * D + g * GW:n * D + (g + 1) * GW] = val.astype(ref.dtype)

    at, rt, bt, kt, v, plast = [], [], [], [], [], []
    for b in range(BATCH):
        ld = x_ref[b, :, D:2 * D]
        hi = ld.astype(BF16)
        rest = ld - hi.astype(F32)
        mid = rest.astype(BF16)
        lo = (rest - mid.astype(F32)).astype(BF16)
        cs = _dot(tril, hi) + _dot(tril, mid) + _dot(tril, lo)
        p_all = jnp.exp(cs)
        pprev_all = jnp.exp(cs - ld)
        pinv_all = jnp.exp(-cs)
        pl_all = p_all[L - 1:L, :]
        pl_out[b, 0] = pl_all
        for g in range(N_GROUPS):
            sl = slice(g * GW, (g + 1) * GW)
            r, k, vv, kk, a = field(b, 0, g), field(b, 2, g), field(b, 3, g), field(b, 4, g), field(b, 5, g)
            nrm = jnp.sqrt(_dot((kk * kk).astype(BF16), mask_bf))
            kk = kk / jnp.maximum(nrm, 1e-12)
            at.append(-kk * pprev_all[:, sl])
            bt.append(kk * a * pinv_all[:, sl])
            rt.append(r * p_all[:, sl])
            kt.append(k * pinv_all[:, sl])
            v.append(vv)
            plast.append(pl_all[:, sl])
            put(bonus_out, b, 0, g, _dot((r * k * rk_all[:, sl]).astype(BF16), mask_bf) * vv)

    a_ab, a_ak, a_rb, a_rk = [], [], [], []
    for n in un:
        lhs = jnp.concatenate([at[n], rt[n]], axis=0).astype(BF16)
        rhs = jnp.concatenate([_bd(bt[n], mask), _bd(kt[n], mask)], axis=0)
        big = lax.dot_general(lhs, rhs, NT_DIMS, preferred_element_type=F32)
        a_ab.append(jnp.where(strict, big[:L, :GW], 0.0))
        a_ak.append(jnp.where(strict, big[:L, GW:], 0.0))
        a_rb.append(jnp.where(incl, big[L:, :GW], 0.0))
        a_rk.append(jnp.where(incl, big[L:, GW:], 0.0))

    x = [_dot(a_ab[n].astype(BF16), _bd(a_ab[n], mask)) for n in un]
    inv = [eye + a_ab[n] for n in un]
    for step in range(5):
        for n in un:
            rhs = _bd(x[n], mask)
            if step < 4:
                res = _dot(jnp.concatenate([x[n], inv[n]], axis=0).astype(BF16), rhs)
                x[n] = res[:L]
                inv[n] = inv[n] + res[L:]
            else:
                inv[n] = inv[n] + _dot(inv[n].astype(BF16), rhs)

    av = [_dot(jnp.concatenate([a_ak[n], a_rk[n]], axis=0).astype(BF16), _bd(v[n], mask)) for n in un]
    wu = [_dot(inv[n].astype(BF16), jnp.concatenate([_bd(at[n], mask), _bd(av[n][:L], mask)], axis=1))
          for n in un]
    aw = [_dot(a_rb[n].astype(BF16),
               jnp.concatenate([_bd(wu[n][:, :GW], mask), _bd(wu[n][:, GW:], mask)], axis=1))
          for n in un]
    for n, (b, g) in enumerate(units):
        put(mm_out, b, 0, g, rt[n] + aw[n][:, :GW])
        put(add_out, b, 0, g, av[n][L:] + aw[n][:, GW:])
        bh = (bt[n] * plast[n]).astype(BF16)
        kh = (kt[n] * plast[n]).astype(BF16)
        w_b, u0_b = wu[n][:, :GW].astype(BF16), wu[n][:, GW:].astype(BF16)
        gfull = lax.dot_general(bh, w_b, TN_DIMS, preferred_element_type=F32)
        put(mm_out, b, 1, g, _diag_blocks(gfull, mask))
        hfull = lax.dot_general(jnp.concatenate([u0_b, v[n].astype(BF16)], axis=0),
                                jnp.concatenate([bh, kh], axis=0), TN_DIMS, preferred_element_type=F32)
        put(add_out, b, 1, g, _diag_blocks(hfull, mask))


def _scan_prep(rkv, r_k):
    return pl.pallas_call(
        _scan_prep_kernel,
        grid=(N_CHUNKS,),
        in_specs=[pl.BlockSpec((BATCH, CHUNK, N_SCAN_IN * D_MODEL), lambda c: (0, c, 0)),
                  pl.BlockSpec((1, D_MODEL), lambda c: (0, 0))],
        out_specs=[pl.BlockSpec((BATCH, CHUNK, 2 * D_MODEL), lambda c: (0, c, 0)),
                   pl.BlockSpec((BATCH, CHUNK, 2 * D_MODEL), lambda c: (0, c, 0)),
                   pl.BlockSpec((BATCH, CHUNK, D_MODEL), lambda c: (0, c, 0)),
                   pl.BlockSpec((BATCH, 1, 1, D_MODEL), lambda c: (0, c, 0, 0))],
        out_shape=[jax.ShapeDtypeStruct((BATCH, TP, 2 * D_MODEL), BF16),
                   jax.ShapeDtypeStruct((BATCH, TP, 2 * D_MODEL), F32),
                   jax.ShapeDtypeStruct((BATCH, TP, D_MODEL), F32),
                   jax.ShapeDtypeStruct((BATCH, N_CHUNKS, 1, D_MODEL), F32)],
        compiler_params=_params("parallel"),
    )(rkv, r_k)


def _scan_kernel(mm_ref, add_ref, pl_ref, y_ref, s_scr):
    c = pl.program_id(0)
    D = D_MODEL

    @pl.when(c == 0)
    def _():
        s_scr[...] = jnp.zeros_like(s_scr)

    mask = _group_masks()
    units = [(b, slice(g * GW, (g + 1) * GW)) for b in range(BATCH) for g in range(N_GROUPS)]
    s = [s_scr[b, :, sl] for b, sl in units]
    for k in range(SCAN_CHUNKS):
        rows = slice(k * CHUNK, (k + 1) * CHUNK)
        field = lambda ref, b, n, sl, rows=rows: ref[b, rows, n * D + sl.start:n * D + sl.stop]
        o = [lax.dot_general(field(mm_ref, b, 0, sl), _bd(s[n], mask), NT_DIMS,
                             preferred_element_type=F32) + field(add_ref, b, 0, sl)
             for n, (b, sl) in enumerate(units)]
        sg = [lax.dot_general(s[n].astype(BF16), _bd(field(mm_ref, b, 1, sl), mask), NT_DIMS,
                              preferred_element_type=F32)
              for n, (b, sl) in enumerate(units)]
        s = [s[n] * pl_ref[b, k, :, sl] + sg[n] + field(add_ref, b, 1, sl)
             for n, (b, sl) in enumerate(units)]
        for n, (b, sl) in enumerate(units):
            y_ref[b, rows, sl] = o[n]
    for n, (b, sl) in enumerate(units):
        s_scr[b, :, sl] = s[n]


def _scan(mm, add, p_last):
    steps = N_CHUNKS // SCAN_CHUNKS
    first = META_CHUNK0 // SCAN_CHUNKS
    phys = lambda c: (c + first) % steps
    pair = pl.BlockSpec((BATCH, SCAN_CHUNKS * CHUNK, 2 * D_MODEL), lambda c: (0, phys(c), 0))
    return pl.pallas_call(
        _scan_kernel,
        grid=(steps,),
        in_specs=[pair, pair, pl.BlockSpec((BATCH, SCAN_CHUNKS, 1, D_MODEL), lambda c: (0, phys(c), 0, 0))],
        out_specs=pl.BlockSpec((BATCH, SCAN_CHUNKS * CHUNK, D_MODEL), lambda c: (0, phys(c), 0)),
        out_shape=jax.ShapeDtypeStruct((BATCH, TP, D_MODEL), F32),
        scratch_shapes=[pltpu.VMEM((BATCH, RWKV_HEAD, D_MODEL), F32)],
        compiler_params=_params("arbitrary"),
    )(mm, add, p_last)


def _rwkv_out_kernel(o_ref, bonus_ref, g_ref, h_ref, gw_ref, gb_ref, wo_ref, gain_ref, wr_ref,
                     h_out, u_out, route_out, cnt_out, carry):
    @pl.when(jnp.logical_and(pl.program_id(0) == 0, pl.program_id(1) == 0))
    def _():
        carry[...] = jnp.zeros_like(carry)

    mask_bf = _group_masks().astype(BF16)
    inv_n = 1.0 / RWKV_HEAD
    o, bonus, gate = o_ref[0], bonus_ref[0], g_ref[0]
    parts = []
    for g in range(N_GROUPS):
        sl = slice(g * GW, (g + 1) * GW)
        og = o[:, sl]
        d = og - _head_sum(og, mask_bf) * inv_n
        var = _dot((d * d).astype(BF16), mask_bf) * inv_n
        yn = d * lax.rsqrt(var + GN_EPS) * gw_ref[:, sl] + gb_ref[:, sl] + bonus[:, sl]
        parts.append((yn * gate[:, sl]).astype(BF16))
    h = h_ref[0] + _dot(jnp.concatenate(parts, axis=1), wo_ref[...])
    h_out[0] = h
    u = _rms(h, gain_ref[...])
    for c in range(N_COL_CHUNKS):
        u_out[0, 0, c] = _pack_piece(u[:, c * COL_CHUNK:(c + 1) * COL_CHUNK])
    u_hi = u.astype(BF16)
    u_lo = (u - u_hi.astype(F32)).astype(BF16)
    logits = lax.dot_general(wr_ref[...], jnp.concatenate([u_hi, u_lo, u_hi], axis=1), NT_DIMS,
                             preferred_element_type=F32)
    e = jnp.exp(logits - jnp.max(logits, axis=0, keepdims=True))
    probs = e / jnp.sum(e, axis=0, keepdims=True)
    idx = lax.broadcasted_iota(jnp.int32, probs.shape, 0).astype(F32)
    m1 = jnp.max(probs, axis=0, keepdims=True)
    i1 = jnp.min(jnp.where(probs == m1, idx, float(N_EXPERTS)), axis=0, keepdims=True)
    sel1 = idx == i1
    rest = jnp.where(sel1, -1.0, probs)
    m2 = jnp.max(rest, axis=0, keepdims=True)
    i2 = jnp.min(jnp.where(rest == m2, idx, float(N_EXPERTS)), axis=0, keepdims=True)
    sel2 = idx == i2
    onehot = jnp.where(jnp.logical_or(sel1, sel2), 1.0, 0.0).astype(F32)
    tr = lax.broadcasted_iota(jnp.int32, (TOKEN_BLOCK, TOKEN_BLOCK), 0)
    tc = lax.broadcasted_iota(jnp.int32, (TOKEN_BLOCK, TOKEN_BLOCK), 1)
    earlier = _dot(onehot.astype(BF16), jnp.where(tr < tc, 1.0, 0.0).astype(BF16)) + carry[...]
    rank1 = jnp.sum(jnp.where(sel1, earlier, 0.0), axis=0, keepdims=True)
    rank2 = jnp.sum(jnp.where(sel2, earlier, 0.0), axis=0, keepdims=True)
    den = m1 + m2
    fields = (i1, i2, rank1, rank2, m1 / den, m2 / den)
    route = jnp.zeros(probs.shape, F32)
    for n, val in enumerate(fields):
        route = jnp.where(idx == float(n), val, route)
    route_out[0] = route
    tile_cnt = jnp.sum(onehot, axis=1, keepdims=True)
    cnt_out[0] = tile_cnt
    carry[...] += tile_cnt


def _rwkv_out(o, bonus, g, h, gn_w, gn_b, w_o, gain, w_router):
    blocks = SEQ // TOKEN_BLOCK
    row = pl.BlockSpec((1, TOKEN_BLOCK, D_MODEL), lambda b, i: (b, i, 0))
    vec = pl.BlockSpec((1, D_MODEL), lambda b, i: (0, 0))
    return pl.pallas_call(
        _rwkv_out_kernel,
        grid=(BATCH, blocks),
        in_specs=[row, row, row, row, vec, vec,
                  pl.BlockSpec((D_MODEL, D_MODEL), lambda b, i: (0, 0)), vec,
                  pl.BlockSpec((N_EXPERTS, 3 * D_MODEL), lambda b, i: (0, 0))],
        out_specs=[row, pl.BlockSpec((1, 1, N_COL_CHUNKS, TOKEN_BLOCK, PIECE_WORDS), lambda b, i: (b, i, 0, 0, 0)),
                   pl.BlockSpec((1, N_EXPERTS, TOKEN_BLOCK), lambda b, i: (b, 0, i)),
                   pl.BlockSpec((1, N_EXPERTS, 1), lambda b, i: (b * blocks + i, 0, 0))],
        out_shape=[jax.ShapeDtypeStruct((BATCH, SEQ, D_MODEL), F32),
                   jax.ShapeDtypeStruct((BATCH, blocks, N_COL_CHUNKS, TOKEN_BLOCK, PIECE_WORDS), U32),
                   jax.ShapeDtypeStruct((BATCH, N_EXPERTS, SEQ), F32),
                   jax.ShapeDtypeStruct((BATCH * blocks, N_EXPERTS, 1), F32)],
        scratch_shapes=[pltpu.VMEM((N_EXPERTS, 1), F32)],
        compiler_params=_params("arbitrary", "arbitrary"),
    )(o, bonus, g, h, gn_w, gn_b, w_o, gain, w_router)


def _routing_tables(route, cnt):
    i32 = jnp.int32
    route = jnp.swapaxes(route, 0, 1).reshape(N_EXPERTS, N_TOKENS)
    expert = route[0:2].astype(i32)
    rank = route[2:4].astype(i32)
    gate = route[4:6].T
    counts = jnp.sum(cnt.reshape(-1, N_EXPERTS).astype(i32), axis=0)
    tiles_e = jnp.where(counts > 0, jnp.maximum((counts - SUB_TILE + EXPERT_TILE - 1) // EXPERT_TILE, 1), 0)
    tile_end = jnp.cumsum(tiles_e)
    n_used = tile_end[-1]
    first_tile = tile_end - tiles_e
    lookup = lambda table: sum(jnp.where(expert == e, table[e], 0) for e in range(N_EXPERTS))
    tile_in_group = jnp.minimum(rank // EXPERT_TILE, lookup(tiles_e) - 1)
    row_in_tile = rank - tile_in_group * EXPERT_TILE
    tile_of = lookup(first_tile) + tile_in_group
    tiles = jnp.arange(MAX_EXPERT_TILES, dtype=i32)
    tile_expert = jnp.minimum(jnp.sum(tiles[:, None] >= tile_end[None, :], axis=1), N_EXPERTS - 1).astype(i32)
    before = (tiles - jnp.take(first_tile, tile_expert)) * EXPERT_TILE
    is_last = tiles == jnp.take(tile_end, tile_expert) - 1
    tile_rows = jnp.where(is_last, jnp.take(counts, tile_expert) - before, EXPERT_TILE)
    tile_rows = jnp.where(tiles < n_used, tile_rows, 0)
    shape = (TOP_K * N_TOKEN_BLOCKS, 1, TOKEN_BLOCK)
    chunk = jnp.arange(N_COL_CHUNKS, dtype=i32)[None, :, None]
    sorted_piece = ((tile_of.reshape(shape) * N_COL_CHUNKS + chunk) * TILE_CAP + row_in_tile.reshape(shape)).reshape(-1)
    return dict(sorted_piece=sorted_piece, gate=gate, tile_expert=tile_expert,
                n_used=n_used.reshape(1).astype(i32), tile_rows=tile_rows.astype(i32))


def _row_gather(x, indices):
    m = indices.shape[0]
    mesh = plsc.VectorSubcoreMesh(core_axis_name="c", subcore_axis_name="s")

    @pl.kernel(out_type=jax.ShapeDtypeStruct((m, PIECE_WORDS), x.dtype), mesh=mesh)
    def gather(x_hbm, i_hbm, o_hbm):
        def body(i_vmem, o_vmem):
            pltpu.sync_copy(x_hbm.at[i_vmem.at[0]], o_vmem)

        pltpu.emit_pipeline(
            body, grid=(m // GATHER_ROWS,),
            in_specs=[pl.BlockSpec((1, GATHER_ROWS), lambda i: (0, i))],
            out_specs=[pl.BlockSpec((GATHER_ROWS, PIECE_WORDS), lambda i: (i, 0))],
            core_axis_name=("c", "s"),
            dimension_semantics=(pltpu.PARALLEL,),
        )(i_hbm, o_hbm)

    return gather(x, indices.reshape(1, m))


def _row_scatter(x, indices, out_rows):
    m = indices.shape[0]
    x_blocks = x.shape[0] // GATHER_ROWS
    mesh = plsc.VectorSubcoreMesh(core_axis_name="c", subcore_axis_name="s")

    @pl.kernel(out_type=jax.ShapeDtypeStruct((out_rows, PIECE_WORDS), x.dtype), mesh=mesh)
    def scatter(x_hbm, i_hbm, o_hbm):
        def body(x_vmem, i_vmem):
            pltpu.sync_copy(x_vmem, o_hbm.at[i_vmem.at[0]])

        pltpu.emit_pipeline(
            body, grid=(m // GATHER_ROWS,),
            in_specs=[pl.BlockSpec((GATHER_ROWS, PIECE_WORDS), lambda i: (i % x_blocks, 0)),
                      pl.BlockSpec((1, GATHER_ROWS), lambda i: (0, i))],
            out_specs=[],
            core_axis_name=("c", "s"),
            dimension_semantics=(pltpu.PARALLEL,),
        )(x_hbm, i_hbm)

    return scatter(x, indices.reshape(1, m))


def _expert_kernel(te_ref, nu_ref, nr_ref, x_ref, wg_ref, wu_ref, wd_ref, y_ref, xb, acc):
    i = pl.program_id(0)
    f = pl.program_id(1)
    n_rows = nr_ref[i]
    n_sub = (n_rows + SUB_TILE - 1) // SUB_TILE
    subs = EXPERT_TILE // SUB_TILE

    @pl.when(n_sub > 0)
    def _():
        @pl.when(f == 0)
        def _():
            real = lax.broadcasted_iota(jnp.int32, (TILE_CAP, 1), 0) < n_rows
            for c in range(N_COL_CHUNKS):
                halves = _unpack_piece(x_ref[c * TILE_CAP:(c + 1) * TILE_CAP, :])
                for n, half in enumerate(halves):
                    lo = c * COL_CHUNK + n * PIECE_WORDS
                    xb[:, lo:lo + PIECE_WORDS] = jnp.where(real, half, 0.0).astype(BF16)
            acc[...] = jnp.zeros_like(acc)

        wg = wg_ref[0].astype(BF16)
        wu = wu_ref[0].astype(BF16)
        wd = wd_ref[0].astype(BF16)

        def block(rows):
            x = xb[rows, :]
            a = _dot(x, wg)
            b = _dot(x, wu)
            acc[rows, :] += _dot((a * _sigmoid(a) * b).astype(BF16), wd)

        @pl.when(n_sub >= subs)
        def _():
            block(slice(0, EXPERT_TILE))

        @pl.when(n_sub > subs)
        def _():
            block(slice(EXPERT_TILE, TILE_CAP))

        @pl.when(n_sub < subs)
        def _():
            for k in range(subs - 1):
                pl.when(k < n_sub)(functools.partial(block, slice(k * SUB_TILE, (k + 1) * SUB_TILE)))

        @pl.when(f == pl.num_programs(1) - 1)
        def _():
            for c in range(N_COL_CHUNKS):
                y_ref[c * TILE_CAP:(c + 1) * TILE_CAP, :] = _pack_piece(acc[:, c * COL_CHUNK:(c + 1) * COL_CHUNK])


def _experts(tab, x_sorted, wg, wu, wd):
    n_ff = D_FF // FF_TILE
    tile = lambda i, nu: jnp.minimum(i, nu[0] - 1)
    ff = lambda i, f, nu: jnp.where(i < nu[0], f, n_ff - 1)
    grid_spec = pltpu.PrefetchScalarGridSpec(
        num_scalar_prefetch=3, grid=(MAX_EXPERT_TILES, n_ff),
        in_specs=[pl.BlockSpec((N_COL_CHUNKS * TILE_CAP, PIECE_WORDS), lambda i, f, te, nu, ns: (tile(i, nu), 0)),
                  pl.BlockSpec((1, D_MODEL, FF_TILE),
                               lambda i, f, te, nu, ns: (te[tile(i, nu)], 0, ff(i, f, nu))),
                  pl.BlockSpec((1, D_MODEL, FF_TILE),
                               lambda i, f, te, nu, ns: (te[tile(i, nu)], 0, ff(i, f, nu))),
                  pl.BlockSpec((1, FF_TILE, D_MODEL),
                               lambda i, f, te, nu, ns: (te[tile(i, nu)], ff(i, f, nu), 0))],
        out_specs=pl.BlockSpec((N_COL_CHUNKS * TILE_CAP, PIECE_WORDS), lambda i, f, te, nu, ns: (tile(i, nu), 0)),
        scratch_shapes=[pltpu.VMEM((TILE_CAP, D_MODEL), BF16), pltpu.VMEM((TILE_CAP, D_MODEL), F32)])
    return pl.pallas_call(
        _expert_kernel, grid_spec=grid_spec,
        out_shape=jax.ShapeDtypeStruct((N_COL_CHUNKS * MAX_SORTED_ROWS, PIECE_WORDS), U32),
        compiler_params=_params("arbitrary", "arbitrary"),
    )(tab["tile_expert"], tab["n_used"], tab["tile_rows"], x_sorted, wg, wu, wd)


def _combine_kernel(h_ref, y_ref, g_ref, o_ref):
    g = g_ref[...]
    for c in range(N_COL_CHUNKS):
        first, second = _unpack_piece(y_ref[0, 0, c]), _unpack_piece(y_ref[1, 0, c])
        for n in range(2):
            cols = slice(c * COL_CHUNK + n * PIECE_WORDS, c * COL_CHUNK + (n + 1) * PIECE_WORDS)
            o_ref[:, cols] = h_ref[:, cols] + g[:, 0:1] * first[n] + g[:, 1:2] * second[n]


def _combine(h, y_pair, gate):
    row = pl.BlockSpec((TOKEN_BLOCK, D_MODEL), lambda i: (i, 0))
    return pl.pallas_call(
        _combine_kernel,
        grid=(N_TOKEN_BLOCKS,),
        in_specs=[row, pl.BlockSpec((TOP_K, 1, N_COL_CHUNKS, TOKEN_BLOCK, PIECE_WORDS), lambda i: (0, i, 0, 0, 0)),
                  pl.BlockSpec((TOKEN_BLOCK, TOP_K), lambda i: (i, 0))],
        out_specs=row,
        out_shape=jax.ShapeDtypeStruct((N_TOKENS, D_MODEL), F32),
        compiler_params=_params("parallel"),
    )(h, y_pair, gate)


def kernel(x, meta_tokens, mixer_norm, ffn_norm, attn_w_qkv, attn_q_norm, attn_k_norm, attn_sinks, attn_w_o, rwkv_mix, rwkv_w0, rwkv_w1, rwkv_w2, rwkv_a0, rwkv_a1, rwkv_a2, rwkv_g1, rwkv_g2, rwkv_k_k, rwkv_k_a, rwkv_r_k, rwkv_w_r, rwkv_w_k, rwkv_w_v, rwkv_w_o, rwkv_gn_w, rwkv_gn_b, ffn_w_gate, ffn_w_up, ffn_w_down, moe_router, moe_w_gate, moe_w_up, moe_w_down):
    bf = lambda a: a.astype(BF16)
    vec = lambda a: a.reshape(1, -1).astype(F32)
    h_meta = jnp.concatenate([jnp.zeros((PAD, D_MODEL), F32), meta_tokens.astype(F32)], axis=0)

    w_qkv = bf(attn_w_qkv[0])
    qkv = _qkv(x.reshape(N_TOKENS, D_MODEL), vec(mixer_norm[0]), w_qkv, TOKEN_BLOCK)
    qkv_meta = _qkv(h_meta, vec(mixer_norm[0]), w_qkv, BLOCK)
    h = _attention(x, h_meta, qkv.reshape(BATCH, SEQ, QKV_DIM), qkv_meta, attn_sinks[0].astype(F32),
                   jnp.tile(vec(attn_q_norm[0]), (1, Q_PER_KV)), jnp.tile(vec(attn_k_norm[0]), (1, N_KV_HEADS)),
                   bf(attn_w_o[0]))
    h = _ffn(h.reshape(N_ROWS, D_MODEL), vec(ffn_norm[0]), bf(ffn_w_gate[0]), bf(ffn_w_up[0]),
             bf(ffn_w_down[0]))

    rkv, g = _rwkv_proj(
        h, vec(mixer_norm[1]), rwkv_mix[0], vec(rwkv_w0[0]), vec(rwkv_a0[0]), vec(rwkv_k_k[0]),
        vec(rwkv_k_a[0]), bf(rwkv_w_r[0]), bf(rwkv_w_k[0]), bf(rwkv_w_v[0]), bf(rwkv_w1[0]),
        bf(rwkv_w2[0]), bf(rwkv_a1[0]), bf(rwkv_a2[0]), bf(rwkv_g1[0]), bf(rwkv_g2[0]))
    b3 = lambda t: t.reshape(BATCH, TP, -1)
    mm, add, bonus, p_last = _scan_prep(b3(rkv), vec(rwkv_r_k[0]))
    o = _scan(mm, add, p_last)
    w_router = moe_router[0].astype(F32).T
    wr_hi = bf(w_router)
    wr_lo = bf(w_router - wr_hi.astype(F32))
    h, u, route, cnt = _rwkv_out(o, bonus, b3(g), b3(h), vec(rwkv_gn_w[0]), vec(rwkv_gn_b[0]),
                                 bf(rwkv_w_o[0]), vec(ffn_norm[1]),
                                 jnp.concatenate([wr_hi, wr_hi, wr_lo], axis=1))
    tab = _routing_tables(route, cnt)
    x_sorted = _row_scatter(u.reshape(-1, PIECE_WORDS), tab["sorted_piece"], N_COL_CHUNKS * MAX_SORTED_ROWS)
    y_sorted = _experts(tab, x_sorted, moe_w_gate[0], moe_w_up[0], moe_w_down[0])
    y_pair = _row_gather(y_sorted, tab["sorted_piece"]).reshape(TOP_K, N_TOKEN_BLOCKS, N_COL_CHUNKS, TOKEN_BLOCK, PIECE_WORDS)
    out = _combine(h.reshape(N_TOKENS, D_MODEL), y_pair, tab["gate"])
    return out.reshape(BATCH, SEQ, D_MODEL)
```

```python
import functools

import jax
import jax.numpy as jnp
from jax import lax
from jax.experimental import pallas as pl
from jax.experimental.pallas import tpu as pltpu
from jax.experimental.pallas import tpu_sc as plsc

F32 = jnp.float32
BF16 = jnp.bfloat16

D_MODEL = 1024
BATCH = 2
SEQ = 8192
N_META = 16
BLOCK = 128
PAD = BLOCK - N_META
TP = SEQ + BLOCK
N_ROWS = BATCH * TP
N_BLOCKS = TP // BLOCK
META_BLOCK = N_BLOCKS - 1
HEAD_DIM = 64
N_Q_HEADS = 16
N_KV_HEADS = 4
Q_PER_KV = 4
QKV_DIM = (N_Q_HEADS + 2 * N_KV_HEADS) * HEAD_DIM
RWKV_HEADS = 16
RWKV_HEAD = 64
D_FF = 3584
N_EXPERTS = 8
NORM_EPS = 1e-5
GN_EPS = 64e-5
CHUNK = 64
N_CHUNKS = TP // CHUNK
META_CHUNK0 = SEQ // CHUNK
SCAN_CHUNKS = 2
HEADS_PER_GROUP = 4
GW = HEADS_PER_GROUP * RWKV_HEAD
N_GROUPS = RWKV_HEADS // HEADS_PER_GROUP
N_SCAN_IN = 6

FFN_ROW_TILE = 1664
PROJ_TILE = 320
FF_TILE = 512
N_TOKENS = BATCH * SEQ
TOP_K = 2
TOKEN_BLOCK = 512
N_TOKEN_BLOCKS = N_TOKENS // TOKEN_BLOCK
EXPERT_TILE = 1024
SUB_TILE = 256
MAX_EXPERT_TILES = TOP_K * N_TOKENS // EXPERT_TILE + N_EXPERTS
TILE_CAP = EXPERT_TILE + SUB_TILE
MAX_SORTED_ROWS = MAX_EXPERT_TILES * TILE_CAP
COL_CHUNK = 256
N_COL_CHUNKS = D_MODEL // COL_CHUNK
PIECE_WORDS = COL_CHUNK // 2
U32 = jnp.uint32
GATHER_ROWS = 128
VMEM_LIMIT = 56 * 1024 * 1024
NEG = -1e30

NT_DIMS = (((1,), (1,)), ((), ()))
TN_DIMS = (((0,), (0,)), ((), ()))


def _params(*sem):
    return pltpu.CompilerParams(dimension_semantics=sem, vmem_limit_bytes=VMEM_LIMIT)


def _rms(x, gain):
    return x * lax.rsqrt(jnp.mean(x * x, axis=-1, keepdims=True) + NORM_EPS) * gain


def _sigmoid(x):
    return 1.0 / (1.0 + jnp.exp(-x))


def _dot(a, b):
    return jnp.dot(a, b, preferred_element_type=F32)


def _pack_piece(x):
    return pltpu.pack_elementwise([x[:, :PIECE_WORDS], x[:, PIECE_WORDS:]], packed_dtype=BF16)


def _unpack_piece(w):
    return tuple(pltpu.unpack_elementwise(w, index=i, packed_dtype=BF16, unpacked_dtype=F32) for i in (0, 1))


def _qkv_kernel(h_ref, g_ref, w_ref, o_ref):
    u = _rms(h_ref[...], g_ref[...])
    o_ref[...] = _dot(u.astype(BF16), w_ref[...]).astype(BF16)


def _qkv(h, gain, w, tile):
    rows = h.shape[0]
    return pl.pallas_call(
        _qkv_kernel,
        grid=(rows // tile,),
        in_specs=[pl.BlockSpec((tile, D_MODEL), lambda i: (i, 0)),
                  pl.BlockSpec((1, D_MODEL), lambda i: (0, 0)),
                  pl.BlockSpec((D_MODEL, QKV_DIM), lambda i: (0, 0))],
        out_specs=pl.BlockSpec((tile, QKV_DIM), lambda i: (i, 0)),
        out_shape=jax.ShapeDtypeStruct((rows, QKV_DIM), BF16),
        compiler_params=_params("parallel"),
    )(h, gain, w)


def _attn_kernel(sink_ref, x_ref, hm_ref, q_ref, kc_ref, vc_ref, kp_ref, vp_ref, qm_ref, km_ref, vm_ref,
                 qg_ref, kg_ref, wo_ref, o_ref, o_scr):
    j = pl.program_id(1)
    n_band = 2 * BLOCK
    slots = 2 * N_META
    far = 4 * BLOCK
    is_real = j < META_BLOCK
    prev_off = jnp.where(jnp.logical_and(j >= 1, is_real), 0, far)
    cur_off = jnp.where(is_real, 0, far)
    rowi = lax.broadcasted_iota(jnp.int32, (BLOCK, n_band), 0)
    col = lax.broadcasted_iota(jnp.int32, (BLOCK, n_band), 1)
    band_ok = jnp.logical_or(jnp.logical_and(col < BLOCK, col > rowi + prev_off),
                             jnp.logical_and(col >= BLOCK, col - BLOCK + cur_off <= rowi))
    band_ok4 = jnp.concatenate([band_ok] * Q_PER_KV, axis=1)
    meta_off = jnp.where(is_real, -far, PAD)
    mrow = lax.broadcasted_iota(jnp.int32, (BLOCK, Q_PER_KV * slots), 0)
    mlane = lax.broadcasted_iota(jnp.int32, (BLOCK, Q_PER_KV * slots), 1)
    mslot = mlane % slots
    meta_ok4 = jnp.logical_and(mslot < N_META, mslot + meta_off <= mrow)
    meta_lane_head = mlane // slots

    groups = range(N_KV_HEADS)
    mask_bf = _group_masks().astype(BF16)
    rb = lax.broadcasted_iota(jnp.int32, (Q_PER_KV * n_band, GW), 0) // n_band
    lb = lax.broadcasted_iota(jnp.int32, (Q_PER_KV * n_band, GW), 1) // HEAD_DIM
    band_mask = jnp.where(rb == lb, 1.0, 0.0).astype(BF16)
    rm = lax.broadcasted_iota(jnp.int32, (Q_PER_KV * slots, GW), 0) // slots
    lm = lax.broadcasted_iota(jnp.int32, (Q_PER_KV * slots, GW), 1) // HEAD_DIM
    meta_mask = jnp.where(rm == lm, 1.0, 0.0).astype(BF16)
    sr = lax.broadcasted_iota(jnp.int32, (GW, GW), 0)
    sc = lax.broadcasted_iota(jnp.int32, (GW, GW), 1)
    lane_head = lax.broadcasted_iota(jnp.int32, (1, GW), 1) // HEAD_DIM
    inv_d = 1.0 / HEAD_DIM
    scale = HEAD_DIM ** -0.5

    real_rows = lax.broadcasted_iota(jnp.int32, (BLOCK, 1), 0) < jnp.where(is_real, BLOCK, 0)
    km, vm = km_ref[...], vm_ref[...]
    kcur = jnp.where(real_rows, kc_ref[0], km)
    vcur = jnp.where(real_rows, vc_ref[0], vm)
    kall = jnp.concatenate([km[PAD:], kp_ref[0], kcur], axis=0).astype(F32)
    vall = jnp.concatenate([vm[PAD:], vp_ref[0], vcur], axis=0)
    kss = _dot((kall * kall).astype(BF16), mask_bf)
    kn = (kall * lax.rsqrt(kss * inv_d + NORM_EPS) * kg_ref[...]).astype(BF16)
    q_all = jnp.where(real_rows, q_ref[0], qm_ref[...]).astype(F32)
    qn = []
    for g in groups:
        qg = q_all[:, g * GW:(g + 1) * GW]
        qss = _dot((qg * qg).astype(BF16), mask_bf)
        qn.append((qg * lax.rsqrt(qss * inv_d + NORM_EPS) * (qg_ref[...] * scale)).astype(BF16))
    sel = [jnp.where(sr == g * HEAD_DIM + sc % HEAD_DIM, 1.0, 0.0).astype(BF16) for g in groups]
    krep = [_dot(kn, sel[g]).astype(BF16) for g in groups]
    vrep = [_dot(vall, sel[g]).astype(BF16) for g in groups]
    unused = jnp.zeros((slots - N_META, GW), BF16)

    def band_bd(rep):
        return jnp.concatenate([rep[N_META:]] * Q_PER_KV, axis=0) * band_mask

    def meta_bd(rep):
        return jnp.concatenate([rep[:N_META], unused] * Q_PER_KV, axis=0) * meta_mask

    s_band = [jnp.where(band_ok4, lax.dot_general(qn[g], band_bd(krep[g]), NT_DIMS,
                                                  preferred_element_type=F32), NEG) for g in groups]
    s_meta = [jnp.where(meta_ok4, lax.dot_general(qn[g], meta_bd(krep[g]), NT_DIMS,
                                                  preferred_element_type=F32), NEG) for g in groups]
    rhs_band = [jnp.concatenate([band_bd(vrep[g]), band_mask], axis=1) for g in groups]
    rhs_meta = [jnp.concatenate([meta_bd(vrep[g]), meta_mask], axis=1) for g in groups]
    ov, sink_den = [], []
    for g in groups:
        parts = []
        sd = jnp.zeros((BLOCK, GW), F32)
        m_lanes = jnp.zeros((BLOCK, Q_PER_KV * slots), F32)
        for hh in range(Q_PER_KV):
            seg = s_band[g][:, hh * n_band:(hh + 1) * n_band]
            own = meta_lane_head == hh
            sink = sink_ref[g * Q_PER_KV + hh]
            m = jnp.maximum(jnp.max(seg, axis=-1, keepdims=True),
                            jnp.max(jnp.where(own, s_meta[g], NEG), axis=-1, keepdims=True))
            m = jnp.maximum(m, sink)
            parts.append(jnp.exp(seg - m).astype(BF16))
            m_lanes = jnp.where(own, m, m_lanes)
            sd = sd + jnp.exp(sink - m) * jnp.where(lane_head == hh, 1.0, 0.0)
        p_band = jnp.concatenate(parts, axis=1)
        p_meta = jnp.exp(s_meta[g] - m_lanes).astype(BF16)
        ov.append(_dot(p_band, rhs_band[g]) + _dot(p_meta, rhs_meta[g]))
        sink_den.append(sd)
    for g in groups:
        o_scr[:, g * GW:(g + 1) * GW] = ov[g][:, :GW] / (ov[g][:, GW:] + sink_den[g])
    h = jnp.where(real_rows, x_ref[0], hm_ref[...])
    o_ref[0] = h + _dot(o_scr[...].astype(BF16), wo_ref[...])


def _attention(x, h_meta, qkv, qkv_meta, sinks, q_gain, k_gain, w_o):
    kcol, vcol = N_Q_HEADS * HEAD_DIM // 256, N_Q_HEADS * HEAD_DIM // 256 + 1
    kvw = N_KV_HEADS * HEAD_DIM
    real = lambda j: jnp.minimum(j, META_BLOCK - 1)
    prev = lambda j: jnp.clip(j - 1, 0, META_BLOCK - 1)
    return pl.pallas_call(
        _attn_kernel,
        grid=(BATCH, N_BLOCKS),
        in_specs=[pl.BlockSpec(memory_space=pltpu.SMEM),
                  pl.BlockSpec((1, BLOCK, D_MODEL), lambda b, j: (b, real(j), 0)),
                  pl.BlockSpec((BLOCK, D_MODEL), lambda b, j: (0, 0)),
                  pl.BlockSpec((1, BLOCK, D_MODEL), lambda b, j: (b, real(j), 0)),
                  pl.BlockSpec((1, BLOCK, kvw), lambda b, j: (b, real(j), kcol)),
                  pl.BlockSpec((1, BLOCK, kvw), lambda b, j: (b, real(j), vcol)),
                  pl.BlockSpec((1, BLOCK, kvw), lambda b, j: (b, prev(j), kcol)),
                  pl.BlockSpec((1, BLOCK, kvw), lambda b, j: (b, prev(j), vcol)),
                  pl.BlockSpec((BLOCK, D_MODEL), lambda b, j: (0, 0)),
                  pl.BlockSpec((BLOCK, kvw), lambda b, j: (0, kcol)),
                  pl.BlockSpec((BLOCK, kvw), lambda b, j: (0, vcol)),
                  pl.BlockSpec((1, GW), lambda b, j: (0, 0)),
                  pl.BlockSpec((1, GW), lambda b, j: (0, 0)),
                  pl.BlockSpec((D_MODEL, D_MODEL), lambda b, j: (0, 0))],
        out_specs=pl.BlockSpec((1, BLOCK, D_MODEL), lambda b, j: (b, j, 0)),
        out_shape=jax.ShapeDtypeStruct((BATCH, TP, D_MODEL), F32),
        scratch_shapes=[pltpu.VMEM((BLOCK, D_MODEL), F32)],
        compiler_params=_params("parallel", "parallel"),
    )(sinks, x, h_meta, qkv, qkv, qkv, qkv, qkv, qkv_meta, qkv_meta, qkv_meta, q_gain, k_gain, w_o)


def _ffn_kernel(h_ref, g_ref, wg_ref, wu_ref, wd_ref, o_ref, u_scr, acc):
    f = pl.program_id(1)

    @pl.when(f == 0)
    def _():
        u_scr[...] = _rms(h_ref[...], g_ref[...]).astype(BF16)
        acc[...] = jnp.zeros_like(acc)

    u = u_scr[...]
    a = _dot(u, wg_ref[...])
    b = _dot(u, wu_ref[...])
    acc[...] += _dot((a * _sigmoid(a) * b).astype(BF16), wd_ref[...])

    @pl.when(f == pl.num_programs(1) - 1)
    def _():
        o_ref[...] = h_ref[...] + acc[...]


def _ffn(h, gain, wg, wu, wd):
    return pl.pallas_call(
        _ffn_kernel,
        grid=(N_ROWS // FFN_ROW_TILE, D_FF // FF_TILE),
        in_specs=[pl.BlockSpec((FFN_ROW_TILE, D_MODEL), lambda i, f: (i, 0)),
                  pl.BlockSpec((1, D_MODEL), lambda i, f: (0, 0)),
                  pl.BlockSpec((D_MODEL, FF_TILE), lambda i, f: (0, f)),
                  pl.BlockSpec((D_MODEL, FF_TILE), lambda i, f: (0, f)),
                  pl.BlockSpec((FF_TILE, D_MODEL), lambda i, f: (f, 0))],
        out_specs=pl.BlockSpec((FFN_ROW_TILE, D_MODEL), lambda i, f: (i, 0)),
        out_shape=jax.ShapeDtypeStruct((N_ROWS, D_MODEL), F32),
        scratch_shapes=[pltpu.VMEM((FFN_ROW_TILE, D_MODEL), BF16), pltpu.VMEM((FFN_ROW_TILE, D_MODEL), F32)],
        compiler_params=_params("parallel", "arbitrary"),
    )(h, gain, wg, wu, wd)


def _rwkv_proj_kernel(h_ref, hp_ref, g_ref, mix_ref, w0_ref, a0_ref, kk_ref, ka_ref,
                      wr_ref, wk_ref, wv_ref, w1_ref, w2_ref, a1_ref, a2_ref, g1_ref, g2_ref,
                      rkv_out, g_out):
    i = pl.program_id(0)
    tiles_per_batch = TP // PROJ_TILE
    r0 = (i % tiles_per_batch) * PROJ_TILE
    local = lax.broadcasted_iota(jnp.int32, (PROJ_TILE, 1), 0)
    lrow = local + r0
    gain = g_ref[...]
    is_pad = jnp.logical_and(lrow >= SEQ, lrow < SEQ + PAD)
    u = jnp.where(is_pad, 0.0, _rms(h_ref[...], gain))
    u_prev_tile = _rms(hp_ref[7:8, :], gain)
    xprev = pltpu.roll(u, 1, 0)
    xprev = jnp.where(local == 0, u_prev_tile, xprev)
    xprev = jnp.where(lrow == SEQ, 0.0, xprev)
    xx = xprev - u
    mix = mix_ref[...]
    lerp = lambda n: (u + xx * mix[n:n + 1, :]).astype(BF16)
    xr, xw, xk, xv, xa, xg = [lerp(n) for n in range(6)]
    r = _dot(xr, wr_ref[...])
    k = _dot(xk, wk_ref[...])
    v = _dot(xv, wv_ref[...])
    lw = _dot(jnp.tanh(_dot(xw, w1_ref[...])).astype(BF16), w2_ref[...])
    z = -(w0_ref[...] + lw)
    softplus = jnp.maximum(z, 0.0) + jnp.log(1.0 + jnp.exp(-jnp.abs(z)))
    w = -softplus - 0.5
    a = _sigmoid(a0_ref[...] + _dot(_dot(xa, a1_ref[...]).astype(BF16), a2_ref[...]))
    g = _dot(_sigmoid(_dot(xg, g1_ref[...])).astype(BF16), g2_ref[...])
    fields = (r, -jnp.exp(w),
              k * (1.0 + (a - 1.0) * ka_ref[...]), v, k * kk_ref[...], a)
    for n, val in enumerate(fields):
        rkv_out[:, n * D_MODEL:(n + 1) * D_MODEL] = val
    g_out[...] = g


def _rwkv_proj(h, gain, mix, w0, a0, k_k, k_a, w_r, w_k, w_v, w1, w2, a1, a2, g1, g2):
    tiles_per_batch = TP // PROJ_TILE
    rows8 = PROJ_TILE // 8

    def prev_map(i):
        b = i // tiles_per_batch
        first = (i % tiles_per_batch) == 0
        return (jnp.where(first, (b * TP + TP - 8) // 8, i * rows8 - 1), 0)

    row = pl.BlockSpec((PROJ_TILE, D_MODEL), lambda i: (i, 0))
    full = lambda a: pl.BlockSpec(a.shape, lambda i: (0,) * a.ndim)
    smalls = (gain, mix, w0, a0, k_k, k_a, w_r, w_k, w_v, w1, w2, a1, a2, g1, g2)
    return pl.pallas_call(
        _rwkv_proj_kernel,
        grid=(N_ROWS // PROJ_TILE,),
        in_specs=[row, pl.BlockSpec((8, D_MODEL), prev_map)] + [full(a) for a in smalls],
        out_specs=[pl.BlockSpec((PROJ_TILE, N_SCAN_IN * D_MODEL), lambda i: (i, 0)), row],
        out_shape=[jax.ShapeDtypeStruct((N_ROWS, N_SCAN_IN * D_MODEL), F32),
                   jax.ShapeDtypeStruct((N_ROWS, D_MODEL), F32)],
        compiler_params=_params("parallel"),
    )(h, h, *smalls)


def _group_masks():
    ri = lax.broadcasted_iota(jnp.int32, (GW, GW), 0) // RWKV_HEAD
    ci = lax.broadcasted_iota(jnp.int32, (GW, GW), 1) // RWKV_HEAD
    return jnp.where(ri == ci, 1.0, 0.0).astype(F32)


def _bd(x, mask):
    return jnp.concatenate([x.astype(BF16)] * HEADS_PER_GROUP, axis=0) * mask.astype(BF16)


def _diag_blocks(full, mask):
    m = full * mask
    n = RWKV_HEAD
    return (m[0:n] + m[n:2 * n]) + (m[2 * n:3 * n] + m[3 * n:4 * n])


def _head_sum(x, mask_bf):
    hi = x.astype(BF16)
    lo = (x - hi.astype(F32)).astype(BF16)
    return _dot(hi, mask_bf) + _dot(lo, mask_bf)


def _scan_prep_kernel(x_ref, rk_ref, mm_out, add_out, bonus_out, pl_out):
    L, D = CHUNK, D_MODEL
    units = [(b, g) for b in range(BATCH) for g in range(N_GROUPS)]
    un = range(len(units))
    mask = _group_masks()
    mask_bf = mask.astype(BF16)
    ri = lax.broadcasted_iota(jnp.int32, (L, GW), 0)
    ci = lax.broadcasted_iota(jnp.int32, (L, GW), 1) % RWKV_HEAD
    incl = ci <= ri
    strict = ci < ri
    eye = jnp.where(ci == ri, 1.0, 0.0).astype(F32)
    t_r = lax.broadcasted_iota(jnp.int32, (L, L), 0)
    t_c = lax.broadcasted_iota(jnp.int32, (L, L), 1)
    tril = jnp.where(t_c <= t_r, 1.0, 0.0).astype(BF16)
    rk_all = rk_ref[...]

    def field(b, n, g):
        return x_ref[b, :, n * D + g * GW:n * D + (g + 1) * GW]

    def put(ref, b, n, g, val):
        ref[b, :, n * D + g * GW:n * D + (g + 1) * GW] = val.astype(ref.dtype)

    at, rt, bt, kt, v, plast = [], [], [], [], [], []
    for b in range(BATCH):
        ld = x_ref[b, :, D:2 * D]
        hi = ld.astype(BF16)
        rest = ld - hi.astype(F32)
        mid = rest.astype(BF16)
        lo = (rest - mid.astype(F32)).astype(BF16)
        cs = _dot(tril, hi) + _dot(tril, mid) + _dot(tril, lo)
        p_all = jnp.exp(cs)
        pprev_all = jnp.exp(cs - ld)
        pinv_all = jnp.exp(-cs)
        pl_all = p_all[L - 1:L, :]
        pl_out[b, 0] = pl_all
        for g in range(N_GROUPS):
            sl = slice(g * GW, (g + 1) * GW)
            r, k, vv, kk, a = field(b, 0, g), field(b, 2, g), field(b, 3, g), field(b, 4, g), field(b, 5, g)
            nrm = jnp.sqrt(_dot((kk * kk).astype(BF16), mask_bf))
            kk = kk / jnp.maximum(nrm, 1e-12)
            at.append(-kk * pprev_all[:, sl])
            bt.append(kk * a * pinv_all[:, sl])
            rt.append(r * p_all[:, sl])
            kt.append(k * pinv_all[:, sl])
            v.append(vv)
            plast.append(pl_all[:, sl])
            put(bonus_out, b, 0, g, _dot((r * k * rk_all[:, sl]).astype(BF16), mask_bf) * vv)

    a_ab, a_ak, a_rb, a_rk = [], [], [], []
    for n in un:
        lhs = jnp.concatenate([at[n], rt[n]], axis=0).astype(BF16)
        rhs = jnp.concatenate([_bd(bt[n], mask), _bd(kt[n], mask)], axis=0)
        big = lax.dot_general(lhs, rhs, NT_DIMS, preferred_element_type=F32)
        a_ab.append(jnp.where(strict, big[:L, :GW], 0.0))
        a_ak.append(jnp.where(strict, big[:L, GW:], 0.0))
        a_rb.append(jnp.where(incl, big[L:, :GW], 0.0))
        a_rk.append(jnp.where(incl, big[L:, GW:], 0.0))

    x = [_dot(a_ab[n].astype(BF16), _bd(a_ab[n], mask)) for n in un]
    inv = [eye + a_ab[n] for n in un]
    for step in range(5):
        for n in un:
            rhs = _bd(x[n], mask)
            if step < 4:
                res = _dot(jnp.concatenate([x[n], inv[n]], axis=0).astype(BF16), rhs)
                x[n] = res[:L]
                inv[n] = inv[n] + res[L:]
            else:
                inv[n] = inv[n] + _dot(inv[n].astype(BF16), rhs)

    av = [_dot(jnp.concatenate([a_ak[n], a_rk[n]], axis=0).astype(BF16), _bd(v[n], mask)) for n in un]
    wu = [_dot(inv[n].astype(BF16), jnp.concatenate([_bd(at[n], mask), _bd(av[n][:L], mask)], axis=1))
          for n in un]
    aw = [_dot(a_rb[n].astype(BF16),
               jnp.concatenate([_bd(wu[n][:, :GW], mask), _bd(wu[n][:, GW:], mask)], axis=1))
          for n in un]
    for n, (b, g) in enumerate(units):
        put(mm_out, b, 0, g, rt[n] + aw[n][:, :GW])
        put(add_out, b, 0, g, av[n][L:] + aw[n][:, GW:])
        bh = (bt[n] * plast[n]).astype(BF16)
        kh = (kt[n] * plast[n]).astype(BF16)
        w_b, u0_b = wu[n][:, :GW].astype(BF16), wu[n][:, GW:].astype(BF16)
        gfull = lax.dot_general(bh, w_b, TN_DIMS, preferred_element_type=F32)
        put(mm_out, b, 1, g, _diag_blocks(gfull, mask))
        hfull = lax.dot_general(jnp.concatenate([u0_b, v[n].astype(BF16)], axis=0),
                                jnp.concatenate([bh, kh], axis=0), TN_DIMS, preferred_element_type=F32)
        put(add_out, b, 1, g, _diag_blocks(hfull, mask))


def _scan_prep(rkv, r_k):
    return pl.pallas_call(
        _scan_prep_kernel,
        grid=(N_CHUNKS,),
        in_specs=[pl.BlockSpec((BATCH, CHUNK, N_SCAN_IN * D_MODEL), lambda c: (0, c, 0)),
                  pl.BlockSpec((1, D_MODEL), lambda c: (0, 0))],
        out_specs=[pl.BlockSpec((BATCH, CHUNK, 2 * D_MODEL), lambda c: (0, c, 0)),
                   pl.BlockSpec((BATCH, CHUNK, 2 * D_MODEL), lambda c: (0, c, 0)),
                   pl.BlockSpec((BATCH, CHUNK, D_MODEL), lambda c: (0, c, 0)),
                   pl.BlockSpec((BATCH, 1, 1, D_MODEL), lambda c: (0, c, 0, 0))],
        out_shape=[jax.ShapeDtypeStruct((BATCH, TP, 2 * D_MODEL), BF16),
                   jax.ShapeDtypeStruct((BATCH, TP, 2 * D_MODEL), F32),
                   jax.ShapeDtypeStruct((BATCH, TP, D_MODEL), F32),
                   jax.ShapeDtypeStruct((BATCH, N_CHUNKS, 1, D_MODEL), F32)],
        compiler_params=_params("parallel"),
    )(rkv, r_k)


def _scan_kernel(mm_ref, add_ref, pl_ref, y_ref, s_scr):
    c = pl.program_id(0)
    D = D_MODEL

    @pl.when(c == 0)
    def _():
        s_scr[...] = jnp.zeros_like(s_scr)

    mask = _group_masks()
    units = [(b, slice(g * GW, (g + 1) * GW)) for b in range(BATCH) for g in range(N_GROUPS)]
    s = [s_scr[b, :, sl] for b, sl in units]
    for k in range(SCAN_CHUNKS):
        rows = slice(k * CHUNK, (k + 1) * CHUNK)
        field = lambda ref, b, n, sl, rows=rows: ref[b, rows, n * D + sl.start:n * D + sl.stop]
        o = [lax.dot_general(field(mm_ref, b, 0, sl), _bd(s[n], mask), NT_DIMS,
                             preferred_element_type=F32) + field(add_ref, b, 0, sl)
             for n, (b, sl) in enumerate(units)]
        sg = [lax.dot_general(s[n].astype(BF16), _bd(field(mm_ref, b, 1, sl), mask), NT_DIMS,
                              preferred_element_type=F32)
              for n, (b, sl) in enumerate(units)]
        s = [s[n] * pl_ref[b, k, :, sl] + sg[n] + field(add_ref, b, 1, sl)
             for n, (b, sl) in enumerate(units)]
        for n, (b, sl) in enumerate(units):
            y_ref[b, rows, sl] = o[n]
    for n, (b, sl) in enumerate(units):
        s_scr[b, :, sl] = s[n]


def _scan(mm, add, p_last):
    steps = N_CHUNKS // SCAN_CHUNKS
    first = META_CHUNK0 // SCAN_CHUNKS
    phys = lambda c: (c + first) % steps
    pair = pl.BlockSpec((BATCH, SCAN_CHUNKS * CHUNK, 2 * D_MODEL), lambda c: (0, phys(c), 0))
    return pl.pallas_call(
        _scan_kernel,
        grid=(steps,),
        in_specs=[pair, pair, pl.BlockSpec((BATCH, SCAN_CHUNKS, 1, D_MODEL), lambda c: (0, phys(c), 0, 0))],
        out_specs=pl.BlockSpec((BATCH, SCAN_CHUNKS * CHUNK, D_MODEL), lambda c: (0, phys(c), 0)),
        out_shape=jax.ShapeDtypeStruct((BATCH, TP, D_MODEL), F32),
        scratch_shapes=[pltpu.VMEM((BATCH, RWKV_HEAD, D_MODEL), F32)],
        compiler_params=_params("arbitrary"),
    )(mm, add, p_last)


def _rwkv_out_kernel(o_ref, bonus_ref, g_ref, h_ref, gw_ref, gb_ref, wo_ref, gain_ref, wr_ref,
                     h_out, u_out, route_out, cnt_out, carry):
    @pl.when(jnp.logical_and(pl.program_id(0) == 0, pl.program_id(1) == 0))
    def _():
        carry[...] = jnp.zeros_like(carry)

    mask_bf = _group_masks().astype(BF16)
    inv_n = 1.0 / RWKV_HEAD
    o, bonus, gate = o_ref[0], bonus_ref[0], g_ref[0]
    parts = []
    for g in range(N_GROUPS):
        sl = slice(g * GW, (g + 1) * GW)
        og = o[:, sl]
        d = og - _head_sum(og, mask_bf) * inv_n
        var = _dot((d * d).astype(BF16), mask_bf) * inv_n
        yn = d * lax.rsqrt(var + GN_EPS) * gw_ref[:, sl] + gb_ref[:, sl] + bonus[:, sl]
        parts.append((yn * gate[:, sl]).astype(BF16))
    h = h_ref[0] + _dot(jnp.concatenate(parts, axis=1), wo_ref[...])
    h_out[0] = h
    u = _rms(h, gain_ref[...])
    for c in range(N_COL_CHUNKS):
        u_out[0, 0, c] = _pack_piece(u[:, c * COL_CHUNK:(c + 1) * COL_CHUNK])
    u_hi = u.astype(BF16)
    u_lo = (u - u_hi.astype(F32)).astype(BF16)
    logits = lax.dot_general(wr_ref[...], jnp.concatenate([u_hi, u_lo, u_hi], axis=1), NT_DIMS,
                             preferred_element_type=F32)
    e = jnp.exp(logits - jnp.max(logits, axis=0, keepdims=True))
    probs = e / jnp.sum(e, axis=0, keepdims=True)
    idx = lax.broadcasted_iota(jnp.int32, probs.shape, 0).astype(F32)
    m1 = jnp.max(probs, axis=0, keepdims=True)
    i1 = jnp.min(jnp.where(probs == m1, idx, float(N_EXPERTS)), axis=0, keepdims=True)
    sel1 = idx == i1
    rest = jnp.where(sel1, -1.0, probs)
    m2 = jnp.max(rest, axis=0, keepdims=True)
    i2 = jnp.min(jnp.where(rest == m2, idx, float(N_EXPERTS)), axis=0, keepdims=True)
    sel2 = idx == i2
    onehot = jnp.where(jnp.logical_or(sel1, sel2), 1.0, 0.0).astype(F32)
    tr = lax.broadcasted_iota(jnp.int32, (TOKEN_BLOCK, TOKEN_BLOCK), 0)
    tc = lax.broadcasted_iota(jnp.int32, (TOKEN_BLOCK, TOKEN_BLOCK), 1)
    earlier = _dot(onehot.astype(BF16), jnp.where(tr < tc, 1.0, 0.0).astype(BF16)) + carry[...]
    rank1 = jnp.sum(jnp.where(sel1, earlier, 0.0), axis=0, keepdims=True)
    rank2 = jnp.sum(jnp.where(sel2, earlier, 0.0), axis=0, keepdims=True)
    den = m1 + m2
    fields = (i1, i2, rank1, rank2, m1 / den, m2 / den)
    route = jnp.zeros(probs.shape, F32)
    for n, val in enumerate(fields):
        route = jnp.where(idx == float(n), val, route)
    route_out[0] = route
    tile_cnt = jnp.sum(onehot, axis=1, keepdims=True)
    cnt_out[0] = tile_cnt
    carry[...] += tile_cnt


def _rwkv_out(o, bonus, g, h, gn_w, gn_b, w_o, gain, w_router):
    blocks = SEQ // TOKEN_BLOCK
    row = pl.BlockSpec((1, TOKEN_BLOCK, D_MODEL), lambda b, i: (b, i, 0))
    vec = pl.BlockSpec((1, D_MODEL), lambda b, i: (0, 0))
    return pl.pallas_call(
        _rwkv_out_kernel,
        grid=(BATCH, blocks),
        in_specs=[row, row, row, row, vec, vec,
                  pl.BlockSpec((D_MODEL, D_MODEL), lambda b, i: (0, 0)), vec,
                  pl.BlockSpec((N_EXPERTS, 3 * D_MODEL), lambda b, i: (0, 0))],
        out_specs=[row, pl.BlockSpec((1, 1, N_COL_CHUNKS, TOKEN_BLOCK, PIECE_WORDS), lambda b, i: (b, i, 0, 0, 0)),
                   pl.BlockSpec((1, N_EXPERTS, TOKEN_BLOCK), lambda b, i: (b, 0, i)),
                   pl.BlockSpec((1, N_EXPERTS, 1), lambda b, i: (b * blocks + i, 0, 0))],
        out_shape=[jax.ShapeDtypeStruct((BATCH, SEQ, D_MODEL), F32),
                   jax.ShapeDtypeStruct((BATCH, blocks, N_COL_CHUNKS, TOKEN_BLOCK, PIECE_WORDS), U32),
                   jax.ShapeDtypeStruct((BATCH, N_EXPERTS, SEQ), F32),
                   jax.ShapeDtypeStruct((BATCH * blocks, N_EXPERTS, 1), F32)],
        scratch_shapes=[pltpu.VMEM((N_EXPERTS, 1), F32)],
        compiler_params=_params("arbitrary", "arbitrary"),
    )(o, bonus, g, h, gn_w, gn_b, w_o, gain, w_router)


def _tile_tables(cnt):
    i32 = jnp.int32
    counts = jnp.sum(cnt.reshape(-1, N_EXPERTS).astype(i32), axis=0)
    tiles_e = jnp.where(counts > 0, jnp.maximum((counts - SUB_TILE + EXPERT_TILE - 1) // EXPERT_TILE, 1), 0)
    tile_end = jnp.cumsum(tiles_e)
    n_used = tile_end[-1]
    first_tile = tile_end - tiles_e
    tiles = jnp.arange(MAX_EXPERT_TILES, dtype=i32)
    tile_expert = jnp.minimum(jnp.sum(tiles[:, None] >= tile_end[None, :], axis=1), N_EXPERTS - 1).astype(i32)
    before = (tiles - jnp.take(first_tile, tile_expert)) * EXPERT_TILE
    is_last = tiles == jnp.take(tile_end, tile_expert) - 1
    tile_rows = jnp.where(is_last, jnp.take(counts, tile_expert) - before, EXPERT_TILE)
    tile_rows = jnp.where(tiles < n_used, tile_rows, 0)
    return dict(first_tile=first_tile.astype(i32), tiles_e=tiles_e.astype(i32), tile_expert=tile_expert,
                n_used=n_used.reshape(1).astype(i32), tile_rows=tile_rows.astype(i32))


def _piece_index_kernel(first_ref, ntile_ref, route_ref, o_ref):
    route = route_ref[0]
    for slot in range(TOP_K):
        expert = route[slot:slot + 1, :].astype(jnp.int32)
        rank = route[TOP_K + slot:TOP_K + slot + 1, :].astype(jnp.int32)
        first = sum(jnp.where(expert == e, first_ref[e], 0) for e in range(N_EXPERTS))
        ntile = sum(jnp.where(expert == e, ntile_ref[e], 0) for e in range(N_EXPERTS))
        tile_in_group = jnp.minimum(rank // EXPERT_TILE, ntile - 1)
        row = rank - tile_in_group * EXPERT_TILE
        tile = first + tile_in_group
        for c in range(N_COL_CHUNKS):
            o_ref[slot, 0, c:c + 1, :] = (tile * N_COL_CHUNKS + c) * TILE_CAP + row


def _piece_index(tab, route):
    blocks = SEQ // TOKEN_BLOCK
    grid_spec = pltpu.PrefetchScalarGridSpec(
        num_scalar_prefetch=2, grid=(BATCH, blocks),
        in_specs=[pl.BlockSpec((1, N_EXPERTS, TOKEN_BLOCK), lambda b, i, ft, nt: (b, 0, i))],
        out_specs=pl.BlockSpec((TOP_K, 1, N_COL_CHUNKS, TOKEN_BLOCK), lambda b, i, ft, nt: (0, b * blocks + i, 0, 0)))
    return pl.pallas_call(
        _piece_index_kernel, grid_spec=grid_spec,
        out_shape=jax.ShapeDtypeStruct((TOP_K, N_TOKEN_BLOCKS, N_COL_CHUNKS, TOKEN_BLOCK), jnp.int32),
        compiler_params=_params("parallel", "parallel"),
    )(tab["first_tile"], tab["tiles_e"], route).reshape(-1)


def _row_gather(x, indices):
    m = indices.shape[0]
    mesh = plsc.VectorSubcoreMesh(core_axis_name="c", subcore_axis_name="s")

    @pl.kernel(out_type=jax.ShapeDtypeStruct((m, PIECE_WORDS), x.dtype), mesh=mesh)
    def gather(x_hbm, i_hbm, o_hbm):
        def body(i_vmem, o_vmem):
            pltpu.sync_copy(x_hbm.at[i_vmem.at[0]], o_vmem)

        pltpu.emit_pipeline(
            body, grid=(m // GATHER_ROWS,),
            in_specs=[pl.BlockSpec((1, GATHER_ROWS), lambda i: (0, i))],
            out_specs=[pl.BlockSpec((GATHER_ROWS, PIECE_WORDS), lambda i: (i, 0))],
            core_axis_name=("c", "s"),
            dimension_semantics=(pltpu.PARALLEL,),
        )(i_hbm, o_hbm)

    return gather(x, indices.reshape(1, m))


def _row_scatter(x, indices, out_rows):
    m = indices.shape[0]
    x_blocks = x.shape[0] // GATHER_ROWS
    mesh = plsc.VectorSubcoreMesh(core_axis_name="c", subcore_axis_name="s")

    @pl.kernel(out_type=jax.ShapeDtypeStruct((out_rows, PIECE_WORDS), x.dtype), mesh=mesh)
    def scatter(x_hbm, i_hbm, o_hbm):
        def body(x_vmem, i_vmem):
            pltpu.sync_copy(x_vmem, o_hbm.at[i_vmem.at[0]])

        pltpu.emit_pipeline(
            body, grid=(m // GATHER_ROWS,),
            in_specs=[pl.BlockSpec((GATHER_ROWS, PIECE_WORDS), lambda i: (i % x_blocks, 0)),
                      pl.BlockSpec((1, GATHER_ROWS), lambda i: (0, i))],
            out_specs=[],
            core_axis_name=("c", "s"),
            dimension_semantics=(pltpu.PARALLEL,),
        )(x_hbm, i_hbm)

    return scatter(x, indices.reshape(1, m))


def _expert_kernel(te_ref, nu_ref, nr_ref, x_ref, wg_ref, wu_ref, wd_ref, y_ref, xb, acc):
    i = pl.program_id(0)
    f = pl.program_id(1)
    n_rows = nr_ref[i]
    n_sub = (n_rows + SUB_TILE - 1) // SUB_TILE
    subs = EXPERT_TILE // SUB_TILE

    @pl.when(n_sub > 0)
    def _():
        @pl.when(f == 0)
        def _():
            real = lax.broadcasted_iota(jnp.int32, (TILE_CAP, 1), 0) < n_rows
            for c in range(N_COL_CHUNKS):
                halves = _unpack_piece(x_ref[c * TILE_CAP:(c + 1) * TILE_CAP, :])
                for n, half in enumerate(halves):
                    lo = c * COL_CHUNK + n * PIECE_WORDS
                    xb[:, lo:lo + PIECE_WORDS] = jnp.where(real, half, 0.0).astype(BF16)
            acc[...] = jnp.zeros_like(acc)

        wg = wg_ref[0].astype(BF16)
        wu = wu_ref[0].astype(BF16)
        wd = wd_ref[0].astype(BF16)

        def block(rows):
            x = xb[rows, :]
            a = _dot(x, wg)
            b = _dot(x, wu)
            acc[rows, :] += _dot((a * _sigmoid(a) * b).astype(BF16), wd)

        @pl.when(n_sub >= subs)
        def _():
            block(slice(0, EXPERT_TILE))

        @pl.when(n_sub > subs)
        def _():
            block(slice(EXPERT_TILE, TILE_CAP))

        @pl.when(n_sub < subs)
        def _():
            for k in range(subs - 1):
                pl.when(k < n_sub)(functools.partial(block, slice(k * SUB_TILE, (k + 1) * SUB_TILE)))

        @pl.when(f == pl.num_programs(1) - 1)
        def _():
            for c in range(N_COL_CHUNKS):
                y_ref[c * TILE_CAP:(c + 1) * TILE_CAP, :] = _pack_piece(acc[:, c * COL_CHUNK:(c + 1) * COL_CHUNK])


def _experts(tab, x_sorted, wg, wu, wd):
    n_ff = D_FF // FF_TILE
    tile = lambda i, nu: jnp.minimum(i, nu[0] - 1)
    ff = lambda i, f, nu: jnp.where(i < nu[0], f, n_ff - 1)
    grid_spec = pltpu.PrefetchScalarGridSpec(
        num_scalar_prefetch=3, grid=(MAX_EXPERT_TILES, n_ff),
        in_specs=[pl.BlockSpec((N_COL_CHUNKS * TILE_CAP, PIECE_WORDS), lambda i, f, te, nu, ns: (tile(i, nu), 0)),
                  pl.BlockSpec((1, D_MODEL, FF_TILE),
                               lambda i, f, te, nu, ns: (te[tile(i, nu)], 0, ff(i, f, nu))),
                  pl.BlockSpec((1, D_MODEL, FF_TILE),
                               lambda i, f, te, nu, ns: (te[tile(i, nu)], 0, ff(i, f, nu))),
                  pl.BlockSpec((1, FF_TILE, D_MODEL),
                               lambda i, f, te, nu, ns: (te[tile(i, nu)], ff(i, f, nu), 0))],
        out_specs=pl.BlockSpec((N_COL_CHUNKS * TILE_CAP, PIECE_WORDS), lambda i, f, te, nu, ns: (tile(i, nu), 0)),
        scratch_shapes=[pltpu.VMEM((TILE_CAP, D_MODEL), BF16), pltpu.VMEM((TILE_CAP, D_MODEL), F32)])
    return pl.pallas_call(
        _expert_kernel, grid_spec=grid_spec,
        out_shape=jax.ShapeDtypeStruct((N_COL_CHUNKS * MAX_SORTED_ROWS, PIECE_WORDS), U32),
        compiler_params=_params("arbitrary", "arbitrary"),
    )(tab["tile_expert"], tab["n_used"], tab["tile_rows"], x_sorted, wg, wu, wd)


def _combine_kernel(h_ref, y_ref, g_ref, o_ref):
    g = g_ref[...]
    for c in range(N_COL_CHUNKS):
        first, second = _unpack_piece(y_ref[0, 0, c]), _unpack_piece(y_ref[1, 0, c])
        for n in range(2):
            cols = slice(c * COL_CHUNK + n * PIECE_WORDS, c * COL_CHUNK + (n + 1) * PIECE_WORDS)
            o_ref[:, cols] = h_ref[:, cols] + g[:, 0:1] * first[n] + g[:, 1:2] * second[n]


def _combine(h, y_pair, gate):
    row = pl.BlockSpec((TOKEN_BLOCK, D_MODEL), lambda i: (i, 0))
    return pl.pallas_call(
        _combine_kernel,
        grid=(N_TOKEN_BLOCKS,),
        in_specs=[row, pl.BlockSpec((TOP_K, 1, N_COL_CHUNKS, TOKEN_BLOCK, PIECE_WORDS), lambda i: (0, i, 0, 0, 0)),
                  pl.BlockSpec((TOKEN_BLOCK, TOP_K), lambda i: (i, 0))],
        out_specs=row,
        out_shape=jax.ShapeDtypeStruct((N_TOKENS, D_MODEL), F32),
        compiler_params=_params("parallel"),
    )(h, y_pair, gate)


def kernel(x, meta_tokens, mixer_norm, ffn_norm, attn_w_qkv, attn_q_norm, attn_k_norm, attn_sinks, attn_w_o, rwkv_mix, rwkv_w0, rwkv_w1, rwkv_w2, rwkv_a0, rwkv_a1, rwkv_a2, rwkv_g1, rwkv_g2, rwkv_k_k, rwkv_k_a, rwkv_r_k, rwkv_w_r, rwkv_w_k, rwkv_w_v, rwkv_w_o, rwkv_gn_w, rwkv_gn_b, ffn_w_gate, ffn_w_up, ffn_w_down, moe_router, moe_w_gate, moe_w_up, moe_w_down):
    bf = lambda a: a.astype(BF16)
    vec = lambda a: a.reshape(1, -1).astype(F32)
    h_meta = jnp.concatenate([jnp.zeros((PAD, D_MODEL), F32), meta_tokens.astype(F32)], axis=0)

    w_qkv = bf(attn_w_qkv[0])
    qkv = _qkv(x.reshape(N_TOKENS, D_MODEL), vec(mixer_norm[0]), w_qkv, TOKEN_BLOCK)
    qkv_meta = _qkv(h_meta, vec(mixer_norm[0]), w_qkv, BLOCK)
    h = _attention(x, h_meta, qkv.reshape(BATCH, SEQ, QKV_DIM), qkv_meta, attn_sinks[0].astype(F32),
                   jnp.tile(vec(attn_q_norm[0]), (1, Q_PER_KV)), jnp.tile(vec(attn_k_norm[0]), (1, N_KV_HEADS)),
                   bf(attn_w_o[0]))
    h = _ffn(h.reshape(N_ROWS, D_MODEL), vec(ffn_norm[0]), bf(ffn_w_gate[0]), bf(ffn_w_up[0]),
             bf(ffn_w_down[0]))

    rkv, g = _rwkv_proj(
        h, vec(mixer_norm[1]), rwkv_mix[0], vec(rwkv_w0[0]), vec(rwkv_a0[0]), vec(rwkv_k_k[0]),
        vec(rwkv_k_a[0]), bf(rwkv_w_r[0]), bf(rwkv_w_k[0]), bf(rwkv_w_v[0]), bf(rwkv_w1[0]),
        bf(rwkv_w2[0]), bf(rwkv_a1[0]), bf(rwkv_a2[0]), bf(rwkv_g1[0]), bf(rwkv_g2[0]))
    b3 = lambda t: t.reshape(BATCH, TP, -1)
    mm, add, bonus, p_last = _scan_prep(b3(rkv), vec(rwkv_r_k[0]))
    o = _scan(mm, add, p_last)
    w_router = moe_router[0].astype(F32).T
    wr_hi = bf(w_router)
    wr_lo = bf(w_router - wr_hi.astype(F32))
    h, u, route, cnt = _rwkv_out(o, bonus, b3(g), b3(h), vec(rwkv_gn_w[0]), vec(rwkv_gn_b[0]),
                                 bf(rwkv_w_o[0]), vec(ffn_norm[1]),
                                 jnp.concatenate([wr_hi, wr_hi, wr_lo], axis=1))
    tab = _tile_tables(cnt)
    sorted_piece = _piece_index(tab, route)
    gate = jnp.swapaxes(route[:, 2 * TOP_K:3 * TOP_K, :], 1, 2).reshape(N_TOKENS, TOP_K)
    x_sorted = _row_scatter(u.reshape(-1, PIECE_WORDS), sorted_piece, N_COL_CHUNKS * MAX_SORTED_ROWS)
    y_sorted = _experts(tab, x_sorted, moe_w_gate[0], moe_w_up[0], moe_w_down[0])
    y_pair = _row_gather(y_sorted, sorted_piece).reshape(TOP_K, N_TOKEN_BLOCKS, N_COL_CHUNKS, TOKEN_BLOCK, PIECE_WORDS)
    out = _combine(h.reshape(N_TOKENS, D_MODEL), y_pair, gate)
    return out.reshape(BATCH, SEQ, D_MODEL)
```

```python
import functools

import jax
import jax.numpy as jnp
from jax import lax
from jax.experimental import pallas as pl
from jax.experimental.pallas import tpu as pltpu
from jax.experimental.pallas import tpu_sc as plsc

F32 = jnp.float32
BF16 = jnp.bfloat16

D_MODEL = 1024
BATCH = 2
SEQ = 8192
N_META = 16
BLOCK = 128
PAD = BLOCK - N_META
TP = SEQ + BLOCK
N_ROWS = BATCH * TP
N_BLOCKS = TP // BLOCK
META_BLOCK = N_BLOCKS - 1
HEAD_DIM = 64
N_Q_HEADS = 16
N_KV_HEADS = 4
Q_PER_KV = 4
QKV_DIM = (N_Q_HEADS + 2 * N_KV_HEADS) * HEAD_DIM
RWKV_HEADS = 16
RWKV_HEAD = 64
D_FF = 3584
N_EXPERTS = 8
NORM_EPS = 1e-5
GN_EPS = 64e-5
CHUNK = 64
N_CHUNKS = TP // CHUNK
META_CHUNK0 = SEQ // CHUNK
SCAN_CHUNKS = 2
HEADS_PER_GROUP = 4
GW = HEADS_PER_GROUP * RWKV_HEAD
N_GROUPS = RWKV_HEADS // HEADS_PER_GROUP
N_SCAN_IN = 6

FFN_ROW_TILE = 1664
PROJ_TILE = 320
FF_TILE = 512
N_TOKENS = BATCH * SEQ
TOP_K = 2
TOKEN_BLOCK = 512
N_TOKEN_BLOCKS = N_TOKENS // TOKEN_BLOCK
EXPERT_TILE = 1024
SUB_TILE = 256
MAX_EXPERT_TILES = TOP_K * N_TOKENS // EXPERT_TILE + N_EXPERTS
TILE_CAP = EXPERT_TILE + SUB_TILE
MAX_SORTED_ROWS = MAX_EXPERT_TILES * TILE_CAP
COL_CHUNK = 256
N_COL_CHUNKS = D_MODEL // COL_CHUNK
PIECE_WORDS = COL_CHUNK // 2
U32 = jnp.uint32
GATHER_ROWS = 128
PIECE_INDEX_BLOCKS = 4
VMEM_LIMIT = 56 * 1024 * 1024
NEG = -1e30

NT_DIMS = (((1,), (1,)), ((), ()))
TN_DIMS = (((0,), (0,)), ((), ()))


def _params(*sem):
    return pltpu.CompilerParams(dimension_semantics=sem, vmem_limit_bytes=VMEM_LIMIT)


def _rms(x, gain):
    return x * lax.rsqrt(jnp.mean(x * x, axis=-1, keepdims=True) + NORM_EPS) * gain


def _sigmoid(x):
    return 1.0 / (1.0 + jnp.exp(-x))


def _dot(a, b):
    return jnp.dot(a, b, preferred_element_type=F32)


def _pack_piece(x):
    return pltpu.pack_elementwise([x[:, :PIECE_WORDS], x[:, PIECE_WORDS:]], packed_dtype=BF16)


def _unpack_piece(w):
    return tuple(pltpu.unpack_elementwise(w, index=i, packed_dtype=BF16, unpacked_dtype=F32) for i in (0, 1))


def _qkv_kernel(h_ref, g_ref, w_ref, o_ref):
    u = _rms(h_ref[...], g_ref[...])
    o_ref[...] = _dot(u.astype(BF16), w_ref[...]).astype(BF16)


def _qkv(h, gain, w, tile):
    rows = h.shape[0]
    return pl.pallas_call(
        _qkv_kernel,
        grid=(rows // tile,),
        in_specs=[pl.BlockSpec((tile, D_MODEL), lambda i: (i, 0)),
                  pl.BlockSpec((1, D_MODEL), lambda i: (0, 0)),
                  pl.BlockSpec((D_MODEL, QKV_DIM), lambda i: (0, 0))],
        out_specs=pl.BlockSpec((tile, QKV_DIM), lambda i: (i, 0)),
        out_shape=jax.ShapeDtypeStruct((rows, QKV_DIM), BF16),
        compiler_params=_params("parallel"),
    )(h, gain, w)


def _attn_kernel(sink_ref, x_ref, hm_ref, q_ref, kc_ref, vc_ref, kp_ref, vp_ref, qm_ref, km_ref, vm_ref,
                 qg_ref, kg_ref, wo_ref, o_ref, o_scr):
    j = pl.program_id(1)
    n_band = 2 * BLOCK
    slots = 2 * N_META
    far = 4 * BLOCK
    is_real = j < META_BLOCK
    prev_off = jnp.where(jnp.logical_and(j >= 1, is_real), 0, far)
    cur_off = jnp.where(is_real, 0, far)
    rowi = lax.broadcasted_iota(jnp.int32, (BLOCK, n_band), 0)
    col = lax.broadcasted_iota(jnp.int32, (BLOCK, n_band), 1)
    band_ok = jnp.logical_or(jnp.logical_and(col < BLOCK, col > rowi + prev_off),
                             jnp.logical_and(col >= BLOCK, col - BLOCK + cur_off <= rowi))
    band_ok4 = jnp.concatenate([band_ok] * Q_PER_KV, axis=1)
    meta_off = jnp.where(is_real, -far, PAD)
    mrow = lax.broadcasted_iota(jnp.int32, (BLOCK, Q_PER_KV * slots), 0)
    mlane = lax.broadcasted_iota(jnp.int32, (BLOCK, Q_PER_KV * slots), 1)
    mslot = mlane % slots
    meta_ok4 = jnp.logical_and(mslot < N_META, mslot + meta_off <= mrow)
    meta_lane_head = mlane // slots

    groups = range(N_KV_HEADS)
    mask_bf = _group_masks().astype(BF16)
    rb = lax.broadcasted_iota(jnp.int32, (Q_PER_KV * n_band, GW), 0) // n_band
    lb = lax.broadcasted_iota(jnp.int32, (Q_PER_KV * n_band, GW), 1) // HEAD_DIM
    band_mask = jnp.where(rb == lb, 1.0, 0.0).astype(BF16)
    rm = lax.broadcasted_iota(jnp.int32, (Q_PER_KV * slots, GW), 0) // slots
    lm = lax.broadcasted_iota(jnp.int32, (Q_PER_KV * slots, GW), 1) // HEAD_DIM
    meta_mask = jnp.where(rm == lm, 1.0, 0.0).astype(BF16)
    sr = lax.broadcasted_iota(jnp.int32, (GW, GW), 0)
    sc = lax.broadcasted_iota(jnp.int32, (GW, GW), 1)
    lane_head = lax.broadcasted_iota(jnp.int32, (1, GW), 1) // HEAD_DIM
    inv_d = 1.0 / HEAD_DIM
    scale = HEAD_DIM ** -0.5

    real_rows = lax.broadcasted_iota(jnp.int32, (BLOCK, 1), 0) < jnp.where(is_real, BLOCK, 0)
    km, vm = km_ref[...], vm_ref[...]
    kcur = jnp.where(real_rows, kc_ref[0], km)
    vcur = jnp.where(real_rows, vc_ref[0], vm)
    kall = jnp.concatenate([km[PAD:], kp_ref[0], kcur], axis=0).astype(F32)
    vall = jnp.concatenate([vm[PAD:], vp_ref[0], vcur], axis=0)
    kss = _dot((kall * kall).astype(BF16), mask_bf)
    kn = (kall * lax.rsqrt(kss * inv_d + NORM_EPS) * kg_ref[...]).astype(BF16)
    q_all = jnp.where(real_rows, q_ref[0], qm_ref[...]).astype(F32)
    qn = []
    for g in groups:
        qg = q_all[:, g * GW:(g + 1) * GW]
        qss = _dot((qg * qg).astype(BF16), mask_bf)
        qn.append((qg * lax.rsqrt(qss * inv_d + NORM_EPS) * (qg_ref[...] * scale)).astype(BF16))
    sel = [jnp.where(sr == g * HEAD_DIM + sc % HEAD_DIM, 1.0, 0.0).astype(BF16) for g in groups]
    krep = [_dot(kn, sel[g]).astype(BF16) for g in groups]
    vrep = [_dot(vall, sel[g]).astype(BF16) for g in groups]
    unused = jnp.zeros((slots - N_META, GW), BF16)

    def band_bd(rep):
        return jnp.concatenate([rep[N_META:]] * Q_PER_KV, axis=0) * band_mask

    def meta_bd(rep):
        return jnp.concatenate([rep[:N_META], unused] * Q_PER_KV, axis=0) * meta_mask

    s_band = [jnp.where(band_ok4, lax.dot_general(qn[g], band_bd(krep[g]), NT_DIMS,
                                                  preferred_element_type=F32), NEG) for g in groups]
    s_meta = [jnp.where(meta_ok4, lax.dot_general(qn[g], meta_bd(krep[g]), NT_DIMS,
                                                  preferred_element_type=F32), NEG) for g in groups]
    rhs_band = [jnp.concatenate([band_bd(vrep[g]), band_mask], axis=1) for g in groups]
    rhs_meta = [jnp.concatenate([meta_bd(vrep[g]), meta_mask], axis=1) for g in groups]
    ov, sink_den = [], []
    for g in groups:
        parts = []
        sd = jnp.zeros((BLOCK, GW), F32)
        m_lanes = jnp.zeros((BLOCK, Q_PER_KV * slots), F32)
        for hh in range(Q_PER_KV):
            seg = s_band[g][:, hh * n_band:(hh + 1) * n_band]
            own = meta_lane_head == hh
            sink = sink_ref[g * Q_PER_KV + hh]
            m = jnp.maximum(jnp.max(seg, axis=-1, keepdims=True),
                            jnp.max(jnp.where(own, s_meta[g], NEG), axis=-1, keepdims=True))
            m = jnp.maximum(m, sink)
            parts.append(jnp.exp(seg - m).astype(BF16))
            m_lanes = jnp.where(own, m, m_lanes)
            sd = sd + jnp.exp(sink - m) * jnp.where(lane_head == hh, 1.0, 0.0)
        p_band = jnp.concatenate(parts, axis=1)
        p_meta = jnp.exp(s_meta[g] - m_lanes).astype(BF16)
        ov.append(_dot(p_band, rhs_band[g]) + _dot(p_meta, rhs_meta[g]))
        sink_den.append(sd)
    for g in groups:
        o_scr[:, g * GW:(g + 1) * GW] = ov[g][:, :GW] / (ov[g][:, GW:] + sink_den[g])
    h = jnp.where(real_rows, x_ref[0], hm_ref[...])
    o_ref[0] = h + _dot(o_scr[...].astype(BF16), wo_ref[...])


def _attention(x, h_meta, qkv, qkv_meta, sinks, q_gain, k_gain, w_o):
    kcol, vcol = N_Q_HEADS * HEAD_DIM // 256, N_Q_HEADS * HEAD_DIM // 256 + 1
    kvw = N_KV_HEADS * HEAD_DIM
    real = lambda j: jnp.minimum(j, META_BLOCK - 1)
    prev = lambda j: jnp.clip(j - 1, 0, META_BLOCK - 1)
    return pl.pallas_call(
        _attn_kernel,
        grid=(BATCH, N_BLOCKS),
        in_specs=[pl.BlockSpec(memory_space=pltpu.SMEM),
                  pl.BlockSpec((1, BLOCK, D_MODEL), lambda b, j: (b, real(j), 0)),
                  pl.BlockSpec((BLOCK, D_MODEL), lambda b, j: (0, 0)),
                  pl.BlockSpec((1, BLOCK, D_MODEL), lambda b, j: (b, real(j), 0)),
                  pl.BlockSpec((1, BLOCK, kvw), lambda b, j: (b, real(j), kcol)),
                  pl.BlockSpec((1, BLOCK, kvw), lambda b, j: (b, real(j), vcol)),
                  pl.BlockSpec((1, BLOCK, kvw), lambda b, j: (b, prev(j), kcol)),
                  pl.BlockSpec((1, BLOCK, kvw), lambda b, j: (b, prev(j), vcol)),
                  pl.BlockSpec((BLOCK, D_MODEL), lambda b, j: (0, 0)),
                  pl.BlockSpec((BLOCK, kvw), lambda b, j: (0, kcol)),
                  pl.BlockSpec((BLOCK, kvw), lambda b, j: (0, vcol)),
                  pl.BlockSpec((1, GW), lambda b, j: (0, 0)),
                  pl.BlockSpec((1, GW), lambda b, j: (0, 0)),
                  pl.BlockSpec((D_MODEL, D_MODEL), lambda b, j: (0, 0))],
        out_specs=pl.BlockSpec((1, BLOCK, D_MODEL), lambda b, j: (b, j, 0)),
        out_shape=jax.ShapeDtypeStruct((BATCH, TP, D_MODEL), F32),
        scratch_shapes=[pltpu.VMEM((BLOCK, D_MODEL), F32)],
        compiler_params=_params("parallel", "parallel"),
    )(sinks, x, h_meta, qkv, qkv, qkv, qkv, qkv, qkv_meta, qkv_meta, qkv_meta, q_gain, k_gain, w_o)


def _ffn_kernel(h_ref, g_ref, wg_ref, wu_ref, wd_ref, o_ref, u_scr, acc):
    f = pl.program_id(1)

    @pl.when(f == 0)
    def _():
        u_scr[...] = _rms(h_ref[...], g_ref[...]).astype(BF16)
        acc[...] = jnp.zeros_like(acc)

    u = u_scr[...]
    a = _dot(u, wg_ref[...])
    b = _dot(u, wu_ref[...])
    acc[...] += _dot((a * _sigmoid(a) * b).astype(BF16), wd_ref[...])

    @pl.when(f == pl.num_programs(1) - 1)
    def _():
        o_ref[...] = h_ref[...] + acc[...]


def _ffn(h, gain, wg, wu, wd):
    return pl.pallas_call(
        _ffn_kernel,
        grid=(N_ROWS // FFN_ROW_TILE, D_FF // FF_TILE),
        in_specs=[pl.BlockSpec((FFN_ROW_TILE, D_MODEL), lambda i, f: (i, 0)),
                  pl.BlockSpec((1, D_MODEL), lambda i, f: (0, 0)),
                  pl.BlockSpec((D_MODEL, FF_TILE), lambda i, f: (0, f)),
                  pl.BlockSpec((D_MODEL, FF_TILE), lambda i, f: (0, f)),
                  pl.BlockSpec((FF_TILE, D_MODEL), lambda i, f: (f, 0))],
        out_specs=pl.BlockSpec((FFN_ROW_TILE, D_MODEL), lambda i, f: (i, 0)),
        out_shape=jax.ShapeDtypeStruct((N_ROWS, D_MODEL), F32),
        scratch_shapes=[pltpu.VMEM((FFN_ROW_TILE, D_MODEL), BF16), pltpu.VMEM((FFN_ROW_TILE, D_MODEL), F32)],
        compiler_params=_params("parallel", "arbitrary"),
    )(h, gain, wg, wu, wd)


def _rwkv_proj_kernel(h_ref, hp_ref, g_ref, mix_ref, w0_ref, a0_ref, kk_ref, ka_ref,
                      wr_ref, wk_ref, wv_ref, w1_ref, w2_ref, a1_ref, a2_ref, g1_ref, g2_ref,
                      rkv_out, g_out):
    i = pl.program_id(0)
    tiles_per_batch = TP // PROJ_TILE
    r0 = (i % tiles_per_batch) * PROJ_TILE
    local = lax.broadcasted_iota(jnp.int32, (PROJ_TILE, 1), 0)
    lrow = local + r0
    gain = g_ref[...]
    is_pad = jnp.logical_and(lrow >= SEQ, lrow < SEQ + PAD)
    u = jnp.where(is_pad, 0.0, _rms(h_ref[...], gain))
    u_prev_tile = _rms(hp_ref[7:8, :], gain)
    xprev = pltpu.roll(u, 1, 0)
    xprev = jnp.where(local == 0, u_prev_tile, xprev)
    xprev = jnp.where(lrow == SEQ, 0.0, xprev)
    xx = xprev - u
    mix = mix_ref[...]
    lerp = lambda n: (u + xx * mix[n:n + 1, :]).astype(BF16)
    xr, xw, xk, xv, xa, xg = [lerp(n) for n in range(6)]
    r = _dot(xr, wr_ref[...])
    k = _dot(xk, wk_ref[...])
    v = _dot(xv, wv_ref[...])
    lw = _dot(jnp.tanh(_dot(xw, w1_ref[...])).astype(BF16), w2_ref[...])
    z = -(w0_ref[...] + lw)
    softplus = jnp.maximum(z, 0.0) + jnp.log(1.0 + jnp.exp(-jnp.abs(z)))
    w = -softplus - 0.5
    a = _sigmoid(a0_ref[...] + _dot(_dot(xa, a1_ref[...]).astype(BF16), a2_ref[...]))
    g = _dot(_sigmoid(_dot(xg, g1_ref[...])).astype(BF16), g2_ref[...])
    fields = (r, -jnp.exp(w),
              k * (1.0 + (a - 1.0) * ka_ref[...]), v, k * kk_ref[...], a)
    for n, val in enumerate(fields):
        rkv_out[:, n * D_MODEL:(n + 1) * D_MODEL] = val
    g_out[...] = g


def _rwkv_proj(h, gain, mix, w0, a0, k_k, k_a, w_r, w_k, w_v, w1, w2, a1, a2, g1, g2):
    tiles_per_batch = TP // PROJ_TILE
    rows8 = PROJ_TILE // 8

    def prev_map(i):
        b = i // tiles_per_batch
        first = (i % tiles_per_batch) == 0
        return (jnp.where(first, (b * TP + TP - 8) // 8, i * rows8 - 1), 0)

    row = pl.BlockSpec((PROJ_TILE, D_MODEL), lambda i: (i, 0))
    full = lambda a: pl.BlockSpec(a.shape, lambda i: (0,) * a.ndim)
    smalls = (gain, mix, w0, a0, k_k, k_a, w_r, w_k, w_v, w1, w2, a1, a2, g1, g2)
    return pl.pallas_call(
        _rwkv_proj_kernel,
        grid=(N_ROWS // PROJ_TILE,),
        in_specs=[row, pl.BlockSpec((8, D_MODEL), prev_map)] + [full(a) for a in smalls],
        out_specs=[pl.BlockSpec((PROJ_TILE, N_SCAN_IN * D_MODEL), lambda i: (i, 0)), row],
        out_shape=[jax.ShapeDtypeStruct((N_ROWS, N_SCAN_IN * D_MODEL), F32),
                   jax.ShapeDtypeStruct((N_ROWS, D_MODEL), F32)],
        compiler_params=_params("parallel"),
    )(h, h, *smalls)


def _group_masks():
    ri = lax.broadcasted_iota(jnp.int32, (GW, GW), 0) // RWKV_HEAD
    ci = lax.broadcasted_iota(jnp.int32, (GW, GW), 1) // RWKV_HEAD
    return jnp.where(ri == ci, 1.0, 0.0).astype(F32)


def _bd(x, mask):
    return jnp.concatenate([x.astype(BF16)] * HEADS_PER_GROUP, axis=0) * mask.astype(BF16)


def _diag_blocks(full, mask):
    m = full * mask
    n = RWKV_HEAD
    return (m[0:n] + m[n:2 * n]) + (m[2 * n:3 * n] + m[3 * n:4 * n])


def _head_sum(x, mask_bf):
    hi = x.astype(BF16)
    lo = (x - hi.astype(F32)).astype(BF16)
    return _dot(hi, mask_bf) + _dot(lo, mask_bf)


def _scan_prep_kernel(x_ref, rk_ref, mm_out, add_out, bonus_out, pl_out):
    L, D = CHUNK, D_MODEL
    units = [(b, g) for b in range(BATCH) for g in range(N_GROUPS)]
    un = range(len(units))
    mask = _group_masks()
    mask_bf = mask.astype(BF16)
    ri = lax.broadcasted_iota(jnp.int32, (L, GW), 0)
    ci = lax.broadcasted_iota(jnp.int32, (L, GW), 1) % RWKV_HEAD
    incl = ci <= ri
    strict = ci < ri
    eye = jnp.where(ci == ri, 1.0, 0.0).astype(F32)
    t_r = lax.broadcasted_iota(jnp.int32, (L, L), 0)
    t_c = lax.broadcasted_iota(jnp.int32, (L, L), 1)
    tril = jnp.where(t_c <= t_r, 1.0, 0.0).astype(BF16)
    rk_all = rk_ref[...]

    def field(b, n, g):
        return x_ref[b, :, n * D + g * GW:n * D + (g + 1) * GW]

    def put(ref, b, n, g, val):
        ref[b, :, n * D + g * GW:n * D + (g + 1) * GW] = val.astype(ref.dtype)

    at, rt, bt, kt, v, plast = [], [], [], [], [], []
    for b in range(BATCH):
        ld = x_ref[b, :, D:2 * D]
        hi = ld.astype(BF16)
        rest = ld - hi.astype(F32)
        mid = rest.astype(BF16)
        lo = (rest - mid.astype(F32)).astype(BF16)
        cs = _dot(tril, hi) + _dot(tril, mid) + _dot(tril, lo)
        p_all = jnp.exp(cs)
        pprev_all = jnp.exp(cs - ld)
        pinv_all = jnp.exp(-cs)
        pl_all = p_all[L - 1:L, :]
        pl_out[b, 0] = pl_all
        for g in range(N_GROUPS):
            sl = slice(g * GW, (g + 1) * GW)
            r, k, vv, kk, a = field(b, 0, g), field(b, 2, g), field(b, 3, g), field(b, 4, g), field(b, 5, g)
            nrm = jnp.sqrt(_dot((kk * kk).astype(BF16), mask_bf))
            kk = kk / jnp.maximum(nrm, 1e-12)
            at.append(-kk * pprev_all[:, sl])
            bt.append(kk * a * pinv_all[:, sl])
            rt.append(r * p_all[:, sl])
            kt.append(k * pinv_all[:, sl])
            v.append(vv)
            plast.append(pl_all[:, sl])
            put(bonus_out, b, 0, g, _dot((r * k * rk_all[:, sl]).astype(BF16), mask_bf) * vv)

    a_ab, a_ak, a_rb, a_rk = [], [], [], []
    for n in un:
        lhs = jnp.concatenate([at[n], rt[n]], axis=0).astype(BF16)
        rhs = jnp.concatenate([_bd(bt[n], mask), _bd(kt[n], mask)], axis=0)
        big = lax.dot_general(lhs, rhs, NT_DIMS, preferred_element_type=F32)
        a_ab.append(jnp.where(strict, big[:L, :GW], 0.0))
        a_ak.append(jnp.where(strict, big[:L, GW:], 0.0))
        a_rb.append(jnp.where(incl, big[L:, :GW], 0.0))
        a_rk.append(jnp.where(incl, big[L:, GW:], 0.0))

    x = [_dot(a_ab[n].astype(BF16), _bd(a_ab[n], mask)) for n in un]
    inv = [eye + a_ab[n] for n in un]
    for step in range(5):
        for n in un:
            rhs = _bd(x[n], mask)
            if step < 4:
                res = _dot(jnp.concatenate([x[n], inv[n]], axis=0).astype(BF16), rhs)
                x[n] = res[:L]
                inv[n] = inv[n] + res[L:]
            else:
                inv[n] = inv[n] + _dot(inv[n].astype(BF16), rhs)

    av = [_dot(jnp.concatenate([a_ak[n], a_rk[n]], axis=0).astype(BF16), _bd(v[n], mask)) for n in un]
    wu = [_dot(inv[n].astype(BF16), jnp.concatenate([_bd(at[n], mask), _bd(av[n][:L], mask)], axis=1))
          for n in un]
    aw = [_dot(a_rb[n].astype(BF16),
               jnp.concatenate([_bd(wu[n][:, :GW], mask), _bd(wu[n][:, GW:], mask)], axis=1))
          for n in un]
    for n, (b, g) in enumerate(units):
        put(mm_out, b, 0, g, rt[n] + aw[n][:, :GW])
        put(add_out, b, 0, g, av[n][L:] + aw[n][:, GW:])
        bh = (bt[n] * plast[n]).astype(BF16)
        kh = (kt[n] * plast[n]).astype(BF16)
        w_b, u0_b = wu[n][:, :GW].astype(BF16), wu[n][:, GW:].astype(BF16)
        gfull = lax.dot_general(bh, w_b, TN_DIMS, preferred_element_type=F32)
        put(mm_out, b, 1, g, _diag_blocks(gfull, mask))
        hfull = lax.dot_general(jnp.concatenate([u0_b, v[n].astype(BF16)], axis=0),
                                jnp.concatenate([bh, kh], axis=0), TN_DIMS, preferred_element_type=F32)
        put(add_out, b, 1, g, _diag_blocks(hfull, mask))


def _scan_prep(rkv, r_k):
    return pl.pallas_call(
        _scan_prep_kernel,
        grid=(N_CHUNKS,),
        in_specs=[pl.BlockSpec((BATCH, CHUNK, N_SCAN_IN * D_MODEL), lambda c: (0, c, 0)),
                  pl.BlockSpec((1, D_MODEL), lambda c: (0, 0))],
        out_specs=[pl.BlockSpec((BATCH, CHUNK, 2 * D_MODEL), lambda c: (0, c, 0)),
                   pl.BlockSpec((BATCH, CHUNK, 2 * D_MODEL), lambda c: (0, c, 0)),
                   pl.BlockSpec((BATCH, CHUNK, D_MODEL), lambda c: (0, c, 0)),
                   pl.BlockSpec((BATCH, 1, 1, D_MODEL), lambda c: (0, c, 0, 0))],
        out_shape=[jax.ShapeDtypeStruct((BATCH, TP, 2 * D_MODEL), BF16),
                   jax.ShapeDtypeStruct((BATCH, TP, 2 * D_MODEL), F32),
                   jax.ShapeDtypeStruct((BATCH, TP, D_MODEL), F32),
                   jax.ShapeDtypeStruct((BATCH, N_CHUNKS, 1, D_MODEL), F32)],
        compiler_params=_params("parallel"),
    )(rkv, r_k)


def _scan_kernel(mm_ref, add_ref, pl_ref, y_ref, s_scr):
    c = pl.program_id(0)
    D = D_MODEL

    @pl.when(c == 0)
    def _():
        s_scr[...] = jnp.zeros_like(s_scr)

    mask = _group_masks()
    units = [(b, slice(g * GW, (g + 1) * GW)) for b in range(BATCH) for g in range(N_GROUPS)]
    s = [s_scr[b, :, sl] for b, sl in units]
    for k in range(SCAN_CHUNKS):
        rows = slice(k * CHUNK, (k + 1) * CHUNK)
        field = lambda ref, b, n, sl, rows=rows: ref[b, rows, n * D + sl.start:n * D + sl.stop]
        o = [lax.dot_general(field(mm_ref, b, 0, sl), _bd(s[n], mask), NT_DIMS,
                             preferred_element_type=F32) + field(add_ref, b, 0, sl)
             for n, (b, sl) in enumerate(units)]
        sg = [lax.dot_general(s[n].astype(BF16), _bd(field(mm_ref, b, 1, sl), mask), NT_DIMS,
                              preferred_element_type=F32)
              for n, (b, sl) in enumerate(units)]
        s = [s[n] * pl_ref[b, k, :, sl] + sg[n] + field(add_ref, b, 1, sl)
             for n, (b, sl) in enumerate(units)]
        for n, (b, sl) in enumerate(units):
            y_ref[b, rows, sl] = o[n]
    for n, (b, sl) in enumerate(units):
        s_scr[b, :, sl] = s[n]


def _scan(mm, add, p_last):
    steps = N_CHUNKS // SCAN_CHUNKS
    first = META_CHUNK0 // SCAN_CHUNKS
    phys = lambda c: (c + first) % steps
    pair = pl.BlockSpec((BATCH, SCAN_CHUNKS * CHUNK, 2 * D_MODEL), lambda c: (0, phys(c), 0))
    return pl.pallas_call(
        _scan_kernel,
        grid=(steps,),
        in_specs=[pair, pair, pl.BlockSpec((BATCH, SCAN_CHUNKS, 1, D_MODEL), lambda c: (0, phys(c), 0, 0))],
        out_specs=pl.BlockSpec((BATCH, SCAN_CHUNKS * CHUNK, D_MODEL), lambda c: (0, phys(c), 0)),
        out_shape=jax.ShapeDtypeStruct((BATCH, TP, D_MODEL), F32),
        scratch_shapes=[pltpu.VMEM((BATCH, RWKV_HEAD, D_MODEL), F32)],
        compiler_params=_params("arbitrary"),
    )(mm, add, p_last)


def _rwkv_out_kernel(o_ref, bonus_ref, g_ref, h_ref, gw_ref, gb_ref, wo_ref, gain_ref, wr_ref,
                     h_out, u_out, route_out, cnt_out, carry):
    @pl.when(jnp.logical_and(pl.program_id(0) == 0, pl.program_id(1) == 0))
    def _():
        carry[...] = jnp.zeros_like(carry)

    mask_bf = _group_masks().astype(BF16)
    inv_n = 1.0 / RWKV_HEAD
    o, bonus, gate = o_ref[0], bonus_ref[0], g_ref[0]
    parts = []
    for g in range(N_GROUPS):
        sl = slice(g * GW, (g + 1) * GW)
        og = o[:, sl]
        d = og - _head_sum(og, mask_bf) * inv_n
        var = _dot((d * d).astype(BF16), mask_bf) * inv_n
        yn = d * lax.rsqrt(var + GN_EPS) * gw_ref[:, sl] + gb_ref[:, sl] + bonus[:, sl]
        parts.append((yn * gate[:, sl]).astype(BF16))
    h = h_ref[0] + _dot(jnp.concatenate(parts, axis=1), wo_ref[...])
    h_out[0] = h
    u = _rms(h, gain_ref[...])
    for c in range(N_COL_CHUNKS):
        u_out[0, 0, c] = _pack_piece(u[:, c * COL_CHUNK:(c + 1) * COL_CHUNK])
    u_hi = u.astype(BF16)
    u_lo = (u - u_hi.astype(F32)).astype(BF16)
    logits = lax.dot_general(wr_ref[...], jnp.concatenate([u_hi, u_lo, u_hi], axis=1), NT_DIMS,
                             preferred_element_type=F32)
    e = jnp.exp(logits - jnp.max(logits, axis=0, keepdims=True))
    probs = e / jnp.sum(e, axis=0, keepdims=True)
    idx = lax.broadcasted_iota(jnp.int32, probs.shape, 0).astype(F32)
    m1 = jnp.max(probs, axis=0, keepdims=True)
    i1 = jnp.min(jnp.where(probs == m1, idx, float(N_EXPERTS)), axis=0, keepdims=True)
    sel1 = idx == i1
    rest = jnp.where(sel1, -1.0, probs)
    m2 = jnp.max(rest, axis=0, keepdims=True)
    i2 = jnp.min(jnp.where(rest == m2, idx, float(N_EXPERTS)), axis=0, keepdims=True)
    sel2 = idx == i2
    onehot = jnp.where(jnp.logical_or(sel1, sel2), 1.0, 0.0).astype(F32)
    tr = lax.broadcasted_iota(jnp.int32, (TOKEN_BLOCK, TOKEN_BLOCK), 0)
    tc = lax.broadcasted_iota(jnp.int32, (TOKEN_BLOCK, TOKEN_BLOCK), 1)
    earlier = _dot(onehot.astype(BF16), jnp.where(tr < tc, 1.0, 0.0).astype(BF16)) + carry[...]
    rank1 = jnp.sum(jnp.where(sel1, earlier, 0.0), axis=0, keepdims=True)
    rank2 = jnp.sum(jnp.where(sel2, earlier, 0.0), axis=0, keepdims=True)
    den = m1 + m2
    fields = (i1, i2, rank1, rank2, m1 / den, m2 / den)
    route = jnp.zeros(probs.shape, F32)
    for n, val in enumerate(fields):
        route = jnp.where(idx == float(n), val, route)
    route_out[0] = route
    tile_cnt = jnp.sum(onehot, axis=1, keepdims=True)
    cnt_out[0] = tile_cnt
    carry[...] += tile_cnt


def _rwkv_out(o, bonus, g, h, gn_w, gn_b, w_o, gain, w_router):
    blocks = SEQ // TOKEN_BLOCK
    row = pl.BlockSpec((1, TOKEN_BLOCK, D_MODEL), lambda b, i: (b, i, 0))
    vec = pl.BlockSpec((1, D_MODEL), lambda b, i: (0, 0))
    return pl.pallas_call(
        _rwkv_out_kernel,
        grid=(BATCH, blocks),
        in_specs=[row, row, row, row, vec, vec,
                  pl.BlockSpec((D_MODEL, D_MODEL), lambda b, i: (0, 0)), vec,
                  pl.BlockSpec((N_EXPERTS, 3 * D_MODEL), lambda b, i: (0, 0))],
        out_specs=[row, pl.BlockSpec((1, 1, N_COL_CHUNKS, TOKEN_BLOCK, PIECE_WORDS), lambda b, i: (b, i, 0, 0, 0)),
                   pl.BlockSpec((1, N_EXPERTS, TOKEN_BLOCK), lambda b, i: (b, 0, i)),
                   pl.BlockSpec((1, N_EXPERTS, 1), lambda b, i: (b * blocks + i, 0, 0))],
        out_shape=[jax.ShapeDtypeStruct((BATCH, SEQ, D_MODEL), F32),
                   jax.ShapeDtypeStruct((BATCH, blocks, N_COL_CHUNKS, TOKEN_BLOCK, PIECE_WORDS), U32),
                   jax.ShapeDtypeStruct((BATCH, N_EXPERTS, SEQ), F32),
                   jax.ShapeDtypeStruct((BATCH * blocks, N_EXPERTS, 1), F32)],
        scratch_shapes=[pltpu.VMEM((N_EXPERTS, 1), F32)],
        compiler_params=_params("arbitrary", "arbitrary"),
    )(o, bonus, g, h, gn_w, gn_b, w_o, gain, w_router)


def _tile_tables(cnt):
    i32 = jnp.int32
    counts = jnp.sum(cnt.reshape(-1, N_EXPERTS).astype(i32), axis=0)
    tiles_e = jnp.where(counts > 0, jnp.maximum((counts - SUB_TILE + EXPERT_TILE - 1) // EXPERT_TILE, 1), 0)
    experts = jnp.arange(N_EXPERTS, dtype=i32)
    tile_end = jnp.sum(jnp.where(experts[None, :] <= experts[:, None], tiles_e[None, :], 0), axis=1)
    first_tile = tile_end - tiles_e
    tiles = jnp.arange(MAX_EXPERT_TILES, dtype=i32)[:, None]
    owns = jnp.logical_and(tiles >= first_tile[None, :], tiles < tile_end[None, :])
    pick = lambda table: jnp.sum(jnp.where(owns, table[None, :], 0), axis=1)
    is_last = tiles[:, 0] == pick(tile_end) - 1
    tile_rows = jnp.where(is_last, pick(counts) - (tiles[:, 0] - pick(first_tile)) * EXPERT_TILE, EXPERT_TILE)
    tile_rows = jnp.where(jnp.any(owns, axis=1), tile_rows, 0)
    return dict(first_tile=first_tile, tiles_e=tiles_e, tile_expert=pick(experts),
                n_used=tile_end[-1:], tile_rows=tile_rows)


def _piece_index_kernel(first_ref, ntile_ref, route_ref, o_ref):
    for j in range(PIECE_INDEX_BLOCKS):
        route = route_ref[0, :, j * TOKEN_BLOCK:(j + 1) * TOKEN_BLOCK]
        for slot in range(TOP_K):
            expert = route[slot:slot + 1, :].astype(jnp.int32)
            rank = route[TOP_K + slot:TOP_K + slot + 1, :].astype(jnp.int32)
            first = sum(jnp.where(expert == e, first_ref[e], 0) for e in range(N_EXPERTS))
            ntile = sum(jnp.where(expert == e, ntile_ref[e], 0) for e in range(N_EXPERTS))
            tile_in_group = jnp.minimum(rank // EXPERT_TILE, ntile - 1)
            row = rank - tile_in_group * EXPERT_TILE
            tile = first + tile_in_group
            for c in range(N_COL_CHUNKS):
                o_ref[slot, j, c:c + 1, :] = (tile * N_COL_CHUNKS + c) * TILE_CAP + row


def _piece_index(tab, route):
    steps = SEQ // (PIECE_INDEX_BLOCKS * TOKEN_BLOCK)
    grid_spec = pltpu.PrefetchScalarGridSpec(
        num_scalar_prefetch=2, grid=(BATCH, steps),
        in_specs=[pl.BlockSpec((1, N_EXPERTS, PIECE_INDEX_BLOCKS * TOKEN_BLOCK), lambda b, i, ft, nt: (b, 0, i))],
        out_specs=pl.BlockSpec((TOP_K, PIECE_INDEX_BLOCKS, N_COL_CHUNKS, TOKEN_BLOCK),
                               lambda b, i, ft, nt: (0, b * steps + i, 0, 0)))
    return pl.pallas_call(
        _piece_index_kernel, grid_spec=grid_spec,
        out_shape=jax.ShapeDtypeStruct((TOP_K, N_TOKEN_BLOCKS, N_COL_CHUNKS, TOKEN_BLOCK), jnp.int32),
        compiler_params=_params("parallel", "parallel"),
    )(tab["first_tile"], tab["tiles_e"], route).reshape(-1)


def _row_gather(x, indices):
    m = indices.shape[0]
    mesh = plsc.VectorSubcoreMesh(core_axis_name="c", subcore_axis_name="s")

    @pl.kernel(out_type=jax.ShapeDtypeStruct((m, PIECE_WORDS), x.dtype), mesh=mesh)
    def gather(x_hbm, i_hbm, o_hbm):
        def body(i_vmem, o_vmem):
            pltpu.sync_copy(x_hbm.at[i_vmem.at[0]], o_vmem)

        pltpu.emit_pipeline(
            body, grid=(m // GATHER_ROWS,),
            in_specs=[pl.BlockSpec((1, GATHER_ROWS), lambda i: (0, i))],
            out_specs=[pl.BlockSpec((GATHER_ROWS, PIECE_WORDS), lambda i: (i, 0))],
            core_axis_name=("c", "s"),
            dimension_semantics=(pltpu.PARALLEL,),
        )(i_hbm, o_hbm)

    return gather(x, indices.reshape(1, m))


def _row_scatter(x, indices, out_rows):
    m = indices.shape[0]
    x_blocks = x.shape[0] // GATHER_ROWS
    mesh = plsc.VectorSubcoreMesh(core_axis_name="c", subcore_axis_name="s")

    @pl.kernel(out_type=jax.ShapeDtypeStruct((out_rows, PIECE_WORDS), x.dtype), mesh=mesh)
    def scatter(x_hbm, i_hbm, o_hbm):
        def body(x_vmem, i_vmem):
            pltpu.sync_copy(x_vmem, o_hbm.at[i_vmem.at[0]])

        pltpu.emit_pipeline(
            body, grid=(m // GATHER_ROWS,),
            in_specs=[pl.BlockSpec((GATHER_ROWS, PIECE_WORDS), lambda i: (i % x_blocks, 0)),
                      pl.BlockSpec((1, GATHER_ROWS), lambda i: (0, i))],
            out_specs=[],
            core_axis_name=("c", "s"),
            dimension_semantics=(pltpu.PARALLEL,),
        )(x_hbm, i_hbm)

    return scatter(x, indices.reshape(1, m))


def _expert_kernel(te_ref, nu_ref, nr_ref, x_ref, wg_ref, wu_ref, wd_ref, y_ref, xb, acc):
    i = pl.program_id(0)
    f = pl.program_id(1)
    n_rows = nr_ref[i]
    n_sub = (n_rows + SUB_TILE - 1) // SUB_TILE
    subs = EXPERT_TILE // SUB_TILE

    @pl.when(n_sub > 0)
    def _():
        @pl.when(f == 0)
        def _():
            real = lax.broadcasted_iota(jnp.int32, (TILE_CAP, 1), 0) < n_rows
            for c in range(N_COL_CHUNKS):
                halves = _unpack_piece(x_ref[c * TILE_CAP:(c + 1) * TILE_CAP, :])
                for n, half in enumerate(halves):
                    lo = c * COL_CHUNK + n * PIECE_WORDS
                    xb[:, lo:lo + PIECE_WORDS] = jnp.where(real, half, 0.0).astype(BF16)
            acc[...] = jnp.zeros_like(acc)

        wg = wg_ref[0].astype(BF16)
        wu = wu_ref[0].astype(BF16)
        wd = wd_ref[0].astype(BF16)

        def block(rows):
            x = xb[rows, :]
            a = _dot(x, wg)
            b = _dot(x, wu)
            acc[rows, :] += _dot((a * _sigmoid(a) * b).astype(BF16), wd)

        @pl.when(n_sub >= subs)
        def _():
            block(slice(0, EXPERT_TILE))

        @pl.when(n_sub > subs)
        def _():
            block(slice(EXPERT_TILE, TILE_CAP))

        @pl.when(n_sub < subs)
        def _():
            for k in range(subs - 1):
                pl.when(k < n_sub)(functools.partial(block, slice(k * SUB_TILE, (k + 1) * SUB_TILE)))

        @pl.when(f == pl.num_programs(1) - 1)
        def _():
            for c in range(N_COL_CHUNKS):
                y_ref[c * TILE_CAP:(c + 1) * TILE_CAP, :] = _pack_piece(acc[:, c * COL_CHUNK:(c + 1) * COL_CHUNK])


def _experts(tab, x_sorted, wg, wu, wd):
    n_ff = D_FF // FF_TILE
    tile = lambda i, nu: jnp.minimum(i, nu[0] - 1)
    ff = lambda i, f, nu: jnp.where(i < nu[0], f, n_ff - 1)
    grid_spec = pltpu.PrefetchScalarGridSpec(
        num_scalar_prefetch=3, grid=(MAX_EXPERT_TILES, n_ff),
        in_specs=[pl.BlockSpec((N_COL_CHUNKS * TILE_CAP, PIECE_WORDS), lambda i, f, te, nu, ns: (tile(i, nu), 0)),
                  pl.BlockSpec((1, D_MODEL, FF_TILE),
                               lambda i, f, te, nu, ns: (te[tile(i, nu)], 0, ff(i, f, nu))),
                  pl.BlockSpec((1, D_MODEL, FF_TILE),
                               lambda i, f, te, nu, ns: (te[tile(i, nu)], 0, ff(i, f, nu))),
                  pl.BlockSpec((1, FF_TILE, D_MODEL),
                               lambda i, f, te, nu, ns: (te[tile(i, nu)], ff(i, f, nu), 0))],
        out_specs=pl.BlockSpec((N_COL_CHUNKS * TILE_CAP, PIECE_WORDS), lambda i, f, te, nu, ns: (tile(i, nu), 0)),
        scratch_shapes=[pltpu.VMEM((TILE_CAP, D_MODEL), BF16), pltpu.VMEM((TILE_CAP, D_MODEL), F32)])
    return pl.pallas_call(
        _expert_kernel, grid_spec=grid_spec,
        out_shape=jax.ShapeDtypeStruct((N_COL_CHUNKS * MAX_SORTED_ROWS, PIECE_WORDS), U32),
        compiler_params=_params("arbitrary", "arbitrary"),
    )(tab["tile_expert"], tab["n_used"], tab["tile_rows"], x_sorted, wg, wu, wd)


def _combine_kernel(h_ref, y_ref, g_ref, o_ref):
    g = g_ref[...]
    for c in range(N_COL_CHUNKS):
        first, second = _unpack_piece(y_ref[0, 0, c]), _unpack_piece(y_ref[1, 0, c])
        for n in range(2):
            cols = slice(c * COL_CHUNK + n * PIECE_WORDS, c * COL_CHUNK + (n + 1) * PIECE_WORDS)
            o_ref[:, cols] = h_ref[:, cols] + g[:, 0:1] * first[n] + g[:, 1:2] * second[n]


def _combine(h, y_pair, gate):
    row = pl.BlockSpec((TOKEN_BLOCK, D_MODEL), lambda i: (i, 0))
    return pl.pallas_call(
        _combine_kernel,
        grid=(N_TOKEN_BLOCKS,),
        in_specs=[row, pl.BlockSpec((TOP_K, 1, N_COL_CHUNKS, TOKEN_BLOCK, PIECE_WORDS), lambda i: (0, i, 0, 0, 0)),
                  pl.BlockSpec((TOKEN_BLOCK, TOP_K), lambda i: (i, 0))],
        out_specs=row,
        out_shape=jax.ShapeDtypeStruct((N_TOKENS, D_MODEL), F32),
        compiler_params=_params("parallel"),
    )(h, y_pair, gate)


def kernel(x, meta_tokens, mixer_norm, ffn_norm, attn_w_qkv, attn_q_norm, attn_k_norm, attn_sinks, attn_w_o, rwkv_mix, rwkv_w0, rwkv_w1, rwkv_w2, rwkv_a0, rwkv_a1, rwkv_a2, rwkv_g1, rwkv_g2, rwkv_k_k, rwkv_k_a, rwkv_r_k, rwkv_w_r, rwkv_w_k, rwkv_w_v, rwkv_w_o, rwkv_gn_w, rwkv_gn_b, ffn_w_gate, ffn_w_up, ffn_w_down, moe_router, moe_w_gate, moe_w_up, moe_w_down):
    bf = lambda a: a.astype(BF16)
    vec = lambda a: a.reshape(1, -1).astype(F32)
    h_meta = jnp.concatenate([jnp.zeros((PAD, D_MODEL), F32), meta_tokens.astype(F32)], axis=0)

    w_qkv = bf(attn_w_qkv[0])
    qkv = _qkv(x.reshape(N_TOKENS, D_MODEL), vec(mixer_norm[0]), w_qkv, TOKEN_BLOCK)
    qkv_meta = _qkv(h_meta, vec(mixer_norm[0]), w_qkv, BLOCK)
    h = _attention(x, h_meta, qkv.reshape(BATCH, SEQ, QKV_DIM), qkv_meta, attn_sinks[0].astype(F32),
                   jnp.tile(vec(attn_q_norm[0]), (1, Q_PER_KV)), jnp.tile(vec(attn_k_norm[0]), (1, N_KV_HEADS)),
                   bf(attn_w_o[0]))
    h = _ffn(h.reshape(N_ROWS, D_MODEL), vec(ffn_norm[0]), bf(ffn_w_gate[0]), bf(ffn_w_up[0]),
             bf(ffn_w_down[0]))

    rkv, g = _rwkv_proj(
        h, vec(mixer_norm[1]), rwkv_mix[0], vec(rwkv_w0[0]), vec(rwkv_a0[0]), vec(rwkv_k_k[0]),
        vec(rwkv_k_a[0]), bf(rwkv_w_r[0]), bf(rwkv_w_k[0]), bf(rwkv_w_v[0]), bf(rwkv_w1[0]),
        bf(rwkv_w2[0]), bf(rwkv_a1[0]), bf(rwkv_a2[0]), bf(rwkv_g1[0]), bf(rwkv_g2[0]))
    b3 = lambda t: t.reshape(BATCH, TP, -1)
    mm, add, bonus, p_last = _scan_prep(b3(rkv), vec(rwkv_r_k[0]))
    o = _scan(mm, add, p_last)
    w_router = moe_router[0].astype(F32).T
    wr_hi = bf(w_router)
    wr_lo = bf(w_router - wr_hi.astype(F32))
    h, u, route, cnt = _rwkv_out(o, bonus, b3(g), b3(h), vec(rwkv_gn_w[0]), vec(rwkv_gn_b[0]),
                                 bf(rwkv_w_o[0]), vec(ffn_norm[1]),
                                 jnp.concatenate([wr_hi, wr_hi, wr_lo], axis=1))
    tab = _tile_tables(cnt)
    sorted_piece = _piece_index(tab, route)
    gate = jnp.swapaxes(route[:, 2 * TOP_K:3 * TOP_K, :], 1, 2).reshape(N_TOKENS, TOP_K)
    x_sorted = _row_scatter(u.reshape(-1, PIECE_WORDS), sorted_piece, N_COL_CHUNKS * MAX_SORTED_ROWS)
    y_sorted = _experts(tab, x_sorted, moe_w_gate[0], moe_w_up[0], moe_w_down[0])
    y_pair = _row_gather(y_sorted, sorted_piece).reshape(TOP_K, N_TOKEN_BLOCKS, N_COL_CHUNKS, TOKEN_BLOCK, PIECE_WORDS)
    out = _combine(h.reshape(N_TOKENS, D_MODEL), y_pair, gate)
    return out.reshape(BATCH, SEQ, D_MODEL)
```

```python
import functools

import jax
import jax.numpy as jnp
from jax import lax
from jax.experimental import pallas as pl
from jax.experimental.pallas import tpu as pltpu
from jax.experimental.pallas import tpu_sc as plsc

F32 = jnp.float32
BF16 = jnp.bfloat16

D_MODEL = 1024
BATCH = 2
SEQ = 8192
N_META = 16
BLOCK = 128
PAD = BLOCK - N_META
TP = SEQ + BLOCK
N_ROWS = BATCH * TP
N_BLOCKS = TP // BLOCK
META_BLOCK = N_BLOCKS - 1
HEAD_DIM = 64
N_Q_HEADS = 16
N_KV_HEADS = 4
Q_PER_KV = 4
QKV_DIM = (N_Q_HEADS + 2 * N_KV_HEADS) * HEAD_DIM
RWKV_HEADS = 16
RWKV_HEAD = 64
D_FF = 3584
N_EXPERTS = 8
NORM_EPS = 1e-5
GN_EPS = 64e-5
CHUNK = 64
N_CHUNKS = TP // CHUNK
META_CHUNK0 = SEQ // CHUNK
SCAN_CHUNKS = 2
HEADS_PER_GROUP = 4
GW = HEADS_PER_GROUP * RWKV_HEAD
N_GROUPS = RWKV_HEADS // HEADS_PER_GROUP
N_SCAN_IN = 6

FFN_ROW_TILE = 1280
PROJ_TILE = 520
FF_TILE = 512
N_TOKENS = BATCH * SEQ
TOP_K = 2
TOKEN_BLOCK = 512
N_TOKEN_BLOCKS = N_TOKENS // TOKEN_BLOCK
EXPERT_TILE = 1024
SUB_TILE = 256
MAX_EXPERT_TILES = TOP_K * N_TOKENS // EXPERT_TILE + N_EXPERTS
TILE_CAP = EXPERT_TILE + SUB_TILE
MAX_SORTED_ROWS = MAX_EXPERT_TILES * TILE_CAP
COL_CHUNK = 256
N_COL_CHUNKS = D_MODEL // COL_CHUNK
PIECE_WORDS = COL_CHUNK // 2
U32 = jnp.uint32
GATHER_ROWS = 128
PIECE_INDEX_BLOCKS = 4
VMEM_LIMIT = 56 * 1024 * 1024
NEG = -1e30

NT_DIMS = (((1,), (1,)), ((), ()))
TN_DIMS = (((0,), (0,)), ((), ()))


def _params(*sem):
    return pltpu.CompilerParams(dimension_semantics=sem, vmem_limit_bytes=VMEM_LIMIT)


def _rms(x, gain):
    return x * lax.rsqrt(jnp.mean(x * x, axis=-1, keepdims=True) + NORM_EPS) * gain


def _sigmoid(x):
    return 1.0 / (1.0 + jnp.exp(-x))


def _dot(a, b):
    return jnp.dot(a, b, preferred_element_type=F32)


def _pack_piece(x):
    return pltpu.pack_elementwise([x[:, :PIECE_WORDS], x[:, PIECE_WORDS:]], packed_dtype=BF16)


def _unpack_piece(w):
    return tuple(pltpu.unpack_elementwise(w, index=i, packed_dtype=BF16, unpacked_dtype=F32) for i in (0, 1))


def _qkv_kernel(h_ref, g_ref, w_ref, o_ref):
    u = _rms(h_ref[...], g_ref[...])
    o_ref[...] = _dot(u.astype(BF16), w_ref[...]).astype(BF16)


def _qkv(h, gain, w, tile):
    rows = h.shape[0]
    return pl.pallas_call(
        _qkv_kernel,
        grid=(rows // tile,),
        in_specs=[pl.BlockSpec((tile, D_MODEL), lambda i: (i, 0)),
                  pl.BlockSpec((1, D_MODEL), lambda i: (0, 0)),
                  pl.BlockSpec((D_MODEL, QKV_DIM), lambda i: (0, 0))],
        out_specs=pl.BlockSpec((tile, QKV_DIM), lambda i: (i, 0)),
        out_shape=jax.ShapeDtypeStruct((rows, QKV_DIM), BF16),
        compiler_params=_params("parallel"),
    )(h, gain, w)


def _attn_kernel(sink_ref, x_ref, hm_ref, q_ref, kc_ref, vc_ref, kp_ref, vp_ref, qm_ref, km_ref, vm_ref,
                 qg_ref, kg_ref, wo_ref, o_ref, o_scr):
    j = pl.program_id(1)
    n_band = 2 * BLOCK
    slots = 2 * N_META
    far = 4 * BLOCK
    is_real = j < META_BLOCK
    prev_off = jnp.where(jnp.logical_and(j >= 1, is_real), 0, far)
    cur_off = jnp.where(is_real, 0, far)
    rowi = lax.broadcasted_iota(jnp.int32, (BLOCK, n_band), 0)
    col = lax.broadcasted_iota(jnp.int32, (BLOCK, n_band), 1)
    band_ok = jnp.logical_or(jnp.logical_and(col < BLOCK, col > rowi + prev_off),
                             jnp.logical_and(col >= BLOCK, col - BLOCK + cur_off <= rowi))
    band_ok4 = jnp.concatenate([band_ok] * Q_PER_KV, axis=1)
    meta_off = jnp.where(is_real, -far, PAD)
    mrow = lax.broadcasted_iota(jnp.int32, (BLOCK, Q_PER_KV * slots), 0)
    mlane = lax.broadcasted_iota(jnp.int32, (BLOCK, Q_PER_KV * slots), 1)
    mslot = mlane % slots
    meta_ok4 = jnp.logical_and(mslot < N_META, mslot + meta_off <= mrow)
    meta_lane_head = mlane // slots

    groups = range(N_KV_HEADS)
    mask_bf = _group_masks().astype(BF16)
    rb = lax.broadcasted_iota(jnp.int32, (Q_PER_KV * n_band, GW), 0) // n_band
    lb = lax.broadcasted_iota(jnp.int32, (Q_PER_KV * n_band, GW), 1) // HEAD_DIM
    band_mask = jnp.where(rb == lb, 1.0, 0.0).astype(BF16)
    rm = lax.broadcasted_iota(jnp.int32, (Q_PER_KV * slots, GW), 0) // slots
    lm = lax.broadcasted_iota(jnp.int32, (Q_PER_KV * slots, GW), 1) // HEAD_DIM
    meta_mask = jnp.where(rm == lm, 1.0, 0.0).astype(BF16)
    sr = lax.broadcasted_iota(jnp.int32, (GW, GW), 0)
    sc = lax.broadcasted_iota(jnp.int32, (GW, GW), 1)
    lane_head = lax.broadcasted_iota(jnp.int32, (1, GW), 1) // HEAD_DIM
    inv_d = 1.0 / HEAD_DIM
    scale = HEAD_DIM ** -0.5

    real_rows = lax.broadcasted_iota(jnp.int32, (BLOCK, 1), 0) < jnp.where(is_real, BLOCK, 0)
    km, vm = km_ref[...], vm_ref[...]
    kcur = jnp.where(real_rows, kc_ref[0], km)
    vcur = jnp.where(real_rows, vc_ref[0], vm)
    kall = jnp.concatenate([km[PAD:], kp_ref[0], kcur], axis=0).astype(F32)
    vall = jnp.concatenate([vm[PAD:], vp_ref[0], vcur], axis=0)
    kss = _dot((kall * kall).astype(BF16), mask_bf)
    kn = (kall * lax.rsqrt(kss * inv_d + NORM_EPS) * kg_ref[...]).astype(BF16)
    q_all = jnp.where(real_rows, q_ref[0], qm_ref[...]).astype(F32)
    qn = []
    for g in groups:
        qg = q_all[:, g * GW:(g + 1) * GW]
        qss = _dot((qg * qg).astype(BF16), mask_bf)
        qn.append((qg * lax.rsqrt(qss * inv_d + NORM_EPS) * (qg_ref[...] * scale)).astype(BF16))
    sel = [jnp.where(sr == g * HEAD_DIM + sc % HEAD_DIM, 1.0, 0.0).astype(BF16) for g in groups]
    krep = [_dot(kn, sel[g]).astype(BF16) for g in groups]
    vrep = [_dot(vall, sel[g]).astype(BF16) for g in groups]
    unused = jnp.zeros((slots - N_META, GW), BF16)

    def band_bd(rep):
        return jnp.concatenate([rep[N_META:]] * Q_PER_KV, axis=0) * band_mask

    def meta_bd(rep):
        return jnp.concatenate([rep[:N_META], unused] * Q_PER_KV, axis=0) * meta_mask

    s_band = [jnp.where(band_ok4, lax.dot_general(qn[g], band_bd(krep[g]), NT_DIMS,
                                                  preferred_element_type=F32), NEG) for g in groups]
    s_meta = [jnp.where(meta_ok4, lax.dot_general(qn[g], meta_bd(krep[g]), NT_DIMS,
                                                  preferred_element_type=F32), NEG) for g in groups]
    rhs_band = [jnp.concatenate([band_bd(vrep[g]), band_mask], axis=1) for g in groups]
    rhs_meta = [jnp.concatenate([meta_bd(vrep[g]), meta_mask], axis=1) for g in groups]
    ov, sink_den = [], []
    for g in groups:
        parts = []
        sd = jnp.zeros((BLOCK, GW), F32)
        m_lanes = jnp.zeros((BLOCK, Q_PER_KV * slots), F32)
        for hh in range(Q_PER_KV):
            seg = s_band[g][:, hh * n_band:(hh + 1) * n_band]
            own = meta_lane_head == hh
            sink = sink_ref[g * Q_PER_KV + hh]
            m = jnp.maximum(jnp.max(seg, axis=-1, keepdims=True),
                            jnp.max(jnp.where(own, s_meta[g], NEG), axis=-1, keepdims=True))
            m = jnp.maximum(m, sink)
            parts.append(jnp.exp(seg - m).astype(BF16))
            m_lanes = jnp.where(own, m, m_lanes)
            sd = sd + jnp.exp(sink - m) * jnp.where(lane_head == hh, 1.0, 0.0)
        p_band = jnp.concatenate(parts, axis=1)
        p_meta = jnp.exp(s_meta[g] - m_lanes).astype(BF16)
        ov.append(_dot(p_band, rhs_band[g]) + _dot(p_meta, rhs_meta[g]))
        sink_den.append(sd)
    for g in groups:
        o_scr[:, g * GW:(g + 1) * GW] = ov[g][:, :GW] / (ov[g][:, GW:] + sink_den[g])
    h = jnp.where(real_rows, x_ref[0], hm_ref[...])
    o_ref[0] = h + _dot(o_scr[...].astype(BF16), wo_ref[...])


def _attention(x, h_meta, qkv, qkv_meta, sinks, q_gain, k_gain, w_o):
    kcol, vcol = N_Q_HEADS * HEAD_DIM // 256, N_Q_HEADS * HEAD_DIM // 256 + 1
    kvw = N_KV_HEADS * HEAD_DIM
    real = lambda j: jnp.minimum(j, META_BLOCK - 1)
    prev = lambda j: jnp.clip(j - 1, 0, META_BLOCK - 1)
    return pl.pallas_call(
        _attn_kernel,
        grid=(BATCH, N_BLOCKS),
        in_specs=[pl.BlockSpec(memory_space=pltpu.SMEM),
                  pl.BlockSpec((1, BLOCK, D_MODEL), lambda b, j: (b, real(j), 0)),
                  pl.BlockSpec((BLOCK, D_MODEL), lambda b, j: (0, 0)),
                  pl.BlockSpec((1, BLOCK, D_MODEL), lambda b, j: (b, real(j), 0)),
                  pl.BlockSpec((1, BLOCK, kvw), lambda b, j: (b, real(j), kcol)),
                  pl.BlockSpec((1, BLOCK, kvw), lambda b, j: (b, real(j), vcol)),
                  pl.BlockSpec((1, BLOCK, kvw), lambda b, j: (b, prev(j), kcol)),
                  pl.BlockSpec((1, BLOCK, kvw), lambda b, j: (b, prev(j), vcol)),
                  pl.BlockSpec((BLOCK, D_MODEL), lambda b, j: (0, 0)),
                  pl.BlockSpec((BLOCK, kvw), lambda b, j: (0, kcol)),
                  pl.BlockSpec((BLOCK, kvw), lambda b, j: (0, vcol)),
                  pl.BlockSpec((1, GW), lambda b, j: (0, 0)),
                  pl.BlockSpec((1, GW), lambda b, j: (0, 0)),
                  pl.BlockSpec((D_MODEL, D_MODEL), lambda b, j: (0, 0))],
        out_specs=pl.BlockSpec((1, BLOCK, D_MODEL), lambda b, j: (b, j, 0)),
        out_shape=jax.ShapeDtypeStruct((BATCH, TP, D_MODEL), F32),
        scratch_shapes=[pltpu.VMEM((BLOCK, D_MODEL), F32)],
        compiler_params=_params("parallel", "parallel"),
    )(sinks, x, h_meta, qkv, qkv, qkv, qkv, qkv, qkv_meta, qkv_meta, qkv_meta, q_gain, k_gain, w_o)


def _ffn_kernel(h_ref, g_ref, wg_ref, wu_ref, wd_ref, o_ref, u_scr, acc):
    f = pl.program_id(1)

    @pl.when(f == 0)
    def _():
        u_scr[...] = _rms(h_ref[...], g_ref[...]).astype(BF16)
        acc[...] = jnp.zeros_like(acc)

    u = u_scr[...]
    a = _dot(u, wg_ref[...].astype(BF16))
    b = _dot(u, wu_ref[...].astype(BF16))
    acc[...] += _dot((a * _sigmoid(a) * b).astype(BF16), wd_ref[...].astype(BF16))

    @pl.when(f == pl.num_programs(1) - 1)
    def _():
        o_ref[...] = h_ref[...] + acc[...]


def _ffn(h, gain, wg, wu, wd):
    return pl.pallas_call(
        _ffn_kernel,
        grid=(N_ROWS // FFN_ROW_TILE, D_FF // FF_TILE),
        in_specs=[pl.BlockSpec((FFN_ROW_TILE, D_MODEL), lambda i, f: (i, 0)),
                  pl.BlockSpec((1, D_MODEL), lambda i, f: (0, 0)),
                  pl.BlockSpec((D_MODEL, FF_TILE), lambda i, f: (0, f)),
                  pl.BlockSpec((D_MODEL, FF_TILE), lambda i, f: (0, f)),
                  pl.BlockSpec((FF_TILE, D_MODEL), lambda i, f: (f, 0))],
        out_specs=pl.BlockSpec((FFN_ROW_TILE, D_MODEL), lambda i, f: (i, 0)),
        out_shape=jax.ShapeDtypeStruct((N_ROWS, D_MODEL), F32),
        scratch_shapes=[pltpu.VMEM((FFN_ROW_TILE, D_MODEL), BF16), pltpu.VMEM((FFN_ROW_TILE, D_MODEL), F32)],
        compiler_params=_params("parallel", "arbitrary"),
    )(h, gain, wg, wu, wd)


def _rwkv_proj_kernel(h_ref, hp_ref, g_ref, mix_ref, w0_ref, a0_ref, kk_ref, ka_ref,
                      wr_ref, wk_ref, wv_ref, w1_ref, w2_ref, a1_ref, a2_ref, g1_ref, g2_ref,
                      rkv_out, g_out):
    i = pl.program_id(0)
    tiles_per_batch = TP // PROJ_TILE
    r0 = (i % tiles_per_batch) * PROJ_TILE
    local = lax.broadcasted_iota(jnp.int32, (PROJ_TILE, 1), 0)
    lrow = local + r0
    gain = g_ref[...]
    is_pad = jnp.logical_and(lrow >= SEQ, lrow < SEQ + PAD)
    u = jnp.where(is_pad, 0.0, _rms(h_ref[...], gain))
    u_prev_tile = _rms(hp_ref[7:8, :], gain)
    xprev = pltpu.roll(u, 1, 0)
    xprev = jnp.where(local == 0, u_prev_tile, xprev)
    xprev = jnp.where(lrow == SEQ, 0.0, xprev)
    xx = xprev - u
    mix = mix_ref[...]
    lerp = lambda n: (u + xx * mix[n:n + 1, :]).astype(BF16)
    xr, xw, xk, xv, xa, xg = [lerp(n) for n in range(6)]
    r = _dot(xr, wr_ref[...])
    k = _dot(xk, wk_ref[...])
    v = _dot(xv, wv_ref[...])
    lw = _dot(jnp.tanh(_dot(xw, w1_ref[...])).astype(BF16), w2_ref[...])
    z = -(w0_ref[...] + lw)
    softplus = jnp.maximum(z, 0.0) + jnp.log(1.0 + jnp.exp(-jnp.abs(z)))
    w = -softplus - 0.5
    a = _sigmoid(a0_ref[...] + _dot(_dot(xa, a1_ref[...]).astype(BF16), a2_ref[...]))
    g = _dot(_sigmoid(_dot(xg, g1_ref[...])).astype(BF16), g2_ref[...])
    fields = (r, -jnp.exp(w),
              k * (1.0 + (a - 1.0) * ka_ref[...]), v, k * kk_ref[...], a)
    for n, val in enumerate(fields):
        rkv_out[:, n * D_MODEL:(n + 1) * D_MODEL] = val
    g_out[...] = g


def _rwkv_proj(h, gain, mix, w0, a0, k_k, k_a, w_r, w_k, w_v, w1, w2, a1, a2, g1, g2):
    tiles_per_batch = TP // PROJ_TILE
    rows8 = PROJ_TILE // 8

    def prev_map(i):
        b = i // tiles_per_batch
        first = (i % tiles_per_batch) == 0
        return (jnp.where(first, (b * TP + TP - 8) // 8, i * rows8 - 1), 0)

    row = pl.BlockSpec((PROJ_TILE, D_MODEL), lambda i: (i, 0))
    full = lambda a: pl.BlockSpec(a.shape, lambda i: (0,) * a.ndim)
    smalls = (gain, mix, w0, a0, k_k, k_a, w_r, w_k, w_v, w1, w2, a1, a2, g1, g2)
    return pl.pallas_call(
        _rwkv_proj_kernel,
        grid=(N_ROWS // PROJ_TILE,),
        in_specs=[row, pl.BlockSpec((8, D_MODEL), prev_map)] + [full(a) for a in smalls],
        out_specs=[pl.BlockSpec((PROJ_TILE, N_SCAN_IN * D_MODEL), lambda i: (i, 0)), row],
        out_shape=[jax.ShapeDtypeStruct((N_ROWS, N_SCAN_IN * D_MODEL), F32),
                   jax.ShapeDtypeStruct((N_ROWS, D_MODEL), F32)],
        compiler_params=_params("parallel"),
    )(h, h, *smalls)


def _group_masks():
    ri = lax.broadcasted_iota(jnp.int32, (GW, GW), 0) // RWKV_HEAD
    ci = lax.broadcasted_iota(jnp.int32, (GW, GW), 1) // RWKV_HEAD
    return jnp.where(ri == ci, 1.0, 0.0).astype(F32)


def _bd(x, mask):
    return jnp.concatenate([x.astype(BF16)] * HEADS_PER_GROUP, axis=0) * mask.astype(BF16)


def _diag_blocks(full, mask):
    m = full * mask
    n = RWKV_HEAD
    return (m[0:n] + m[n:2 * n]) + (m[2 * n:3 * n] + m[3 * n:4 * n])


def _head_sum(x, mask_bf):
    hi = x.astype(BF16)
    lo = (x - hi.astype(F32)).astype(BF16)
    return _dot(hi, mask_bf) + _dot(lo, mask_bf)


def _scan_prep_kernel(x_ref, rk_ref, mm_out, add_out, bonus_out, pl_out):
    L, D = CHUNK, D_MODEL
    units = [(b, g) for b in range(BATCH) for g in range(N_GROUPS)]
    un = range(len(units))
    mask = _group_masks()
    mask_bf = mask.astype(BF16)
    ri = lax.broadcasted_iota(jnp.int32, (L, GW), 0)
    ci = lax.broadcasted_iota(jnp.int32, (L, GW), 1) % RWKV_HEAD
    incl = ci <= ri
    strict = ci < ri
    eye = jnp.where(ci == ri, 1.0, 0.0).astype(F32)
    t_r = lax.broadcasted_iota(jnp.int32, (L, L), 0)
    t_c = lax.broadcasted_iota(jnp.int32, (L, L), 1)
    tril = jnp.where(t_c <= t_r, 1.0, 0.0).astype(BF16)
    rk_all = rk_ref[...]

    def field(b, n, g):
        return x_ref[b, :, n * D + g * GW:n * D + (g + 1) * GW]

    def put(ref, b, n, g, val):
        ref[b, :, n * D + g * GW:n * D + (g + 1) * GW] = val.astype(ref.dtype)

    at, rt, bt, kt, v, plast = [], [], [], [], [], []
    for b in range(BATCH):
        ld = x_ref[b, :, D:2 * D]
        hi = ld.astype(BF16)
        rest = ld - hi.astype(F32)
        mid = rest.astype(BF16)
        lo = (rest - mid.astype(F32)).astype(BF16)
        cs = _dot(tril, hi) + _dot(tril, mid) + _dot(tril, lo)
        p_all = jnp.exp(cs)
        pprev_all = jnp.exp(cs - ld)
        pinv_all = jnp.exp(-cs)
        pl_all = p_all[L - 1:L, :]
        pl_out[b, 0] = pl_all
        for g in range(N_GROUPS):
            sl = slice(g * GW, (g + 1) * GW)
            r, k, vv, kk, a = field(b, 0, g), field(b, 2, g), field(b, 3, g), field(b, 4, g), field(b, 5, g)
            nrm = jnp.sqrt(_dot((kk * kk).astype(BF16), mask_bf))
            kk = kk / jnp.maximum(nrm, 1e-12)
            at.append(-kk * pprev_all[:, sl])
            bt.append(kk * a * pinv_all[:, sl])
            rt.append(r * p_all[:, sl])
            kt.append(k * pinv_all[:, sl])
            v.append(vv)
            plast.append(pl_all[:, sl])
            put(bonus_out, b, 0, g, _dot((r * k * rk_all[:, sl]).astype(BF16), mask_bf) * vv)

    a_ab, a_ak, a_rb, a_rk = [], [], [], []
    for n in un:
        lhs = jnp.concatenate([at[n], rt[n]], axis=0).astype(BF16)
        rhs = jnp.concatenate([_bd(bt[n], mask), _bd(kt[n], mask)], axis=0)
        big = lax.dot_general(lhs, rhs, NT_DIMS, preferred_element_type=F32)
        a_ab.append(jnp.where(strict, big[:L, :GW], 0.0))
        a_ak.append(jnp.where(strict, big[:L, GW:], 0.0))
        a_rb.append(jnp.where(incl, big[L:, :GW], 0.0))
        a_rk.append(jnp.where(incl, big[L:, GW:], 0.0))

    x = [_dot(a_ab[n].astype(BF16), _bd(a_ab[n], mask)) for n in un]
    inv = [eye + a_ab[n] for n in un]
    for step in range(5):
        for n in un:
            rhs = _bd(x[n], mask)
            if step < 4:
                res = _dot(jnp.concatenate([x[n], inv[n]], axis=0).astype(BF16), rhs)
                x[n] = res[:L]
                inv[n] = inv[n] + res[L:]
            else:
                inv[n] = inv[n] + _dot(inv[n].astype(BF16), rhs)

    av = [_dot(jnp.concatenate([a_ak[n], a_rk[n]], axis=0).astype(BF16), _bd(v[n], mask)) for n in un]
    wu = [_dot(inv[n].astype(BF16), jnp.concatenate([_bd(at[n], mask), _bd(av[n][:L], mask)], axis=1))
          for n in un]
    aw = [_dot(a_rb[n].astype(BF16),
               jnp.concatenate([_bd(wu[n][:, :GW], mask), _bd(wu[n][:, GW:], mask)], axis=1))
          for n in un]
    for n, (b, g) in enumerate(units):
        put(mm_out, b, 0, g, rt[n] + aw[n][:, :GW])
        put(add_out, b, 0, g, av[n][L:] + aw[n][:, GW:])
        bh = (bt[n] * plast[n]).astype(BF16)
        kh = (kt[n] * plast[n]).astype(BF16)
        w_b, u0_b = wu[n][:, :GW].astype(BF16), wu[n][:, GW:].astype(BF16)
        gfull = lax.dot_general(bh, w_b, TN_DIMS, preferred_element_type=F32)
        put(mm_out, b, 1, g, _diag_blocks(gfull, mask))
        hfull = lax.dot_general(jnp.concatenate([u0_b, v[n].astype(BF16)], axis=0),
                                jnp.concatenate([bh, kh], axis=0), TN_DIMS, preferred_element_type=F32)
        put(add_out, b, 1, g, _diag_blocks(hfull, mask))


def _scan_prep(rkv, r_k):
    return pl.pallas_call(
        _scan_prep_kernel,
        grid=(N_CHUNKS,),
        in_specs=[pl.BlockSpec((BATCH, CHUNK, N_SCAN_IN * D_MODEL), lambda c: (0, c, 0)),
                  pl.BlockSpec((1, D_MODEL), lambda c: (0, 0))],
        out_specs=[pl.BlockSpec((BATCH, CHUNK, 2 * D_MODEL), lambda c: (0, c, 0)),
                   pl.BlockSpec((BATCH, CHUNK, 2 * D_MODEL), lambda c: (0, c, 0)),
                   pl.BlockSpec((BATCH, CHUNK, D_MODEL), lambda c: (0, c, 0)),
                   pl.BlockSpec((BATCH, 1, 1, D_MODEL), lambda c: (0, c, 0, 0))],
        out_shape=[jax.ShapeDtypeStruct((BATCH, TP, 2 * D_MODEL), BF16),
                   jax.ShapeDtypeStruct((BATCH, TP, 2 * D_MODEL), F32),
                   jax.ShapeDtypeStruct((BATCH, TP, D_MODEL), F32),
                   jax.ShapeDtypeStruct((BATCH, N_CHUNKS, 1, D_MODEL), F32)],
        compiler_params=_params("parallel"),
    )(rkv, r_k)


def _scan_kernel(mm_ref, add_ref, pl_ref, y_ref, s_scr):
    c = pl.program_id(0)
    D = D_MODEL

    @pl.when(c == 0)
    def _():
        s_scr[...] = jnp.zeros_like(s_scr)

    mask = _group_masks()
    units = [(b, slice(g * GW, (g + 1) * GW)) for b in range(BATCH) for g in range(N_GROUPS)]
    s = [s_scr[b, :, sl] for b, sl in units]
    for k in range(SCAN_CHUNKS):
        rows = slice(k * CHUNK, (k + 1) * CHUNK)
        field = lambda ref, b, n, sl, rows=rows: ref[b, rows, n * D + sl.start:n * D + sl.stop]
        o = [lax.dot_general(field(mm_ref, b, 0, sl), _bd(s[n], mask), NT_DIMS,
                             preferred_element_type=F32) + field(add_ref, b, 0, sl)
             for n, (b, sl) in enumerate(units)]
        sg = [lax.dot_general(s[n].astype(BF16), _bd(field(mm_ref, b, 1, sl), mask), NT_DIMS,
                              preferred_element_type=F32)
              for n, (b, sl) in enumerate(units)]
        s = [s[n] * pl_ref[b, k, :, sl] + sg[n] + field(add_ref, b, 1, sl)
             for n, (b, sl) in enumerate(units)]
        for n, (b, sl) in enumerate(units):
            y_ref[b, rows, sl] = o[n]
    for n, (b, sl) in enumerate(units):
        s_scr[b, :, sl] = s[n]


def _scan(mm, add, p_last):
    steps = N_CHUNKS // SCAN_CHUNKS
    first = META_CHUNK0 // SCAN_CHUNKS
    phys = lambda c: (c + first) % steps
    pair = pl.BlockSpec((BATCH, SCAN_CHUNKS * CHUNK, 2 * D_MODEL), lambda c: (0, phys(c), 0))
    return pl.pallas_call(
        _scan_kernel,
        grid=(steps,),
        in_specs=[pair, pair, pl.BlockSpec((BATCH, SCAN_CHUNKS, 1, D_MODEL), lambda c: (0, phys(c), 0, 0))],
        out_specs=pl.BlockSpec((BATCH, SCAN_CHUNKS * CHUNK, D_MODEL), lambda c: (0, phys(c), 0)),
        out_shape=jax.ShapeDtypeStruct((BATCH, TP, D_MODEL), F32),
        scratch_shapes=[pltpu.VMEM((BATCH, RWKV_HEAD, D_MODEL), F32)],
        compiler_params=_params("arbitrary"),
    )(mm, add, p_last)


def _rwkv_out_kernel(o_ref, bonus_ref, g_ref, h_ref, gw_ref, gb_ref, wo_ref, gain_ref, wr_ref,
                     h_out, u_out, route_out, cnt_out, carry):
    @pl.when(jnp.logical_and(pl.program_id(0) == 0, pl.program_id(1) == 0))
    def _():
        carry[...] = jnp.zeros_like(carry)

    mask_bf = _group_masks().astype(BF16)
    inv_n = 1.0 / RWKV_HEAD
    o, bonus, gate = o_ref[0], bonus_ref[0], g_ref[0]
    parts = []
    for g in range(N_GROUPS):
        sl = slice(g * GW, (g + 1) * GW)
        og = o[:, sl]
        d = og - _head_sum(og, mask_bf) * inv_n
        var = _dot((d * d).astype(BF16), mask_bf) * inv_n
        yn = d * lax.rsqrt(var + GN_EPS) * gw_ref[:, sl] + gb_ref[:, sl] + bonus[:, sl]
        parts.append((yn * gate[:, sl]).astype(BF16))
    h = h_ref[0] + _dot(jnp.concatenate(parts, axis=1), wo_ref[...])
    h_out[0] = h
    u = _rms(h, gain_ref[...])
    for c in range(N_COL_CHUNKS):
        u_out[0, 0, c] = _pack_piece(u[:, c * COL_CHUNK:(c + 1) * COL_CHUNK])
    u_hi = u.astype(BF16)
    u_lo = (u - u_hi.astype(F32)).astype(BF16)
    logits = lax.dot_general(wr_ref[...], jnp.concatenate([u_hi, u_lo, u_hi], axis=1), NT_DIMS,
                             preferred_element_type=F32)
    e = jnp.exp(logits - jnp.max(logits, axis=0, keepdims=True))
    probs = e / jnp.sum(e, axis=0, keepdims=True)
    idx = lax.broadcasted_iota(jnp.int32, probs.shape, 0).astype(F32)
    m1 = jnp.max(probs, axis=0, keepdims=True)
    i1 = jnp.min(jnp.where(probs == m1, idx, float(N_EXPERTS)), axis=0, keepdims=True)
    sel1 = idx == i1
    rest = jnp.where(sel1, -1.0, probs)
    m2 = jnp.max(rest, axis=0, keepdims=True)
    i2 = jnp.min(jnp.where(rest == m2, idx, float(N_EXPERTS)), axis=0, keepdims=True)
    sel2 = idx == i2
    onehot = jnp.where(jnp.logical_or(sel1, sel2), 1.0, 0.0).astype(F32)
    tr = lax.broadcasted_iota(jnp.int32, (TOKEN_BLOCK, TOKEN_BLOCK), 0)
    tc = lax.broadcasted_iota(jnp.int32, (TOKEN_BLOCK, TOKEN_BLOCK), 1)
    earlier = _dot(onehot.astype(BF16), jnp.where(tr < tc, 1.0, 0.0).astype(BF16)) + carry[...]
    rank1 = jnp.sum(jnp.where(sel1, earlier, 0.0), axis=0, keepdims=True)
    rank2 = jnp.sum(jnp.where(sel2, earlier, 0.0), axis=0, keepdims=True)
    den = m1 + m2
    fields = (i1, i2, rank1, rank2, m1 / den, m2 / den)
    route = jnp.zeros(probs.shape, F32)
    for n, val in enumerate(fields):
        route = jnp.where(idx == float(n), val, route)
    route_out[0] = route
    tile_cnt = jnp.sum(onehot, axis=1, keepdims=True)
    cnt_out[0] = tile_cnt
    carry[...] += tile_cnt


def _rwkv_out(o, bonus, g, h, gn_w, gn_b, w_o, gain, w_router):
    blocks = SEQ // TOKEN_BLOCK
    row = pl.BlockSpec((1, TOKEN_BLOCK, D_MODEL), lambda b, i: (b, i, 0))
    vec = pl.BlockSpec((1, D_MODEL), lambda b, i: (0, 0))
    return pl.pallas_call(
        _rwkv_out_kernel,
        grid=(BATCH, blocks),
        in_specs=[row, row, row, row, vec, vec,
                  pl.BlockSpec((D_MODEL, D_MODEL), lambda b, i: (0, 0)), vec,
                  pl.BlockSpec((N_EXPERTS, 3 * D_MODEL), lambda b, i: (0, 0))],
        out_specs=[row, pl.BlockSpec((1, 1, N_COL_CHUNKS, TOKEN_BLOCK, PIECE_WORDS), lambda b, i: (b, i, 0, 0, 0)),
                   pl.BlockSpec((1, N_EXPERTS, TOKEN_BLOCK), lambda b, i: (b, 0, i)),
                   pl.BlockSpec((1, N_EXPERTS, 1), lambda b, i: (b * blocks + i, 0, 0))],
        out_shape=[jax.ShapeDtypeStruct((BATCH, SEQ, D_MODEL), F32),
                   jax.ShapeDtypeStruct((BATCH, blocks, N_COL_CHUNKS, TOKEN_BLOCK, PIECE_WORDS), U32),
                   jax.ShapeDtypeStruct((BATCH, N_EXPERTS, SEQ), F32),
                   jax.ShapeDtypeStruct((BATCH * blocks, N_EXPERTS, 1), F32)],
        scratch_shapes=[pltpu.VMEM((N_EXPERTS, 1), F32)],
        compiler_params=_params("arbitrary", "arbitrary"),
    )(o, bonus, g, h, gn_w, gn_b, w_o, gain, w_router)


def _tile_tables(cnt):
    i32 = jnp.int32
    counts = jnp.sum(cnt.reshape(-1, N_EXPERTS).astype(i32), axis=0)
    tiles_e = jnp.where(counts > 0, jnp.maximum((counts - SUB_TILE + EXPERT_TILE - 1) // EXPERT_TILE, 1), 0)
    experts = jnp.arange(N_EXPERTS, dtype=i32)
    tile_end = jnp.sum(jnp.where(experts[None, :] <= experts[:, None], tiles_e[None, :], 0), axis=1)
    first_tile = tile_end - tiles_e
    tiles = jnp.arange(MAX_EXPERT_TILES, dtype=i32)[:, None]
    owns = jnp.logical_and(tiles >= first_tile[None, :], tiles < tile_end[None, :])
    pick = lambda table: jnp.sum(jnp.where(owns, table[None, :], 0), axis=1)
    is_last = tiles[:, 0] == pick(tile_end) - 1
    tile_rows = jnp.where(is_last, pick(counts) - (tiles[:, 0] - pick(first_tile)) * EXPERT_TILE, EXPERT_TILE)
    tile_rows = jnp.where(jnp.any(owns, axis=1), tile_rows, 0)
    return dict(first_tile=first_tile, tiles_e=tiles_e, tile_expert=pick(experts),
                n_used=tile_end[-1:], tile_rows=tile_rows)


def _piece_index_kernel(first_ref, ntile_ref, route_ref, o_ref):
    for j in range(PIECE_INDEX_BLOCKS):
        route = route_ref[0, :, j * TOKEN_BLOCK:(j + 1) * TOKEN_BLOCK]
        for slot in range(TOP_K):
            expert = route[slot:slot + 1, :].astype(jnp.int32)
            rank = route[TOP_K + slot:TOP_K + slot + 1, :].astype(jnp.int32)
            first = sum(jnp.where(expert == e, first_ref[e], 0) for e in range(N_EXPERTS))
            ntile = sum(jnp.where(expert == e, ntile_ref[e], 0) for e in range(N_EXPERTS))
            tile_in_group = jnp.minimum(rank // EXPERT_TILE, ntile - 1)
            row = rank - tile_in_group * EXPERT_TILE
            tile = first + tile_in_group
            for c in range(N_COL_CHUNKS):
                o_ref[slot, j, c:c + 1, :] = (tile * N_COL_CHUNKS + c) * TILE_CAP + row


def _piece_index(tab, route):
    steps = SEQ // (PIECE_INDEX_BLOCKS * TOKEN_BLOCK)
    grid_spec = pltpu.PrefetchScalarGridSpec(
        num_scalar_prefetch=2, grid=(BATCH, steps),
        in_specs=[pl.BlockSpec((1, N_EXPERTS, PIECE_INDEX_BLOCKS * TOKEN_BLOCK), lambda b, i, ft, nt: (b, 0, i))],
        out_specs=pl.BlockSpec((TOP_K, PIECE_INDEX_BLOCKS, N_COL_CHUNKS, TOKEN_BLOCK),
                               lambda b, i, ft, nt: (0, b * steps + i, 0, 0)))
    return pl.pallas_call(
        _piece_index_kernel, grid_spec=grid_spec,
        out_shape=jax.ShapeDtypeStruct((TOP_K, N_TOKEN_BLOCKS, N_COL_CHUNKS, TOKEN_BLOCK), jnp.int32),
        compiler_params=_params("parallel", "parallel"),
    )(tab["first_tile"], tab["tiles_e"], route).reshape(-1)


def _row_gather(x, indices):
    m = indices.shape[0]
    mesh = plsc.VectorSubcoreMesh(core_axis_name="c", subcore_axis_name="s")

    @pl.kernel(out_type=jax.ShapeDtypeStruct((m, PIECE_WORDS), x.dtype), mesh=mesh)
    def gather(x_hbm, i_hbm, o_hbm):
        def body(i_vmem, o_vmem):
            pltpu.sync_copy(x_hbm.at[i_vmem.at[0]], o_vmem)

        pltpu.emit_pipeline(
            body, grid=(m // GATHER_ROWS,),
            in_specs=[pl.BlockSpec((1, GATHER_ROWS), lambda i: (0, i))],
            out_specs=[pl.BlockSpec((GATHER_ROWS, PIECE_WORDS), lambda i: (i, 0))],
            core_axis_name=("c", "s"),
            dimension_semantics=(pltpu.PARALLEL,),
        )(i_hbm, o_hbm)

    return gather(x, indices.reshape(1, m))


def _row_scatter(x, indices, out_rows):
    m = indices.shape[0]
    x_blocks = x.shape[0] // GATHER_ROWS
    mesh = plsc.VectorSubcoreMesh(core_axis_name="c", subcore_axis_name="s")

    @pl.kernel(out_type=jax.ShapeDtypeStruct((out_rows, PIECE_WORDS), x.dtype), mesh=mesh)
    def scatter(x_hbm, i_hbm, o_hbm):
        def body(x_vmem, i_vmem):
            pltpu.sync_copy(x_vmem, o_hbm.at[i_vmem.at[0]])

        pltpu.emit_pipeline(
            body, grid=(m // GATHER_ROWS,),
            in_specs=[pl.BlockSpec((GATHER_ROWS, PIECE_WORDS), lambda i: (i % x_blocks, 0)),
                      pl.BlockSpec((1, GATHER_ROWS), lambda i: (0, i))],
            out_specs=[],
            core_axis_name=("c", "s"),
            dimension_semantics=(pltpu.PARALLEL,),
        )(x_hbm, i_hbm)

    return scatter(x, indices.reshape(1, m))


def _expert_kernel(te_ref, nu_ref, nr_ref, x_ref, wg_ref, wu_ref, wd_ref, y_ref, xb, acc):
    i = pl.program_id(0)
    f = pl.program_id(1)
    n_rows = nr_ref[i]
    n_sub = (n_rows + SUB_TILE - 1) // SUB_TILE
    subs = EXPERT_TILE // SUB_TILE

    @pl.when(n_sub > 0)
    def _():
        @pl.when(f == 0)
        def _():
            real = lax.broadcasted_iota(jnp.int32, (TILE_CAP, 1), 0) < n_rows
            for c in range(N_COL_CHUNKS):
                halves = _unpack_piece(x_ref[c * TILE_CAP:(c + 1) * TILE_CAP, :])
                for n, half in enumerate(halves):
                    lo = c * COL_CHUNK + n * PIECE_WORDS
                    xb[:, lo:lo + PIECE_WORDS] = jnp.where(real, half, 0.0).astype(BF16)
            acc[...] = jnp.zeros_like(acc)

        wg = wg_ref[0].astype(BF16)
        wu = wu_ref[0].astype(BF16)
        wd = wd_ref[0].astype(BF16)

        def block(rows):
            x = xb[rows, :]
            a = _dot(x, wg)
            b = _dot(x, wu)
            acc[rows, :] += _dot((a * _sigmoid(a) * b).astype(BF16), wd)

        @pl.when(n_sub >= subs)
        def _():
            block(slice(0, EXPERT_TILE))

        @pl.when(n_sub > subs)
        def _():
            block(slice(EXPERT_TILE, TILE_CAP))

        @pl.when(n_sub < subs)
        def _():
            for k in range(subs - 1):
                pl.when(k < n_sub)(functools.partial(block, slice(k * SUB_TILE, (k + 1) * SUB_TILE)))

        @pl.when(f == pl.num_programs(1) - 1)
        def _():
            for c in range(N_COL_CHUNKS):
                y_ref[c * TILE_CAP:(c + 1) * TILE_CAP, :] = _pack_piece(acc[:, c * COL_CHUNK:(c + 1) * COL_CHUNK])


def _experts(tab, x_sorted, wg, wu, wd):
    n_ff = D_FF // FF_TILE
    tile = lambda i, nu: jnp.minimum(i, nu[0] - 1)
    ff = lambda i, f, nu: jnp.where(i < nu[0], f, n_ff - 1)
    grid_spec = pltpu.PrefetchScalarGridSpec(
        num_scalar_prefetch=3, grid=(MAX_EXPERT_TILES, n_ff),
        in_specs=[pl.BlockSpec((N_COL_CHUNKS * TILE_CAP, PIECE_WORDS), lambda i, f, te, nu, ns: (tile(i, nu), 0)),
                  pl.BlockSpec((1, D_MODEL, FF_TILE),
                               lambda i, f, te, nu, ns: (te[tile(i, nu)], 0, ff(i, f, nu))),
                  pl.BlockSpec((1, D_MODEL, FF_TILE),
                               lambda i, f, te, nu, ns: (te[tile(i, nu)], 0, ff(i, f, nu))),
                  pl.BlockSpec((1, FF_TILE, D_MODEL),
                               lambda i, f, te, nu, ns: (te[tile(i, nu)], ff(i, f, nu), 0))],
        out_specs=pl.BlockSpec((N_COL_CHUNKS * TILE_CAP, PIECE_WORDS), lambda i, f, te, nu, ns: (tile(i, nu), 0)),
        scratch_shapes=[pltpu.VMEM((TILE_CAP, D_MODEL), BF16), pltpu.VMEM((TILE_CAP, D_MODEL), F32)])
    return pl.pallas_call(
        _expert_kernel, grid_spec=grid_spec,
        out_shape=jax.ShapeDtypeStruct((N_COL_CHUNKS * MAX_SORTED_ROWS, PIECE_WORDS), U32),
        compiler_params=_params("arbitrary", "arbitrary"),
    )(tab["tile_expert"], tab["n_used"], tab["tile_rows"], x_sorted, wg, wu, wd)


def _combine_kernel(h_ref, y_ref, g_ref, o_ref):
    g = g_ref[...]
    for c in range(N_COL_CHUNKS):
        first, second = _unpack_piece(y_ref[0, 0, c]), _unpack_piece(y_ref[1, 0, c])
        for n in range(2):
            cols = slice(c * COL_CHUNK + n * PIECE_WORDS, c * COL_CHUNK + (n + 1) * PIECE_WORDS)
            o_ref[:, cols] = h_ref[:, cols] + g[:, 0:1] * first[n] + g[:, 1:2] * second[n]


def _combine(h, y_pair, gate):
    row = pl.BlockSpec((TOKEN_BLOCK, D_MODEL), lambda i: (i, 0))
    return pl.pallas_call(
        _combine_kernel,
        grid=(N_TOKEN_BLOCKS,),
        in_specs=[row, pl.BlockSpec((TOP_K, 1, N_COL_CHUNKS, TOKEN_BLOCK, PIECE_WORDS), lambda i: (0, i, 0, 0, 0)),
                  pl.BlockSpec((TOKEN_BLOCK, TOP_K), lambda i: (i, 0))],
        out_specs=row,
        out_shape=jax.ShapeDtypeStruct((N_TOKENS, D_MODEL), F32),
        compiler_params=_params("parallel"),
    )(h, y_pair, gate)


def kernel(x, meta_tokens, mixer_norm, ffn_norm, attn_w_qkv, attn_q_norm, attn_k_norm, attn_sinks, attn_w_o, rwkv_mix, rwkv_w0, rwkv_w1, rwkv_w2, rwkv_a0, rwkv_a1, rwkv_a2, rwkv_g1, rwkv_g2, rwkv_k_k, rwkv_k_a, rwkv_r_k, rwkv_w_r, rwkv_w_k, rwkv_w_v, rwkv_w_o, rwkv_gn_w, rwkv_gn_b, ffn_w_gate, ffn_w_up, ffn_w_down, moe_router, moe_w_gate, moe_w_up, moe_w_down):
    bf = lambda a: a.astype(BF16)
    vec = lambda a: a.reshape(1, -1).astype(F32)
    h_meta = jnp.concatenate([jnp.zeros((PAD, D_MODEL), F32), meta_tokens.astype(F32)], axis=0)

    w_qkv = bf(attn_w_qkv[0])
    qkv = _qkv(x.reshape(N_TOKENS, D_MODEL), vec(mixer_norm[0]), w_qkv, TOKEN_BLOCK)
    qkv_meta = _qkv(h_meta, vec(mixer_norm[0]), w_qkv, BLOCK)
    h = _attention(x, h_meta, qkv.reshape(BATCH, SEQ, QKV_DIM), qkv_meta, attn_sinks[0].astype(F32),
                   jnp.tile(vec(attn_q_norm[0]), (1, Q_PER_KV)), jnp.tile(vec(attn_k_norm[0]), (1, N_KV_HEADS)),
                   bf(attn_w_o[0]))
    h = _ffn(h.reshape(N_ROWS, D_MODEL), vec(ffn_norm[0]), ffn_w_gate[0], ffn_w_up[0], ffn_w_down[0])

    rkv, g = _rwkv_proj(
        h, vec(mixer_norm[1]), rwkv_mix[0], vec(rwkv_w0[0]), vec(rwkv_a0[0]), vec(rwkv_k_k[0]),
        vec(rwkv_k_a[0]), bf(rwkv_w_r[0]), bf(rwkv_w_k[0]), bf(rwkv_w_v[0]), bf(rwkv_w1[0]),
        bf(rwkv_w2[0]), bf(rwkv_a1[0]), bf(rwkv_a2[0]), bf(rwkv_g1[0]), bf(rwkv_g2[0]))
    b3 = lambda t: t.reshape(BATCH, TP, -1)
    mm, add, bonus, p_last = _scan_prep(b3(rkv), vec(rwkv_r_k[0]))
    o = _scan(mm, add, p_last)
    w_router = moe_router[0].astype(F32).T
    wr_hi = bf(w_router)
    wr_lo = bf(w_router - wr_hi.astype(F32))
    h, u, route, cnt = _rwkv_out(o, bonus, b3(g), b3(h), vec(rwkv_gn_w[0]), vec(rwkv_gn_b[0]),
                                 bf(rwkv_w_o[0]), vec(ffn_norm[1]),
                                 jnp.concatenate([wr_hi, wr_hi, wr_lo], axis=1))
    tab = _tile_tables(cnt)
    sorted_piece = _piece_index(tab, route)
    gate = jnp.swapaxes(route[:, 2 * TOP_K:3 * TOP_K, :], 1, 2).reshape(N_TOKENS, TOP_K)
    x_sorted = _row_scatter(u.reshape(-1, PIECE_WORDS), sorted_piece, N_COL_CHUNKS * MAX_SORTED_ROWS)
    y_sorted = _experts(tab, x_sorted, moe_w_gate[0], moe_w_up[0], moe_w_down[0])
    y_pair = _row_gather(y_sorted, sorted_piece).reshape(TOP_K, N_TOKEN_BLOCKS, N_COL_CHUNKS, TOKEN_BLOCK, PIECE_WORDS)
    out = _combine(h.reshape(N_TOKENS, D_MODEL), y_pair, gate)
    return out.reshape(BATCH, SEQ, D_MODEL)
```

```python
import functools

import jax
import jax.numpy as jnp
from jax import lax
from jax.experimental import pallas as pl
from jax.experimental.pallas import tpu as pltpu
from jax.experimental.pallas import tpu_sc as plsc

F32 = jnp.float32
BF16 = jnp.bfloat16

D_MODEL = 1024
BATCH = 2
SEQ = 8192
N_META = 16
BLOCK = 128
PAD = BLOCK - N_META
TP = SEQ + BLOCK
N_ROWS = BATCH * TP
N_BLOCKS = TP // BLOCK
META_BLOCK = N_BLOCKS - 1
HEAD_DIM = 64
N_Q_HEADS = 16
N_KV_HEADS = 4
Q_PER_KV = 4
QKV_DIM = (N_Q_HEADS + 2 * N_KV_HEADS) * HEAD_DIM
RWKV_HEADS = 16
RWKV_HEAD = 64
D_FF = 3584
N_EXPERTS = 8
NORM_EPS = 1e-5
GN_EPS = 64e-5
CHUNK = 64
N_CHUNKS = TP // CHUNK
META_CHUNK0 = SEQ // CHUNK
SCAN_CHUNKS = 2
HEADS_PER_GROUP = 4
GW = HEADS_PER_GROUP * RWKV_HEAD
N_GROUPS = RWKV_HEADS // HEADS_PER_GROUP
N_SCAN_IN = 6

FFN_ROW_TILE = 1280
PROJ_TILE = 520
FF_TILE = 512
N_TOKENS = BATCH * SEQ
TOP_K = 2
TOKEN_BLOCK = 512
N_TOKEN_BLOCKS = N_TOKENS // TOKEN_BLOCK
EXPERT_TILE = 1024
SUB_TILE = 256
MAX_EXPERT_TILES = TOP_K * N_TOKENS // EXPERT_TILE + N_EXPERTS
TILE_CAP = EXPERT_TILE + SUB_TILE
MAX_SORTED_ROWS = MAX_EXPERT_TILES * TILE_CAP
COL_CHUNK = 256
N_COL_CHUNKS = D_MODEL // COL_CHUNK
PIECE_WORDS = COL_CHUNK // 2
U32 = jnp.uint32
GATHER_ROWS = 128
PIECE_INDEX_BLOCKS = 4
VMEM_LIMIT = 56 * 1024 * 1024
NEG = -1e30

NT_DIMS = (((1,), (1,)), ((), ()))
TN_DIMS = (((0,), (0,)), ((), ()))


def _params(*sem):
    return pltpu.CompilerParams(dimension_semantics=sem, vmem_limit_bytes=VMEM_LIMIT)


def _rms(x, gain):
    return x * lax.rsqrt(jnp.mean(x * x, axis=-1, keepdims=True) + NORM_EPS) * gain


def _sigmoid(x):
    return 1.0 / (1.0 + jnp.exp(-x))


def _dot(a, b):
    return jnp.dot(a, b, preferred_element_type=F32)


def _pack_piece(x):
    return pltpu.pack_elementwise([x[:, :PIECE_WORDS], x[:, PIECE_WORDS:]], packed_dtype=BF16)


def _unpack_piece(w):
    return tuple(pltpu.unpack_elementwise(w, index=i, packed_dtype=BF16, unpacked_dtype=F32) for i in (0, 1))


def _qkv_kernel(h_ref, g_ref, w_ref, o_ref):
    u = _rms(h_ref[...], g_ref[...])
    o_ref[...] = _dot(u.astype(BF16), w_ref[...]).astype(BF16)


def _qkv(h, gain, w, tile):
    rows = h.shape[0]
    return pl.pallas_call(
        _qkv_kernel,
        grid=(rows // tile,),
        in_specs=[pl.BlockSpec((tile, D_MODEL), lambda i: (i, 0)),
                  pl.BlockSpec((1, D_MODEL), lambda i: (0, 0)),
                  pl.BlockSpec((D_MODEL, QKV_DIM), lambda i: (0, 0))],
        out_specs=pl.BlockSpec((tile, QKV_DIM), lambda i: (i, 0)),
        out_shape=jax.ShapeDtypeStruct((rows, QKV_DIM), BF16),
        compiler_params=_params("parallel"),
    )(h, gain, w)


def _attn_kernel(sink_ref, x_ref, hm_ref, q_ref, kc_ref, vc_ref, kp_ref, vp_ref, qm_ref, km_ref, vm_ref,
                 qg_ref, kg_ref, wo_ref, o_ref, o_scr):
    j = pl.program_id(1)
    n_band = 2 * BLOCK
    slots = 2 * N_META
    far = 4 * BLOCK
    is_real = j < META_BLOCK
    prev_off = jnp.where(jnp.logical_and(j >= 1, is_real), 0, far)
    cur_off = jnp.where(is_real, 0, far)
    rowi = lax.broadcasted_iota(jnp.int32, (BLOCK, n_band), 0)
    col = lax.broadcasted_iota(jnp.int32, (BLOCK, n_band), 1)
    band_ok = jnp.logical_or(jnp.logical_and(col < BLOCK, col > rowi + prev_off),
                             jnp.logical_and(col >= BLOCK, col - BLOCK + cur_off <= rowi))
    band_ok4 = jnp.concatenate([band_ok] * Q_PER_KV, axis=1)
    meta_off = jnp.where(is_real, -far, PAD)
    mrow = lax.broadcasted_iota(jnp.int32, (BLOCK, Q_PER_KV * slots), 0)
    mlane = lax.broadcasted_iota(jnp.int32, (BLOCK, Q_PER_KV * slots), 1)
    mslot = mlane % slots
    meta_ok4 = jnp.logical_and(mslot < N_META, mslot + meta_off <= mrow)
    meta_lane_head = mlane // slots

    groups = range(N_KV_HEADS)
    mask_bf = _group_masks().astype(BF16)
    rb = lax.broadcasted_iota(jnp.int32, (Q_PER_KV * n_band, GW), 0) // n_band
    lb = lax.broadcasted_iota(jnp.int32, (Q_PER_KV * n_band, GW), 1) // HEAD_DIM
    band_mask = jnp.where(rb == lb, 1.0, 0.0).astype(BF16)
    rm = lax.broadcasted_iota(jnp.int32, (Q_PER_KV * slots, GW), 0) // slots
    lm = lax.broadcasted_iota(jnp.int32, (Q_PER_KV * slots, GW), 1) // HEAD_DIM
    meta_mask = jnp.where(rm == lm, 1.0, 0.0).astype(BF16)
    sr = lax.broadcasted_iota(jnp.int32, (GW, GW), 0)
    sc = lax.broadcasted_iota(jnp.int32, (GW, GW), 1)
    lane_head = lax.broadcasted_iota(jnp.int32, (1, GW), 1) // HEAD_DIM
    inv_d = 1.0 / HEAD_DIM
    scale = HEAD_DIM ** -0.5

    real_rows = lax.broadcasted_iota(jnp.int32, (BLOCK, 1), 0) < jnp.where(is_real, BLOCK, 0)
    km, vm = km_ref[...], vm_ref[...]
    kcur = jnp.where(real_rows, kc_ref[0], km)
    vcur = jnp.where(real_rows, vc_ref[0], vm)
    kall = jnp.concatenate([km[PAD:], kp_ref[0], kcur], axis=0).astype(F32)
    vall = jnp.concatenate([vm[PAD:], vp_ref[0], vcur], axis=0)
    kss = _dot((kall * kall).astype(BF16), mask_bf)
    kn = (kall * lax.rsqrt(kss * inv_d + NORM_EPS) * kg_ref[...]).astype(BF16)
    q_all = jnp.where(real_rows, q_ref[0], qm_ref[...]).astype(F32)
    qn = []
    for g in groups:
        qg = q_all[:, g * GW:(g + 1) * GW]
        qss = _dot((qg * qg).astype(BF16), mask_bf)
        qn.append((qg * lax.rsqrt(qss * inv_d + NORM_EPS) * (qg_ref[...] * scale)).astype(BF16))
    sel = [jnp.where(sr == g * HEAD_DIM + sc % HEAD_DIM, 1.0, 0.0).astype(BF16) for g in groups]
    krep = [_dot(kn, sel[g]).astype(BF16) for g in groups]
    vrep = [_dot(vall, sel[g]).astype(BF16) for g in groups]
    unused = jnp.zeros((slots - N_META, GW), BF16)

    def band_bd(rep):
        return jnp.concatenate([rep[N_META:]] * Q_PER_KV, axis=0) * band_mask

    def meta_bd(rep):
        return jnp.concatenate([rep[:N_META], unused] * Q_PER_KV, axis=0) * meta_mask

    s_band = [jnp.where(band_ok4, lax.dot_general(qn[g], band_bd(krep[g]), NT_DIMS,
                                                  preferred_element_type=F32), NEG) for g in groups]
    s_meta = [jnp.where(meta_ok4, lax.dot_general(qn[g], meta_bd(krep[g]), NT_DIMS,
                                                  preferred_element_type=F32), NEG) for g in groups]
    rhs_band = [jnp.concatenate([band_bd(vrep[g]), band_mask], axis=1) for g in groups]
    rhs_meta = [jnp.concatenate([meta_bd(vrep[g]), meta_mask], axis=1) for g in groups]
    ov, sink_den = [], []
    for g in groups:
        parts = []
        sd = jnp.zeros((BLOCK, GW), F32)
        m_lanes = jnp.zeros((BLOCK, Q_PER_KV * slots), F32)
        for hh in range(Q_PER_KV):
            seg = s_band[g][:, hh * n_band:(hh + 1) * n_band]
            own = meta_lane_head == hh
            sink = sink_ref[g * Q_PER_KV + hh]
            m = jnp.maximum(jnp.max(seg, axis=-1, keepdims=True),
                            jnp.max(jnp.where(own, s_meta[g], NEG), axis=-1, keepdims=True))
            m = jnp.maximum(m, sink)
            parts.append(jnp.exp(seg - m).astype(BF16))
            m_lanes = jnp.where(own, m, m_lanes)
            sd = sd + jnp.exp(sink - m) * jnp.where(lane_head == hh, 1.0, 0.0)
        p_band = jnp.concatenate(parts, axis=1)
        p_meta = jnp.exp(s_meta[g] - m_lanes).astype(BF16)
        ov.append(_dot(p_band, rhs_band[g]) + _dot(p_meta, rhs_meta[g]))
        sink_den.append(sd)
    for g in groups:
        o_scr[:, g * GW:(g + 1) * GW] = ov[g][:, :GW] / (ov[g][:, GW:] + sink_den[g])
    h = jnp.where(real_rows, x_ref[0], hm_ref[...])
    o_ref[0] = h + _dot(o_scr[...].astype(BF16), wo_ref[...])


def _attention(x, h_meta, qkv, qkv_meta, sinks, q_gain, k_gain, w_o):
    kcol, vcol = N_Q_HEADS * HEAD_DIM // 256, N_Q_HEADS * HEAD_DIM // 256 + 1
    kvw = N_KV_HEADS * HEAD_DIM
    real = lambda j: jnp.minimum(j, META_BLOCK - 1)
    prev = lambda j: jnp.clip(j - 1, 0, META_BLOCK - 1)
    return pl.pallas_call(
        _attn_kernel,
        grid=(BATCH, N_BLOCKS),
        in_specs=[pl.BlockSpec(memory_space=pltpu.SMEM),
                  pl.BlockSpec((1, BLOCK, D_MODEL), lambda b, j: (b, real(j), 0)),
                  pl.BlockSpec((BLOCK, D_MODEL), lambda b, j: (0, 0)),
                  pl.BlockSpec((1, BLOCK, D_MODEL), lambda b, j: (b, real(j), 0)),
                  pl.BlockSpec((1, BLOCK, kvw), lambda b, j: (b, real(j), kcol)),
                  pl.BlockSpec((1, BLOCK, kvw), lambda b, j: (b, real(j), vcol)),
                  pl.BlockSpec((1, BLOCK, kvw), lambda b, j: (b, prev(j), kcol)),
                  pl.BlockSpec((1, BLOCK, kvw), lambda b, j: (b, prev(j), vcol)),
                  pl.BlockSpec((BLOCK, D_MODEL), lambda b, j: (0, 0)),
                  pl.BlockSpec((BLOCK, kvw), lambda b, j: (0, kcol)),
                  pl.BlockSpec((BLOCK, kvw), lambda b, j: (0, vcol)),
                  pl.BlockSpec((1, GW), lambda b, j: (0, 0)),
                  pl.BlockSpec((1, GW), lambda b, j: (0, 0)),
                  pl.BlockSpec((D_MODEL, D_MODEL), lambda b, j: (0, 0))],
        out_specs=pl.BlockSpec((1, BLOCK, D_MODEL), lambda b, j: (b, j, 0)),
        out_shape=jax.ShapeDtypeStruct((BATCH, TP, D_MODEL), F32),
        scratch_shapes=[pltpu.VMEM((BLOCK, D_MODEL), F32)],
        compiler_params=_params("parallel", "parallel"),
    )(sinks, x, h_meta, qkv, qkv, qkv, qkv, qkv, qkv_meta, qkv_meta, qkv_meta, q_gain, k_gain, w_o)


def _ffn_kernel(h_ref, g_ref, wg_ref, wu_ref, wd_ref, o_ref, u_scr, acc):
    f = pl.program_id(1)

    @pl.when(f == 0)
    def _():
        u_scr[...] = _rms(h_ref[...], g_ref[...]).astype(BF16)
        acc[...] = jnp.zeros_like(acc)

    u = u_scr[...]
    a = _dot(u, wg_ref[...].astype(BF16))
    b = _dot(u, wu_ref[...].astype(BF16))
    acc[...] += _dot((a * _sigmoid(a) * b).astype(BF16), wd_ref[...].astype(BF16))

    @pl.when(f == pl.num_programs(1) - 1)
    def _():
        o_ref[...] = h_ref[...] + acc[...]


def _ffn(h, gain, wg, wu, wd):
    return pl.pallas_call(
        _ffn_kernel,
        grid=(N_ROWS // FFN_ROW_TILE, D_FF // FF_TILE),
        in_specs=[pl.BlockSpec((FFN_ROW_TILE, D_MODEL), lambda i, f: (i, 0)),
                  pl.BlockSpec((1, D_MODEL), lambda i, f: (0, 0)),
                  pl.BlockSpec((D_MODEL, FF_TILE), lambda i, f: (0, f)),
                  pl.BlockSpec((D_MODEL, FF_TILE), lambda i, f: (0, f)),
                  pl.BlockSpec((FF_TILE, D_MODEL), lambda i, f: (f, 0))],
        out_specs=pl.BlockSpec((FFN_ROW_TILE, D_MODEL), lambda i, f: (i, 0)),
        out_shape=jax.ShapeDtypeStruct((N_ROWS, D_MODEL), F32),
        scratch_shapes=[pltpu.VMEM((FFN_ROW_TILE, D_MODEL), BF16), pltpu.VMEM((FFN_ROW_TILE, D_MODEL), F32)],
        compiler_params=_params("parallel", "arbitrary"),
    )(h, gain, wg, wu, wd)


def _rwkv_proj_kernel(h_ref, hp_ref, g_ref, mix_ref, w0_ref, a0_ref, kk_ref, ka_ref,
                      wr_ref, wk_ref, wv_ref, w1_ref, w2_ref, a1_ref, a2_ref, g1_ref, g2_ref,
                      rkv_out, g_out):
    i = pl.program_id(0)
    tiles_per_batch = TP // PROJ_TILE
    r0 = (i % tiles_per_batch) * PROJ_TILE
    local = lax.broadcasted_iota(jnp.int32, (PROJ_TILE, 1), 0)
    lrow = local + r0
    gain = g_ref[...]
    is_pad = jnp.logical_and(lrow >= SEQ, lrow < SEQ + PAD)
    u = jnp.where(is_pad, 0.0, _rms(h_ref[...], gain))
    u_prev_tile = _rms(hp_ref[7:8, :], gain)
    xprev = pltpu.roll(u, 1, 0)
    xprev = jnp.where(local == 0, u_prev_tile, xprev)
    xprev = jnp.where(lrow == SEQ, 0.0, xprev)
    xx = xprev - u
    mix = mix_ref[...]
    lerp = lambda n: (u + xx * mix[n:n + 1, :]).astype(BF16)
    xr, xw, xk, xv, xa, xg = [lerp(n) for n in range(6)]
    r = _dot(xr, wr_ref[...])
    k = _dot(xk, wk_ref[...])
    v = _dot(xv, wv_ref[...])
    lw = _dot(jnp.tanh(_dot(xw, w1_ref[...])).astype(BF16), w2_ref[...])
    z = -(w0_ref[...] + lw)
    softplus = jnp.maximum(z, 0.0) + jnp.log(1.0 + jnp.exp(-jnp.abs(z)))
    w = -softplus - 0.5
    a = _sigmoid(a0_ref[...] + _dot(_dot(xa, a1_ref[...]).astype(BF16), a2_ref[...]))
    g = _dot(_sigmoid(_dot(xg, g1_ref[...])).astype(BF16), g2_ref[...])
    fields = (r, -jnp.exp(w),
              k * (1.0 + (a - 1.0) * ka_ref[...]), v, k * kk_ref[...], a)
    for n, val in enumerate(fields):
        rkv_out[:, n * D_MODEL:(n + 1) * D_MODEL] = val
    g_out[...] = g


def _rwkv_proj(h, gain, mix, w0, a0, k_k, k_a, w_r, w_k, w_v, w1, w2, a1, a2, g1, g2):
    tiles_per_batch = TP // PROJ_TILE
    rows8 = PROJ_TILE // 8

    def prev_map(i):
        b = i // tiles_per_batch
        first = (i % tiles_per_batch) == 0
        return (jnp.where(first, (b * TP + TP - 8) // 8, i * rows8 - 1), 0)

    row = pl.BlockSpec((PROJ_TILE, D_MODEL), lambda i: (i, 0))
    full = lambda a: pl.BlockSpec(a.shape, lambda i: (0,) * a.ndim)
    smalls = (gain, mix, w0, a0, k_k, k_a, w_r, w_k, w_v, w1, w2, a1, a2, g1, g2)
    return pl.pallas_call(
        _rwkv_proj_kernel,
        grid=(N_ROWS // PROJ_TILE,),
        in_specs=[row, pl.BlockSpec((8, D_MODEL), prev_map)] + [full(a) for a in smalls],
        out_specs=[pl.BlockSpec((PROJ_TILE, N_SCAN_IN * D_MODEL), lambda i: (i, 0)), row],
        out_shape=[jax.ShapeDtypeStruct((N_ROWS, N_SCAN_IN * D_MODEL), F32),
                   jax.ShapeDtypeStruct((N_ROWS, D_MODEL), F32)],
        compiler_params=_params("parallel"),
    )(h, h, *smalls)


def _group_masks():
    ri = lax.broadcasted_iota(jnp.int32, (GW, GW), 0) // RWKV_HEAD
    ci = lax.broadcasted_iota(jnp.int32, (GW, GW), 1) // RWKV_HEAD
    return jnp.where(ri == ci, 1.0, 0.0).astype(F32)


def _bd(x, mask):
    return jnp.concatenate([x.astype(BF16)] * HEADS_PER_GROUP, axis=0) * mask.astype(BF16)


def _diag_blocks(full, mask):
    m = full * mask
    n = RWKV_HEAD
    return (m[0:n] + m[n:2 * n]) + (m[2 * n:3 * n] + m[3 * n:4 * n])


def _head_sum(x, mask_bf):
    hi = x.astype(BF16)
    lo = (x - hi.astype(F32)).astype(BF16)
    return _dot(hi, mask_bf) + _dot(lo, mask_bf)


def _scan_prep_kernel(x_ref, rk_ref, mm_out, add_out, bonus_out, pl_out):
    L, D = CHUNK, D_MODEL
    units = [(b, g) for b in range(BATCH) for g in range(N_GROUPS)]
    un = range(len(units))
    mask = _group_masks()
    mask_bf = mask.astype(BF16)
    ri = lax.broadcasted_iota(jnp.int32, (L, GW), 0)
    ci = lax.broadcasted_iota(jnp.int32, (L, GW), 1) % RWKV_HEAD
    incl = ci <= ri
    strict = ci < ri
    eye = jnp.where(ci == ri, 1.0, 0.0).astype(F32)
    t_r = lax.broadcasted_iota(jnp.int32, (L, L), 0)
    t_c = lax.broadcasted_iota(jnp.int32, (L, L), 1)
    tril = jnp.where(t_c <= t_r, 1.0, 0.0).astype(BF16)
    rk_all = rk_ref[...]

    def field(b, n, g):
        return x_ref[b, :, n * D + g * GW:n * D + (g + 1) * GW]

    def put(ref, b, n, g, val):
        ref[b, :, n * D + g * GW:n * D + (g + 1) * GW] = val.astype(ref.dtype)

    at, rt, bt, kt, v, plast = [], [], [], [], [], []
    for b in range(BATCH):
        ld = x_ref[b, :, D:2 * D]
        hi = ld.astype(BF16)
        rest = ld - hi.astype(F32)
        mid = rest.astype(BF16)
        lo = (rest - mid.astype(F32)).astype(BF16)
        cs = _dot(tril, hi) + _dot(tril, mid) + _dot(tril, lo)
        p_all = jnp.exp(cs)
        pprev_all = jnp.exp(cs - ld)
        pinv_all = jnp.exp(-cs)
        pl_all = p_all[L - 1:L, :]
        pl_out[b, 0] = pl_all
        for g in range(N_GROUPS):
            sl = slice(g * GW, (g + 1) * GW)
            r, k, vv, kk, a = field(b, 0, g), field(b, 2, g), field(b, 3, g), field(b, 4, g), field(b, 5, g)
            nrm = jnp.sqrt(_dot((kk * kk).astype(BF16), mask_bf))
            kk = kk / jnp.maximum(nrm, 1e-12)
            at.append(-kk * pprev_all[:, sl])
            bt.append(kk * a * pinv_all[:, sl])
            rt.append(r * p_all[:, sl])
            kt.append(k * pinv_all[:, sl])
            v.append(vv)
            plast.append(pl_all[:, sl])
            put(bonus_out, b, 0, g, _dot((r * k * rk_all[:, sl]).astype(BF16), mask_bf) * vv)

    a_ab, a_ak, a_rb, a_rk = [], [], [], []
    for n in un:
        lhs = jnp.concatenate([at[n], rt[n]], axis=0).astype(BF16)
        rhs = jnp.concatenate([_bd(bt[n], mask), _bd(kt[n], mask)], axis=0)
        big = lax.dot_general(lhs, rhs, NT_DIMS, preferred_element_type=F32)
        a_ab.append(jnp.where(strict, big[:L, :GW], 0.0))
        a_ak.append(jnp.where(strict, big[:L, GW:], 0.0))
        a_rb.append(jnp.where(incl, big[L:, :GW], 0.0))
        a_rk.append(jnp.where(incl, big[L:, GW:], 0.0))

    x = [_dot(a_ab[n].astype(BF16), _bd(a_ab[n], mask)) for n in un]
    inv = [eye + a_ab[n] for n in un]
    for step in range(5):
        for n in un:
            rhs = _bd(x[n], mask)
            if step < 4:
                res = _dot(jnp.concatenate([x[n], inv[n]], axis=0).astype(BF16), rhs)
                x[n] = res[:L]
                inv[n] = inv[n] + res[L:]
            else:
                inv[n] = inv[n] + _dot(inv[n].astype(BF16), rhs)

    av = [_dot(jnp.concatenate([a_ak[n], a_rk[n]], axis=0).astype(BF16), _bd(v[n], mask)) for n in un]
    wu = [_dot(inv[n].astype(BF16), jnp.concatenate([_bd(at[n], mask), _bd(av[n][:L], mask)], axis=1))
          for n in un]
    aw = [_dot(a_rb[n].astype(BF16),
               jnp.concatenate([_bd(wu[n][:, :GW], mask), _bd(wu[n][:, GW:], mask)], axis=1))
          for n in un]
    for n, (b, g) in enumerate(units):
        put(mm_out, b, 0, g, rt[n] + aw[n][:, :GW])
        put(add_out, b, 0, g, av[n][L:] + aw[n][:, GW:])
        bh = (bt[n] * plast[n]).astype(BF16)
        kh = (kt[n] * plast[n]).astype(BF16)
        w_b, u0_b = wu[n][:, :GW].astype(BF16), wu[n][:, GW:].astype(BF16)
        gfull = lax.dot_general(bh, w_b, TN_DIMS, preferred_element_type=F32)
        put(mm_out, b, 1, g, _diag_blocks(gfull, mask))
        hfull = lax.dot_general(jnp.concatenate([u0_b, v[n].astype(BF16)], axis=0),
                                jnp.concatenate([bh, kh], axis=0), TN_DIMS, preferred_element_type=F32)
        put(add_out, b, 1, g, _diag_blocks(hfull, mask))


def _scan_prep(rkv, r_k):
    return pl.pallas_call(
        _scan_prep_kernel,
        grid=(N_CHUNKS,),
        in_specs=[pl.BlockSpec((BATCH, CHUNK, N_SCAN_IN * D_MODEL), lambda c: (0, c, 0)),
                  pl.BlockSpec((1, D_MODEL), lambda c: (0, 0))],
        out_specs=[pl.BlockSpec((BATCH, CHUNK, 2 * D_MODEL), lambda c: (0, c, 0)),
                   pl.BlockSpec((BATCH, CHUNK, 2 * D_MODEL), lambda c: (0, c, 0)),
                   pl.BlockSpec((BATCH, CHUNK, D_MODEL), lambda c: (0, c, 0)),
                   pl.BlockSpec((BATCH, 1, 1, D_MODEL), lambda c: (0, c, 0, 0))],
        out_shape=[jax.ShapeDtypeStruct((BATCH, TP, 2 * D_MODEL), BF16),
                   jax.ShapeDtypeStruct((BATCH, TP, 2 * D_MODEL), F32),
                   jax.ShapeDtypeStruct((BATCH, TP, D_MODEL), F32),
                   jax.ShapeDtypeStruct((BATCH, N_CHUNKS, 1, D_MODEL), F32)],
        compiler_params=_params("parallel"),
    )(rkv, r_k)


def _scan_kernel(mm_ref, add_ref, pl_ref, y_ref, s_scr):
    c = pl.program_id(0)
    D = D_MODEL

    @pl.when(c == 0)
    def _():
        s_scr[...] = jnp.zeros_like(s_scr)

    mask = _group_masks()
    units = [(b, slice(g * GW, (g + 1) * GW)) for b in range(BATCH) for g in range(N_GROUPS)]
    s = [s_scr[b, :, sl] for b, sl in units]
    for k in range(SCAN_CHUNKS):
        rows = slice(k * CHUNK, (k + 1) * CHUNK)
        field = lambda ref, b, n, sl, rows=rows: ref[b, rows, n * D + sl.start:n * D + sl.stop]
        o = [lax.dot_general(field(mm_ref, b, 0, sl), _bd(s[n], mask), NT_DIMS,
                             preferred_element_type=F32) + field(add_ref, b, 0, sl)
             for n, (b, sl) in enumerate(units)]
        sg = [lax.dot_general(s[n].astype(BF16), _bd(field(mm_ref, b, 1, sl), mask), NT_DIMS,
                              preferred_element_type=F32)
              for n, (b, sl) in enumerate(units)]
        s = [s[n] * pl_ref[b, k, :, sl] + sg[n] + field(add_ref, b, 1, sl)
             for n, (b, sl) in enumerate(units)]
        for n, (b, sl) in enumerate(units):
            y_ref[b, rows, sl] = o[n]
    for n, (b, sl) in enumerate(units):
        s_scr[b, :, sl] = s[n]


def _scan(mm, add, p_last):
    steps = N_CHUNKS // SCAN_CHUNKS
    first = META_CHUNK0 // SCAN_CHUNKS
    phys = lambda c: (c + first) % steps
    pair = pl.BlockSpec((BATCH, SCAN_CHUNKS * CHUNK, 2 * D_MODEL), lambda c: (0, phys(c), 0))
    return pl.pallas_call(
        _scan_kernel,
        grid=(steps,),
        in_specs=[pair, pair, pl.BlockSpec((BATCH, SCAN_CHUNKS, 1, D_MODEL), lambda c: (0, phys(c), 0, 0))],
        out_specs=pl.BlockSpec((BATCH, SCAN_CHUNKS * CHUNK, D_MODEL), lambda c: (0, phys(c), 0)),
        out_shape=jax.ShapeDtypeStruct((BATCH, TP, D_MODEL), F32),
        scratch_shapes=[pltpu.VMEM((BATCH, RWKV_HEAD, D_MODEL), F32)],
        compiler_params=_params("arbitrary"),
    )(mm, add, p_last)


def _rwkv_out_kernel(o_ref, bonus_ref, g_ref, h_ref, gw_ref, gb_ref, wo_ref, gain_ref, wr_ref,
                     h_out, u_out, route_out, cnt_out, carry):
    @pl.when(jnp.logical_and(pl.program_id(0) == 0, pl.program_id(1) == 0))
    def _():
        carry[...] = jnp.zeros_like(carry)

    mask_bf = _group_masks().astype(BF16)
    inv_n = 1.0 / RWKV_HEAD
    o, bonus, gate = o_ref[0], bonus_ref[0], g_ref[0]
    parts = []
    for g in range(N_GROUPS):
        sl = slice(g * GW, (g + 1) * GW)
        og = o[:, sl]
        d = og - _head_sum(og, mask_bf) * inv_n
        var = _dot((d * d).astype(BF16), mask_bf) * inv_n
        yn = d * lax.rsqrt(var + GN_EPS) * gw_ref[:, sl] + gb_ref[:, sl] + bonus[:, sl]
        parts.append((yn * gate[:, sl]).astype(BF16))
    h = h_ref[0] + _dot(jnp.concatenate(parts, axis=1), wo_ref[...])
    h_out[0] = h
    u = _rms(h, gain_ref[...])
    for c in range(N_COL_CHUNKS):
        u_out[0, 0, c] = _pack_piece(u[:, c * COL_CHUNK:(c + 1) * COL_CHUNK])
    u_hi = u.astype(BF16)
    u_lo = (u - u_hi.astype(F32)).astype(BF16)
    logits = lax.dot_general(wr_ref[...], jnp.concatenate([u_hi, u_lo, u_hi], axis=1), NT_DIMS,
                             preferred_element_type=F32)
    e = jnp.exp(logits - jnp.max(logits, axis=0, keepdims=True))
    probs = e / jnp.sum(e, axis=0, keepdims=True)
    idx = lax.broadcasted_iota(jnp.int32, probs.shape, 0).astype(F32)
    m1 = jnp.max(probs, axis=0, keepdims=True)
    i1 = jnp.min(jnp.where(probs == m1, idx, float(N_EXPERTS)), axis=0, keepdims=True)
    sel1 = idx == i1
    rest = jnp.where(sel1, -1.0, probs)
    m2 = jnp.max(rest, axis=0, keepdims=True)
    i2 = jnp.min(jnp.where(rest == m2, idx, float(N_EXPERTS)), axis=0, keepdims=True)
    sel2 = idx == i2
    onehot = jnp.where(jnp.logical_or(sel1, sel2), 1.0, 0.0).astype(F32)
    tr = lax.broadcasted_iota(jnp.int32, (TOKEN_BLOCK, TOKEN_BLOCK), 0)
    tc = lax.broadcasted_iota(jnp.int32, (TOKEN_BLOCK, TOKEN_BLOCK), 1)
    earlier = _dot(onehot.astype(BF16), jnp.where(tr < tc, 1.0, 0.0).astype(BF16)) + carry[...]
    rank1 = jnp.sum(jnp.where(sel1, earlier, 0.0), axis=0, keepdims=True)
    rank2 = jnp.sum(jnp.where(sel2, earlier, 0.0), axis=0, keepdims=True)
    den = m1 + m2
    fields = (i1, i2, rank1, rank2, m1 / den, m2 / den)
    route = jnp.zeros(probs.shape, F32)
    for n, val in enumerate(fields):
        route = jnp.where(idx == float(n), val, route)
    route_out[0] = route
    tile_cnt = jnp.sum(onehot, axis=1, keepdims=True)
    cnt_out[0] = tile_cnt
    carry[...] += tile_cnt


def _rwkv_out(o, bonus, g, h, gn_w, gn_b, w_o, gain, w_router):
    blocks = SEQ // TOKEN_BLOCK
    row = pl.BlockSpec((1, TOKEN_BLOCK, D_MODEL), lambda b, i: (b, i, 0))
    vec = pl.BlockSpec((1, D_MODEL), lambda b, i: (0, 0))
    return pl.pallas_call(
        _rwkv_out_kernel,
        grid=(BATCH, blocks),
        in_specs=[row, row, row, row, vec, vec,
                  pl.BlockSpec((D_MODEL, D_MODEL), lambda b, i: (0, 0)), vec,
                  pl.BlockSpec((N_EXPERTS, 3 * D_MODEL), lambda b, i: (0, 0))],
        out_specs=[row, pl.BlockSpec((1, 1, N_COL_CHUNKS, TOKEN_BLOCK, PIECE_WORDS), lambda b, i: (b, i, 0, 0, 0)),
                   pl.BlockSpec((1, N_EXPERTS, TOKEN_BLOCK), lambda b, i: (b, 0, i)),
                   pl.BlockSpec((1, N_EXPERTS, 1), lambda b, i: (b * blocks + i, 0, 0))],
        out_shape=[jax.ShapeDtypeStruct((BATCH, SEQ, D_MODEL), F32),
                   jax.ShapeDtypeStruct((BATCH, blocks, N_COL_CHUNKS, TOKEN_BLOCK, PIECE_WORDS), U32),
                   jax.ShapeDtypeStruct((BATCH, N_EXPERTS, SEQ), F32),
                   jax.ShapeDtypeStruct((BATCH * blocks, N_EXPERTS, 1), F32)],
        scratch_shapes=[pltpu.VMEM((N_EXPERTS, 1), F32)],
        compiler_params=_params("arbitrary", "arbitrary"),
    )(o, bonus, g, h, gn_w, gn_b, w_o, gain, w_router)


def _tile_tables(cnt):
    i32 = jnp.int32
    counts = jnp.sum(cnt.reshape(-1, N_EXPERTS).astype(i32), axis=0)
    tiles_e = jnp.where(counts > 0, jnp.maximum((counts - SUB_TILE + EXPERT_TILE - 1) // EXPERT_TILE, 1), 0)
    experts = jnp.arange(N_EXPERTS, dtype=i32)
    tile_end = jnp.sum(jnp.where(experts[None, :] <= experts[:, None], tiles_e[None, :], 0), axis=1)
    first_tile = tile_end - tiles_e
    tiles = jnp.arange(MAX_EXPERT_TILES, dtype=i32)[:, None]
    owns = jnp.logical_and(tiles >= first_tile[None, :], tiles < tile_end[None, :])
    pick = lambda table: jnp.sum(jnp.where(owns, table[None, :], 0), axis=1)
    is_last = tiles[:, 0] == pick(tile_end) - 1
    tile_rows = jnp.where(is_last, pick(counts) - (tiles[:, 0] - pick(first_tile)) * EXPERT_TILE, EXPERT_TILE)
    tile_rows = jnp.where(jnp.any(owns, axis=1), tile_rows, 0)
    return dict(first_tile=first_tile, tiles_e=tiles_e, tile_expert=pick(experts),
                n_used=tile_end[-1:], tile_rows=tile_rows)


def _piece_index_kernel(first_ref, ntile_ref, route_ref, o_ref):
    for j in range(PIECE_INDEX_BLOCKS):
        route = route_ref[0, :, j * TOKEN_BLOCK:(j + 1) * TOKEN_BLOCK]
        for slot in range(TOP_K):
            expert = route[slot:slot + 1, :].astype(jnp.int32)
            rank = route[TOP_K + slot:TOP_K + slot + 1, :].astype(jnp.int32)
            first = sum(jnp.where(expert == e, first_ref[e], 0) for e in range(N_EXPERTS))
            ntile = sum(jnp.where(expert == e, ntile_ref[e], 0) for e in range(N_EXPERTS))
            tile_in_group = jnp.minimum(rank // EXPERT_TILE, ntile - 1)
            row = rank - tile_in_group * EXPERT_TILE
            tile = first + tile_in_group
            for c in range(N_COL_CHUNKS):
                o_ref[slot, j, c:c + 1, :] = (tile * N_COL_CHUNKS + c) * TILE_CAP + row


def _piece_index(tab, route):
    steps = SEQ // (PIECE_INDEX_BLOCKS * TOKEN_BLOCK)
    grid_spec = pltpu.PrefetchScalarGridSpec(
        num_scalar_prefetch=2, grid=(BATCH, steps),
        in_specs=[pl.BlockSpec((1, N_EXPERTS, PIECE_INDEX_BLOCKS * TOKEN_BLOCK), lambda b, i, ft, nt: (b, 0, i))],
        out_specs=pl.BlockSpec((TOP_K, PIECE_INDEX_BLOCKS, N_COL_CHUNKS, TOKEN_BLOCK),
                               lambda b, i, ft, nt: (0, b * steps + i, 0, 0)))
    return pl.pallas_call(
        _piece_index_kernel, grid_spec=grid_spec,
        out_shape=jax.ShapeDtypeStruct((TOP_K, N_TOKEN_BLOCKS, N_COL_CHUNKS, TOKEN_BLOCK), jnp.int32),
        compiler_params=_params("parallel", "parallel"),
    )(tab["first_tile"], tab["tiles_e"], route).reshape(-1)


def _row_gather(x, indices):
    m = indices.shape[0]
    mesh = plsc.VectorSubcoreMesh(core_axis_name="c", subcore_axis_name="s")

    @pl.kernel(out_type=jax.ShapeDtypeStruct((m, PIECE_WORDS), x.dtype), mesh=mesh)
    def gather(x_hbm, i_hbm, o_hbm):
        def body(i_vmem, o_vmem):
            pltpu.sync_copy(x_hbm.at[i_vmem.at[0]], o_vmem)

        pltpu.emit_pipeline(
            body, grid=(m // GATHER_ROWS,),
            in_specs=[pl.BlockSpec((1, GATHER_ROWS), lambda i: (0, i))],
            out_specs=[pl.BlockSpec((GATHER_ROWS, PIECE_WORDS), lambda i: (i, 0))],
            core_axis_name=("c", "s"),
            dimension_semantics=(pltpu.PARALLEL,),
        )(i_hbm, o_hbm)

    return gather(x, indices.reshape(1, m))


def _row_scatter(x, indices, out_rows):
    m = indices.shape[0]
    x_blocks = x.shape[0] // GATHER_ROWS
    mesh = plsc.VectorSubcoreMesh(core_axis_name="c", subcore_axis_name="s")

    @pl.kernel(out_type=jax.ShapeDtypeStruct((out_rows, PIECE_WORDS), x.dtype), mesh=mesh)
    def scatter(x_hbm, i_hbm, o_hbm):
        def body(x_vmem, i_vmem):
            pltpu.sync_copy(x_vmem, o_hbm.at[i_vmem.at[0]])

        pltpu.emit_pipeline(
            body, grid=(m // GATHER_ROWS,),
            in_specs=[pl.BlockSpec((GATHER_ROWS, PIECE_WORDS), lambda i: (i % x_blocks, 0)),
                      pl.BlockSpec((1, GATHER_ROWS), lambda i: (0, i))],
            out_specs=[],
            core_axis_name=("c", "s"),
            dimension_semantics=(pltpu.PARALLEL,),
        )(x_hbm, i_hbm)

    return scatter(x, indices.reshape(1, m))


def _expert_kernel(te_ref, nu_ref, nr_ref, x_ref, wg_ref, wu_ref, wd_ref, y_ref, xb, acc):
    i = pl.program_id(0)
    f = pl.program_id(1)
    n_rows = nr_ref[i]
    n_sub = (n_rows + SUB_TILE - 1) // SUB_TILE
    subs = EXPERT_TILE // SUB_TILE

    @pl.when(n_sub > 0)
    def _():
        @pl.when(f == 0)
        def _():
            real = lax.broadcasted_iota(jnp.int32, (TILE_CAP, 1), 0) < n_rows
            for c in range(N_COL_CHUNKS):
                halves = _unpack_piece(x_ref[c * TILE_CAP:(c + 1) * TILE_CAP, :])
                for n, half in enumerate(halves):
                    lo = c * COL_CHUNK + n * PIECE_WORDS
                    xb[:, lo:lo + PIECE_WORDS] = jnp.where(real, half, 0.0).astype(BF16)
            acc[...] = jnp.zeros_like(acc)

        wg = wg_ref[0].astype(BF16)
        wu = wu_ref[0].astype(BF16)
        wd = wd_ref[0].astype(BF16)

        def block(rows):
            x = xb[rows, :]
            a = _dot(x, wg)
            b = _dot(x, wu)
            acc[rows, :] += _dot((a * _sigmoid(a) * b).astype(BF16), wd)

        @pl.when(n_sub >= subs)
        def _():
            block(slice(0, EXPERT_TILE))

        @pl.when(n_sub > subs)
        def _():
            block(slice(EXPERT_TILE, TILE_CAP))

        @pl.when(n_sub < subs)
        def _():
            for k in range(subs - 1):
                pl.when(k < n_sub)(functools.partial(block, slice(k * SUB_TILE, (k + 1) * SUB_TILE)))

        @pl.when(f == pl.num_programs(1) - 1)
        def _():
            for c in range(N_COL_CHUNKS):
                y_ref[c * TILE_CAP:(c + 1) * TILE_CAP, :] = _pack_piece(acc[:, c * COL_CHUNK:(c + 1) * COL_CHUNK])


def _experts(tab, x_sorted, wg, wu, wd):
    n_ff = D_FF // FF_TILE
    tile = lambda i, nu: jnp.minimum(i, nu[0] - 1)
    ff = lambda i, f, nu: jnp.where(i < nu[0], f, n_ff - 1)
    grid_spec = pltpu.PrefetchScalarGridSpec(
        num_scalar_prefetch=3, grid=(MAX_EXPERT_TILES, n_ff),
        in_specs=[pl.BlockSpec((N_COL_CHUNKS * TILE_CAP, PIECE_WORDS), lambda i, f, te, nu, ns: (tile(i, nu), 0)),
                  pl.BlockSpec((1, D_MODEL, FF_TILE),
                               lambda i, f, te, nu, ns: (te[tile(i, nu)], 0, ff(i, f, nu))),
                  pl.BlockSpec((1, D_MODEL, FF_TILE),
                               lambda i, f, te, nu, ns: (te[tile(i, nu)], 0, ff(i, f, nu))),
                  pl.BlockSpec((1, FF_TILE, D_MODEL),
                               lambda i, f, te, nu, ns: (te[tile(i, nu)], ff(i, f, nu), 0))],
        out_specs=pl.BlockSpec((N_COL_CHUNKS * TILE_CAP, PIECE_WORDS), lambda i, f, te, nu, ns: (tile(i, nu), 0)),
        scratch_shapes=[pltpu.VMEM((TILE_CAP, D_MODEL), BF16), pltpu.VMEM((TILE_CAP, D_MODEL), F32)])
    return pl.pallas_call(
        _expert_kernel, grid_spec=grid_spec,
        out_shape=jax.ShapeDtypeStruct((N_COL_CHUNKS * MAX_SORTED_ROWS, PIECE_WORDS), U32),
        compiler_params=_params("arbitrary", "arbitrary"),
    )(tab["tile_expert"], tab["n_used"], tab["tile_rows"], x_sorted, wg, wu, wd)


def _combine_kernel(h_ref, y_ref, g_ref, o_ref):
    g = g_ref[...]
    for c in range(N_COL_CHUNKS):
        first, second = _unpack_piece(y_ref[0, 0, c]), _unpack_piece(y_ref[1, 0, c])
        for n in range(2):
            cols = slice(c * COL_CHUNK + n * PIECE_WORDS, c * COL_CHUNK + (n + 1) * PIECE_WORDS)
            o_ref[:, cols] = h_ref[:, cols] + g[:, 0:1] * first[n] + g[:, 1:2] * second[n]


def _combine_into_kernel(h_ref, y_ref, g_ref, prev_ref, o_ref):
    del prev_ref
    _combine_kernel(h_ref, y_ref, g_ref, o_ref)


def _combine(h, y_pair, gate, batch, out_prev=None):
    blocks = SEQ // TOKEN_BLOCK
    row = pl.BlockSpec((TOKEN_BLOCK, D_MODEL), lambda i: (batch * blocks + i, 0))
    in_specs = [row, pl.BlockSpec((TOP_K, 1, N_COL_CHUNKS, TOKEN_BLOCK, PIECE_WORDS), lambda i: (0, i, 0, 0, 0)),
                pl.BlockSpec((TOKEN_BLOCK, TOP_K), lambda i: (batch * blocks + i, 0))]
    args = [h, y_pair, gate]
    if out_prev is not None:
        in_specs.append(pl.BlockSpec(memory_space=pl.ANY))
        args.append(out_prev)
    return pl.pallas_call(
        _combine_kernel if out_prev is None else _combine_into_kernel,
        grid=(blocks,),
        in_specs=in_specs,
        out_specs=row,
        out_shape=jax.ShapeDtypeStruct((N_TOKENS, D_MODEL), F32),
        input_output_aliases={} if out_prev is None else {3: 0},
        compiler_params=_params("parallel"),
    )(*args)


def kernel(x, meta_tokens, mixer_norm, ffn_norm, attn_w_qkv, attn_q_norm, attn_k_norm, attn_sinks, attn_w_o, rwkv_mix, rwkv_w0, rwkv_w1, rwkv_w2, rwkv_a0, rwkv_a1, rwkv_a2, rwkv_g1, rwkv_g2, rwkv_k_k, rwkv_k_a, rwkv_r_k, rwkv_w_r, rwkv_w_k, rwkv_w_v, rwkv_w_o, rwkv_gn_w, rwkv_gn_b, ffn_w_gate, ffn_w_up, ffn_w_down, moe_router, moe_w_gate, moe_w_up, moe_w_down):
    bf = lambda a: a.astype(BF16)
    vec = lambda a: a.reshape(1, -1).astype(F32)
    h_meta = jnp.concatenate([jnp.zeros((PAD, D_MODEL), F32), meta_tokens.astype(F32)], axis=0)

    w_qkv = bf(attn_w_qkv[0])
    qkv = _qkv(x.reshape(N_TOKENS, D_MODEL), vec(mixer_norm[0]), w_qkv, TOKEN_BLOCK)
    qkv_meta = _qkv(h_meta, vec(mixer_norm[0]), w_qkv, BLOCK)
    h = _attention(x, h_meta, qkv.reshape(BATCH, SEQ, QKV_DIM), qkv_meta, attn_sinks[0].astype(F32),
                   jnp.tile(vec(attn_q_norm[0]), (1, Q_PER_KV)), jnp.tile(vec(attn_k_norm[0]), (1, N_KV_HEADS)),
                   bf(attn_w_o[0]))
    h = _ffn(h.reshape(N_ROWS, D_MODEL), vec(ffn_norm[0]), ffn_w_gate[0], ffn_w_up[0], ffn_w_down[0])

    rkv, g = _rwkv_proj(
        h, vec(mixer_norm[1]), rwkv_mix[0], vec(rwkv_w0[0]), vec(rwkv_a0[0]), vec(rwkv_k_k[0]),
        vec(rwkv_k_a[0]), bf(rwkv_w_r[0]), bf(rwkv_w_k[0]), bf(rwkv_w_v[0]), bf(rwkv_w1[0]),
        bf(rwkv_w2[0]), bf(rwkv_a1[0]), bf(rwkv_a2[0]), bf(rwkv_g1[0]), bf(rwkv_g2[0]))
    b3 = lambda t: t.reshape(BATCH, TP, -1)
    mm, add, bonus, p_last = _scan_prep(b3(rkv), vec(rwkv_r_k[0]))
    o = _scan(mm, add, p_last)
    w_router = moe_router[0].astype(F32).T
    wr_hi = bf(w_router)
    wr_lo = bf(w_router - wr_hi.astype(F32))
    h, u, route, cnt = _rwkv_out(o, bonus, b3(g), b3(h), vec(rwkv_gn_w[0]), vec(rwkv_gn_b[0]),
                                 bf(rwkv_w_o[0]), vec(ffn_norm[1]),
                                 jnp.concatenate([wr_hi, wr_hi, wr_lo], axis=1))
    tab = _tile_tables(cnt)
    sorted_piece = _piece_index(tab, route)
    gate = jnp.swapaxes(route[:, 2 * TOP_K:3 * TOP_K, :], 1, 2).reshape(N_TOKENS, TOP_K)
    x_sorted = _row_scatter(u.reshape(-1, PIECE_WORDS), sorted_piece, N_COL_CHUNKS * MAX_SORTED_ROWS)
    y_sorted = _experts(tab, x_sorted, moe_w_gate[0], moe_w_up[0], moe_w_down[0])
    pieces = sorted_piece.reshape(TOP_K, BATCH, -1)
    out = None
    for b in range(BATCH):
        y_pair = _row_gather(y_sorted, pieces[:, b].reshape(-1))
        y_pair = y_pair.reshape(TOP_K, SEQ // TOKEN_BLOCK, N_COL_CHUNKS, TOKEN_BLOCK, PIECE_WORDS)
        out = _combine(h.reshape(N_TOKENS, D_MODEL), y_pair, gate, b, out)
    return out.reshape(BATCH, SEQ, D_MODEL)
```
